```python
import math
import jax, jax.numpy as jnp
from jax import lax
import numpy as np

D_MODEL = 1024
BATCH = 1
SEQ = 16384
DEPTH = 1
DEC_BATCH = 32
DEC_SEQ = 16
PAST_LEN = 4096

CHUNK = 64
N_META = 16
HEAD_DIM = 64
DA_HEADS = 4
DA_WIDTH = DA_HEADS * 2 * HEAD_DIM
RW_HEADS = 8
RW_WIDTH = RW_HEADS * HEAD_DIM
W_LORA = 64
A_LORA = 64
G_LORA = 128
RW_PROJ = 3 * RW_WIDTH + W_LORA + A_LORA + G_LORA
RW_SPLITS = (RW_WIDTH, 2 * RW_WIDTH, 3 * RW_WIDTH, 3 * RW_WIDTH + W_LORA, 3 * RW_WIDTH + W_LORA + A_LORA)
IN_PROJ = 3 * DA_WIDTH + RW_PROJ
D_FF = 2816
CONV_W = 3
N_BUCKETS = 32
MAX_DISTANCE = 128
Q_BLOCK = 128
NORM_EPS = 1e-6
GN_EPS = 64e-5
NEG_INF = -1e30
F32 = jnp.float32

kernel_name = 'hymba_diffattn_rwkv7_convglu_stream_step'


def rms_norm(x, g, eps=NORM_EPS):
    xf = x.astype(F32)
    y = xf * lax.rsqrt(jnp.mean(xf * xf, axis=-1, keepdims=True) + eps)
    return (y * g.astype(F32)).astype(x.dtype)


def ext_pos_chunk(n):
    pos = jnp.arange(n, dtype=jnp.int32)
    chunk = jnp.where(pos < N_META, -1, (pos - N_META) // CHUNK)
    return pos, chunk


def rel_bucket(rel):
    nb = N_BUCKETS // 2
    max_exact = nb // 2
    bucket = jnp.where(rel > 0, nb, 0)
    n = jnp.abs(rel)
    nf = jnp.maximum(n, 1).astype(F32)
    large = max_exact + (jnp.log(nf / max_exact) / math.log(MAX_DISTANCE / max_exact) * (nb - max_exact)).astype(jnp.int32)
    large = jnp.minimum(large, nb - 1)
    return bucket + jnp.where(n < max_exact, n, large)


def diff_attention(q, k, v, q_pos, q_chunk, k_pos, k_chunk, rel_bias, lam):
    s = jnp.einsum('bqhcd,bkhcd->bchqk', q, k, preferred_element_type=F32) * (HEAD_DIM ** -0.5)
    bias = jnp.transpose(rel_bias[rel_bucket(k_pos[None, :] - q_pos[:, None])], (2, 0, 1)).astype(F32)
    mask = k_chunk[None, :] <= q_chunk[:, None]
    p = jax.nn.softmax(jnp.where(mask, s + bias, NEG_INF), axis=-1)
    a = p[:, 0] - lam * p[:, 1]
    return jnp.einsum('bhqk,bkhe->bqhe', a.astype(v.dtype), v)


def prompt_diff_attention(q, k, v, rel_bias, lam):
    B, L = q.shape[0], q.shape[1]
    n_blk = -(-L // Q_BLOCK)
    pad = n_blk * Q_BLOCK - L
    pos, chunk = ext_pos_chunk(n_blk * Q_BLOCK)
    qb = jnp.pad(q, ((0, 0), (0, pad), (0, 0), (0, 0), (0, 0)))
    qb = jnp.moveaxis(qb.reshape(B, n_blk, Q_BLOCK, DA_HEADS, 2, HEAD_DIM), 1, 0)
    k_pos, k_chunk = pos[:L], chunk[:L]

    def one_block(blk):
        q_blk, qp, qc = blk
        return diff_attention(q_blk, k, v, qp, qc, k_pos, k_chunk, rel_bias, lam)

    out = lax.map(one_block, (qb, pos.reshape(n_blk, Q_BLOCK), chunk.reshape(n_blk, Q_BLOCK)))
    out = jnp.moveaxis(out, 0, 1).reshape(B, n_blk * Q_BLOCK, DA_HEADS, 2 * HEAD_DIM)
    return out[:, :L]


def rwkv7_scan(r, w, k, v, a_in, b, s0):
    def step(S, xs):
        r_t, w_t, k_t, v_t, a_t, b_t = xs
        sa = jnp.einsum('bhij,bhj->bhi', S, a_t)
        S = S * w_t[:, :, None, :] + sa[..., None] * b_t[:, :, None, :] + v_t[..., None] * k_t[:, :, None, :]
        return S, jnp.einsum('bhij,bhj->bhi', S, r_t)

    xs = tuple(jnp.moveaxis(t.astype(F32), 1, 0) for t in (r, w, k, v, a_in, b))
    s_final, ys = lax.scan(step, s0.astype(F32), xs)
    return jnp.moveaxis(ys, 0, 1), s_final


def hybrid_layer(x, past_k, past_v, s0, shift0, conv0, rel_bias, lam_init,
                 ln1_g, w_in, q_norm_g, k_norm_g, lam_q1, lam_k1, lam_q2, lam_k2, subln_g,
                 mu_shift, w0, w2, a0, a2, g2, k_k, k_a, r_k, lnx_w, lnx_b, w_out,
                 ln2_g, w_up, conv_w, conv_b, w_down):
    B, T, _ = x.shape
    h = rms_norm(x, ln1_g)
    proj = h @ w_in
    q, k, v, pr = jnp.split(proj, [DA_WIDTH, 2 * DA_WIDTH, 3 * DA_WIDTH], axis=-1)

    q = rms_norm(q.reshape(B, T, DA_HEADS, 2, HEAD_DIM), q_norm_g)
    k = rms_norm(k.reshape(B, T, DA_HEADS, 2, HEAD_DIM), k_norm_g)
    v = v.reshape(B, T, DA_HEADS, 2 * HEAD_DIM)
    lam = (jnp.exp(jnp.sum(lam_q1.astype(F32) * lam_k1.astype(F32)))
           - jnp.exp(jnp.sum(lam_q2.astype(F32) * lam_k2.astype(F32))) + lam_init)
    if past_k is None:
        o = prompt_diff_attention(q, k, v, rel_bias, lam)
    else:
        k_all = jnp.concatenate([past_k.astype(k.dtype).reshape(B, -1, DA_HEADS, 2, HEAD_DIM), k], axis=1)
        v_all = jnp.concatenate([past_v.astype(v.dtype), v], axis=1)
        n_keys = k_all.shape[1]
        k_pos, k_chunk = ext_pos_chunk(n_keys)
        o = diff_attention(q, k_all, v_all, k_pos[n_keys - T:], k_chunk[n_keys - T:], k_pos, k_chunk, rel_bias, lam)
    o = rms_norm(o, subln_g) * (1.0 - lam_init)

    prev = jnp.concatenate([shift0[:, None, :].astype(pr.dtype), pr[:, :-1]], axis=1)
    xm = pr + (prev - pr) * mu_shift
    r, kr, vr, wd, ad, gd = jnp.split(xm, RW_SPLITS, axis=-1)
    w_log = -jax.nn.softplus(-(w0 + jnp.tanh(wd) @ w2).astype(F32)) - 0.5
    decay = jnp.exp(-jnp.exp(w_log))
    a = jax.nn.sigmoid(a0 + ad @ a2)
    g = jax.nn.sigmoid(gd) @ g2

    def heads(t):
        return t.reshape(B, T, RW_HEADS, HEAD_DIM)

    kk = heads(kr * k_k).astype(F32)
    kk = kk / jnp.maximum(jnp.sqrt(jnp.sum(kk * kk, axis=-1, keepdims=True)), 1e-12)
    kr = kr * (1.0 + (a - 1.0) * k_a)
    y, s_new = rwkv7_scan(heads(r), heads(decay), heads(kr), heads(vr), -kk, kk * heads(a).astype(F32), s0)
    mean = jnp.mean(y, axis=-1, keepdims=True)
    var = jnp.mean(jnp.square(y - mean), axis=-1, keepdims=True)
    y = ((y - mean) * lax.rsqrt(var + GN_EPS)).reshape(B, T, RW_WIDTH) * lnx_w + lnx_b
    bonus = jnp.sum(heads(r) * heads(kr) * r_k, axis=-1, keepdims=True) * heads(vr)
    o_rw = ((y + bonus.reshape(B, T, RW_WIDTH).astype(F32)) * g).astype(x.dtype)

    mixed = jnp.concatenate([o.reshape(B, T, DA_HEADS * 2 * HEAD_DIM).astype(x.dtype), o_rw], axis=-1) @ w_out
    x = x + mixed

    z = rms_norm(x, ln2_g) @ w_up
    zp = jnp.concatenate([conv0.astype(z.dtype), z], axis=1)
    zc = conv_b
    for j in range(CONV_W):
        zc = zc + zp[:, j:j + T] * conv_w[j]
    gate, up = jnp.split(zc, 2, axis=-1)
    x = x + (jax.nn.silu(gate) * up) @ w_down
    return x, k.reshape(B, T, DA_HEADS, 2 * HEAD_DIM), v, s_new, pr[:, -1], zp[:, T:]


def setup_inputs(seed: int = 0) -> dict:
    key = jax.random.key(seed)
    ks = jax.random.split(key, 40)

    def nrm(i, shape, scale=1.0):
        return jax.random.normal(ks[i], shape, F32) * scale

    def gain(i, shape):
        return 1.0 + nrm(i, shape, 0.02)

    L = DEPTH
    return {
        'x_prompt': nrm(0, (BATCH, SEQ, D_MODEL)),
        'x_sample': nrm(1, (DEC_BATCH, DEC_SEQ, D_MODEL)),
        'cache_k': nrm(2, (L, DEC_BATCH, N_META + PAST_LEN, DA_HEADS, 2 * HEAD_DIM)),
        'cache_v': nrm(3, (L, DEC_BATCH, N_META + PAST_LEN, DA_HEADS, 2 * HEAD_DIM)),
        'state_rwkv': nrm(4, (L, DEC_BATCH, RW_HEADS, HEAD_DIM, HEAD_DIM), 0.5),
        'state_shift': nrm(5, (L, DEC_BATCH, RW_PROJ)),
        'state_conv': nrm(6, (L, DEC_BATCH, CONV_W - 1, 2 * D_FF)),
        'meta_tokens': nrm(7, (N_META, D_MODEL)),
        'rel_bias': nrm(8, (N_BUCKETS, DA_HEADS), 0.5),
        'ln1_g': gain(9, (L, D_MODEL)),
        'w_in': nrm(10, (L, D_MODEL, IN_PROJ), D_MODEL ** -0.5),
        'q_norm_g': gain(11, (L, HEAD_DIM)),
        'k_norm_g': gain(12, (L, HEAD_DIM)),
        'lam_q1': nrm(13, (L, HEAD_DIM), 0.1),
        'lam_k1': nrm(14, (L, HEAD_DIM), 0.1),
        'lam_q2': nrm(15, (L, HEAD_DIM), 0.1),
        'lam_k2': nrm(16, (L, HEAD_DIM), 0.1),
        'subln_g': gain(17, (L, 2 * HEAD_DIM)),
        'mu_shift': jax.random.uniform(ks[18], (L, RW_PROJ), F32),
        'w0': jax.random.uniform(ks[19], (L, RW_WIDTH), F32, -4.0, 1.0),
        'w2': nrm(20, (L, W_LORA, RW_WIDTH), 0.1),
        'a0': nrm(21, (L, RW_WIDTH), 0.1),
        'a2': nrm(22, (L, A_LORA, RW_WIDTH), 0.1),
        'g2': nrm(23, (L, G_LORA, RW_WIDTH), G_LORA ** -0.5),
        'k_k': 0.85 + nrm(24, (L, RW_WIDTH), 0.02),
        'k_a': gain(25, (L, RW_WIDTH)),
        'r_k': nrm(26, (L, RW_HEADS, HEAD_DIM), 0.1),
        'lnx_w': gain(27, (L, RW_WIDTH)),
        'lnx_b': nrm(28, (L, RW_WIDTH), 0.02),
        'w_out': nrm(29, (L, D_MODEL, D_MODEL), D_MODEL ** -0.5),
        'ln2_g': gain(30, (L, D_MODEL)),
        'w_up': nrm(31, (L, D_MODEL, 2 * D_FF), D_MODEL ** -0.5),
        'conv_w': nrm(32, (L, CONV_W, 2 * D_FF), CONV_W ** -0.5),
        'conv_b': nrm(33, (L, 2 * D_FF), 0.02),
        'w_down': nrm(34, (L, D_FF, D_MODEL), D_FF ** -0.5),
    }


def reference(x_prompt, x_sample, cache_k, cache_v, state_rwkv, state_shift, state_conv,
              meta_tokens, rel_bias, ln1_g, w_in, q_norm_g, k_norm_g, lam_q1, lam_k1, lam_q2, lam_k2,
              subln_g, mu_shift, w0, w2, a0, a2, g2, k_k, k_a, r_k, lnx_w, lnx_b, w_out,
              ln2_g, w_up, conv_w, conv_b, w_down):
    Bp = x_prompt.shape[0]
    meta = jnp.broadcast_to(meta_tokens.astype(x_prompt.dtype)[None], (Bp, N_META, D_MODEL))
    xp = jnp.concatenate([meta, x_prompt], axis=1)
    xs = x_sample
    kp, vp, sp, hp, cp = [], [], [], [], []
    kq, vq, sq, hq, cq = [], [], [], [], []
    for l in range(DEPTH):
        lam_init = 0.8 - 0.6 * math.exp(-0.3 * l)
        lw = (ln1_g[l], w_in[l], q_norm_g[l], k_norm_g[l], lam_q1[l], lam_k1[l], lam_q2[l], lam_k2[l],
              subln_g[l], mu_shift[l], w0[l], w2[l], a0[l], a2[l], g2[l], k_k[l], k_a[l], r_k[l],
              lnx_w[l], lnx_b[l], w_out[l], ln2_g[l], w_up[l], conv_w[l], conv_b[l], w_down[l])
        xp, k_n, v_n, s_n, h_n, c_n = hybrid_layer(
            xp, None, None,
            jnp.zeros((Bp, RW_HEADS, HEAD_DIM, HEAD_DIM), F32),
            jnp.zeros((Bp, RW_PROJ), xp.dtype),
            jnp.zeros((Bp, CONV_W - 1, 2 * D_FF), xp.dtype),
            rel_bias, lam_init, *lw)
        kp.append(k_n); vp.append(v_n); sp.append(s_n); hp.append(h_n); cp.append(c_n)
        xs, k_n, v_n, s_n, h_n, c_n = hybrid_layer(
            xs, cache_k[l], cache_v[l], state_rwkv[l], state_shift[l], state_conv[l],
            rel_bias, lam_init, *lw)
        kq.append(k_n); vq.append(v_n); sq.append(s_n); hq.append(h_n); cq.append(c_n)
    return (xp[:, N_META:], xs,
            jnp.stack(kp), jnp.stack(vp), jnp.stack(sp), jnp.stack(hp), jnp.stack(cp),
            jnp.stack(kq), jnp.stack(vq), jnp.stack(sq), jnp.stack(hq), jnp.stack(cq))
```

```python
import functools
import math

import numpy as np
import jax
import jax.numpy as jnp
from jax import lax
from jax.experimental import pallas as pl
from jax.experimental.pallas import tpu as pltpu

F32 = jnp.float32
BF16 = jnp.bfloat16

CHUNK = 64
N_META = 16
HEAD_DIM = 64
DA_HEADS = 4
RW_HEADS = 8
W_LORA = 64
A_LORA = 64
G_LORA = 128
CONV_W = 3
N_BUCKETS = 32
MAX_DISTANCE = 128
NORM_EPS = 1e-6
GN_EPS = 64e-5
NEG_INF = -1e30

DA_WIDTH = DA_HEADS * 2 * HEAD_DIM
RW_WIDTH = RW_HEADS * HEAD_DIM
RW_PROJ = 3 * RW_WIDTH + W_LORA + A_LORA + G_LORA
LANES = 128
PAIR = 2 * HEAD_DIM
N_PAIR = RW_WIDTH // PAIR
RW_CHUNK = 64
VMEM_LIMIT = 56 * 1024 * 1024


def _dot(a, b):
    return jnp.dot(a, b, preferred_element_type=F32)


def _dot_nt(a, b):
    return lax.dot_general(a, b, (((1,), (1,)), ((), ())), preferred_element_type=F32)


def _cparams(sem):
    return pltpu.CompilerParams(dimension_semantics=sem, vmem_limit_bytes=VMEM_LIMIT)


def _const_spec(shape):
    nd = len(shape)
    return pl.BlockSpec(shape, lambda *_: (0,) * nd)


def _resident_spec(shape):
    nd = len(shape)
    return pl.BlockSpec(shape, lambda *_: (0,) * nd, pipeline_mode=pl.Buffered(1))


def _proj_kernel(x_ref, g_ref, w_ref, qg_ref, kg_ref, gm_ref,
                 q_ref, k_ref, v_ref, kb_ref, vb_ref, pr_ref):
    x = x_ref[...]
    ms = jnp.mean(x * x, axis=-1, keepdims=True)
    h = (x * lax.rsqrt(ms + NORM_EPS) * g_ref[...]).astype(BF16)
    gm = gm_ref[...]

    def group_norm(t, g):
        ms_g = _dot((t * t).astype(BF16), gm)
        return t * lax.rsqrt(ms_g + NORM_EPS) * g

    q = _dot(h, w_ref[:, 0:DA_WIDTH])
    q_ref[...] = (group_norm(q, qg_ref[...]) * (HEAD_DIM ** -0.5)).astype(BF16)
    k = group_norm(_dot(h, w_ref[:, DA_WIDTH:2 * DA_WIDTH]), kg_ref[...])
    k_ref[...] = k
    kb_ref[...] = k.astype(BF16)
    v = _dot(h, w_ref[:, 2 * DA_WIDTH:3 * DA_WIDTH])
    v_ref[...] = v
    vb_ref[...] = v.astype(BF16)
    pr_ref[...] = _dot(h, w_ref[:, 3 * DA_WIDTH:])


def _proj(x, ln1_g, w_in_bf, qg, kg, gmean, tm):
    rows, d = x.shape
    n_in = w_in_bf.shape[1]
    assert rows % tm == 0
    row = lambda w: pl.BlockSpec((tm, w), lambda i: (i, 0))
    return pl.pallas_call(
        _proj_kernel,
        grid=(rows // tm,),
        in_specs=[row(d), _const_spec((1, d)), _resident_spec((d, n_in)),
                  _const_spec((1, DA_WIDTH)), _const_spec((1, DA_WIDTH)),
                  _resident_spec((DA_WIDTH, DA_WIDTH))],
        out_specs=[row(DA_WIDTH), row(DA_WIDTH), row(DA_WIDTH), row(DA_WIDTH), row(DA_WIDTH),
                   row(RW_PROJ)],
        out_shape=[jax.ShapeDtypeStruct((rows, DA_WIDTH), BF16),
                   jax.ShapeDtypeStruct((rows, DA_WIDTH), F32),
                   jax.ShapeDtypeStruct((rows, DA_WIDTH), F32),
                   jax.ShapeDtypeStruct((rows, DA_WIDTH), BF16),
                   jax.ShapeDtypeStruct((rows, DA_WIDTH), BF16),
                   jax.ShapeDtypeStruct((rows, RW_PROJ), F32)],
        compiler_params=_cparams(("arbitrary",)),
        name="proj",
    )(x, ln1_g, w_in_bf, qg, kg, gmean)


ATT_TQ = 256


def _stack_components(q):
    lo = lax.broadcasted_iota(jnp.int32, q.shape, 1) < HEAD_DIM
    zero = jnp.zeros_like(q)
    return jnp.concatenate([jnp.where(lo, q, zero), jnp.where(lo, zero, q)], axis=0)


def _sub_layer_norm(acc, l, lam, g, out_scale):
    n = acc.shape[0] // 2
    accn = acc / l
    o = accn[:n] - lam * accn[n:]
    ms = jnp.mean(o * o, axis=-1, keepdims=True)
    return o * lax.rsqrt(ms + NORM_EPS) * g * out_scale


def _attn_prompt_kernel(cfar_ref, lam_ref, q_ref, k_ref, v_ref, km_ref, vm_ref, qm_ref,
                        b0_ref, bm1_ref, bq0_ref, bmm_ref, g_ref, o_ref, om_ref,
                        acc_ref, m_ref, l_ref, *, out_scale):
    h = pl.program_id(0)
    i = pl.program_id(1)
    tq = ATT_TQ
    cf = cfar_ref[h]
    lam = lam_ref[0]
    qst = _stack_components(q_ref[...])

    s = _dot_nt(qst, km_ref[...])
    bias_m = jnp.where(i == 0, bq0_ref[0], cf)
    s = s + jnp.concatenate([bias_m, bias_m], axis=0)
    m0 = jnp.max(s, axis=-1, keepdims=True)
    p = jnp.exp(s - m0)
    m_ref[...] = m0
    l_ref[...] = jnp.sum(p, axis=-1, keepdims=True)
    acc_ref[...] = _dot(p.astype(BF16), vm_ref[...])

    def tile(j, bias):
        start = pl.multiple_of(j * tq, tq)
        kt = k_ref[pl.ds(start, tq), :]
        vt = v_ref[pl.ds(start, tq), :]
        s = _dot_nt(qst, kt)
        m_prev = m_ref[...]
        if bias is None:
            m_new = jnp.maximum(m_prev, jnp.max(s, axis=-1, keepdims=True) + cf)
            p = jnp.exp(s - (m_new - cf))
        else:
            s = s + jnp.concatenate([bias, bias], axis=0)
            m_new = jnp.maximum(m_prev, jnp.max(s, axis=-1, keepdims=True))
            p = jnp.exp(s - m_new)
        alpha = jnp.exp(m_prev - m_new)
        l_ref[...] = alpha * l_ref[...] + jnp.sum(p, axis=-1, keepdims=True)
        acc_ref[...] = alpha * acc_ref[...] + _dot(p.astype(BF16), vt)
        m_ref[...] = m_new

    def far_body(j, carry):
        tile(j, None)
        return carry

    lax.fori_loop(0, jnp.maximum(i - 1, 0), far_body, 0)

    @pl.when(i >= 1)
    def _():
        tile(i - 1, bm1_ref[0])

    tile(i, b0_ref[0])

    o_ref[...] = _sub_layer_norm(acc_ref[...], l_ref[...], lam, g_ref[...], out_scale).astype(BF16)

    @pl.when(i == 0)
    def _():
        qm = _stack_components(qm_ref[...])
        bmm = bmm_ref[0]
        sm = _dot_nt(qm, km_ref[...]) + jnp.concatenate([bmm, bmm], axis=0)
        mm = jnp.max(sm, axis=-1, keepdims=True)
        pm = jnp.exp(sm - mm)
        lm = jnp.sum(pm, axis=-1, keepdims=True)
        accm = _dot(pm.astype(BF16), vm_ref[...])
        om_ref[...] = _sub_layer_norm(accm, lm, lam, g_ref[...], out_scale).astype(BF16)


def _attn_prompt(cfar, lam, q_f, kb_f, vb_f, kb_m, vb_m, q_m, b0, bm1, bq0, bmm, subln_g, out_scale):
    tf = q_f.shape[0]
    tq = ATT_TQ
    assert tf % tq == 0
    smem = pl.BlockSpec(memory_space=pltpu.SMEM)
    head_col = lambda rows: pl.BlockSpec((rows, LANES), lambda h, i: (0, h))
    head_tile = lambda a, b: pl.BlockSpec((1, a, b), lambda h, i: (h, 0, 0))
    return pl.pallas_call(
        functools.partial(_attn_prompt_kernel, out_scale=out_scale),
        grid=(DA_HEADS, tf // tq),
        in_specs=[smem, smem,
                  pl.BlockSpec((tq, LANES), lambda h, i: (i, h)),
                  head_col(tf), head_col(tf), head_col(N_META), head_col(N_META), head_col(N_META),
                  head_tile(tq, tq), head_tile(tq, tq), head_tile(tq, N_META),
                  head_tile(N_META, N_META), _const_spec((1, LANES))],
        out_specs=[pl.BlockSpec((tq, LANES), lambda h, i: (i, h)), head_col(N_META)],
        out_shape=[jax.ShapeDtypeStruct((tf, DA_WIDTH), BF16),
                   jax.ShapeDtypeStruct((N_META, DA_WIDTH), BF16)],
        scratch_shapes=[pltpu.VMEM((2 * tq, LANES), F32), pltpu.VMEM((2 * tq, 1), F32),
                        pltpu.VMEM((2 * tq, 1), F32)],
        compiler_params=_cparams(("arbitrary", "arbitrary")),
        name="attn_prompt",
    )(cfar, lam, q_f, kb_f, vb_f, kb_m, vb_m, q_m, b0, bm1, bq0, bmm, subln_g)


def _attn_decode_kernel(lam_ref, q_ref, k_ref, v_ref, kn_ref, vn_ref, b_ref, g_ref, o_ref,
                        *, n_main, out_scale):
    lam = lam_ref[0]
    qst = q_ref[0, 0]
    n_cache = k_ref.shape[1]
    nq = kn_ref.shape[1]
    k_main = k_ref[0, 0:n_main, :].astype(BF16)
    v_main = v_ref[0, 0:n_main, :].astype(BF16)
    k_tail = jnp.concatenate([k_ref[0, n_main:n_cache, :], kn_ref[0]], axis=0).astype(BF16)
    v_tail = jnp.concatenate([v_ref[0, n_main:n_cache, :], vn_ref[0]], axis=0).astype(BF16)
    s1 = _dot_nt(qst, k_main) + b_ref[0, :, 0:n_main]
    s2 = _dot_nt(qst, k_tail) + b_ref[0, :, n_main:]
    m = jnp.maximum(jnp.max(s1, axis=-1, keepdims=True), jnp.max(s2, axis=-1, keepdims=True))
    p1 = jnp.exp(s1 - m)
    p2 = jnp.exp(s2 - m)
    l = jnp.sum(p1, axis=-1, keepdims=True) + jnp.sum(p2, axis=-1, keepdims=True)
    res = (_dot(p1.astype(BF16), v_main) + _dot(p2.astype(BF16), v_tail)) / l
    g = g_ref[...]
    for h2 in range(2):
        cols = slice(LANES * h2, LANES * (h2 + 1))
        o = res[nq * h2:nq * (h2 + 1), cols] - lam * res[nq * (2 + h2):nq * (3 + h2), cols]
        ms = jnp.mean(o * o, axis=-1, keepdims=True)
        o_ref[0, :, cols] = (o * lax.rsqrt(ms + NORM_EPS) * g * out_scale).astype(BF16)


def _attn_decode(lam, qst, cache_k, cache_v, k_new, v_new, bias, subln_g, out_scale):
    b, n_cache, _ = cache_k.shape
    nq = k_new.shape[1]
    n_main = (n_cache // LANES) * LANES
    n_keys = n_cache + nq
    smem = pl.BlockSpec(memory_space=pltpu.SMEM)
    return pl.pallas_call(
        functools.partial(_attn_decode_kernel, n_main=n_main, out_scale=out_scale),
        grid=(b, 2),
        in_specs=[smem,
                  pl.BlockSpec((1, 1, 4 * nq, 2 * LANES), lambda bi, p: (bi, p, 0, 0)),
                  pl.BlockSpec((1, n_cache, 2 * LANES), lambda bi, p: (bi, 0, p)),
                  pl.BlockSpec((1, n_cache, 2 * LANES), lambda bi, p: (bi, 0, p)),
                  pl.BlockSpec((1, nq, 2 * LANES), lambda bi, p: (bi, 0, p)),
                  pl.BlockSpec((1, nq, 2 * LANES), lambda bi, p: (bi, 0, p)),
                  pl.BlockSpec((1, 4 * nq, n_keys), lambda bi, p: (p, 0, 0)),
                  _const_spec((1, LANES))],
        out_specs=pl.BlockSpec((1, nq, 2 * LANES), lambda bi, p: (bi, 0, p)),
        out_shape=jax.ShapeDtypeStruct((b, nq, DA_WIDTH), BF16),
        compiler_params=_cparams(("arbitrary", "arbitrary")),
        name="attn_decode",
    )(lam, qst, cache_k, cache_v, k_new, v_new, bias, subln_g)


def _sigmoid(x):
    return 1.0 / (1.0 + jnp.exp(-x))


def _softplus(x):
    return jnp.maximum(x, 0.0) + jnp.log(1.0 + jnp.exp(-jnp.abs(x)))


def _split3(x):
    x1 = x.astype(BF16)
    r1 = x - x1.astype(F32)
    x2 = r1.astype(BF16)
    x3 = (r1 - x2.astype(F32)).astype(BF16)
    return x1, x2, x3


def _stack_heads(x):
    lo = lax.broadcasted_iota(jnp.int32, x.shape, 1) < HEAD_DIM
    zero = jnp.zeros_like(x)
    return jnp.concatenate([jnp.where(lo, x, zero), jnp.where(lo, zero, x)], axis=0)


def _rwkv_kernel(pr_ref, sh0_ref, h0_ref, mu_ref, w0_ref, a0_ref, kk_ref, ka_ref, rk_ref,
                 lw_ref, lb_ref, w2_ref, a2_ref, g2_ref, gsum_ref, tri_ref,
                 o_ref, hn_ref,
                 xbuf, carry, at_s, rt_s, bt_s, kt_s, v_s, ep_s, y_s,
                 t_s, tav_s, lrb_s, lrkv_s, zbt_s, zkv_s, ar_s, pc_s,
                 *, tr, t_valid):
    c_len = RW_CHUNK
    n_ch = tr // c_len
    ti = pl.program_id(1)

    @pl.when(ti == 0)
    def _():
        carry[...] = sh0_ref[0]
        hn_ref[...] = h0_ref[...]

    x = pr_ref[0]
    xbuf[8:8 + tr, :] = x
    xbuf[7:8, :] = carry[...]
    prev = xbuf[7:7 + tr, :]
    carry[...] = pr_ref[0, tr - 1:tr, :]
    xm = x + (prev - x) * mu_ref[...]
    r = xm[:, 0:RW_WIDTH]
    kr = xm[:, RW_WIDTH:2 * RW_WIDTH]
    vr = xm[:, 2 * RW_WIDTH:3 * RW_WIDTH]
    wa = xm[:, 3 * RW_WIDTH:3 * RW_WIDTH + W_LORA + A_LORA]
    gd = xm[:, 3 * RW_WIDTH + W_LORA + A_LORA:]
    lane_wa = lax.broadcasted_iota(jnp.int32, wa.shape, 1)
    twa = jnp.where(lane_wa < W_LORA, jnp.tanh(wa), wa).astype(BF16)
    w_log = -_softplus(-(w0_ref[...] + _dot(twa, w2_ref[...]))) - 0.5
    logw = -jnp.exp(w_log)
    a_sig = _sigmoid(a0_ref[...] + _dot(twa, a2_ref[...]))
    g = _dot(_sigmoid(gd).astype(BF16), g2_ref[...])
    gsum = gsum_ref[...]
    kk = kr * kk_ref[...]
    kk_ss = _dot((kk * kk).astype(BF16), gsum)
    kk = kk / jnp.maximum(jnp.sqrt(kk_ss), 1e-12)
    kr2 = kr * (1.0 + (a_sig - 1.0) * ka_ref[...])
    a_vec = -kk
    b_vec = kk * a_sig
    bonus = _dot((r * kr2 * rk_ref[...]).astype(BF16), gsum) * vr
    if t_valid % tr != 0:
        row = lax.broadcasted_iota(jnp.int32, (tr, 1), 0) + ti * tr
        valid = row < t_valid
        logw = jnp.where(valid, logw, 0.0)
        a_vec = jnp.where(valid, a_vec, 0.0)
        b_vec = jnp.where(valid, b_vec, 0.0)
        kr2 = jnp.where(valid, kr2, 0.0)
        vr = jnp.where(valid, vr, 0.0)
        bonus = jnp.where(valid, bonus, 0.0)
    l1, l2, l3 = _split3(logw)
    tri = tri_ref[...]
    cs = _dot(tri, l1) + _dot(tri, l2) + _dot(tri, l3)
    e_pos = jnp.exp(cs)
    e_neg = jnp.exp(-cs)
    at_s[...] = a_vec * jnp.exp(cs - logw)
    rt_s[...] = r * e_pos
    bt_s[...] = b_vec * e_neg
    kt_s[...] = kr2 * e_neg
    v_s[...] = vr
    ep_s[...] = e_pos

    idx_r = lax.broadcasted_iota(jnp.int32, (PAIR, PAIR), 0)
    idx_c = lax.broadcasted_iota(jnp.int32, (PAIR, PAIR), 1)
    same = (idx_r // c_len) == (idx_c // c_len)
    strict = same & ((idx_r % c_len) > (idx_c % c_len))
    incl = same & ((idx_r % c_len) >= (idx_c % c_len))
    eye = idx_r == idx_c
    eye_f = jnp.where(eye, 1.0, 0.0).astype(F32)
    lo128 = idx_c < HEAD_DIM

    def phase1(c, carry_):
        rows = pl.ds(pl.multiple_of(c * c_len, c_len), c_len)
        for p in range(N_PAIR):
            cols = slice(PAIR * p, PAIR * (p + 1))
            at = at_s[rows, cols]
            rt = rt_s[rows, cols]
            yb = _stack_heads(bt_s[rows, cols])
            yk = _stack_heads(kt_s[rows, cols])
            vst = _stack_heads(v_s[rows, cols]).astype(BF16)
            xr = jnp.concatenate([_stack_heads(at), _stack_heads(rt)], axis=0).astype(BF16)
            ybk = jnp.concatenate([yb, yk], axis=0).astype(BF16)
            gmat = _dot_nt(xr, ybk)
            zero = jnp.zeros((PAIR, PAIR), F32)
            aab = jnp.where(strict, gmat[0:PAIR, 0:PAIR], zero)
            aak = jnp.where(strict, gmat[0:PAIR, PAIR:], zero)
            lrb = jnp.where(incl, gmat[PAIR:, 0:PAIR], zero)
            lrk = jnp.where(incl, gmat[PAIR:, PAIR:], zero)
            tinv = eye_f + aab
            lp = aab
            n = 1
            while 2 * n < c_len:
                lpb = lp.astype(BF16)
                lp = _dot(lpb, lpb)
                tinv = tinv + _dot(tinv.astype(BF16), lp.astype(BF16))
                n *= 2
            tb = tinv.astype(BF16)
            av = _dot(aak.astype(BF16), vst)
            pc = ep_s[rows, cols][c_len - 1:c_len]
            t_s[c, p] = tb
            tav_s[c, p] = _dot(tb, av.astype(BF16))
            lrb_s[c, p] = lrb.astype(BF16)
            lrkv_s[c, p] = _dot(lrk.astype(BF16), vst)
            zbt_s[c, p] = jnp.transpose(yb * pc).astype(BF16)
            zkv_s[c, p] = _dot(jnp.transpose(yk * pc).astype(BF16), vst)
            ar_s[c, p] = jnp.concatenate([at, rt], axis=0).astype(BF16)
            pc_s[c, p] = jnp.sum(jnp.where(eye, jnp.broadcast_to(pc, (PAIR, PAIR)), zero),
                                 axis=-1, keepdims=True)
        return carry_

    lax.fori_loop(0, n_ch, phase1, 0)

    def phase2(c, carry_):
        rows = pl.ds(pl.multiple_of(c * c_len, c_len), c_len)
        for p in range(N_PAIR):
            cols = slice(PAIR * p, PAIR * (p + 1))
            hbd = hn_ref[0, p]
            arh = _dot(ar_s[c, p], hbd.astype(BF16))
            ust = _dot(t_s[c, p], _stack_heads(arh[0:c_len]).astype(BF16)) + tav_s[c, p]
            ub = ust.astype(BF16)
            yst = _dot(lrb_s[c, p], ub) + lrkv_s[c, p]
            y_s[rows, cols] = arh[c_len:] + yst[0:c_len] + yst[c_len:]
            hn_ref[0, p] = pc_s[c, p] * hbd + _dot(zbt_s[c, p], ub) + zkv_s[c, p]
        return carry_

    lax.fori_loop(0, n_ch, phase2, 0)

    y = y_s[...]
    inv_n = 1.0 / HEAD_DIM
    mean = _dot(y.astype(BF16), gsum) * inv_n
    d = y - mean
    var = _dot((d * d).astype(BF16), gsum) * inv_n
    yn = d * lax.rsqrt(var + GN_EPS) * lw_ref[...] + lb_ref[...]
    o_ref[0] = ((yn + bonus) * g).astype(BF16)


def _rwkv(pr, shift0, h0, prm, tr, t_valid):
    b, t_pad, _ = pr.shape
    assert t_pad % tr == 0 and tr % RW_CHUNK == 0
    n_ch = tr // RW_CHUNK
    tri = np.zeros((tr, tr), np.float32)
    for c in range(n_ch):
        tri[c * RW_CHUNK:(c + 1) * RW_CHUNK, c * RW_CHUNK:(c + 1) * RW_CHUNK] = np.tril(
            np.ones((RW_CHUNK, RW_CHUNK), np.float32))
    tri = jnp.asarray(tri, BF16)
    vec = _const_spec((1, RW_WIDTH))
    mat = lambda dt: pltpu.VMEM((n_ch, N_PAIR, PAIR, PAIR), dt)
    tile = lambda: pltpu.VMEM((tr, RW_WIDTH), F32)
    return pl.pallas_call(
        functools.partial(_rwkv_kernel, tr=tr, t_valid=t_valid),
        grid=(b, t_pad // tr),
        in_specs=[pl.BlockSpec((1, tr, RW_PROJ), lambda bi, ti: (bi, ti, 0)),
                  pl.BlockSpec((1, 1, RW_PROJ), lambda bi, ti: (bi, 0, 0)),
                  pl.BlockSpec((1, N_PAIR, PAIR, PAIR), lambda bi, ti: (bi, 0, 0, 0)),
                  _const_spec((1, RW_PROJ)), vec, vec, vec, vec, vec, vec, vec,
                  _const_spec((W_LORA + A_LORA, RW_WIDTH)), _const_spec((W_LORA + A_LORA, RW_WIDTH)),
                  _const_spec((G_LORA, RW_WIDTH)), _const_spec((RW_WIDTH, RW_WIDTH)),
                  _const_spec((tr, tr))],
        out_specs=[pl.BlockSpec((1, tr, RW_WIDTH), lambda bi, ti: (bi, ti, 0)),
                   pl.BlockSpec((1, N_PAIR, PAIR, PAIR), lambda bi, ti: (bi, 0, 0, 0))],
        out_shape=[jax.ShapeDtypeStruct((b, t_pad, RW_WIDTH), BF16),
                   jax.ShapeDtypeStruct((b, N_PAIR, PAIR, PAIR), F32)],
        scratch_shapes=[pltpu.VMEM((8 + tr, RW_PROJ), F32), pltpu.VMEM((1, RW_PROJ), F32),
                        tile(), tile(), tile(), tile(), tile(), tile(), tile(),
                        mat(BF16), mat(F32), mat(BF16), mat(F32), mat(BF16), mat(F32), mat(BF16),
                        pltpu.VMEM((n_ch, N_PAIR, PAIR, 1), F32)],
        compiler_params=_cparams(("arbitrary", "arbitrary")),
        name="rwkv",
    )(pr, shift0, h0, prm["mu"], prm["w0"], prm["a0"], prm["k_k"], prm["k_a"], prm["r_k"],
      prm["lnx_w"], prm["lnx_b"], prm["w2p"], prm["a2p"], prm["g2"], prm["gsum"], tri)


FF_COLS = 256


def _ffn_kernel(x_ref, oa_ref, orw_ref, c0_ref, wo_ref, g_ref, wu_ref, cw_ref, cb_ref, wd_ref,
                y_ref, cn_ref, zbuf, cbuf, acc_ref, h_ref, x1_ref, *, tm, ts, d_ff):
    off = zbuf.shape[0] - tm
    da = oa_ref.shape[1]

    @pl.when(pl.program_id(0) == 0)
    def _():
        cbuf[...] = c0_ref[...]

    x1 = x_ref[...] + _dot(oa_ref[...], wo_ref[0:da, :]) + _dot(orw_ref[...], wo_ref[da:, :])
    x1_ref[...] = x1
    ms = jnp.mean(x1 * x1, axis=-1, keepdims=True)
    h_ref[...] = (x1 * lax.rsqrt(ms + NORM_EPS) * g_ref[...]).astype(BF16)
    acc_ref[...] = jnp.zeros_like(acc_ref)

    def conv_cols(c0):
        cols = slice(c0, c0 + FF_COLS)
        z = _dot(h_ref[...], wu_ref[:, cols])
        zbuf[off - 2 * ts:off, :] = cbuf[:, cols]
        zbuf[off:off + tm, :] = z
        cbuf[:, cols] = zbuf[off + tm - 2 * ts:off + tm, :]
        z2 = zbuf[off - 2 * ts:off - 2 * ts + tm, :]
        z1 = zbuf[off - ts:off - ts + tm, :]
        return (cb_ref[:, cols] + z2 * cw_ref[0:1, cols] + z1 * cw_ref[1:2, cols]
                + z * cw_ref[2:3, cols])

    for c in range(d_ff // FF_COLS):
        gate = conv_cols(c * FF_COLS)
        up = conv_cols(d_ff + c * FF_COLS)
        act = (gate * _sigmoid(gate) * up).astype(BF16)
        acc_ref[...] += _dot(act, wd_ref[c * FF_COLS:(c + 1) * FF_COLS, :])

    y_ref[...] = x1_ref[...] + acc_ref[...]
    cn_ref[...] = cbuf[...]


def _ffn(x, oa, orw, conv0, prm, tm, ts):
    rows, d = x.shape
    d_ff = prm["w_down"].shape[0]
    assert rows % tm == 0 and d_ff % FF_COLS == 0 and (ts == 1 or ts % 8 == 0)
    off = -(-2 * ts // 8) * 8
    row = lambda w: pl.BlockSpec((tm, w), lambda i: (i, 0))
    return pl.pallas_call(
        functools.partial(_ffn_kernel, tm=tm, ts=ts, d_ff=d_ff),
        grid=(rows // tm,),
        in_specs=[row(d), row(DA_WIDTH), row(RW_WIDTH), _const_spec((2 * ts, 2 * d_ff)),
                  _resident_spec((DA_WIDTH + RW_WIDTH, d)), _const_spec((1, d)),
                  _resident_spec((d, 2 * d_ff)), _const_spec((CONV_W, 2 * d_ff)),
                  _const_spec((1, 2 * d_ff)), _resident_spec((d_ff, d))],
        out_specs=[row(d), _const_spec((2 * ts, 2 * d_ff))],
        out_shape=[jax.ShapeDtypeStruct((rows, d), F32),
                   jax.ShapeDtypeStruct((2 * ts, 2 * d_ff), F32)],
        scratch_shapes=[pltpu.VMEM((off + tm, FF_COLS), F32), pltpu.VMEM((2 * ts, 2 * d_ff), F32),
                        pltpu.VMEM((tm, d), F32), pltpu.VMEM((tm, d), BF16), pltpu.VMEM((tm, d), F32)],
        compiler_params=_cparams(("arbitrary",)),
        name="ffn",
    )(x, oa, orw, conv0, prm["w_out"], prm["ln2_g"], prm["w_up"], prm["conv_w"], prm["conv_b"],
      prm["w_down"])


def _rel_bucket(rel):
    nb = N_BUCKETS // 2
    max_exact = nb // 2
    bucket = jnp.where(rel > 0, nb, 0)
    n = jnp.abs(rel)
    nf = jnp.maximum(n, 1).astype(F32)
    large = max_exact + (jnp.log(nf / max_exact) / math.log(MAX_DISTANCE / max_exact)
                         * (nb - max_exact)).astype(jnp.int32)
    large = jnp.minimum(large, nb - 1)
    return bucket + jnp.where(n < max_exact, n, large)


def _bias_table(rel_bias, q_pos, k_pos, mask):
    rel = jnp.asarray(k_pos[None, :] - q_pos[:, None], jnp.int32)
    bias = jnp.transpose(rel_bias[_rel_bucket(rel)], (2, 0, 1)).astype(F32)
    return jnp.where(jnp.asarray(mask)[None], bias, NEG_INF)


def _ext_chunk(pos):
    return np.where(pos < N_META, -1, (pos - N_META) // CHUNK)


def _prompt_bias(rel_bias):
    tq = ATT_TQ
    fr = np.arange(tq) + N_META
    meta = np.arange(N_META)
    causal = _ext_chunk(fr)[None, :] <= _ext_chunk(fr)[:, None]
    b0 = _bias_table(rel_bias, fr, fr, causal)
    bm1 = _bias_table(rel_bias, fr + tq, fr, np.ones((tq, tq), bool))
    bq0 = _bias_table(rel_bias, fr, meta, np.ones((tq, N_META), bool))
    bmm = _bias_table(rel_bias, meta, meta, np.ones((N_META, N_META), bool))
    assert tq + 1 >= MAX_DISTANCE
    cfar = rel_bias[_rel_bucket(jnp.asarray(-(tq + 1), jnp.int32))].astype(F32)
    return cfar, b0, bm1, bq0, bmm


def _decode_bias(rel_bias, n_cache, nq):
    n_keys = n_cache + nq
    k_pos = np.arange(n_keys)
    q_pos = k_pos[n_cache:]
    mask = _ext_chunk(k_pos)[None, :] <= _ext_chunk(q_pos)[:, None]
    bias = _bias_table(rel_bias, q_pos, k_pos, mask)
    bias = bias.reshape(2, 2, nq, n_keys)
    bias = jnp.broadcast_to(bias[:, None], (2, 2, 2, nq, n_keys))
    return bias.reshape(2, 4 * nq, n_keys)


def _decode_query_rows(q):
    b, nq, _ = q.shape
    q6 = q.reshape(b, nq, 2, 2, 2, HEAD_DIM)
    eye2 = jnp.eye(2, dtype=q.dtype)
    out = jnp.einsum('bqphcd,hg,ce->bpchqged', q6, eye2, eye2)
    return out.reshape(b, 2, 4 * nq, 4 * HEAD_DIM)


def _state_to_pairs(s):
    b = s.shape[0]
    st = jnp.swapaxes(s, -1, -2).reshape(b, N_PAIR, 2, HEAD_DIM, HEAD_DIM)
    eye2 = jnp.eye(2, dtype=s.dtype)
    return jnp.einsum('bphkv,hg->bphkgv', st, eye2).reshape(b, N_PAIR, PAIR, PAIR)


def _pairs_to_state(hp):
    b = hp.shape[0]
    h6 = hp.reshape(b, N_PAIR, 2, HEAD_DIM, 2, HEAD_DIM)
    diag = jnp.stack([h6[:, :, 0, :, 0, :], h6[:, :, 1, :, 1, :]], axis=2)
    return jnp.swapaxes(diag, -1, -2).reshape(b, RW_HEADS, HEAD_DIM, HEAD_DIM)


def _block_ones(n, blk, dtype):
    idx = np.arange(n) // blk
    return jnp.asarray((idx[:, None] == idx[None, :]).astype(np.float32), dtype)


def kernel(x_prompt, x_sample, cache_k, cache_v, state_rwkv, state_shift, state_conv, meta_tokens,
           rel_bias, ln1_g, w_in, q_norm_g, k_norm_g, lam_q1, lam_k1, lam_q2, lam_k2, subln_g,
           mu_shift, w0, w2, a0, a2, g2, k_k, k_a, r_k, lnx_w, lnx_b, w_out, ln2_g, w_up, conv_w,
           conv_b, w_down):
    bp, seq, d = x_prompt.shape
    db, dt, _ = x_sample.shape
    depth = w_in.shape[0]
    d_ff = w_down.shape[1]
    n_cache = cache_k.shape[2]
    assert bp == 1 and dt == N_META, "the meta stream rides with the decode streams"
    nb = db + 1
    nb_pad = -(-nb // 8) * 8

    cfar, b0, bm1, bq0, bmm = _prompt_bias(rel_bias)
    bias_dec = _decode_bias(rel_bias, n_cache, dt)
    gmean = _block_ones(DA_WIDTH, HEAD_DIM, BF16) * (1.0 / HEAD_DIM)
    gsum = _block_ones(RW_WIDTH, HEAD_DIM, BF16)
    zrow = lambda n: jnp.zeros((n, RW_WIDTH), BF16)

    x_f = x_prompt[0]
    x_s = jnp.concatenate([x_sample, meta_tokens.astype(x_sample.dtype)[None]], axis=0)
    outs = [[] for _ in range(10)]
    for l in range(depth):
        lam_init = 0.8 - 0.6 * math.exp(-0.3 * l)
        lam = (jnp.exp(jnp.sum(lam_q1[l].astype(F32) * lam_k1[l].astype(F32)))
               - jnp.exp(jnp.sum(lam_q2[l].astype(F32) * lam_k2[l].astype(F32))) + lam_init).reshape(1)
        out_scale = 1.0 - lam_init
        tile128 = lambda g_: jnp.tile(g_.reshape(1, -1), (1, DA_WIDTH // g_.shape[-1]))
        qg, kg = tile128(q_norm_g[l]), tile128(k_norm_g[l])
        sg = subln_g[l].reshape(1, LANES)
        w_in_bf = w_in[l].astype(BF16)
        rw = {
            "mu": mu_shift[l].reshape(1, -1), "w0": w0[l].reshape(1, -1), "a0": a0[l].reshape(1, -1),
            "k_k": k_k[l].reshape(1, -1), "k_a": k_a[l].reshape(1, -1), "r_k": r_k[l].reshape(1, -1),
            "lnx_w": lnx_w[l].reshape(1, -1), "lnx_b": lnx_b[l].reshape(1, -1),
            "w2p": jnp.concatenate([w2[l].astype(BF16), zrow(A_LORA)], axis=0),
            "a2p": jnp.concatenate([zrow(W_LORA), a2[l].astype(BF16)], axis=0),
            "g2": g2[l].astype(BF16), "gsum": gsum,
        }
        ff = {
            "w_out": w_out[l].astype(BF16), "ln2_g": ln2_g[l].reshape(1, -1),
            "w_up": w_up[l].astype(BF16), "conv_w": conv_w[l], "conv_b": conv_b[l].reshape(1, -1),
            "w_down": w_down[l].astype(BF16),
        }

        q_f, k_f, v_f, kb_f, vb_f, pr_f = _proj(x_f, ln1_g[l].reshape(1, -1), w_in_bf, qg, kg, gmean, 512)
        q_s, k_s, v_s, kb_s, vb_s, pr_s = _proj(x_s.reshape(nb * dt, d), ln1_g[l].reshape(1, -1),
                                                w_in_bf, qg, kg, gmean, nb * dt)
        m0 = db * dt

        o_f, o_m = _attn_prompt(cfar, lam, q_f, kb_f, vb_f, kb_s[m0:], vb_s[m0:], q_s[m0:],
                                b0, bm1, bq0, bmm, sg, out_scale)
        o_d = _attn_decode(lam, _decode_query_rows(q_s[:m0].reshape(db, dt, DA_WIDTH)),
                           cache_k[l].reshape(db, n_cache, DA_WIDTH),
                           cache_v[l].reshape(db, n_cache, DA_WIDTH),
                           k_s[:m0].reshape(db, dt, DA_WIDTH), v_s[:m0].reshape(db, dt, DA_WIDTH),
                           bias_dec, sg, out_scale)
        o_s = jnp.concatenate([o_d.reshape(m0, DA_WIDTH), o_m], axis=0)

        pr_s3 = pr_s.reshape(nb, dt, RW_PROJ)
        pr_pad = jnp.pad(pr_s3, ((0, 0), (0, RW_CHUNK - dt), (0, 0)))
        shift_s = jnp.concatenate([state_shift[l], jnp.zeros((1, RW_PROJ), F32)], axis=0)[:, None, :]
        h_s = _state_to_pairs(jnp.concatenate(
            [state_rwkv[l], jnp.zeros((1,) + state_rwkv.shape[2:], F32)], axis=0))
        orw_s, hn_s = _rwkv(pr_pad, shift_s, h_s, rw, RW_CHUNK, dt)
        orw_f, hn_f = _rwkv(pr_f[None], pr_s3[db:, dt - 1:dt, :], hn_s[db:], rw, 512, seq)

        def time_major(a):
            a = jnp.pad(a.reshape(nb, dt, -1), ((0, nb_pad - nb), (0, 0), (0, 0)))
            return jnp.swapaxes(a, 0, 1).reshape(dt * nb_pad, -1)

        conv_s = jnp.concatenate([state_conv[l], jnp.zeros((1, CONV_W - 1, 2 * d_ff), F32)], axis=0)
        conv_s = jnp.pad(conv_s, ((0, nb_pad - nb), (0, 0), (0, 0)))
        conv_s = jnp.swapaxes(conv_s, 0, 1).reshape(2 * nb_pad, 2 * d_ff)
        y_s, cn_s = _ffn(time_major(x_s), time_major(o_s), time_major(orw_s[:, :dt]), conv_s, ff,
                         dt * nb_pad, nb_pad)
        cn_s = jnp.swapaxes(cn_s.reshape(2, nb_pad, 2 * d_ff), 0, 1)
        y_f, cn_f = _ffn(x_f, o_f, orw_f[0], cn_s[db], ff, 512, 1)
        y_s = jnp.swapaxes(y_s.reshape(dt, nb_pad, d), 0, 1)[:nb]

        hw = (DA_HEADS, 2 * HEAD_DIM)
        outs[0].append(jnp.concatenate([k_s[m0:], k_f], axis=0).reshape(bp, N_META + seq, *hw))
        outs[1].append(jnp.concatenate([v_s[m0:], v_f], axis=0).reshape(bp, N_META + seq, *hw))
        outs[2].append(_pairs_to_state(hn_f))
        outs[3].append(pr_f[seq - 1:seq])
        outs[4].append(cn_f[None])
        outs[5].append(k_s[:m0].reshape(db, dt, *hw))
        outs[6].append(v_s[:m0].reshape(db, dt, *hw))
        outs[7].append(_pairs_to_state(hn_s[:db]))
        outs[8].append(pr_s3[:db, dt - 1])
        outs[9].append(cn_s[:db])
        x_f, x_s = y_f, y_s

    return (x_f[None], x_s[:db], *[jnp.stack(o) for o in outs])
```

```python
import functools
import math

import numpy as np
import jax
import jax.numpy as jnp
from jax import lax
from jax.experimental import pallas as pl
from jax.experimental.pallas import tpu as pltpu

F32 = jnp.float32
BF16 = jnp.bfloat16

CHUNK = 64
N_META = 16
HEAD_DIM = 64
DA_HEADS = 4
RW_HEADS = 8
W_LORA = 64
A_LORA = 64
G_LORA = 128
CONV_W = 3
N_BUCKETS = 32
MAX_DISTANCE = 128
NORM_EPS = 1e-6
GN_EPS = 64e-5
NEG_INF = -1e30

DA_WIDTH = DA_HEADS * 2 * HEAD_DIM
RW_WIDTH = RW_HEADS * HEAD_DIM
RW_PROJ = 3 * RW_WIDTH + W_LORA + A_LORA + G_LORA
LANES = 128
PAIR = 2 * HEAD_DIM
N_PAIR = RW_WIDTH // PAIR
RW_CHUNK = 64
ATT_T = 256
VMEM_LIMIT = 56 * 1024 * 1024


def _dot(a, b):
    return jnp.dot(a, b, preferred_element_type=F32)


def _dot_nt(a, b):
    return lax.dot_general(a, b, (((1,), (1,)), ((), ())), preferred_element_type=F32)


def _cparams(sem):
    return pltpu.CompilerParams(dimension_semantics=sem, vmem_limit_bytes=VMEM_LIMIT)


def _const_spec(shape):
    nd = len(shape)
    return pl.BlockSpec(shape, lambda *_: (0,) * nd)


def _resident_spec(shape):
    nd = len(shape)
    return pl.BlockSpec(shape, lambda *_: (0,) * nd, pipeline_mode=pl.Buffered(1))


def _proj_kernel(x_ref, g_ref, w_ref, qg_ref, kg_ref, gm_ref,
                 q_ref, k_ref, v_ref, pr_ref, *tile_refs, tm):
    x = x_ref[...]
    ms = jnp.mean(x * x, axis=-1, keepdims=True)
    h = (x * lax.rsqrt(ms + NORM_EPS) * g_ref[...]).astype(BF16)
    gm = gm_ref[...]

    def group_norm(t, g):
        ms_g = _dot((t * t).astype(BF16), gm)
        return t * lax.rsqrt(ms_g + NORM_EPS) * g

    q = _dot(h, w_ref[:, 0:DA_WIDTH])
    q_ref[...] = (group_norm(q, qg_ref[...]) * (HEAD_DIM ** -0.5)).astype(BF16)
    k = group_norm(_dot(h, w_ref[:, DA_WIDTH:2 * DA_WIDTH]), kg_ref[...])
    k_ref[...] = k
    v = _dot(h, w_ref[:, 2 * DA_WIDTH:3 * DA_WIDTH])
    v_ref[...] = v
    pr_ref[...] = _dot(h, w_ref[:, 3 * DA_WIDTH:])
    if tile_refs:
        kb_ref, vt_ref = tile_refs
        kb = k.astype(BF16)
        vt = jnp.transpose(v).astype(BF16)
        for hd in range(DA_HEADS):
            for jj in range(tm // ATT_T):
                kb_ref[hd, jj] = kb[jj * ATT_T:(jj + 1) * ATT_T, hd * LANES:(hd + 1) * LANES]
                vt_ref[hd, jj] = vt[hd * LANES:(hd + 1) * LANES, jj * ATT_T:(jj + 1) * ATT_T]


def _proj(x, ln1_g, w_in_bf, qg, kg, gmean, tm, emit_tiles):
    rows, d = x.shape
    n_in = w_in_bf.shape[1]
    assert rows % tm == 0
    row = lambda w: pl.BlockSpec((tm, w), lambda i: (i, 0))
    out_specs = [row(DA_WIDTH), row(DA_WIDTH), row(DA_WIDTH), row(RW_PROJ)]
    out_shape = [jax.ShapeDtypeStruct((rows, DA_WIDTH), BF16),
                 jax.ShapeDtypeStruct((rows, DA_WIDTH), F32),
                 jax.ShapeDtypeStruct((rows, DA_WIDTH), F32),
                 jax.ShapeDtypeStruct((rows, RW_PROJ), F32)]
    if emit_tiles:
        assert tm % ATT_T == 0
        tpt = tm // ATT_T
        out_specs += [pl.BlockSpec((DA_HEADS, tpt, ATT_T, LANES), lambda i: (0, i, 0, 0)),
                      pl.BlockSpec((DA_HEADS, tpt, LANES, ATT_T), lambda i: (0, i, 0, 0))]
        out_shape += [jax.ShapeDtypeStruct((DA_HEADS, rows // ATT_T, ATT_T, LANES), BF16),
                      jax.ShapeDtypeStruct((DA_HEADS, rows // ATT_T, LANES, ATT_T), BF16)]
    return pl.pallas_call(
        functools.partial(_proj_kernel, tm=tm),
        grid=(rows // tm,),
        in_specs=[row(d), _const_spec((1, d)), _resident_spec((d, n_in)),
                  _const_spec((1, DA_WIDTH)), _const_spec((1, DA_WIDTH)),
                  _resident_spec((DA_WIDTH, DA_WIDTH))],
        out_specs=out_specs,
        out_shape=out_shape,
        compiler_params=_cparams(("arbitrary",)),
        name="proj",
    )(x, ln1_g, w_in_bf, qg, kg, gmean)


def _stack_components(q):
    lo = lax.broadcasted_iota(jnp.int32, q.shape, 1) < HEAD_DIM
    zero = jnp.zeros_like(q)
    return jnp.concatenate([jnp.where(lo, q, zero), jnp.where(lo, zero, q)], axis=0)


def _sub_layer_norm(o, g, out_scale):
    ms = jnp.mean(o * o, axis=-1, keepdims=True)
    return o * lax.rsqrt(ms + NORM_EPS) * g * out_scale


def _attn_prompt_kernel(cfar_ref, lam_ref, q_ref, k_ref, vt_ref, km_ref, vm_ref, vmt_ref, qm_ref,
                        b0_ref, bm1_ref, bq0_ref, bmm_ref, g_ref, o_ref, om_ref,
                        acc_ref, m_ref, l_ref, *, out_scale):
    h = pl.program_id(0)
    i = pl.program_id(1)
    tq = ATT_T
    cf = cfar_ref[h]
    lam = lam_ref[0]
    qst = _stack_components(q_ref[...])

    def both(b):
        return jnp.concatenate([b, b], axis=1)

    s = _dot_nt(km_ref[...], qst)
    s = s + both(jnp.where(i == 0, bq0_ref[0], cf))
    m0 = jnp.max(s, axis=0, keepdims=True)
    p = jnp.exp(s - m0)
    m_ref[...] = m0
    l_ref[...] = jnp.sum(p, axis=0, keepdims=True)
    acc_ref[...] = _dot(vmt_ref[...], p.astype(BF16))

    def tile(j, bias):
        s = _dot_nt(k_ref[0, j], qst)
        m_prev = m_ref[...]
        if bias is None:
            m_new = jnp.maximum(m_prev, jnp.max(s, axis=0, keepdims=True) + cf)
            p = jnp.exp(s - (m_new - cf))
        else:
            s = s + both(bias)
            m_new = jnp.maximum(m_prev, jnp.max(s, axis=0, keepdims=True))
            p = jnp.exp(s - m_new)
        alpha = jnp.exp(m_prev - m_new)
        l_ref[...] = alpha * l_ref[...] + jnp.sum(p, axis=0, keepdims=True)
        acc_ref[...] = alpha * acc_ref[...] + _dot(vt_ref[0, j], p.astype(BF16))
        m_ref[...] = m_new

    def far_body(j, carry):
        tile(j, None)
        return carry

    lax.fori_loop(0, jnp.maximum(i - 1, 0), far_body, 0)

    @pl.when(i >= 1)
    def _():
        tile(i - 1, bm1_ref[0])

    tile(i, b0_ref[0])

    accn = acc_ref[...] / l_ref[...]
    o_t = accn[:, 0:tq] - lam * accn[:, tq:]
    o_ref[...] = _sub_layer_norm(jnp.transpose(o_t), g_ref[...], out_scale).astype(BF16)

    @pl.when(i == 0)
    def _():
        qm = _stack_components(qm_ref[...])
        bmm = bmm_ref[0]
        sm = _dot_nt(qm, km_ref[...]) + jnp.concatenate([bmm, bmm], axis=0)
        mm = jnp.max(sm, axis=-1, keepdims=True)
        pm = jnp.exp(sm - mm)
        accm = _dot(pm.astype(BF16), vm_ref[...]) / jnp.sum(pm, axis=-1, keepdims=True)
        om = accm[0:N_META] - lam * accm[N_META:]
        om_ref[...] = _sub_layer_norm(om, g_ref[...], out_scale).astype(BF16)


def _attn_prompt(cfar, lam, q_f, kb_t, vt_t, kb_m, vb_m, vbt_m, q_m, b0, bm1, bq0, bmm, subln_g,
                 out_scale):
    tf = q_f.shape[0]
    tq = ATT_T
    assert tf % tq == 0
    n_t = tf // tq
    smem = pl.BlockSpec(memory_space=pltpu.SMEM)
    head_col = lambda rows: pl.BlockSpec((rows, LANES), lambda h, i: (0, h))
    head_tile = lambda a, b: pl.BlockSpec((1, a, b), lambda h, i: (h, 0, 0))
    return pl.pallas_call(
        functools.partial(_attn_prompt_kernel, out_scale=out_scale),
        grid=(DA_HEADS, n_t),
        in_specs=[smem, smem,
                  pl.BlockSpec((tq, LANES), lambda h, i: (i, h)),
                  pl.BlockSpec((1, n_t, tq, LANES), lambda h, i: (h, 0, 0, 0)),
                  pl.BlockSpec((1, n_t, LANES, tq), lambda h, i: (h, 0, 0, 0)),
                  head_col(N_META), head_col(N_META),
                  pl.BlockSpec((LANES, N_META), lambda h, i: (h, 0)),
                  head_col(N_META),
                  head_tile(tq, tq), head_tile(tq, tq), head_tile(N_META, tq),
                  head_tile(N_META, N_META), _const_spec((1, LANES))],
        out_specs=[pl.BlockSpec((tq, LANES), lambda h, i: (i, h)), head_col(N_META)],
        out_shape=[jax.ShapeDtypeStruct((tf, DA_WIDTH), BF16),
                   jax.ShapeDtypeStruct((N_META, DA_WIDTH), BF16)],
        scratch_shapes=[pltpu.VMEM((LANES, 2 * tq), F32), pltpu.VMEM((1, 2 * tq), F32),
                        pltpu.VMEM((1, 2 * tq), F32)],
        compiler_params=_cparams(("arbitrary", "arbitrary")),
        name="attn_prompt",
    )(cfar, lam, q_f, kb_t, vt_t, kb_m, vb_m, vbt_m, q_m, b0, bm1, bq0, bmm, subln_g)


def _attn_decode_kernel(lam_ref, q_ref, k_ref, v_ref, kn_ref, vn_ref, b_ref, g_ref, o_ref,
                        *, n_main, out_scale):
    lam = lam_ref[0]
    nq = q_ref.shape[0]
    n_cache = k_ref.shape[0]
    qst = _stack_components(q_ref[...])
    k_main = k_ref[0:n_main, :].astype(BF16)
    v_main = v_ref[0:n_main, :].astype(BF16)
    k_tail = jnp.concatenate([k_ref[n_main:n_cache, :], kn_ref[...]], axis=0).astype(BF16)
    v_tail = jnp.concatenate([v_ref[n_main:n_cache, :], vn_ref[...]], axis=0).astype(BF16)
    bias = jnp.concatenate([b_ref[0], b_ref[0]], axis=0)
    s1 = _dot_nt(qst, k_main) + bias[:, 0:n_main]
    s2 = _dot_nt(qst, k_tail) + bias[:, n_main:]
    m = jnp.maximum(jnp.max(s1, axis=-1, keepdims=True), jnp.max(s2, axis=-1, keepdims=True))
    p1 = jnp.exp(s1 - m)
    p2 = jnp.exp(s2 - m)
    l = jnp.sum(p1, axis=-1, keepdims=True) + jnp.sum(p2, axis=-1, keepdims=True)
    res = (_dot(p1.astype(BF16), v_main) + _dot(p2.astype(BF16), v_tail)) / l
    o = res[0:nq] - lam * res[nq:]
    o_ref[...] = _sub_layer_norm(o, g_ref[...], out_scale).astype(BF16)


def _attn_decode(lam, q_s, cache_k, cache_v, k_s, v_s, bias, subln_g, out_scale, nq):
    b, n_cache, width = cache_k.shape
    n_h = width // LANES
    n_main = (n_cache // LANES) * LANES
    n_keys = n_cache + nq
    smem = pl.BlockSpec(memory_space=pltpu.SMEM)
    cache = pl.BlockSpec((None, n_cache, LANES), lambda bi, h: (bi, 0, h))
    new = pl.BlockSpec((nq, LANES), lambda bi, h: (bi, h))
    return pl.pallas_call(
        functools.partial(_attn_decode_kernel, n_main=n_main, out_scale=out_scale),
        grid=(b, n_h),
        in_specs=[smem, new, cache, cache, new, new,
                  pl.BlockSpec((1, nq, n_keys), lambda bi, h: (h, 0, 0)), _const_spec((1, LANES))],
        out_specs=new,
        out_shape=jax.ShapeDtypeStruct((b * nq, DA_WIDTH), BF16),
        compiler_params=_cparams(("arbitrary", "arbitrary")),
        name="attn_decode",
    )(lam, q_s, cache_k, cache_v, k_s, v_s, bias, subln_g)


def _sigmoid(x):
    return 1.0 / (1.0 + jnp.exp(-x))


def _softplus(x):
    return jnp.maximum(x, 0.0) + jnp.log(1.0 + jnp.exp(-jnp.abs(x)))


def _split3(x):
    x1 = x.astype(BF16)
    r1 = x - x1.astype(F32)
    x2 = r1.astype(BF16)
    x3 = (r1 - x2.astype(F32)).astype(BF16)
    return x1, x2, x3


def _stack_heads(x):
    lo = lax.broadcasted_iota(jnp.int32, x.shape, 1) < HEAD_DIM
    zero = jnp.zeros_like(x)
    return jnp.concatenate([jnp.where(lo, x, zero), jnp.where(lo, zero, x)], axis=0)


def _rwkv_kernel(pr_ref, sh0_ref, h0_ref, mu_ref, w0_ref, a0_ref, kk_ref, ka_ref, rk_ref,
                 lw_ref, lb_ref, w2_ref, a2_ref, g2_ref, gsum_ref, tri_ref,
                 o_ref, hn_ref,
                 xbuf, carry, at_s, rt_s, bt_s, kt_s, v_s, ep_s, y_s,
                 t_s, tav_s, lrb_s, lrkv_s, zbt_s, zkv_s, ar_s, pc_s,
                 *, tr, t_valid):
    c_len = RW_CHUNK
    n_ch = tr // c_len
    ti = pl.program_id(1)

    @pl.when(ti == 0)
    def _():
        carry[...] = sh0_ref[0]
        hn_ref[...] = h0_ref[...]

    x = pr_ref[0]
    xbuf[8:8 + tr, :] = x
    xbuf[7:8, :] = carry[...]
    prev = xbuf[7:7 + tr, :]
    carry[...] = pr_ref[0, tr - 1:tr, :]
    xm = x + (prev - x) * mu_ref[...]
    r = xm[:, 0:RW_WIDTH]
    kr = xm[:, RW_WIDTH:2 * RW_WIDTH]
    vr = xm[:, 2 * RW_WIDTH:3 * RW_WIDTH]
    wa = xm[:, 3 * RW_WIDTH:3 * RW_WIDTH + W_LORA + A_LORA]
    gd = xm[:, 3 * RW_WIDTH + W_LORA + A_LORA:]
    lane_wa = lax.broadcasted_iota(jnp.int32, wa.shape, 1)
    twa = jnp.where(lane_wa < W_LORA, jnp.tanh(wa), wa).astype(BF16)
    w_log = -_softplus(-(w0_ref[...] + _dot(twa, w2_ref[...]))) - 0.5
    logw = -jnp.exp(w_log)
    a_sig = _sigmoid(a0_ref[...] + _dot(twa, a2_ref[...]))
    g = _dot(_sigmoid(gd).astype(BF16), g2_ref[...])
    gsum = gsum_ref[...]
    kk = kr * kk_ref[...]
    kk_ss = _dot((kk * kk).astype(BF16), gsum)
    kk = kk / jnp.maximum(jnp.sqrt(kk_ss), 1e-12)
    kr2 = kr * (1.0 + (a_sig - 1.0) * ka_ref[...])
    a_vec = -kk
    b_vec = kk * a_sig
    bonus = _dot((r * kr2 * rk_ref[...]).astype(BF16), gsum) * vr
    if t_valid % tr != 0:
        row = lax.broadcasted_iota(jnp.int32, (tr, 1), 0) + ti * tr
        valid = row < t_valid
        logw = jnp.where(valid, logw, 0.0)
        a_vec = jnp.where(valid, a_vec, 0.0)
        b_vec = jnp.where(valid, b_vec, 0.0)
        kr2 = jnp.where(valid, kr2, 0.0)
        vr = jnp.where(valid, vr, 0.0)
        bonus = jnp.where(valid, bonus, 0.0)
    l1, l2, l3 = _split3(logw)
    tri = tri_ref[...]
    cs = _dot(tri, l1) + _dot(tri, l2) + _dot(tri, l3)
    e_pos = jnp.exp(cs)
    e_neg = jnp.exp(-cs)
    at_s[...] = a_vec * jnp.exp(cs - logw)
    rt_s[...] = r * e_pos
    bt_s[...] = b_vec * e_neg
    kt_s[...] = kr2 * e_neg
    v_s[...] = vr
    ep_s[...] = e_pos

    idx_r = lax.broadcasted_iota(jnp.int32, (PAIR, PAIR), 0)
    idx_c = lax.broadcasted_iota(jnp.int32, (PAIR, PAIR), 1)
    same = (idx_r // c_len) == (idx_c // c_len)
    strict = same & ((idx_r % c_len) > (idx_c % c_len))
    incl = same & ((idx_r % c_len) >= (idx_c % c_len))
    eye = idx_r == idx_c
    eye_f = jnp.where(eye, 1.0, 0.0).astype(F32)

    def phase1(c, carry_):
        rows = pl.ds(pl.multiple_of(c * c_len, c_len), c_len)
        for p in range(N_PAIR):
            cols = slice(PAIR * p, PAIR * (p + 1))
            at = at_s[rows, cols]
            rt = rt_s[rows, cols]
            yb = _stack_heads(bt_s[rows, cols])
            yk = _stack_heads(kt_s[rows, cols])
            vst = _stack_heads(v_s[rows, cols]).astype(BF16)
            xr = jnp.concatenate([_stack_heads(at), _stack_heads(rt)], axis=0).astype(BF16)
            ybk = jnp.concatenate([yb, yk], axis=0).astype(BF16)
            gmat = _dot_nt(xr, ybk)
            zero = jnp.zeros((PAIR, PAIR), F32)
            aab = jnp.where(strict, gmat[0:PAIR, 0:PAIR], zero)
            aak = jnp.where(strict, gmat[0:PAIR, PAIR:], zero)
            lrb = jnp.where(incl, gmat[PAIR:, 0:PAIR], zero)
            lrk = jnp.where(incl, gmat[PAIR:, PAIR:], zero)
            tinv = eye_f + aab
            lp = aab
            n = 1
            while 2 * n < c_len:
                lpb = lp.astype(BF16)
                lp = _dot(lpb, lpb)
                tinv = tinv + _dot(tinv.astype(BF16), lp.astype(BF16))
                n *= 2
            tb = tinv.astype(BF16)
            av = _dot(aak.astype(BF16), vst)
            pc = ep_s[rows, cols][c_len - 1:c_len]
            t_s[c, p] = tb
            tav_s[c, p] = _dot(tb, av.astype(BF16))
            lrb_s[c, p] = lrb.astype(BF16)
            lrkv_s[c, p] = _dot(lrk.astype(BF16), vst)
            zbt_s[c, p] = jnp.transpose(yb * pc).astype(BF16)
            zkv_s[c, p] = _dot(jnp.transpose(yk * pc).astype(BF16), vst)
            ar_s[c, p] = jnp.concatenate([at, rt], axis=0).astype(BF16)
            pc_s[c, p] = jnp.sum(jnp.where(eye, jnp.broadcast_to(pc, (PAIR, PAIR)), zero),
                                 axis=-1, keepdims=True)
        return carry_

    lax.fori_loop(0, n_ch, phase1, 0)

    def phase2(c, carry_):
        rows = pl.ds(pl.multiple_of(c * c_len, c_len), c_len)
        for p in range(N_PAIR):
            cols = slice(PAIR * p, PAIR * (p + 1))
            hbd = hn_ref[0, p]
            arh = _dot(ar_s[c, p], hbd.astype(BF16))
            ust = _dot(t_s[c, p], _stack_heads(arh[0:c_len]).astype(BF16)) + tav_s[c, p]
            ub = ust.astype(BF16)
            yst = _dot(lrb_s[c, p], ub) + lrkv_s[c, p]
            y_s[rows, cols] = arh[c_len:] + yst[0:c_len] + yst[c_len:]
            hn_ref[0, p] = pc_s[c, p] * hbd + _dot(zbt_s[c, p], ub) + zkv_s[c, p]
        return carry_

    lax.fori_loop(0, n_ch, phase2, 0)

    y = y_s[...]
    inv_n = 1.0 / HEAD_DIM
    mean = _dot(y.astype(BF16), gsum) * inv_n
    d = y - mean
    var = _dot((d * d).astype(BF16), gsum) * inv_n
    yn = d * lax.rsqrt(var + GN_EPS) * lw_ref[...] + lb_ref[...]
    o_ref[0] = ((yn + bonus) * g).astype(BF16)


def _rwkv(pr, shift0, h0, prm, tr, t_valid):
    b, t_pad, _ = pr.shape
    assert t_pad % tr == 0 and tr % RW_CHUNK == 0
    n_ch = tr // RW_CHUNK
    tri = np.zeros((tr, tr), np.float32)
    for c in range(n_ch):
        tri[c * RW_CHUNK:(c + 1) * RW_CHUNK, c * RW_CHUNK:(c + 1) * RW_CHUNK] = np.tril(
            np.ones((RW_CHUNK, RW_CHUNK), np.float32))
    tri = jnp.asarray(tri, BF16)
    vec = _const_spec((1, RW_WIDTH))
    mat = lambda dt: pltpu.VMEM((n_ch, N_PAIR, PAIR, PAIR), dt)
    tile = lambda: pltpu.VMEM((tr, RW_WIDTH), F32)
    return pl.pallas_call(
        functools.partial(_rwkv_kernel, tr=tr, t_valid=t_valid),
        grid=(b, t_pad // tr),
        in_specs=[pl.BlockSpec((1, tr, RW_PROJ), lambda bi, ti: (bi, ti, 0)),
                  pl.BlockSpec((1, 1, RW_PROJ), lambda bi, ti: (bi, 0, 0)),
                  pl.BlockSpec((1, N_PAIR, PAIR, PAIR), lambda bi, ti: (bi, 0, 0, 0)),
                  _const_spec((1, RW_PROJ)), vec, vec, vec, vec, vec, vec, vec,
                  _const_spec((W_LORA + A_LORA, RW_WIDTH)), _const_spec((W_LORA + A_LORA, RW_WIDTH)),
                  _const_spec((G_LORA, RW_WIDTH)), _const_spec((RW_WIDTH, RW_WIDTH)),
                  _const_spec((tr, tr))],
        out_specs=[pl.BlockSpec((1, tr, RW_WIDTH), lambda bi, ti: (bi, ti, 0)),
                   pl.BlockSpec((1, N_PAIR, PAIR, PAIR), lambda bi, ti: (bi, 0, 0, 0))],
        out_shape=[jax.ShapeDtypeStruct((b, t_pad, RW_WIDTH), BF16),
                   jax.ShapeDtypeStruct((b, N_PAIR, PAIR, PAIR), F32)],
        scratch_shapes=[pltpu.VMEM((8 + tr, RW_PROJ), F32), pltpu.VMEM((1, RW_PROJ), F32),
                        tile(), tile(), tile(), tile(), tile(), tile(), tile(),
                        mat(BF16), mat(F32), mat(BF16), mat(F32), mat(BF16), mat(F32), mat(BF16),
                        pltpu.VMEM((n_ch, N_PAIR, PAIR, 1), F32)],
        compiler_params=_cparams(("arbitrary", "arbitrary")),
        name="rwkv",
    )(pr, shift0, h0, prm["mu"], prm["w0"], prm["a0"], prm["k_k"], prm["k_a"], prm["r_k"],
      prm["lnx_w"], prm["lnx_b"], prm["w2p"], prm["a2p"], prm["g2"], prm["gsum"], tri)


FF_COLS = 256


def _ffn_kernel(x_ref, oa_ref, orw_ref, c0_ref, wo_ref, g_ref, wu_ref, cw_ref, cb_ref, wd_ref,
                y_ref, cn_ref, zbuf, cbuf, acc_ref, h_ref, x1_ref, *, tm, ts, d_ff):
    off = zbuf.shape[0] - tm
    da = oa_ref.shape[1]

    @pl.when(pl.program_id(0) == 0)
    def _():
        cbuf[...] = c0_ref[...]

    x1 = x_ref[...] + _dot(oa_ref[...], wo_ref[0:da, :]) + _dot(orw_ref[...], wo_ref[da:, :])
    x1_ref[...] = x1
    ms = jnp.mean(x1 * x1, axis=-1, keepdims=True)
    h_ref[...] = (x1 * lax.rsqrt(ms + NORM_EPS) * g_ref[...]).astype(BF16)
    acc_ref[...] = jnp.zeros_like(acc_ref)

    def conv_cols(c0):
        cols = slice(c0, c0 + FF_COLS)
        z = _dot(h_ref[...], wu_ref[:, cols])
        zbuf[off - 2 * ts:off, :] = cbuf[:, cols]
        zbuf[off:off + tm, :] = z
        cbuf[:, cols] = zbuf[off + tm - 2 * ts:off + tm, :]
        z2 = zbuf[off - 2 * ts:off - 2 * ts + tm, :]
        z1 = zbuf[off - ts:off - ts + tm, :]
        return (cb_ref[:, cols] + z2 * cw_ref[0:1, cols] + z1 * cw_ref[1:2, cols]
                + z * cw_ref[2:3, cols])

    for c in range(d_ff // FF_COLS):
        gate = conv_cols(c * FF_COLS)
        up = conv_cols(d_ff + c * FF_COLS)
        act = (gate * _sigmoid(gate) * up).astype(BF16)
        acc_ref[...] += _dot(act, wd_ref[c * FF_COLS:(c + 1) * FF_COLS, :])

    y_ref[...] = x1_ref[...] + acc_ref[...]
    cn_ref[...] = cbuf[...]


def _ffn(x, oa, orw, conv0, prm, tm, ts):
    rows, d = x.shape
    d_ff = prm["w_down"].shape[0]
    assert rows % tm == 0 and d_ff % FF_COLS == 0 and (ts == 1 or ts % 8 == 0)
    off = -(-2 * ts // 8) * 8
    row = lambda w: pl.BlockSpec((tm, w), lambda i: (i, 0))
    return pl.pallas_call(
        functools.partial(_ffn_kernel, tm=tm, ts=ts, d_ff=d_ff),
        grid=(rows // tm,),
        in_specs=[row(d), row(DA_WIDTH), row(RW_WIDTH), _const_spec((2 * ts, 2 * d_ff)),
                  _resident_spec((DA_WIDTH + RW_WIDTH, d)), _const_spec((1, d)),
                  _resident_spec((d, 2 * d_ff)), _const_spec((CONV_W, 2 * d_ff)),
                  _const_spec((1, 2 * d_ff)), _resident_spec((d_ff, d))],
        out_specs=[row(d), _const_spec((2 * ts, 2 * d_ff))],
        out_shape=[jax.ShapeDtypeStruct((rows, d), F32),
                   jax.ShapeDtypeStruct((2 * ts, 2 * d_ff), F32)],
        scratch_shapes=[pltpu.VMEM((off + tm, FF_COLS), F32), pltpu.VMEM((2 * ts, 2 * d_ff), F32),
                        pltpu.VMEM((tm, d), F32), pltpu.VMEM((tm, d), BF16), pltpu.VMEM((tm, d), F32)],
        compiler_params=_cparams(("arbitrary",)),
        name="ffn",
    )(x, oa, orw, conv0, prm["w_out"], prm["ln2_g"], prm["w_up"], prm["conv_w"], prm["conv_b"],
      prm["w_down"])


def _rel_bucket(rel):
    nb = N_BUCKETS // 2
    max_exact = nb // 2
    bucket = jnp.where(rel > 0, nb, 0)
    n = jnp.abs(rel)
    nf = jnp.maximum(n, 1).astype(F32)
    large = max_exact + (jnp.log(nf / max_exact) / math.log(MAX_DISTANCE / max_exact)
                         * (nb - max_exact)).astype(jnp.int32)
    large = jnp.minimum(large, nb - 1)
    return bucket + jnp.where(n < max_exact, n, large)


def _bias_table(rel_bias, q_pos, k_pos, mask):
    rel = jnp.asarray(k_pos[None, :] - q_pos[:, None], jnp.int32)
    onehot = _rel_bucket(rel)[None, :, :, None] == jnp.arange(N_BUCKETS, dtype=jnp.int32)
    table = jnp.transpose(rel_bias).astype(F32)[:, None, None, :]
    bias = jnp.sum(jnp.where(onehot, table, 0.0), axis=-1)
    return jnp.where(jnp.asarray(mask)[None], bias, NEG_INF)


def _ext_chunk(pos):
    return np.where(pos < N_META, -1, (pos - N_META) // CHUNK)


def _prompt_bias(rel_bias):
    tq = ATT_T
    fr = np.arange(tq) + N_META
    meta = np.arange(N_META)
    causal = _ext_chunk(fr)[None, :] <= _ext_chunk(fr)[:, None]
    tr = lambda b: jnp.swapaxes(b, 1, 2)
    b0 = tr(_bias_table(rel_bias, fr, fr, causal))
    bm1 = tr(_bias_table(rel_bias, fr + tq, fr, np.ones((tq, tq), bool)))
    bq0 = tr(_bias_table(rel_bias, fr, meta, np.ones((tq, N_META), bool)))
    bmm = _bias_table(rel_bias, meta, meta, np.ones((N_META, N_META), bool))
    assert tq + 1 >= MAX_DISTANCE
    cfar = rel_bias[_rel_bucket(jnp.asarray(-(tq + 1), jnp.int32))].astype(F32)
    return cfar, b0, bm1, bq0, bmm


def _decode_bias(rel_bias, n_cache, nq):
    k_pos = np.arange(n_cache + nq)
    q_pos = k_pos[n_cache:]
    mask = _ext_chunk(k_pos)[None, :] <= _ext_chunk(q_pos)[:, None]
    return _bias_table(rel_bias, q_pos, k_pos, mask)


def _state_to_pairs(s):
    b = s.shape[0]
    st = jnp.swapaxes(s, -1, -2).reshape(b, N_PAIR, 2, HEAD_DIM, HEAD_DIM)
    eye2 = jnp.eye(2, dtype=s.dtype)
    return jnp.einsum('bphkv,hg->bphkgv', st, eye2).reshape(b, N_PAIR, PAIR, PAIR)


def _pairs_to_state(hp):
    b = hp.shape[0]
    h6 = hp.reshape(b, N_PAIR, 2, HEAD_DIM, 2, HEAD_DIM)
    diag = jnp.stack([h6[:, :, 0, :, 0, :], h6[:, :, 1, :, 1, :]], axis=2)
    return jnp.swapaxes(diag, -1, -2).reshape(b, RW_HEADS, HEAD_DIM, HEAD_DIM)


def _block_ones(n, blk, dtype):
    idx = np.arange(n) // blk
    return jnp.asarray((idx[:, None] == idx[None, :]).astype(np.float32), dtype)


def kernel(x_prompt, x_sample, cache_k, cache_v, state_rwkv, state_shift, state_conv, meta_tokens,
           rel_bias, ln1_g, w_in, q_norm_g, k_norm_g, lam_q1, lam_k1, lam_q2, lam_k2, subln_g,
           mu_shift, w0, w2, a0, a2, g2, k_k, k_a, r_k, lnx_w, lnx_b, w_out, ln2_g, w_up, conv_w,
           conv_b, w_down):
    bp, seq, d = x_prompt.shape
    db, dt, _ = x_sample.shape
    depth = w_in.shape[0]
    d_ff = w_down.shape[1]
    n_cache = cache_k.shape[2]
    assert bp == 1 and dt == N_META, "the meta stream rides with the decode streams"
    assert cache_k.shape[3] == DA_HEADS and cache_k.shape[4] == 2 * HEAD_DIM
    nb = db + 1
    nb_pad = -(-nb // 8) * 8

    cfar, b0, bm1, bq0, bmm = _prompt_bias(rel_bias)
    bias_dec = _decode_bias(rel_bias, n_cache, dt)
    gmean = _block_ones(DA_WIDTH, HEAD_DIM, BF16) * (1.0 / HEAD_DIM)
    gsum = _block_ones(RW_WIDTH, HEAD_DIM, BF16)
    zrow = lambda n: jnp.zeros((n, RW_WIDTH), BF16)

    x_f = x_prompt[0]
    x_s = jnp.concatenate([x_sample, meta_tokens.astype(x_sample.dtype)[None]], axis=0)
    outs = [[] for _ in range(10)]
    for l in range(depth):
        lam_init = 0.8 - 0.6 * math.exp(-0.3 * l)
        lam = (jnp.exp(jnp.sum(lam_q1[l].astype(F32) * lam_k1[l].astype(F32)))
               - jnp.exp(jnp.sum(lam_q2[l].astype(F32) * lam_k2[l].astype(F32))) + lam_init).reshape(1)
        out_scale = 1.0 - lam_init
        tile128 = lambda g_: jnp.tile(g_.reshape(1, -1), (1, DA_WIDTH // g_.shape[-1]))
        qg, kg = tile128(q_norm_g[l]), tile128(k_norm_g[l])
        sg = subln_g[l].reshape(1, LANES)
        w_in_bf = w_in[l].astype(BF16)
        rw = {
            "mu": mu_shift[l].reshape(1, -1), "w0": w0[l].reshape(1, -1), "a0": a0[l].reshape(1, -1),
            "k_k": k_k[l].reshape(1, -1), "k_a": k_a[l].reshape(1, -1), "r_k": r_k[l].reshape(1, -1),
            "lnx_w": lnx_w[l].reshape(1, -1), "lnx_b": lnx_b[l].reshape(1, -1),
            "w2p": jnp.concatenate([w2[l].astype(BF16), zrow(A_LORA)], axis=0),
            "a2p": jnp.concatenate([zrow(W_LORA), a2[l].astype(BF16)], axis=0),
            "g2": g2[l].astype(BF16), "gsum": gsum,
        }
        ff = {
            "w_out": w_out[l].astype(BF16), "ln2_g": ln2_g[l].reshape(1, -1),
            "w_up": w_up[l].astype(BF16), "conv_w": conv_w[l], "conv_b": conv_b[l].reshape(1, -1),
            "w_down": w_down[l].astype(BF16),
        }

        q_f, k_f, v_f, pr_f, kb_t, vt_t = _proj(x_f, ln1_g[l].reshape(1, -1), w_in_bf, qg, kg, gmean,
                                                 512, True)
        q_s, k_s, v_s, pr_s = _proj(x_s.reshape(nb * dt, d), ln1_g[l].reshape(1, -1), w_in_bf, qg, kg,
                                    gmean, nb * dt, False)
        m0 = db * dt

        kb_m = k_s[m0:].astype(BF16)
        vb_m = v_s[m0:].astype(BF16)
        o_f, o_m = _attn_prompt(cfar, lam, q_f, kb_t, vt_t, kb_m, vb_m, jnp.transpose(vb_m), q_s[m0:],
                                b0, bm1, bq0, bmm, sg, out_scale)
        o_d = _attn_decode(lam, q_s, cache_k[l].reshape(db, n_cache, DA_WIDTH),
                           cache_v[l].reshape(db, n_cache, DA_WIDTH), k_s, v_s, bias_dec, sg,
                           out_scale, dt)
        o_s = jnp.concatenate([o_d, o_m], axis=0)

        pr_s3 = pr_s.reshape(nb, dt, RW_PROJ)
        pr_pad = jnp.pad(pr_s3, ((0, 0), (0, RW_CHUNK - dt), (0, 0)))
        shift_s = jnp.concatenate([state_shift[l], jnp.zeros((1, RW_PROJ), F32)], axis=0)[:, None, :]
        h_s = _state_to_pairs(jnp.concatenate(
            [state_rwkv[l], jnp.zeros((1,) + state_rwkv.shape[2:], F32)], axis=0))
        orw_s, hn_s = _rwkv(pr_pad, shift_s, h_s, rw, RW_CHUNK, dt)
        orw_f, hn_f = _rwkv(pr_f[None], pr_s3[db:, dt - 1:dt, :], hn_s[db:], rw, 512, seq)

        def time_major(a):
            a = jnp.pad(a.reshape(nb, dt, -1), ((0, nb_pad - nb), (0, 0), (0, 0)))
            return jnp.swapaxes(a, 0, 1).reshape(dt * nb_pad, -1)

        conv_s = jnp.concatenate([state_conv[l], jnp.zeros((1, CONV_W - 1, 2 * d_ff), F32)], axis=0)
        conv_s = jnp.pad(conv_s, ((0, nb_pad - nb), (0, 0), (0, 0)))
        conv_s = jnp.swapaxes(conv_s, 0, 1).reshape(2 * nb_pad, 2 * d_ff)
        y_s, cn_s = _ffn(time_major(x_s), time_major(o_s), time_major(orw_s[:, :dt]), conv_s, ff,
                         dt * nb_pad, nb_pad)
        cn_s = jnp.swapaxes(cn_s.reshape(2, nb_pad, 2 * d_ff), 0, 1)
        y_f, cn_f = _ffn(x_f, o_f, orw_f[0], cn_s[db], ff, 512, 1)
        y_s = jnp.swapaxes(y_s.reshape(dt, nb_pad, d), 0, 1)[:nb]

        hw = (DA_HEADS, 2 * HEAD_DIM)
        outs[0].append(jnp.concatenate([k_s[m0:], k_f], axis=0).reshape(bp, N_META + seq, *hw))
        outs[1].append(jnp.concatenate([v_s[m0:], v_f], axis=0).reshape(bp, N_META + seq, *hw))
        outs[2].append(_pairs_to_state(hn_f))
        outs[3].append(pr_f[seq - 1:seq])
        outs[4].append(cn_f[None])
        outs[5].append(k_s[:m0].reshape(db, dt, *hw))
        outs[6].append(v_s[:m0].reshape(db, dt, *hw))
        outs[7].append(_pairs_to_state(hn_s[:db]))
        outs[8].append(pr_s3[:db, dt - 1])
        outs[9].append(cn_s[:db])
        x_f, x_s = y_f, y_s

    return (x_f[None], x_s[:db], *[jnp.stack(o) for o in outs])
```

```python
import functools
import math

import numpy as np
import jax
import jax.numpy as jnp
from jax import lax
from jax.experimental import pallas as pl
from jax.experimental.pallas import tpu as pltpu

F32 = jnp.float32
BF16 = jnp.bfloat16

CHUNK = 64
N_META = 16
HEAD_DIM = 64
DA_HEADS = 4
RW_HEADS = 8
W_LORA = 64
A_LORA = 64
G_LORA = 128
CONV_W = 3
N_BUCKETS = 32
MAX_DISTANCE = 128
NORM_EPS = 1e-6
GN_EPS = 64e-5
NEG_INF = -1e30
LOG2E = math.log2(math.e)

DA_WIDTH = DA_HEADS * 2 * HEAD_DIM
RW_WIDTH = RW_HEADS * HEAD_DIM
RW_PROJ = 3 * RW_WIDTH + W_LORA + A_LORA + G_LORA
LANES = 128
PAIR = 2 * HEAD_DIM
N_PAIR = RW_WIDTH // PAIR
RW_CHUNK = 64
ATT_T = 256
VMEM_LIMIT = 56 * 1024 * 1024


def _dot(a, b):
    return jnp.dot(a, b, preferred_element_type=F32)


def _dot_nt(a, b):
    return lax.dot_general(a, b, (((1,), (1,)), ((), ())), preferred_element_type=F32)


def _cparams(sem):
    return pltpu.CompilerParams(dimension_semantics=sem, vmem_limit_bytes=VMEM_LIMIT)


def _const_spec(shape):
    nd = len(shape)
    return pl.BlockSpec(shape, lambda *_: (0,) * nd)


def _resident_spec(shape):
    nd = len(shape)
    return pl.BlockSpec(shape, lambda *_: (0,) * nd, pipeline_mode=pl.Buffered(1))


def _proj_kernel(x_ref, g_ref, w_ref, qg_ref, kg_ref, gm_ref,
                 q_ref, k_ref, v_ref, pr_ref, *tile_refs, tm):
    x = x_ref[...]
    ms = jnp.mean(x * x, axis=-1, keepdims=True)
    h = (x * lax.rsqrt(ms + NORM_EPS) * g_ref[...]).astype(BF16)
    gm = gm_ref[...]

    def group_norm(t, g):
        ms_g = _dot((t * t).astype(BF16), gm)
        return t * lax.rsqrt(ms_g + NORM_EPS) * g

    q = _dot(h, w_ref[:, 0:DA_WIDTH])
    q_ref[...] = (group_norm(q, qg_ref[...]) * (HEAD_DIM ** -0.5 * LOG2E)).astype(BF16)
    k = group_norm(_dot(h, w_ref[:, DA_WIDTH:2 * DA_WIDTH]), kg_ref[...])
    k_ref[...] = k
    v = _dot(h, w_ref[:, 2 * DA_WIDTH:3 * DA_WIDTH])
    v_ref[...] = v
    pr_ref[...] = _dot(h, w_ref[:, 3 * DA_WIDTH:])
    if tile_refs:
        kb_ref, vt_ref = tile_refs
        kb = k.astype(BF16)
        vt = jnp.transpose(v).astype(BF16)
        for hd in range(DA_HEADS):
            for jj in range(tm // ATT_T):
                kb_ref[hd, jj] = kb[jj * ATT_T:(jj + 1) * ATT_T, hd * LANES:(hd + 1) * LANES]
                vt_ref[hd, jj] = vt[hd * LANES:(hd + 1) * LANES, jj * ATT_T:(jj + 1) * ATT_T]


def _proj(x, ln1_g, w_in_bf, qg, kg, gmean, tm, emit_tiles):
    rows, d = x.shape
    n_in = w_in_bf.shape[1]
    assert rows % tm == 0
    row = lambda w: pl.BlockSpec((tm, w), lambda i: (i, 0))
    out_specs = [row(DA_WIDTH), row(DA_WIDTH), row(DA_WIDTH), row(RW_PROJ)]
    out_shape = [jax.ShapeDtypeStruct((rows, DA_WIDTH), BF16),
                 jax.ShapeDtypeStruct((rows, DA_WIDTH), F32),
                 jax.ShapeDtypeStruct((rows, DA_WIDTH), F32),
                 jax.ShapeDtypeStruct((rows, RW_PROJ), F32)]
    if emit_tiles:
        assert tm % ATT_T == 0
        tpt = tm // ATT_T
        out_specs += [pl.BlockSpec((DA_HEADS, tpt, ATT_T, LANES), lambda i: (0, i, 0, 0)),
                      pl.BlockSpec((DA_HEADS, tpt, LANES, ATT_T), lambda i: (0, i, 0, 0))]
        out_shape += [jax.ShapeDtypeStruct((DA_HEADS, rows // ATT_T, ATT_T, LANES), BF16),
                      jax.ShapeDtypeStruct((DA_HEADS, rows // ATT_T, LANES, ATT_T), BF16)]
    return pl.pallas_call(
        functools.partial(_proj_kernel, tm=tm),
        grid=(rows // tm,),
        in_specs=[row(d), _const_spec((1, d)), _resident_spec((d, n_in)),
                  _const_spec((1, DA_WIDTH)), _const_spec((1, DA_WIDTH)),
                  _resident_spec((DA_WIDTH, DA_WIDTH))],
        out_specs=out_specs,
        out_shape=out_shape,
        compiler_params=_cparams(("arbitrary",)),
        name="proj",
    )(x, ln1_g, w_in_bf, qg, kg, gmean)


def _stack_components(q):
    lo = lax.broadcasted_iota(jnp.int32, q.shape, 1) < HEAD_DIM
    zero = jnp.zeros_like(q)
    return jnp.concatenate([jnp.where(lo, q, zero), jnp.where(lo, zero, q)], axis=0)


def _sub_layer_norm(o, g, out_scale):
    ms = jnp.mean(o * o, axis=-1, keepdims=True)
    return o * lax.rsqrt(ms + NORM_EPS) * g * out_scale


def _attn_prompt_kernel(cfar_ref, lam_ref, q_ref, k_ref, vt_ref, km_ref, vm_ref, vmt_ref, qm_ref,
                        b0_ref, bm1_ref, bq0_ref, bmm_ref, g_ref, o_ref, om_ref,
                        acc_ref, m_ref, l_ref, s_buf, p_buf, a_buf, *, out_scale):
    h = pl.program_id(0)
    i = pl.program_id(1)
    tq = ATT_T
    cf = cfar_ref[h]
    lam = lam_ref[0]
    qst = _stack_components(q_ref[...])

    def both(b):
        return jnp.concatenate([b, b], axis=1)

    s = _dot_nt(km_ref[...], qst)
    s = s + both(jnp.where(i == 0, bq0_ref[0], cf))
    m0 = jnp.max(s, axis=0, keepdims=True)
    p = jnp.exp2(s - m0)
    m_ref[...] = m0
    l_ref[...] = jnp.sum(p, axis=0, keepdims=True)
    acc_ref[...] = _dot(vmt_ref[...], p.astype(BF16))

    def scores(idx, slot):
        s_buf[slot] = _dot_nt(k_ref[0, idx], qst)

    def softmax(slot, bias):
        s = s_buf[slot]
        m_prev = m_ref[...]
        if bias.ndim == 0:
            m_new = jnp.maximum(m_prev, jnp.max(s, axis=0, keepdims=True) + bias)
            p = jnp.exp2(s - (m_new - bias))
        else:
            s = s + both(bias)
            m_new = jnp.maximum(m_prev, jnp.max(s, axis=0, keepdims=True))
            p = jnp.exp2(s - m_new)
        alpha = jnp.exp2(m_prev - m_new)
        l_ref[...] = alpha * l_ref[...] + jnp.sum(p, axis=0, keepdims=True)
        m_ref[...] = m_new
        p_buf[slot] = p.astype(BF16)
        a_buf[slot] = alpha

    def accumulate(idx, slot):
        acc_ref[...] = a_buf[slot] * acc_ref[...] + _dot(vt_ref[0, idx], p_buf[slot])

    n_far = jnp.maximum(i - 1, 0)
    off = n_far % 2
    p_buf[1] = jnp.zeros(p_buf.shape[1:], BF16)
    a_buf[1] = jnp.ones(a_buf.shape[1:], F32)
    scores(0, 0)

    def far_body(t, carry):
        u = 2 * t - off
        scores(u + 1, 1)
        softmax(0, jnp.where(u >= 0, cf, NEG_INF))
        accumulate(jnp.maximum(u - 1, 0), 1)
        scores(u + 2, 0)
        softmax(1, cf)
        accumulate(jnp.maximum(u, 0), 0)
        return carry

    lax.fori_loop(0, (n_far + 1) // 2, far_body, 0)

    scores(i, 1)
    softmax(0, jnp.where(i >= 1, bm1_ref[0], NEG_INF))
    accumulate(jnp.maximum(n_far - 1, 0), 1)
    softmax(1, b0_ref[0])
    accumulate(n_far, 0)
    accumulate(i, 1)

    accn = acc_ref[...] / l_ref[...]
    o_t = accn[:, 0:tq] - lam * accn[:, tq:]
    o_ref[...] = _sub_layer_norm(jnp.transpose(o_t), g_ref[...], out_scale).astype(BF16)

    @pl.when(i == 0)
    def _():
        qm = _stack_components(qm_ref[...])
        bmm = bmm_ref[0]
        sm = _dot_nt(qm, km_ref[...]) + jnp.concatenate([bmm, bmm], axis=0)
        mm = jnp.max(sm, axis=-1, keepdims=True)
        pm = jnp.exp2(sm - mm)
        accm = _dot(pm.astype(BF16), vm_ref[...]) / jnp.sum(pm, axis=-1, keepdims=True)
        om = accm[0:N_META] - lam * accm[N_META:]
        om_ref[...] = _sub_layer_norm(om, g_ref[...], out_scale).astype(BF16)


def _attn_prompt(cfar, lam, q_f, kb_t, vt_t, kb_m, vb_m, vbt_m, q_m, b0, bm1, bq0, bmm, subln_g,
                 out_scale):
    tf = q_f.shape[0]
    tq = ATT_T
    assert tf % tq == 0
    n_t = tf // tq
    smem = pl.BlockSpec(memory_space=pltpu.SMEM)
    head_col = lambda rows: pl.BlockSpec((rows, LANES), lambda h, i: (0, h))
    head_tile = lambda a, b: pl.BlockSpec((1, a, b), lambda h, i: (h, 0, 0))
    return pl.pallas_call(
        functools.partial(_attn_prompt_kernel, out_scale=out_scale),
        grid=(DA_HEADS, n_t),
        in_specs=[smem, smem,
                  pl.BlockSpec((tq, LANES), lambda h, i: (i, h)),
                  pl.BlockSpec((1, n_t, tq, LANES), lambda h, i: (h, 0, 0, 0)),
                  pl.BlockSpec((1, n_t, LANES, tq), lambda h, i: (h, 0, 0, 0)),
                  head_col(N_META), head_col(N_META),
                  pl.BlockSpec((LANES, N_META), lambda h, i: (h, 0)),
                  head_col(N_META),
                  head_tile(tq, tq), head_tile(tq, tq), head_tile(N_META, tq),
                  head_tile(N_META, N_META), _const_spec((1, LANES))],
        out_specs=[pl.BlockSpec((tq, LANES), lambda h, i: (i, h)), head_col(N_META)],
        out_shape=[jax.ShapeDtypeStruct((tf, DA_WIDTH), BF16),
                   jax.ShapeDtypeStruct((N_META, DA_WIDTH), BF16)],
        scratch_shapes=[pltpu.VMEM((LANES, 2 * tq), F32), pltpu.VMEM((1, 2 * tq), F32),
                        pltpu.VMEM((1, 2 * tq), F32), pltpu.VMEM((2, tq, 2 * tq), F32),
                        pltpu.VMEM((2, tq, 2 * tq), BF16), pltpu.VMEM((2, 1, 2 * tq), F32)],
        compiler_params=_cparams(("arbitrary", "arbitrary")),
        name="attn_prompt",
    )(cfar, lam, q_f, kb_t, vt_t, kb_m, vb_m, vbt_m, q_m, b0, bm1, bq0, bmm, subln_g)


def _attn_decode_kernel(lam_ref, q_ref, k_ref, v_ref, kn_ref, vn_ref, b_ref, g_ref, o_ref,
                        *, n_main, out_scale):
    lam = lam_ref[0]
    nq = q_ref.shape[0]
    n_cache = k_ref.shape[0]
    qst = _stack_components(q_ref[...])
    k_main = k_ref[0:n_main, :].astype(BF16)
    v_main = v_ref[0:n_main, :].astype(BF16)
    k_tail = jnp.concatenate([k_ref[n_main:n_cache, :], kn_ref[...]], axis=0).astype(BF16)
    v_tail = jnp.concatenate([v_ref[n_main:n_cache, :], vn_ref[...]], axis=0).astype(BF16)
    bias = jnp.concatenate([b_ref[0], b_ref[0]], axis=0)
    s1 = _dot_nt(qst, k_main) + bias[:, 0:n_main]
    s2 = _dot_nt(qst, k_tail) + bias[:, n_main:]
    m = jnp.maximum(jnp.max(s1, axis=-1, keepdims=True), jnp.max(s2, axis=-1, keepdims=True))
    p1 = jnp.exp2(s1 - m)
    p2 = jnp.exp2(s2 - m)
    l = jnp.sum(p1, axis=-1, keepdims=True) + jnp.sum(p2, axis=-1, keepdims=True)
    res = (_dot(p1.astype(BF16), v_main) + _dot(p2.astype(BF16), v_tail)) / l
    o = res[0:nq] - lam * res[nq:]
    o_ref[...] = _sub_layer_norm(o, g_ref[...], out_scale).astype(BF16)


def _attn_decode(lam, q_s, cache_k, cache_v, k_s, v_s, bias, subln_g, out_scale, nq):
    b, n_cache, width = cache_k.shape
    n_h = width // LANES
    n_main = (n_cache // LANES) * LANES
    n_keys = n_cache + nq
    smem = pl.BlockSpec(memory_space=pltpu.SMEM)
    cache = pl.BlockSpec((None, n_cache, LANES), lambda bi, h: (bi, 0, h))
    new = pl.BlockSpec((nq, LANES), lambda bi, h: (bi, h))
    return pl.pallas_call(
        functools.partial(_attn_decode_kernel, n_main=n_main, out_scale=out_scale),
        grid=(b, n_h),
        in_specs=[smem, new, cache, cache, new, new,
                  pl.BlockSpec((1, nq, n_keys), lambda bi, h: (h, 0, 0)), _const_spec((1, LANES))],
        out_specs=new,
        out_shape=jax.ShapeDtypeStruct((b * nq, DA_WIDTH), BF16),
        compiler_params=_cparams(("arbitrary", "arbitrary")),
        name="attn_decode",
    )(lam, q_s, cache_k, cache_v, k_s, v_s, bias, subln_g)


def _sigmoid(x):
    return 1.0 / (1.0 + jnp.exp(-x))


def _softplus(x):
    return jnp.maximum(x, 0.0) + jnp.log(1.0 + jnp.exp(-jnp.abs(x)))


def _split3(x):
    x1 = x.astype(BF16)
    r1 = x - x1.astype(F32)
    x2 = r1.astype(BF16)
    x3 = (r1 - x2.astype(F32)).astype(BF16)
    return x1, x2, x3


def _stack_heads(x):
    lo = lax.broadcasted_iota(jnp.int32, x.shape, 1) < HEAD_DIM
    zero = jnp.zeros_like(x)
    return jnp.concatenate([jnp.where(lo, x, zero), jnp.where(lo, zero, x)], axis=0)


def _rwkv_kernel(pr_ref, sh0_ref, h0_ref, mu_ref, w0_ref, a0_ref, kk_ref, ka_ref, rk_ref,
                 lw_ref, lb_ref, w2_ref, a2_ref, g2_ref, gsum_ref, tri_ref,
                 o_ref, hn_ref,
                 xbuf, carry, at_s, rt_s, bt_s, kt_s, v_s, ep_s, y_s,
                 t_s, tav_s, lrb_s, lrkv_s, zbt_s, zkv_s, ar_s, pc_s,
                 *, tr, t_valid):
    c_len = RW_CHUNK
    n_ch = tr // c_len
    ti = pl.program_id(1)

    @pl.when(ti == 0)
    def _():
        carry[...] = sh0_ref[0]
        hn_ref[...] = h0_ref[...]

    x = pr_ref[0]
    xbuf[8:8 + tr, :] = x
    xbuf[7:8, :] = carry[...]
    prev = xbuf[7:7 + tr, :]
    carry[...] = pr_ref[0, tr - 1:tr, :]
    xm = x + (prev - x) * mu_ref[...]
    r = xm[:, 0:RW_WIDTH]
    kr = xm[:, RW_WIDTH:2 * RW_WIDTH]
    vr = xm[:, 2 * RW_WIDTH:3 * RW_WIDTH]
    wa = xm[:, 3 * RW_WIDTH:3 * RW_WIDTH + W_LORA + A_LORA]
    gd = xm[:, 3 * RW_WIDTH + W_LORA + A_LORA:]
    lane_wa = lax.broadcasted_iota(jnp.int32, wa.shape, 1)
    twa = jnp.where(lane_wa < W_LORA, jnp.tanh(wa), wa).astype(BF16)
    w_log = -_softplus(-(w0_ref[...] + _dot(twa, w2_ref[...]))) - 0.5
    logw = -jnp.exp(w_log)
    a_sig = _sigmoid(a0_ref[...] + _dot(twa, a2_ref[...]))
    g = _dot(_sigmoid(gd).astype(BF16), g2_ref[...])
    gsum = gsum_ref[...]
    kk = kr * kk_ref[...]
    kk_ss = _dot((kk * kk).astype(BF16), gsum)
    kk = kk / jnp.maximum(jnp.sqrt(kk_ss), 1e-12)
    kr2 = kr * (1.0 + (a_sig - 1.0) * ka_ref[...])
    a_vec = -kk
    b_vec = kk * a_sig
    bonus = _dot((r * kr2 * rk_ref[...]).astype(BF16), gsum) * vr
    if t_valid % tr != 0:
        row = lax.broadcasted_iota(jnp.int32, (tr, 1), 0) + ti * tr
        valid = row < t_valid
        logw = jnp.where(valid, logw, 0.0)
        a_vec = jnp.where(valid, a_vec, 0.0)
        b_vec = jnp.where(valid, b_vec, 0.0)
        kr2 = jnp.where(valid, kr2, 0.0)
        vr = jnp.where(valid, vr, 0.0)
        bonus = jnp.where(valid, bonus, 0.0)
    l1, l2, l3 = _split3(logw)
    tri = tri_ref[...]
    cs = _dot(tri, l1) + _dot(tri, l2) + _dot(tri, l3)
    e_pos = jnp.exp(cs)
    e_neg = jnp.exp(-cs)
    at_s[...] = a_vec * jnp.exp(cs - logw)
    rt_s[...] = r * e_pos
    bt_s[...] = b_vec * e_neg
    kt_s[...] = kr2 * e_neg
    v_s[...] = vr
    ep_s[...] = e_pos

    idx_r = lax.broadcasted_iota(jnp.int32, (PAIR, PAIR), 0)
    idx_c = lax.broadcasted_iota(jnp.int32, (PAIR, PAIR), 1)
    same = (idx_r // c_len) == (idx_c // c_len)
    strict = same & ((idx_r % c_len) > (idx_c % c_len))
    incl = same & ((idx_r % c_len) >= (idx_c % c_len))
    eye = idx_r == idx_c
    eye_f = jnp.where(eye, 1.0, 0.0).astype(F32)

    def phase1(c, carry_):
        rows = pl.ds(pl.multiple_of(c * c_len, c_len), c_len)
        for p in range(N_PAIR):
            cols = slice(PAIR * p, PAIR * (p + 1))
            at = at_s[rows, cols]
            rt = rt_s[rows, cols]
            yb = _stack_heads(bt_s[rows, cols])
            yk = _stack_heads(kt_s[rows, cols])
            vst = _stack_heads(v_s[rows, cols]).astype(BF16)
            xr = jnp.concatenate([_stack_heads(at), _stack_heads(rt)], axis=0).astype(BF16)
            ybk = jnp.concatenate([yb, yk], axis=0).astype(BF16)
            gmat = _dot_nt(xr, ybk)
            zero = jnp.zeros((PAIR, PAIR), F32)
            aab = jnp.where(strict, gmat[0:PAIR, 0:PAIR], zero)
            aak = jnp.where(strict, gmat[0:PAIR, PAIR:], zero)
            lrb = jnp.where(incl, gmat[PAIR:, 0:PAIR], zero)
            lrk = jnp.where(incl, gmat[PAIR:, PAIR:], zero)
            tinv = eye_f + aab
            lp = aab
            n = 1
            while 2 * n < c_len:
                lpb = lp.astype(BF16)
                lp = _dot(lpb, lpb)
                tinv = tinv + _dot(tinv.astype(BF16), lp.astype(BF16))
                n *= 2
            tb = tinv.astype(BF16)
            av = _dot(aak.astype(BF16), vst)
            pc = ep_s[rows, cols][c_len - 1:c_len]
            t_s[c, p] = tb
            tav_s[c, p] = _dot(tb, av.astype(BF16))
            lrb_s[c, p] = lrb.astype(BF16)
            lrkv_s[c, p] = _dot(lrk.astype(BF16), vst)
            zbt_s[c, p] = jnp.transpose(yb * pc).astype(BF16)
            zkv_s[c, p] = _dot(jnp.transpose(yk * pc).astype(BF16), vst)
            ar_s[c, p] = jnp.concatenate([at, rt], axis=0).astype(BF16)
            pc_s[c, p] = jnp.sum(jnp.where(eye, jnp.broadcast_to(pc, (PAIR, PAIR)), zero),
                                 axis=-1, keepdims=True)
        return carry_

    lax.fori_loop(0, n_ch, phase1, 0)

    def phase2(c, carry_):
        rows = pl.ds(pl.multiple_of(c * c_len, c_len), c_len)
        for p in range(N_PAIR):
            cols = slice(PAIR * p, PAIR * (p + 1))
            hbd = hn_ref[0, p]
            arh = _dot(ar_s[c, p], hbd.astype(BF16))
            ust = _dot(t_s[c, p], _stack_heads(arh[0:c_len]).astype(BF16)) + tav_s[c, p]
            ub = ust.astype(BF16)
            yst = _dot(lrb_s[c, p], ub) + lrkv_s[c, p]
            y_s[rows, cols] = arh[c_len:] + yst[0:c_len] + yst[c_len:]
            hn_ref[0, p] = pc_s[c, p] * hbd + _dot(zbt_s[c, p], ub) + zkv_s[c, p]
        return carry_

    lax.fori_loop(0, n_ch, phase2, 0)

    y = y_s[...]
    inv_n = 1.0 / HEAD_DIM
    mean = _dot(y.astype(BF16), gsum) * inv_n
    d = y - mean
    var = _dot((d * d).astype(BF16), gsum) * inv_n
    yn = d * lax.rsqrt(var + GN_EPS) * lw_ref[...] + lb_ref[...]
    o_ref[0] = ((yn + bonus) * g).astype(BF16)


def _rwkv(pr, shift0, h0, prm, tr, t_valid):
    b, t_pad, _ = pr.shape
    assert t_pad % tr == 0 and tr % RW_CHUNK == 0
    n_ch = tr // RW_CHUNK
    tri = np.zeros((tr, tr), np.float32)
    for c in range(n_ch):
        tri[c * RW_CHUNK:(c + 1) * RW_CHUNK, c * RW_CHUNK:(c + 1) * RW_CHUNK] = np.tril(
            np.ones((RW_CHUNK, RW_CHUNK), np.float32))
    tri = jnp.asarray(tri, BF16)
    vec = _const_spec((1, RW_WIDTH))
    mat = lambda dt: pltpu.VMEM((n_ch, N_PAIR, PAIR, PAIR), dt)
    tile = lambda: pltpu.VMEM((tr, RW_WIDTH), F32)
    return pl.pallas_call(
        functools.partial(_rwkv_kernel, tr=tr, t_valid=t_valid),
        grid=(b, t_pad // tr),
        in_specs=[pl.BlockSpec((1, tr, RW_PROJ), lambda bi, ti: (bi, ti, 0)),
                  pl.BlockSpec((1, 1, RW_PROJ), lambda bi, ti: (bi, 0, 0)),
                  pl.BlockSpec((1, N_PAIR, PAIR, PAIR), lambda bi, ti: (bi, 0, 0, 0)),
                  _const_spec((1, RW_PROJ)), vec, vec, vec, vec, vec, vec, vec,
                  _const_spec((W_LORA + A_LORA, RW_WIDTH)), _const_spec((W_LORA + A_LORA, RW_WIDTH)),
                  _const_spec((G_LORA, RW_WIDTH)), _const_spec((RW_WIDTH, RW_WIDTH)),
                  _const_spec((tr, tr))],
        out_specs=[pl.BlockSpec((1, tr, RW_WIDTH), lambda bi, ti: (bi, ti, 0)),
                   pl.BlockSpec((1, N_PAIR, PAIR, PAIR), lambda bi, ti: (bi, 0, 0, 0))],
        out_shape=[jax.ShapeDtypeStruct((b, t_pad, RW_WIDTH), BF16),
                   jax.ShapeDtypeStruct((b, N_PAIR, PAIR, PAIR), F32)],
        scratch_shapes=[pltpu.VMEM((8 + tr, RW_PROJ), F32), pltpu.VMEM((1, RW_PROJ), F32),
                        tile(), tile(), tile(), tile(), tile(), tile(), tile(),
                        mat(BF16), mat(F32), mat(BF16), mat(F32), mat(BF16), mat(F32), mat(BF16),
                        pltpu.VMEM((n_ch, N_PAIR, PAIR, 1), F32)],
        compiler_params=_cparams(("arbitrary", "arbitrary")),
        name="rwkv",
    )(pr, shift0, h0, prm["mu"], prm["w0"], prm["a0"], prm["k_k"], prm["k_a"], prm["r_k"],
      prm["lnx_w"], prm["lnx_b"], prm["w2p"], prm["a2p"], prm["g2"], prm["gsum"], tri)


FF_COLS = 256


def _ffn_kernel(x_ref, oa_ref, orw_ref, c0_ref, wo_ref, g_ref, wu_ref, cw_ref, cb_ref, wd_ref,
                y_ref, cn_ref, zbuf, cbuf, acc_ref, h_ref, x1_ref, *, tm, ts, d_ff):
    off = zbuf.shape[0] - tm
    da = oa_ref.shape[1]

    @pl.when(pl.program_id(0) == 0)
    def _():
        cbuf[...] = c0_ref[...]

    x1 = x_ref[...] + _dot(oa_ref[...], wo_ref[0:da, :]) + _dot(orw_ref[...], wo_ref[da:, :])
    x1_ref[...] = x1
    ms = jnp.mean(x1 * x1, axis=-1, keepdims=True)
    h_ref[...] = (x1 * lax.rsqrt(ms + NORM_EPS) * g_ref[...]).astype(BF16)
    acc_ref[...] = jnp.zeros_like(acc_ref)

    def conv_cols(c0):
        cols = slice(c0, c0 + FF_COLS)
        z = _dot(h_ref[...], wu_ref[:, cols])
        zbuf[off - 2 * ts:off, :] = cbuf[:, cols]
        zbuf[off:off + tm, :] = z
        cbuf[:, cols] = zbuf[off + tm - 2 * ts:off + tm, :]
        z2 = zbuf[off - 2 * ts:off - 2 * ts + tm, :]
        z1 = zbuf[off - ts:off - ts + tm, :]
        return (cb_ref[:, cols] + z2 * cw_ref[0:1, cols] + z1 * cw_ref[1:2, cols]
                + z * cw_ref[2:3, cols])

    for c in range(d_ff // FF_COLS):
        gate = conv_cols(c * FF_COLS)
        up = conv_cols(d_ff + c * FF_COLS)
        act = (gate * _sigmoid(gate) * up).astype(BF16)
        acc_ref[...] += _dot(act, wd_ref[c * FF_COLS:(c + 1) * FF_COLS, :])

    y_ref[...] = x1_ref[...] + acc_ref[...]
    cn_ref[...] = cbuf[...]


def _ffn(x, oa, orw, conv0, prm, tm, ts):
    rows, d = x.shape
    d_ff = prm["w_down"].shape[0]
    assert rows % tm == 0 and d_ff % FF_COLS == 0 and (ts == 1 or ts % 8 == 0)
    off = -(-2 * ts // 8) * 8
    row = lambda w: pl.BlockSpec((tm, w), lambda i: (i, 0))
    return pl.pallas_call(
        functools.partial(_ffn_kernel, tm=tm, ts=ts, d_ff=d_ff),
        grid=(rows // tm,),
        in_specs=[row(d), row(DA_WIDTH), row(RW_WIDTH), _const_spec((2 * ts, 2 * d_ff)),
                  _resident_spec((DA_WIDTH + RW_WIDTH, d)), _const_spec((1, d)),
                  _resident_spec((d, 2 * d_ff)), _const_spec((CONV_W, 2 * d_ff)),
                  _const_spec((1, 2 * d_ff)), _resident_spec((d_ff, d))],
        out_specs=[row(d), _const_spec((2 * ts, 2 * d_ff))],
        out_shape=[jax.ShapeDtypeStruct((rows, d), F32),
                   jax.ShapeDtypeStruct((2 * ts, 2 * d_ff), F32)],
        scratch_shapes=[pltpu.VMEM((off + tm, FF_COLS), F32), pltpu.VMEM((2 * ts, 2 * d_ff), F32),
                        pltpu.VMEM((tm, d), F32), pltpu.VMEM((tm, d), BF16), pltpu.VMEM((tm, d), F32)],
        compiler_params=_cparams(("arbitrary",)),
        name="ffn",
    )(x, oa, orw, conv0, prm["w_out"], prm["ln2_g"], prm["w_up"], prm["conv_w"], prm["conv_b"],
      prm["w_down"])


def _rel_bucket(rel):
    nb = N_BUCKETS // 2
    max_exact = nb // 2
    bucket = jnp.where(rel > 0, nb, 0)
    n = jnp.abs(rel)
    nf = jnp.maximum(n, 1).astype(F32)
    large = max_exact + (jnp.log(nf / max_exact) / math.log(MAX_DISTANCE / max_exact)
                         * (nb - max_exact)).astype(jnp.int32)
    large = jnp.minimum(large, nb - 1)
    return bucket + jnp.where(n < max_exact, n, large)


def _bias_table(rel_bias, q_pos, k_pos, mask):
    rel = jnp.asarray(k_pos[None, :] - q_pos[:, None], jnp.int32)
    onehot = _rel_bucket(rel)[None, :, :, None] == jnp.arange(N_BUCKETS, dtype=jnp.int32)
    table = jnp.transpose(rel_bias).astype(F32)[:, None, None, :]
    bias = jnp.sum(jnp.where(onehot, table, 0.0), axis=-1) * LOG2E
    return jnp.where(jnp.asarray(mask)[None], bias, NEG_INF)


def _ext_chunk(pos):
    return np.where(pos < N_META, -1, (pos - N_META) // CHUNK)


def _prompt_bias(rel_bias):
    tq = ATT_T
    fr = np.arange(tq) + N_META
    meta = np.arange(N_META)
    causal = _ext_chunk(fr)[None, :] <= _ext_chunk(fr)[:, None]
    tr = lambda b: jnp.swapaxes(b, 1, 2)
    b0 = tr(_bias_table(rel_bias, fr, fr, causal))
    bm1 = tr(_bias_table(rel_bias, fr + tq, fr, np.ones((tq, tq), bool)))
    bq0 = tr(_bias_table(rel_bias, fr, meta, np.ones((tq, N_META), bool)))
    bmm = _bias_table(rel_bias, meta, meta, np.ones((N_META, N_META), bool))
    assert tq + 1 >= MAX_DISTANCE
    cfar = rel_bias[_rel_bucket(jnp.asarray(-(tq + 1), jnp.int32))].astype(F32) * LOG2E
    return cfar, b0, bm1, bq0, bmm


def _decode_bias(rel_bias, n_cache, nq):
    k_pos = np.arange(n_cache + nq)
    q_pos = k_pos[n_cache:]
    mask = _ext_chunk(k_pos)[None, :] <= _ext_chunk(q_pos)[:, None]
    return _bias_table(rel_bias, q_pos, k_pos, mask)


def _state_to_pairs(s):
    b = s.shape[0]
    st = jnp.swapaxes(s, -1, -2).reshape(b, N_PAIR, 2, HEAD_DIM, HEAD_DIM)
    eye2 = jnp.eye(2, dtype=s.dtype)
    return jnp.einsum('bphkv,hg->bphkgv', st, eye2).reshape(b, N_PAIR, PAIR, PAIR)


def _pairs_to_state(hp):
    b = hp.shape[0]
    h6 = hp.reshape(b, N_PAIR, 2, HEAD_DIM, 2, HEAD_DIM)
    diag = jnp.stack([h6[:, :, 0, :, 0, :], h6[:, :, 1, :, 1, :]], axis=2)
    return jnp.swapaxes(diag, -1, -2).reshape(b, RW_HEADS, HEAD_DIM, HEAD_DIM)


def _block_ones(n, blk, dtype):
    idx = np.arange(n) // blk
    return jnp.asarray((idx[:, None] == idx[None, :]).astype(np.float32), dtype)


def kernel(x_prompt, x_sample, cache_k, cache_v, state_rwkv, state_shift, state_conv, meta_tokens,
           rel_bias, ln1_g, w_in, q_norm_g, k_norm_g, lam_q1, lam_k1, lam_q2, lam_k2, subln_g,
           mu_shift, w0, w2, a0, a2, g2, k_k, k_a, r_k, lnx_w, lnx_b, w_out, ln2_g, w_up, conv_w,
           conv_b, w_down):
    bp, seq, d = x_prompt.shape
    db, dt, _ = x_sample.shape
    depth = w_in.shape[0]
    d_ff = w_down.shape[1]
    n_cache = cache_k.shape[2]
    assert bp == 1 and dt == N_META, "the meta stream rides with the decode streams"
    assert cache_k.shape[3] == DA_HEADS and cache_k.shape[4] == 2 * HEAD_DIM
    nb = db + 1
    nb_pad = -(-nb // 8) * 8

    cfar, b0, bm1, bq0, bmm = _prompt_bias(rel_bias)
    bias_dec = _decode_bias(rel_bias, n_cache, dt)
    gmean = _block_ones(DA_WIDTH, HEAD_DIM, BF16) * (1.0 / HEAD_DIM)
    gsum = _block_ones(RW_WIDTH, HEAD_DIM, BF16)
    zrow = lambda n: jnp.zeros((n, RW_WIDTH), BF16)

    x_f = x_prompt[0]
    x_s = jnp.concatenate([x_sample, meta_tokens.astype(x_sample.dtype)[None]], axis=0)
    outs = [[] for _ in range(10)]
    for l in range(depth):
        lam_init = 0.8 - 0.6 * math.exp(-0.3 * l)
        lam = (jnp.exp(jnp.sum(lam_q1[l].astype(F32) * lam_k1[l].astype(F32)))
               - jnp.exp(jnp.sum(lam_q2[l].astype(F32) * lam_k2[l].astype(F32))) + lam_init).reshape(1)
        out_scale = 1.0 - lam_init
        tile128 = lambda g_: jnp.tile(g_.reshape(1, -1), (1, DA_WIDTH // g_.shape[-1]))
        qg, kg = tile128(q_norm_g[l]), tile128(k_norm_g[l])
        sg = subln_g[l].reshape(1, LANES)
        w_in_bf = w_in[l].astype(BF16)
        rw = {
            "mu": mu_shift[l].reshape(1, -1), "w0": w0[l].reshape(1, -1), "a0": a0[l].reshape(1, -1),
            "k_k": k_k[l].reshape(1, -1), "k_a": k_a[l].reshape(1, -1), "r_k": r_k[l].reshape(1, -1),
            "lnx_w": lnx_w[l].reshape(1, -1), "lnx_b": lnx_b[l].reshape(1, -1),
            "w2p": jnp.concatenate([w2[l].astype(BF16), zrow(A_LORA)], axis=0),
            "a2p": jnp.concatenate([zrow(W_LORA), a2[l].astype(BF16)], axis=0),
            "g2": g2[l].astype(BF16), "gsum": gsum,
        }
        ff = {
            "w_out": w_out[l].astype(BF16), "ln2_g": ln2_g[l].reshape(1, -1),
            "w_up": w_up[l].astype(BF16), "conv_w": conv_w[l], "conv_b": conv_b[l].reshape(1, -1),
            "w_down": w_down[l].astype(BF16),
        }

        q_f, k_f, v_f, pr_f, kb_t, vt_t = _proj(x_f, ln1_g[l].reshape(1, -1), w_in_bf, qg, kg, gmean,
                                                 512, True)
        q_s, k_s, v_s, pr_s = _proj(x_s.reshape(nb * dt, d), ln1_g[l].reshape(1, -1), w_in_bf, qg, kg,
                                    gmean, nb * dt, False)
        m0 = db * dt

        kb_m = k_s[m0:].astype(BF16)
        vb_m = v_s[m0:].astype(BF16)
        o_f, o_m = _attn_prompt(cfar, lam, q_f, kb_t, vt_t, kb_m, vb_m, jnp.transpose(vb_m), q_s[m0:],
                                b0, bm1, bq0, bmm, sg, out_scale)
        o_d = _attn_decode(lam, q_s, cache_k[l].reshape(db, n_cache, DA_WIDTH),
                           cache_v[l].reshape(db, n_cache, DA_WIDTH), k_s, v_s, bias_dec, sg,
                           out_scale, dt)
        o_s = jnp.concatenate([o_d, o_m], axis=0)

        pr_s3 = pr_s.reshape(nb, dt, RW_PROJ)
        pr_pad = jnp.pad(pr_s3, ((0, 0), (0, RW_CHUNK - dt), (0, 0)))
        shift_s = jnp.concatenate([state_shift[l], jnp.zeros((1, RW_PROJ), F32)], axis=0)[:, None, :]
        h_s = _state_to_pairs(jnp.concatenate(
            [state_rwkv[l], jnp.zeros((1,) + state_rwkv.shape[2:], F32)], axis=0))
        orw_s, hn_s = _rwkv(pr_pad, shift_s, h_s, rw, RW_CHUNK, dt)
        orw_f, hn_f = _rwkv(pr_f[None], pr_s3[db:, dt - 1:dt, :], hn_s[db:], rw, 512, seq)

        def time_major(a):
            a = jnp.pad(a.reshape(nb, dt, -1), ((0, nb_pad - nb), (0, 0), (0, 0)))
            return jnp.swapaxes(a, 0, 1).reshape(dt * nb_pad, -1)

        conv_s = jnp.concatenate([state_conv[l], jnp.zeros((1, CONV_W - 1, 2 * d_ff), F32)], axis=0)
        conv_s = jnp.pad(conv_s, ((0, nb_pad - nb), (0, 0), (0, 0)))
        conv_s = jnp.swapaxes(conv_s, 0, 1).reshape(2 * nb_pad, 2 * d_ff)
        y_s, cn_s = _ffn(time_major(x_s), time_major(o_s), time_major(orw_s[:, :dt]), conv_s, ff,
                         dt * nb_pad, nb_pad)
        cn_s = jnp.swapaxes(cn_s.reshape(2, nb_pad, 2 * d_ff), 0, 1)
        y_f, cn_f = _ffn(x_f, o_f, orw_f[0], cn_s[db], ff, 512, 1)
        y_s = jnp.swapaxes(y_s.reshape(dt, nb_pad, d), 0, 1)[:nb]

        hw = (DA_HEADS, 2 * HEAD_DIM)
        outs[0].append(jnp.concatenate([k_s[m0:], k_f], axis=0).reshape(bp, N_META + seq, *hw))
        outs[1].append(jnp.concatenate([v_s[m0:], v_f], axis=0).reshape(bp, N_META + seq, *hw))
        outs[2].append(_pairs_to_state(hn_f))
        outs[3].append(pr_f[seq - 1:seq])
        outs[4].append(cn_f[None])
        outs[5].append(k_s[:m0].reshape(db, dt, *hw))
        outs[6].append(v_s[:m0].reshape(db, dt, *hw))
        outs[7].append(_pairs_to_state(hn_s[:db]))
        outs[8].append(pr_s3[:db, dt - 1])
        outs[9].append(cn_s[:db])
        x_f, x_s = y_f, y_s

    return (x_f[None], x_s[:db], *[jnp.stack(o) for o in outs])
```

```python
import functools
import math

import numpy as np
import jax
import jax.numpy as jnp
from jax import lax
from jax.experimental import pallas as pl
from jax.experimental.pallas import tpu as pltpu

F32 = jnp.float32
BF16 = jnp.bfloat16

CHUNK = 64
N_META = 16
HEAD_DIM = 64
DA_HEADS = 4
RW_HEADS = 8
W_LORA = 64
A_LORA = 64
G_LORA = 128
CONV_W = 3
N_BUCKETS = 32
MAX_DISTANCE = 128
NORM_EPS = 1e-6
GN_EPS = 64e-5
NEG_INF = -1e30
LOG2E = math.log2(math.e)

DA_WIDTH = DA_HEADS * 2 * HEAD_DIM
RW_WIDTH = RW_HEADS * HEAD_DIM
RW_PROJ = 3 * RW_WIDTH + W_LORA + A_LORA + G_LORA
LANES = 128
PAIR = 2 * HEAD_DIM
N_PAIR = RW_WIDTH // PAIR
RW_CHUNK = 64
ATT_T = 256
VMEM_LIMIT = 56 * 1024 * 1024


def _dot(a, b):
    return jnp.dot(a, b, preferred_element_type=F32)


def _dot_nt(a, b):
    return lax.dot_general(a, b, (((1,), (1,)), ((), ())), preferred_element_type=F32)


def _cparams(sem):
    return pltpu.CompilerParams(dimension_semantics=sem, vmem_limit_bytes=VMEM_LIMIT)


def _const_spec(shape):
    nd = len(shape)
    return pl.BlockSpec(shape, lambda *_: (0,) * nd)


def _resident_spec(shape):
    nd = len(shape)
    return pl.BlockSpec(shape, lambda *_: (0,) * nd, pipeline_mode=pl.Buffered(1))


def _proj_kernel(x_ref, g_ref, w_ref, qg_ref, kg_ref, gm_ref,
                 q_ref, k_ref, v_ref, pr_ref, *tile_refs, tm):
    x = x_ref[...]
    ms = jnp.mean(x * x, axis=-1, keepdims=True)
    h = (x * lax.rsqrt(ms + NORM_EPS) * g_ref[...]).astype(BF16)
    gm = gm_ref[...]

    def group_norm(t, g):
        ms_g = _dot((t * t).astype(BF16), gm)
        return t * lax.rsqrt(ms_g + NORM_EPS) * g

    q = _dot(h, w_ref[:, 0:DA_WIDTH])
    q_ref[...] = (group_norm(q, qg_ref[...]) * (HEAD_DIM ** -0.5 * LOG2E)).astype(BF16)
    k = group_norm(_dot(h, w_ref[:, DA_WIDTH:2 * DA_WIDTH]), kg_ref[...])
    k_ref[...] = k
    v = _dot(h, w_ref[:, 2 * DA_WIDTH:3 * DA_WIDTH])
    v_ref[...] = v
    pr_ref[...] = _dot(h, w_ref[:, 3 * DA_WIDTH:])
    if tile_refs:
        kb_ref, vt_ref = tile_refs
        kb = k.astype(BF16)
        vt = jnp.transpose(v).astype(BF16)
        for hd in range(DA_HEADS):
            for jj in range(tm // ATT_T):
                kb_ref[hd, jj] = kb[jj * ATT_T:(jj + 1) * ATT_T, hd * LANES:(hd + 1) * LANES]
                vt_ref[hd, jj] = vt[hd * LANES:(hd + 1) * LANES, jj * ATT_T:(jj + 1) * ATT_T]


def _proj(x, ln1_g, w_in_bf, qg, kg, gmean, tm, emit_tiles):
    rows, d = x.shape
    n_in = w_in_bf.shape[1]
    assert rows % tm == 0
    row = lambda w: pl.BlockSpec((tm, w), lambda i: (i, 0))
    out_specs = [row(DA_WIDTH), row(DA_WIDTH), row(DA_WIDTH), row(RW_PROJ)]
    out_shape = [jax.ShapeDtypeStruct((rows, DA_WIDTH), BF16),
                 jax.ShapeDtypeStruct((rows, DA_WIDTH), F32),
                 jax.ShapeDtypeStruct((rows, DA_WIDTH), F32),
                 jax.ShapeDtypeStruct((rows, RW_PROJ), F32)]
    if emit_tiles:
        assert tm % ATT_T == 0
        tpt = tm // ATT_T
        out_specs += [pl.BlockSpec((DA_HEADS, tpt, ATT_T, LANES), lambda i: (0, i, 0, 0)),
                      pl.BlockSpec((DA_HEADS, tpt, LANES, ATT_T), lambda i: (0, i, 0, 0))]
        out_shape += [jax.ShapeDtypeStruct((DA_HEADS, rows // ATT_T, ATT_T, LANES), BF16),
                      jax.ShapeDtypeStruct((DA_HEADS, rows // ATT_T, LANES, ATT_T), BF16)]
    return pl.pallas_call(
        functools.partial(_proj_kernel, tm=tm),
        grid=(rows // tm,),
        in_specs=[row(d), _const_spec((1, d)), _resident_spec((d, n_in)),
                  _const_spec((1, DA_WIDTH)), _const_spec((1, DA_WIDTH)),
                  _resident_spec((DA_WIDTH, DA_WIDTH))],
        out_specs=out_specs,
        out_shape=out_shape,
        compiler_params=_cparams(("arbitrary",)),
        name="proj",
    )(x, ln1_g, w_in_bf, qg, kg, gmean)


def _stack_components(q):
    lo = lax.broadcasted_iota(jnp.int32, q.shape, 1) < HEAD_DIM
    zero = jnp.zeros_like(q)
    return jnp.concatenate([jnp.where(lo, q, zero), jnp.where(lo, zero, q)], axis=0)


def _sub_layer_norm(o, g, out_scale):
    ms = jnp.mean(o * o, axis=-1, keepdims=True)
    return o * lax.rsqrt(ms + NORM_EPS) * g * out_scale


def _attn_prompt_kernel(cfar_ref, lam_ref, q_ref, k_ref, vt_ref, km_ref, vm_ref, vmt_ref, qm_ref,
                        b0_ref, bm1_ref, bq0_ref, bmm_ref, g_ref, o_ref, om_ref,
                        acc_ref, m_ref, l_ref, s_buf, p_buf, a_buf, *, out_scale):
    h = pl.program_id(0)
    i = pl.program_id(1)
    tq = ATT_T
    cf = cfar_ref[h]
    lam = lam_ref[0]
    qst = _stack_components(q_ref[...])

    def both(b):
        return jnp.concatenate([b, b], axis=1)

    s = _dot_nt(km_ref[...], qst)
    s = s + both(jnp.where(i == 0, bq0_ref[0], cf))
    m0 = jnp.max(s, axis=0, keepdims=True)
    p = jnp.exp2(s - m0)
    m_ref[...] = m0
    l_ref[...] = jnp.sum(p, axis=0, keepdims=True)
    acc_ref[...] = _dot(vmt_ref[...], p.astype(BF16))

    def scores(idx, slot):
        s_buf[slot] = _dot_nt(k_ref[0, idx], qst)

    def softmax(slot, bias):
        s = s_buf[slot]
        m_prev = m_ref[...]
        if bias.ndim == 0:
            m_new = jnp.maximum(m_prev, jnp.max(s, axis=0, keepdims=True) + bias)
            p = jnp.exp2(s - (m_new - bias))
        else:
            s = s + both(bias)
            m_new = jnp.maximum(m_prev, jnp.max(s, axis=0, keepdims=True))
            p = jnp.exp2(s - m_new)
        alpha = jnp.exp2(m_prev - m_new)
        l_ref[...] = alpha * l_ref[...] + jnp.sum(p, axis=0, keepdims=True)
        m_ref[...] = m_new
        p_buf[slot] = p.astype(BF16)
        a_buf[slot] = alpha

    def accumulate(idx, slot):
        acc_ref[...] = a_buf[slot] * acc_ref[...] + _dot(vt_ref[0, idx], p_buf[slot])

    n_far = jnp.maximum(i - 1, 0)
    off = n_far % 2
    p_buf[1] = jnp.zeros(p_buf.shape[1:], BF16)
    a_buf[1] = jnp.ones(a_buf.shape[1:], F32)
    scores(0, 0)

    def far_body(t, carry):
        u = 2 * t - off
        scores(u + 1, 1)
        softmax(0, jnp.where(u >= 0, cf, NEG_INF))
        accumulate(jnp.maximum(u - 1, 0), 1)
        scores(u + 2, 0)
        softmax(1, cf)
        accumulate(jnp.maximum(u, 0), 0)
        return carry

    lax.fori_loop(0, (n_far + 1) // 2, far_body, 0)

    scores(i, 1)
    softmax(0, jnp.where(i >= 1, bm1_ref[0], NEG_INF))
    accumulate(jnp.maximum(n_far - 1, 0), 1)
    softmax(1, b0_ref[0])
    accumulate(n_far, 0)
    accumulate(i, 1)

    accn = acc_ref[...] / l_ref[...]
    o_t = accn[:, 0:tq] - lam * accn[:, tq:]
    o_ref[...] = _sub_layer_norm(jnp.transpose(o_t), g_ref[...], out_scale).astype(BF16)

    @pl.when(i == 0)
    def _():
        qm = _stack_components(qm_ref[...])
        bmm = bmm_ref[0]
        sm = _dot_nt(qm, km_ref[...]) + jnp.concatenate([bmm, bmm], axis=0)
        mm = jnp.max(sm, axis=-1, keepdims=True)
        pm = jnp.exp2(sm - mm)
        accm = _dot(pm.astype(BF16), vm_ref[...]) / jnp.sum(pm, axis=-1, keepdims=True)
        om = accm[0:N_META] - lam * accm[N_META:]
        om_ref[...] = _sub_layer_norm(om, g_ref[...], out_scale).astype(BF16)


def _attn_prompt(cfar, lam, q_f, kb_t, vt_t, kb_m, vb_m, vbt_m, q_m, b0, bm1, bq0, bmm, subln_g,
                 out_scale):
    tf = q_f.shape[0]
    tq = ATT_T
    assert tf % tq == 0
    n_t = tf // tq
    smem = pl.BlockSpec(memory_space=pltpu.SMEM)
    head_col = lambda rows: pl.BlockSpec((rows, LANES), lambda h, i: (0, h))
    head_tile = lambda a, b: pl.BlockSpec((1, a, b), lambda h, i: (h, 0, 0))
    return pl.pallas_call(
        functools.partial(_attn_prompt_kernel, out_scale=out_scale),
        grid=(DA_HEADS, n_t),
        in_specs=[smem, smem,
                  pl.BlockSpec((tq, LANES), lambda h, i: (i, h)),
                  pl.BlockSpec((1, n_t, tq, LANES), lambda h, i: (h, 0, 0, 0)),
                  pl.BlockSpec((1, n_t, LANES, tq), lambda h, i: (h, 0, 0, 0)),
                  head_col(N_META), head_col(N_META),
                  pl.BlockSpec((LANES, N_META), lambda h, i: (h, 0)),
                  head_col(N_META),
                  head_tile(tq, tq), head_tile(tq, tq), head_tile(N_META, tq),
                  head_tile(N_META, N_META), _const_spec((1, LANES))],
        out_specs=[pl.BlockSpec((tq, LANES), lambda h, i: (i, h)), head_col(N_META)],
        out_shape=[jax.ShapeDtypeStruct((tf, DA_WIDTH), BF16),
                   jax.ShapeDtypeStruct((N_META, DA_WIDTH), BF16)],
        scratch_shapes=[pltpu.VMEM((LANES, 2 * tq), F32), pltpu.VMEM((1, 2 * tq), F32),
                        pltpu.VMEM((1, 2 * tq), F32), pltpu.VMEM((2, tq, 2 * tq), F32),
                        pltpu.VMEM((2, tq, 2 * tq), BF16), pltpu.VMEM((2, 1, 2 * tq), F32)],
        compiler_params=_cparams(("arbitrary", "arbitrary")),
        name="attn_prompt",
    )(cfar, lam, q_f, kb_t, vt_t, kb_m, vb_m, vbt_m, q_m, b0, bm1, bq0, bmm, subln_g)


def _attn_decode_kernel(lam_ref, q_ref, k_hbm, v_hbm, kn_ref, vn_ref, b_ref, g_ref, o_ref,
                        kbuf, vbuf, sem, *, layer, n_main, out_scale):
    bi = pl.program_id(0)
    h = pl.program_id(1)
    n_h = pl.num_programs(1)
    step = bi * n_h + h
    slot = step % 2

    def cache_copies(b_, h_, slot_):
        return (pltpu.make_async_copy(k_hbm.at[layer, b_, :, h_, :], kbuf.at[slot_], sem.at[0, slot_]),
                pltpu.make_async_copy(v_hbm.at[layer, b_, :, h_, :], vbuf.at[slot_], sem.at[1, slot_]))

    @pl.when(step == 0)
    def _():
        for cp in cache_copies(bi, h, slot):
            cp.start()

    @pl.when(step + 1 < pl.num_programs(0) * n_h)
    def _():
        nxt = step + 1
        for cp in cache_copies(nxt // n_h, nxt % n_h, 1 - slot):
            cp.start()

    for cp in cache_copies(bi, h, slot):
        cp.wait()

    lam = lam_ref[0]
    nq = q_ref.shape[0]
    n_cache = kbuf.shape[1]
    k_ref = kbuf.at[slot]
    v_ref = vbuf.at[slot]
    qst = _stack_components(q_ref[...])
    k_main = k_ref[0:n_main, :].astype(BF16)
    v_main = v_ref[0:n_main, :].astype(BF16)
    k_tail = jnp.concatenate([k_ref[n_main:n_cache, :], kn_ref[...]], axis=0).astype(BF16)
    v_tail = jnp.concatenate([v_ref[n_main:n_cache, :], vn_ref[...]], axis=0).astype(BF16)
    bias = jnp.concatenate([b_ref[0], b_ref[0]], axis=0)
    s1 = _dot_nt(qst, k_main) + bias[:, 0:n_main]
    s2 = _dot_nt(qst, k_tail) + bias[:, n_main:]
    m = jnp.maximum(jnp.max(s1, axis=-1, keepdims=True), jnp.max(s2, axis=-1, keepdims=True))
    p1 = jnp.exp2(s1 - m)
    p2 = jnp.exp2(s2 - m)
    l = jnp.sum(p1, axis=-1, keepdims=True) + jnp.sum(p2, axis=-1, keepdims=True)
    res = (_dot(p1.astype(BF16), v_main) + _dot(p2.astype(BF16), v_tail)) / l
    o = res[0:nq] - lam * res[nq:]
    o_ref[...] = _sub_layer_norm(o, g_ref[...], out_scale).astype(BF16)


def _attn_decode(lam, q_s, cache_k, cache_v, layer, k_s, v_s, bias, subln_g, out_scale, nq):
    _, b, n_cache, n_h, _ = cache_k.shape
    n_main = (n_cache // LANES) * LANES
    n_keys = n_cache + nq
    smem = pl.BlockSpec(memory_space=pltpu.SMEM)
    hbm = pl.BlockSpec(memory_space=pl.ANY)
    new = pl.BlockSpec((nq, LANES), lambda bi, h: (bi, h))
    return pl.pallas_call(
        functools.partial(_attn_decode_kernel, layer=layer, n_main=n_main, out_scale=out_scale),
        grid=(b, n_h),
        in_specs=[smem, new, hbm, hbm, new, new,
                  pl.BlockSpec((1, nq, n_keys), lambda bi, h: (h, 0, 0)), _const_spec((1, LANES))],
        out_specs=new,
        out_shape=jax.ShapeDtypeStruct((b * nq, DA_WIDTH), BF16),
        scratch_shapes=[pltpu.VMEM((2, n_cache, LANES), F32), pltpu.VMEM((2, n_cache, LANES), F32),
                        pltpu.SemaphoreType.DMA((2, 2))],
        compiler_params=_cparams(("arbitrary", "arbitrary")),
        name="attn_decode",
    )(lam, q_s, cache_k, cache_v, k_s, v_s, bias, subln_g)


def _sigmoid(x):
    return 1.0 / (1.0 + jnp.exp(-x))


def _softplus(x):
    return jnp.maximum(x, 0.0) + jnp.log(1.0 + jnp.exp(-jnp.abs(x)))


def _split3(x):
    x1 = x.astype(BF16)
    r1 = x - x1.astype(F32)
    x2 = r1.astype(BF16)
    x3 = (r1 - x2.astype(F32)).astype(BF16)
    return x1, x2, x3


def _stack_heads(x):
    lo = lax.broadcasted_iota(jnp.int32, x.shape, 1) < HEAD_DIM
    zero = jnp.zeros_like(x)
    return jnp.concatenate([jnp.where(lo, x, zero), jnp.where(lo, zero, x)], axis=0)


def _rwkv_kernel(pr_ref, sh0_ref, h0_ref, mu_ref, w0_ref, a0_ref, kk_ref, ka_ref, rk_ref,
                 lw_ref, lb_ref, w2_ref, a2_ref, g2_ref, gsum_ref, tri_ref,
                 o_ref, hn_ref,
                 xbuf, carry, at_s, rt_s, bt_s, kt_s, v_s, ep_s, y_s,
                 t_s, tav_s, lrb_s, lrkv_s, zbt_s, zkv_s, ar_s, pc_s,
                 *, tr, t_valid):
    c_len = RW_CHUNK
    n_ch = tr // c_len
    ti = pl.program_id(1)

    @pl.when(ti == 0)
    def _():
        carry[...] = sh0_ref[0]
        hn_ref[...] = h0_ref[...]

    x = pr_ref[0]
    xbuf[8:8 + tr, :] = x
    xbuf[7:8, :] = carry[...]
    prev = xbuf[7:7 + tr, :]
    carry[...] = pr_ref[0, tr - 1:tr, :]
    xm = x + (prev - x) * mu_ref[...]
    r = xm[:, 0:RW_WIDTH]
    kr = xm[:, RW_WIDTH:2 * RW_WIDTH]
    vr = xm[:, 2 * RW_WIDTH:3 * RW_WIDTH]
    wa = xm[:, 3 * RW_WIDTH:3 * RW_WIDTH + W_LORA + A_LORA]
    gd = xm[:, 3 * RW_WIDTH + W_LORA + A_LORA:]
    lane_wa = lax.broadcasted_iota(jnp.int32, wa.shape, 1)
    twa = jnp.where(lane_wa < W_LORA, jnp.tanh(wa), wa).astype(BF16)
    w_log = -_softplus(-(w0_ref[...] + _dot(twa, w2_ref[...]))) - 0.5
    logw = -jnp.exp(w_log)
    a_sig = _sigmoid(a0_ref[...] + _dot(twa, a2_ref[...]))
    g = _dot(_sigmoid(gd).astype(BF16), g2_ref[...])
    gsum = gsum_ref[...]
    kk = kr * kk_ref[...]
    kk_ss = _dot((kk * kk).astype(BF16), gsum)
    kk = kk / jnp.maximum(jnp.sqrt(kk_ss), 1e-12)
    kr2 = kr * (1.0 + (a_sig - 1.0) * ka_ref[...])
    a_vec = -kk
    b_vec = kk * a_sig
    bonus = _dot((r * kr2 * rk_ref[...]).astype(BF16), gsum) * vr
    if t_valid % tr != 0:
        row = lax.broadcasted_iota(jnp.int32, (tr, 1), 0) + ti * tr
        valid = row < t_valid
        logw = jnp.where(valid, logw, 0.0)
        a_vec = jnp.where(valid, a_vec, 0.0)
        b_vec = jnp.where(valid, b_vec, 0.0)
        kr2 = jnp.where(valid, kr2, 0.0)
        vr = jnp.where(valid, vr, 0.0)
        bonus = jnp.where(valid, bonus, 0.0)
    l1, l2, l3 = _split3(logw)
    tri = tri_ref[...]
    cs = _dot(tri, l1) + _dot(tri, l2) + _dot(tri, l3)
    e_pos = jnp.exp(cs)
    e_neg = jnp.exp(-cs)
    at_s[...] = a_vec * jnp.exp(cs - logw)
    rt_s[...] = r * e_pos
    bt_s[...] = b_vec * e_neg
    kt_s[...] = kr2 * e_neg
    v_s[...] = vr
    ep_s[...] = e_pos

    idx_r = lax.broadcasted_iota(jnp.int32, (PAIR, PAIR), 0)
    idx_c = lax.broadcasted_iota(jnp.int32, (PAIR, PAIR), 1)
    same = (idx_r // c_len) == (idx_c // c_len)
    strict = same & ((idx_r % c_len) > (idx_c % c_len))
    incl = same & ((idx_r % c_len) >= (idx_c % c_len))
    eye = idx_r == idx_c
    eye_f = jnp.where(eye, 1.0, 0.0).astype(F32)

    pairs = range(N_PAIR)
    zero = jnp.zeros((PAIR, PAIR), F32)

    def phase1(c, carry_):
        rows = pl.ds(pl.multiple_of(c * c_len, c_len), c_len)
        cols = [slice(PAIR * p, PAIR * (p + 1)) for p in pairs]
        at = [at_s[rows, cl] for cl in cols]
        rt = [rt_s[rows, cl] for cl in cols]
        yb = [_stack_heads(bt_s[rows, cl]) for cl in cols]
        yk = [_stack_heads(kt_s[rows, cl]) for cl in cols]
        vst = [_stack_heads(v_s[rows, cl]).astype(BF16) for cl in cols]
        pc = [ep_s[rows, cl][c_len - 1:c_len] for cl in cols]
        gmat = [_dot_nt(jnp.concatenate([_stack_heads(at[p]), _stack_heads(rt[p])], axis=0).astype(BF16),
                        jnp.concatenate([yb[p], yk[p]], axis=0).astype(BF16)) for p in pairs]
        aab = [jnp.where(strict, gmat[p][0:PAIR, 0:PAIR], zero) for p in pairs]
        aak = [jnp.where(strict, gmat[p][0:PAIR, PAIR:], zero).astype(BF16) for p in pairs]
        lrb = [jnp.where(incl, gmat[p][PAIR:, 0:PAIR], zero).astype(BF16) for p in pairs]
        lrk = [jnp.where(incl, gmat[p][PAIR:, PAIR:], zero).astype(BF16) for p in pairs]
        tinv = [eye_f + aab[p] for p in pairs]
        lp = aab
        n = 1
        while 2 * n < c_len:
            lpb = [x.astype(BF16) for x in lp]
            lp = [_dot(x, x) for x in lpb]
            tinv = [tinv[p] + _dot(tinv[p].astype(BF16), lp[p].astype(BF16)) for p in pairs]
            n *= 2
        tb = [x.astype(BF16) for x in tinv]
        av = [_dot(aak[p], vst[p]).astype(BF16) for p in pairs]
        tav = [_dot(tb[p], av[p]) for p in pairs]
        lrkv = [_dot(lrk[p], vst[p]) for p in pairs]
        zbt = [jnp.transpose(yb[p] * pc[p]).astype(BF16) for p in pairs]
        zkv = [_dot(jnp.transpose(yk[p] * pc[p]).astype(BF16), vst[p]) for p in pairs]
        pcc = [jnp.sum(jnp.where(eye, jnp.broadcast_to(pc[p], (PAIR, PAIR)), zero), axis=-1, keepdims=True)
               for p in pairs]
        for p in pairs:
            t_s[c, p] = tb[p]
            tav_s[c, p] = tav[p]
            lrb_s[c, p] = lrb[p]
            lrkv_s[c, p] = lrkv[p]
            zbt_s[c, p] = zbt[p]
            zkv_s[c, p] = zkv[p]
            ar_s[c, p] = jnp.concatenate([at[p], rt[p]], axis=0).astype(BF16)
            pc_s[c, p] = pcc[p]
        return carry_

    lax.fori_loop(0, n_ch, phase1, 0)

    def phase2(c, carry_):
        rows = pl.ds(pl.multiple_of(c * c_len, c_len), c_len)
        hbd = [hn_ref[0, p] for p in pairs]
        ar = [ar_s[c, p] for p in pairs]
        tb = [t_s[c, p] for p in pairs]
        tav = [tav_s[c, p] for p in pairs]
        lrb = [lrb_s[c, p] for p in pairs]
        lrkv = [lrkv_s[c, p] for p in pairs]
        zbt = [zbt_s[c, p] for p in pairs]
        zkv = [zkv_s[c, p] for p in pairs]
        pcc = [pc_s[c, p] for p in pairs]
        arh = [_dot(ar[p], hbd[p].astype(BF16)) for p in pairs]
        ub = [(_dot(tb[p], _stack_heads(arh[p][0:c_len]).astype(BF16)) + tav[p]).astype(BF16)
              for p in pairs]
        yst = [_dot(lrb[p], ub[p]) + lrkv[p] for p in pairs]
        hn = [pcc[p] * hbd[p] + _dot(zbt[p], ub[p]) + zkv[p] for p in pairs]
        for p in pairs:
            y_s[rows, PAIR * p:PAIR * (p + 1)] = arh[p][c_len:] + yst[p][0:c_len] + yst[p][c_len:]
            hn_ref[0, p] = hn[p]
        return carry_

    lax.fori_loop(0, n_ch, phase2, 0)

    y = y_s[...]
    inv_n = 1.0 / HEAD_DIM
    mean = _dot(y.astype(BF16), gsum) * inv_n
    d = y - mean
    var = _dot((d * d).astype(BF16), gsum) * inv_n
    yn = d * lax.rsqrt(var + GN_EPS) * lw_ref[...] + lb_ref[...]
    o_ref[0] = ((yn + bonus) * g).astype(BF16)


def _rwkv(pr, shift0, h0, prm, tr, t_valid):
    b, t_pad, _ = pr.shape
    assert t_pad % tr == 0 and tr % RW_CHUNK == 0
    n_ch = tr // RW_CHUNK
    tri = np.zeros((tr, tr), np.float32)
    for c in range(n_ch):
        tri[c * RW_CHUNK:(c + 1) * RW_CHUNK, c * RW_CHUNK:(c + 1) * RW_CHUNK] = np.tril(
            np.ones((RW_CHUNK, RW_CHUNK), np.float32))
    tri = jnp.asarray(tri, BF16)
    vec = _const_spec((1, RW_WIDTH))
    mat = lambda dt: pltpu.VMEM((n_ch, N_PAIR, PAIR, PAIR), dt)
    tile = lambda: pltpu.VMEM((tr, RW_WIDTH), F32)
    return pl.pallas_call(
        functools.partial(_rwkv_kernel, tr=tr, t_valid=t_valid),
        grid=(b, t_pad // tr),
        in_specs=[pl.BlockSpec((1, tr, RW_PROJ), lambda bi, ti: (bi, ti, 0)),
                  pl.BlockSpec((1, 1, RW_PROJ), lambda bi, ti: (bi, 0, 0)),
                  pl.BlockSpec((1, N_PAIR, PAIR, PAIR), lambda bi, ti: (bi, 0, 0, 0)),
                  _const_spec((1, RW_PROJ)), vec, vec, vec, vec, vec, vec, vec,
                  _const_spec((W_LORA + A_LORA, RW_WIDTH)), _const_spec((W_LORA + A_LORA, RW_WIDTH)),
                  _const_spec((G_LORA, RW_WIDTH)), _const_spec((RW_WIDTH, RW_WIDTH)),
                  _const_spec((tr, tr))],
        out_specs=[pl.BlockSpec((1, tr, RW_WIDTH), lambda bi, ti: (bi, ti, 0)),
                   pl.BlockSpec((1, N_PAIR, PAIR, PAIR), lambda bi, ti: (bi, 0, 0, 0))],
        out_shape=[jax.ShapeDtypeStruct((b, t_pad, RW_WIDTH), BF16),
                   jax.ShapeDtypeStruct((b, N_PAIR, PAIR, PAIR), F32)],
        scratch_shapes=[pltpu.VMEM((8 + tr, RW_PROJ), F32), pltpu.VMEM((1, RW_PROJ), F32),
                        tile(), tile(), tile(), tile(), tile(), tile(), tile(),
                        mat(BF16), mat(F32), mat(BF16), mat(F32), mat(BF16), mat(F32), mat(BF16),
                        pltpu.VMEM((n_ch, N_PAIR, PAIR, 1), F32)],
        compiler_params=_cparams(("arbitrary", "arbitrary")),
        name="rwkv",
    )(pr, shift0, h0, prm["mu"], prm["w0"], prm["a0"], prm["k_k"], prm["k_a"], prm["r_k"],
      prm["lnx_w"], prm["lnx_b"], prm["w2p"], prm["a2p"], prm["g2"], prm["gsum"], tri)


FF_COLS = 256


def _ffn_kernel(x_ref, oa_ref, orw_ref, c0_ref, wo_ref, g_ref, wu_ref, cw_ref, cb_ref, wd_ref,
                y_ref, cn_ref, zbuf, cbuf, acc_ref, h_ref, x1_ref, *, tm, ts, d_ff):
    off = zbuf.shape[0] - tm
    da = oa_ref.shape[1]

    @pl.when(pl.program_id(0) == 0)
    def _():
        cbuf[...] = c0_ref[...]

    x1 = x_ref[...] + _dot(oa_ref[...], wo_ref[0:da, :]) + _dot(orw_ref[...], wo_ref[da:, :])
    x1_ref[...] = x1
    ms = jnp.mean(x1 * x1, axis=-1, keepdims=True)
    h_ref[...] = (x1 * lax.rsqrt(ms + NORM_EPS) * g_ref[...]).astype(BF16)
    acc_ref[...] = jnp.zeros_like(acc_ref)

    def conv_cols(c0):
        cols = slice(c0, c0 + FF_COLS)
        z = _dot(h_ref[...], wu_ref[:, cols])
        zbuf[off - 2 * ts:off, :] = cbuf[:, cols]
        zbuf[off:off + tm, :] = z
        cbuf[:, cols] = zbuf[off + tm - 2 * ts:off + tm, :]
        z2 = zbuf[off - 2 * ts:off - 2 * ts + tm, :]
        z1 = zbuf[off - ts:off - ts + tm, :]
        return (cb_ref[:, cols] + z2 * cw_ref[0:1, cols] + z1 * cw_ref[1:2, cols]
                + z * cw_ref[2:3, cols])

    for c in range(d_ff // FF_COLS):
        gate = conv_cols(c * FF_COLS)
        up = conv_cols(d_ff + c * FF_COLS)
        act = (gate * _sigmoid(gate) * up).astype(BF16)
        acc_ref[...] += _dot(act, wd_ref[c * FF_COLS:(c + 1) * FF_COLS, :])

    y_ref[...] = x1_ref[...] + acc_ref[...]
    cn_ref[...] = cbuf[...]


def _ffn(x, oa, orw, conv0, prm, tm, ts):
    rows, d = x.shape
    d_ff = prm["w_down"].shape[0]
    assert rows % tm == 0 and d_ff % FF_COLS == 0 and (ts == 1 or ts % 8 == 0)
    off = -(-2 * ts // 8) * 8
    row = lambda w: pl.BlockSpec((tm, w), lambda i: (i, 0))
    return pl.pallas_call(
        functools.partial(_ffn_kernel, tm=tm, ts=ts, d_ff=d_ff),
        grid=(rows // tm,),
        in_specs=[row(d), row(DA_WIDTH), row(RW_WIDTH), _const_spec((2 * ts, 2 * d_ff)),
                  _resident_spec((DA_WIDTH + RW_WIDTH, d)), _const_spec((1, d)),
                  _resident_spec((d, 2 * d_ff)), _const_spec((CONV_W, 2 * d_ff)),
                  _const_spec((1, 2 * d_ff)), _resident_spec((d_ff, d))],
        out_specs=[row(d), _const_spec((2 * ts, 2 * d_ff))],
        out_shape=[jax.ShapeDtypeStruct((rows, d), F32),
                   jax.ShapeDtypeStruct((2 * ts, 2 * d_ff), F32)],
        scratch_shapes=[pltpu.VMEM((off + tm, FF_COLS), F32), pltpu.VMEM((2 * ts, 2 * d_ff), F32),
                        pltpu.VMEM((tm, d), F32), pltpu.VMEM((tm, d), BF16), pltpu.VMEM((tm, d), F32)],
        compiler_params=_cparams(("arbitrary",)),
        name="ffn",
    )(x, oa, orw, conv0, prm["w_out"], prm["ln2_g"], prm["w_up"], prm["conv_w"], prm["conv_b"],
      prm["w_down"])


def _rel_bucket(rel):
    nb = N_BUCKETS // 2
    max_exact = nb // 2
    bucket = jnp.where(rel > 0, nb, 0)
    n = jnp.abs(rel)
    nf = jnp.maximum(n, 1).astype(F32)
    large = max_exact + (jnp.log(nf / max_exact) / math.log(MAX_DISTANCE / max_exact)
                         * (nb - max_exact)).astype(jnp.int32)
    large = jnp.minimum(large, nb - 1)
    return bucket + jnp.where(n < max_exact, n, large)


def _bias_table(rel_bias, q_pos, k_pos, mask):
    rel = jnp.asarray(k_pos[None, :] - q_pos[:, None], jnp.int32)
    onehot = _rel_bucket(rel)[None, :, :, None] == jnp.arange(N_BUCKETS, dtype=jnp.int32)
    table = jnp.transpose(rel_bias).astype(F32)[:, None, None, :]
    bias = jnp.sum(jnp.where(onehot, table, 0.0), axis=-1) * LOG2E
    return jnp.where(jnp.asarray(mask)[None], bias, NEG_INF)


def _ext_chunk(pos):
    return np.where(pos < N_META, -1, (pos - N_META) // CHUNK)


def _prompt_bias(rel_bias):
    tq = ATT_T
    fr = np.arange(tq) + N_META
    meta = np.arange(N_META)
    causal = _ext_chunk(fr)[None, :] <= _ext_chunk(fr)[:, None]
    tr = lambda b: jnp.swapaxes(b, 1, 2)
    b0 = tr(_bias_table(rel_bias, fr, fr, causal))
    bm1 = tr(_bias_table(rel_bias, fr + tq, fr, np.ones((tq, tq), bool)))
    bq0 = tr(_bias_table(rel_bias, fr, meta, np.ones((tq, N_META), bool)))
    bmm = _bias_table(rel_bias, meta, meta, np.ones((N_META, N_META), bool))
    assert tq + 1 >= MAX_DISTANCE
    cfar = rel_bias[_rel_bucket(jnp.asarray(-(tq + 1), jnp.int32))].astype(F32) * LOG2E
    return cfar, b0, bm1, bq0, bmm


def _decode_bias(rel_bias, n_cache, nq):
    k_pos = np.arange(n_cache + nq)
    q_pos = k_pos[n_cache:]
    mask = _ext_chunk(k_pos)[None, :] <= _ext_chunk(q_pos)[:, None]
    return _bias_table(rel_bias, q_pos, k_pos, mask)


def _state_to_pairs(s):
    b = s.shape[0]
    st = jnp.swapaxes(s, -1, -2).reshape(b, N_PAIR, 2, HEAD_DIM, HEAD_DIM)
    eye2 = jnp.eye(2, dtype=s.dtype)
    return jnp.einsum('bphkv,hg->bphkgv', st, eye2).reshape(b, N_PAIR, PAIR, PAIR)


def _pairs_to_state(hp):
    b = hp.shape[0]
    h6 = hp.reshape(b, N_PAIR, 2, HEAD_DIM, 2, HEAD_DIM)
    diag = jnp.stack([h6[:, :, 0, :, 0, :], h6[:, :, 1, :, 1, :]], axis=2)
    return jnp.swapaxes(diag, -1, -2).reshape(b, RW_HEADS, HEAD_DIM, HEAD_DIM)


def _block_ones(n, blk, dtype):
    idx = np.arange(n) // blk
    return jnp.asarray((idx[:, None] == idx[None, :]).astype(np.float32), dtype)


def kernel(x_prompt, x_sample, cache_k, cache_v, state_rwkv, state_shift, state_conv, meta_tokens,
           rel_bias, ln1_g, w_in, q_norm_g, k_norm_g, lam_q1, lam_k1, lam_q2, lam_k2, subln_g,
           mu_shift, w0, w2, a0, a2, g2, k_k, k_a, r_k, lnx_w, lnx_b, w_out, ln2_g, w_up, conv_w,
           conv_b, w_down):
    bp, seq, d = x_prompt.shape
    db, dt, _ = x_sample.shape
    depth = w_in.shape[0]
    d_ff = w_down.shape[1]
    n_cache = cache_k.shape[2]
    assert bp == 1 and dt == N_META, "the meta stream rides with the decode streams"
    assert cache_k.shape[3] == DA_HEADS and cache_k.shape[4] == 2 * HEAD_DIM
    nb = db + 1
    nb_pad = -(-nb // 8) * 8

    cfar, b0, bm1, bq0, bmm = _prompt_bias(rel_bias)
    bias_dec = _decode_bias(rel_bias, n_cache, dt)
    gmean = _block_ones(DA_WIDTH, HEAD_DIM, BF16) * (1.0 / HEAD_DIM)
    gsum = _block_ones(RW_WIDTH, HEAD_DIM, BF16)
    zrow = lambda n: jnp.zeros((n, RW_WIDTH), BF16)

    x_f = x_prompt[0]
    x_s = jnp.concatenate([x_sample, meta_tokens.astype(x_sample.dtype)[None]], axis=0)
    outs = [[] for _ in range(10)]
    for l in range(depth):
        lam_init = 0.8 - 0.6 * math.exp(-0.3 * l)
        lam = (jnp.exp(jnp.sum(lam_q1[l].astype(F32) * lam_k1[l].astype(F32)))
               - jnp.exp(jnp.sum(lam_q2[l].astype(F32) * lam_k2[l].astype(F32))) + lam_init).reshape(1)
        out_scale = 1.0 - lam_init
        tile128 = lambda g_: jnp.tile(g_.reshape(1, -1), (1, DA_WIDTH // g_.shape[-1]))
        qg, kg = tile128(q_norm_g[l]), tile128(k_norm_g[l])
        sg = subln_g[l].reshape(1, LANES)
        w_in_bf = w_in[l].astype(BF16)
        rw = {
            "mu": mu_shift[l].reshape(1, -1), "w0": w0[l].reshape(1, -1), "a0": a0[l].reshape(1, -1),
            "k_k": k_k[l].reshape(1, -1), "k_a": k_a[l].reshape(1, -1), "r_k": r_k[l].reshape(1, -1),
            "lnx_w": lnx_w[l].reshape(1, -1), "lnx_b": lnx_b[l].reshape(1, -1),
            "w2p": jnp.concatenate([w2[l].astype(BF16), zrow(A_LORA)], axis=0),
            "a2p": jnp.concatenate([zrow(W_LORA), a2[l].astype(BF16)], axis=0),
            "g2": g2[l].astype(BF16), "gsum": gsum,
        }
        ff = {
            "w_out": w_out[l].astype(BF16), "ln2_g": ln2_g[l].reshape(1, -1),
            "w_up": w_up[l].astype(BF16), "conv_w": conv_w[l], "conv_b": conv_b[l].reshape(1, -1),
            "w_down": w_down[l].astype(BF16),
        }

        q_f, k_f, v_f, pr_f, kb_t, vt_t = _proj(x_f, ln1_g[l].reshape(1, -1), w_in_bf, qg, kg, gmean,
                                                 512, True)
        q_s, k_s, v_s, pr_s = _proj(x_s.reshape(nb * dt, d), ln1_g[l].reshape(1, -1), w_in_bf, qg, kg,
                                    gmean, nb * dt, False)
        m0 = db * dt

        kb_m = k_s[m0:].astype(BF16)
        vb_m = v_s[m0:].astype(BF16)
        o_f, o_m = _attn_prompt(cfar, lam, q_f, kb_t, vt_t, kb_m, vb_m, jnp.transpose(vb_m), q_s[m0:],
                                b0, bm1, bq0, bmm, sg, out_scale)
        o_d = _attn_decode(lam, q_s, cache_k, cache_v, l, k_s, v_s, bias_dec, sg, out_scale, dt)
        o_s = jnp.concatenate([o_d, o_m], axis=0)

        pr_s3 = pr_s.reshape(nb, dt, RW_PROJ)
        pr_pad = jnp.pad(pr_s3, ((0, 0), (0, RW_CHUNK - dt), (0, 0)))
        shift_s = jnp.concatenate([state_shift[l], jnp.zeros((1, RW_PROJ), F32)], axis=0)[:, None, :]
        h_s = _state_to_pairs(jnp.concatenate(
            [state_rwkv[l], jnp.zeros((1,) + state_rwkv.shape[2:], F32)], axis=0))
        orw_s, hn_s = _rwkv(pr_pad, shift_s, h_s, rw, RW_CHUNK, dt)
        orw_f, hn_f = _rwkv(pr_f[None], pr_s3[db:, dt - 1:dt, :], hn_s[db:], rw, 512, seq)

        def time_major(a):
            a = jnp.pad(a.reshape(nb, dt, -1), ((0, nb_pad - nb), (0, 0), (0, 0)))
            return jnp.swapaxes(a, 0, 1).reshape(dt * nb_pad, -1)

        conv_s = jnp.concatenate([state_conv[l], jnp.zeros((1, CONV_W - 1, 2 * d_ff), F32)], axis=0)
        conv_s = jnp.pad(conv_s, ((0, nb_pad - nb), (0, 0), (0, 0)))
        conv_s = jnp.swapaxes(conv_s, 0, 1).reshape(2 * nb_pad, 2 * d_ff)
        y_s, cn_s = _ffn(time_major(x_s), time_major(o_s), time_major(orw_s[:, :dt]), conv_s, ff,
                         dt * nb_pad, nb_pad)
        cn_s = jnp.swapaxes(cn_s.reshape(2, nb_pad, 2 * d_ff), 0, 1)
        y_f, cn_f = _ffn(x_f, o_f, orw_f[0], cn_s[db], ff, 512, 1)
        y_s = jnp.swapaxes(y_s.reshape(dt, nb_pad, d), 0, 1)[:nb]

        hw = (DA_HEADS, 2 * HEAD_DIM)
        outs[0].append(jnp.concatenate([k_s[m0:], k_f], axis=0).reshape(bp, N_META + seq, *hw))
        outs[1].append(jnp.concatenate([v_s[m0:], v_f], axis=0).reshape(bp, N_META + seq, *hw))
        outs[2].append(_pairs_to_state(hn_f))
        outs[3].append(pr_f[seq - 1:seq])
        outs[4].append(cn_f[None])
        outs[5].append(k_s[:m0].reshape(db, dt, *hw))
        outs[6].append(v_s[:m0].reshape(db, dt, *hw))
        outs[7].append(_pairs_to_state(hn_s[:db]))
        outs[8].append(pr_s3[:db, dt - 1])
        outs[9].append(cn_s[:db])
        x_f, x_s = y_f, y_s

    return (x_f[None], x_s[:db], *[jnp.stack(o) for o in outs])
```

```python
import functools
import math

import numpy as np
import jax
import jax.numpy as jnp
from jax import lax
from jax.experimental import pallas as pl
from jax.experimental.pallas import tpu as pltpu

F32 = jnp.float32
BF16 = jnp.bfloat16

CHUNK = 64
N_META = 16
HEAD_DIM = 64
DA_HEADS = 4
RW_HEADS = 8
W_LORA = 64
A_LORA = 64
G_LORA = 128
CONV_W = 3
N_BUCKETS = 32
MAX_DISTANCE = 128
NORM_EPS = 1e-6
GN_EPS = 64e-5
NEG_INF = -1e30
LOG2E = math.log2(math.e)

DA_WIDTH = DA_HEADS * 2 * HEAD_DIM
RW_WIDTH = RW_HEADS * HEAD_DIM
RW_PROJ = 3 * RW_WIDTH + W_LORA + A_LORA + G_LORA
LANES = 128
PAIR = 2 * HEAD_DIM
N_PAIR = RW_WIDTH // PAIR
RW_CHUNK = 64
ATT_T = 512
VMEM_LIMIT = 56 * 1024 * 1024


def _dot(a, b):
    return jnp.dot(a, b, preferred_element_type=F32)


def _dot_nt(a, b):
    return lax.dot_general(a, b, (((1,), (1,)), ((), ())), preferred_element_type=F32)


def _cparams(sem):
    return pltpu.CompilerParams(dimension_semantics=sem, vmem_limit_bytes=VMEM_LIMIT)


def _const_spec(shape):
    nd = len(shape)
    return pl.BlockSpec(shape, lambda *_: (0,) * nd)


def _resident_spec(shape):
    nd = len(shape)
    return pl.BlockSpec(shape, lambda *_: (0,) * nd, pipeline_mode=pl.Buffered(1))


def _proj_kernel(x_ref, g_ref, w_ref, qg_ref, kg_ref, gm_ref,
                 q_ref, k_ref, v_ref, pr_ref, *tile_refs, tm):
    x = x_ref[...]
    ms = jnp.mean(x * x, axis=-1, keepdims=True)
    h = (x * lax.rsqrt(ms + NORM_EPS) * g_ref[...]).astype(BF16)
    gm = gm_ref[...]

    def group_norm(t, g):
        ms_g = _dot((t * t).astype(BF16), gm)
        return t * lax.rsqrt(ms_g + NORM_EPS) * g

    q = _dot(h, w_ref[:, 0:DA_WIDTH])
    q_ref[...] = (group_norm(q, qg_ref[...]) * (HEAD_DIM ** -0.5 * LOG2E)).astype(BF16)
    k = group_norm(_dot(h, w_ref[:, DA_WIDTH:2 * DA_WIDTH]), kg_ref[...])
    k_ref[...] = k
    v = _dot(h, w_ref[:, 2 * DA_WIDTH:3 * DA_WIDTH])
    v_ref[...] = v
    pr_ref[...] = _dot(h, w_ref[:, 3 * DA_WIDTH:])
    if tile_refs:
        kb_ref, vt_ref = tile_refs
        kb = k.astype(BF16)
        vt = jnp.transpose(v).astype(BF16)
        for hd in range(DA_HEADS):
            for jj in range(tm // ATT_T):
                kb_ref[hd, jj] = kb[jj * ATT_T:(jj + 1) * ATT_T, hd * LANES:(hd + 1) * LANES]
                vt_ref[hd, jj] = vt[hd * LANES:(hd + 1) * LANES, jj * ATT_T:(jj + 1) * ATT_T]


def _proj(x, ln1_g, w_in_bf, qg, kg, gmean, tm, emit_tiles):
    rows, d = x.shape
    n_in = w_in_bf.shape[1]
    assert rows % tm == 0
    row = lambda w: pl.BlockSpec((tm, w), lambda i: (i, 0))
    out_specs = [row(DA_WIDTH), row(DA_WIDTH), row(DA_WIDTH), row(RW_PROJ)]
    out_shape = [jax.ShapeDtypeStruct((rows, DA_WIDTH), BF16),
                 jax.ShapeDtypeStruct((rows, DA_WIDTH), F32),
                 jax.ShapeDtypeStruct((rows, DA_WIDTH), F32),
                 jax.ShapeDtypeStruct((rows, RW_PROJ), F32)]
    if emit_tiles:
        assert tm % ATT_T == 0
        tpt = tm // ATT_T
        out_specs += [pl.BlockSpec((DA_HEADS, tpt, ATT_T, LANES), lambda i: (0, i, 0, 0)),
                      pl.BlockSpec((DA_HEADS, tpt, LANES, ATT_T), lambda i: (0, i, 0, 0))]
        out_shape += [jax.ShapeDtypeStruct((DA_HEADS, rows // ATT_T, ATT_T, LANES), BF16),
                      jax.ShapeDtypeStruct((DA_HEADS, rows // ATT_T, LANES, ATT_T), BF16)]
    return pl.pallas_call(
        functools.partial(_proj_kernel, tm=tm),
        grid=(rows // tm,),
        in_specs=[row(d), _const_spec((1, d)), _resident_spec((d, n_in)),
                  _const_spec((1, DA_WIDTH)), _const_spec((1, DA_WIDTH)),
                  _resident_spec((DA_WIDTH, DA_WIDTH))],
        out_specs=out_specs,
        out_shape=out_shape,
        compiler_params=_cparams(("arbitrary",)),
        name="proj",
    )(x, ln1_g, w_in_bf, qg, kg, gmean)


def _stack_components(q):
    lo = lax.broadcasted_iota(jnp.int32, q.shape, 1) < HEAD_DIM
    zero = jnp.zeros_like(q)
    return jnp.concatenate([jnp.where(lo, q, zero), jnp.where(lo, zero, q)], axis=0)


def _sub_layer_norm(o, g, out_scale):
    ms = jnp.mean(o * o, axis=-1, keepdims=True)
    return o * lax.rsqrt(ms + NORM_EPS) * g * out_scale


def _attn_prompt_kernel(cfar_ref, lam_ref, q_ref, k_ref, vt_ref, km_ref, vm_ref, vmt_ref, qm_ref,
                        b0_ref, bm1_ref, bq0_ref, bmm_ref, g_ref, o_ref, om_ref,
                        acc_ref, m_ref, l_ref, s_buf, p_buf, a_buf, *, out_scale):
    h = pl.program_id(0)
    i = pl.program_id(1)
    tq = ATT_T
    cf = cfar_ref[h]
    lam = lam_ref[0]
    qst = _stack_components(q_ref[...])

    def both(b):
        return jnp.concatenate([b, b], axis=1)

    s = _dot_nt(km_ref[...], qst)
    s = s + both(jnp.where(i == 0, bq0_ref[0], cf))
    m0 = jnp.max(s, axis=0, keepdims=True)
    p = jnp.exp2(s - m0)
    m_ref[...] = m0
    l_ref[...] = jnp.sum(p, axis=0, keepdims=True)
    acc_ref[...] = _dot(vmt_ref[...], p.astype(BF16))

    def scores(idx, slot):
        s_buf[slot] = _dot_nt(k_ref[0, idx], qst)

    def softmax(slot, bias):
        s = s_buf[slot]
        m_prev = m_ref[...]
        if bias.ndim == 0:
            m_new = jnp.maximum(m_prev, jnp.max(s, axis=0, keepdims=True) + bias)
            p = jnp.exp2(s - (m_new - bias))
        else:
            s = s + both(bias)
            m_new = jnp.maximum(m_prev, jnp.max(s, axis=0, keepdims=True))
            p = jnp.exp2(s - m_new)
        alpha = jnp.exp2(m_prev - m_new)
        l_ref[...] = alpha * l_ref[...] + jnp.sum(p, axis=0, keepdims=True)
        m_ref[...] = m_new
        p_buf[slot] = p.astype(BF16)
        a_buf[slot] = alpha

    def accumulate(idx, slot):
        acc_ref[...] = a_buf[slot] * acc_ref[...] + _dot(vt_ref[0, idx], p_buf[slot])

    n_far = jnp.maximum(i - 1, 0)
    off = n_far % 2
    p_buf[1] = jnp.zeros(p_buf.shape[1:], BF16)
    a_buf[1] = jnp.ones(a_buf.shape[1:], F32)
    scores(0, 0)

    def far_body(t, carry):
        u = 2 * t - off
        scores(u + 1, 1)
        softmax(0, jnp.where(u >= 0, cf, NEG_INF))
        accumulate(jnp.maximum(u - 1, 0), 1)
        scores(u + 2, 0)
        softmax(1, cf)
        accumulate(jnp.maximum(u, 0), 0)
        return carry

    lax.fori_loop(0, (n_far + 1) // 2, far_body, 0)

    scores(i, 1)
    softmax(0, jnp.where(i >= 1, bm1_ref[0], NEG_INF))
    accumulate(jnp.maximum(n_far - 1, 0), 1)
    softmax(1, b0_ref[0])
    accumulate(n_far, 0)
    accumulate(i, 1)

    accn = acc_ref[...] / l_ref[...]
    o_t = accn[:, 0:tq] - lam * accn[:, tq:]
    o_ref[...] = _sub_layer_norm(jnp.transpose(o_t), g_ref[...], out_scale).astype(BF16)

    @pl.when(i == 0)
    def _():
        qm = _stack_components(qm_ref[...])
        bmm = bmm_ref[0]
        sm = _dot_nt(qm, km_ref[...]) + jnp.concatenate([bmm, bmm], axis=0)
        mm = jnp.max(sm, axis=-1, keepdims=True)
        pm = jnp.exp2(sm - mm)
        accm = _dot(pm.astype(BF16), vm_ref[...]) / jnp.sum(pm, axis=-1, keepdims=True)
        om = accm[0:N_META] - lam * accm[N_META:]
        om_ref[...] = _sub_layer_norm(om, g_ref[...], out_scale).astype(BF16)


def _attn_prompt(cfar, lam, q_f, kb_t, vt_t, kb_m, vb_m, vbt_m, q_m, b0, bm1, bq0, bmm, subln_g,
                 out_scale):
    tf = q_f.shape[0]
    tq = ATT_T
    assert tf % tq == 0
    n_t = tf // tq
    smem = pl.BlockSpec(memory_space=pltpu.SMEM)
    head_col = lambda rows: pl.BlockSpec((rows, LANES), lambda h, i: (0, h))
    head_tile = lambda a, b: pl.BlockSpec((1, a, b), lambda h, i: (h, 0, 0))
    return pl.pallas_call(
        functools.partial(_attn_prompt_kernel, out_scale=out_scale),
        grid=(DA_HEADS, n_t),
        in_specs=[smem, smem,
                  pl.BlockSpec((tq, LANES), lambda h, i: (i, h)),
                  pl.BlockSpec((1, n_t, tq, LANES), lambda h, i: (h, 0, 0, 0)),
                  pl.BlockSpec((1, n_t, LANES, tq), lambda h, i: (h, 0, 0, 0)),
                  head_col(N_META), head_col(N_META),
                  pl.BlockSpec((LANES, N_META), lambda h, i: (h, 0)),
                  head_col(N_META),
                  head_tile(tq, tq), head_tile(tq, tq), head_tile(N_META, tq),
                  head_tile(N_META, N_META), _const_spec((1, LANES))],
        out_specs=[pl.BlockSpec((tq, LANES), lambda h, i: (i, h)), head_col(N_META)],
        out_shape=[jax.ShapeDtypeStruct((tf, DA_WIDTH), BF16),
                   jax.ShapeDtypeStruct((N_META, DA_WIDTH), BF16)],
        scratch_shapes=[pltpu.VMEM((LANES, 2 * tq), F32), pltpu.VMEM((1, 2 * tq), F32),
                        pltpu.VMEM((1, 2 * tq), F32), pltpu.VMEM((2, tq, 2 * tq), F32),
                        pltpu.VMEM((2, tq, 2 * tq), BF16), pltpu.VMEM((2, 1, 2 * tq), F32)],
        compiler_params=_cparams(("arbitrary", "arbitrary")),
        name="attn_prompt",
    )(cfar, lam, q_f, kb_t, vt_t, kb_m, vb_m, vbt_m, q_m, b0, bm1, bq0, bmm, subln_g)


def _attn_decode_kernel(lam_ref, q_ref, k_hbm, v_hbm, kn_ref, vn_ref, b_ref, g_ref, o_ref,
                        kbuf, vbuf, sem, *, layer, n_main, out_scale):
    bi = pl.program_id(0)
    h = pl.program_id(1)
    n_h = pl.num_programs(1)
    step = bi * n_h + h
    slot = step % 2

    def cache_copies(b_, h_, slot_):
        return (pltpu.make_async_copy(k_hbm.at[layer, b_, :, h_, :], kbuf.at[slot_], sem.at[0, slot_]),
                pltpu.make_async_copy(v_hbm.at[layer, b_, :, h_, :], vbuf.at[slot_], sem.at[1, slot_]))

    @pl.when(step == 0)
    def _():
        for cp in cache_copies(bi, h, slot):
            cp.start()

    @pl.when(step + 1 < pl.num_programs(0) * n_h)
    def _():
        nxt = step + 1
        for cp in cache_copies(nxt // n_h, nxt % n_h, 1 - slot):
            cp.start()

    for cp in cache_copies(bi, h, slot):
        cp.wait()

    lam = lam_ref[0]
    nq = q_ref.shape[0]
    n_cache = kbuf.shape[1]
    k_ref = kbuf.at[slot]
    v_ref = vbuf.at[slot]
    qst = _stack_components(q_ref[...])
    k_main = k_ref[0:n_main, :].astype(BF16)
    v_main = v_ref[0:n_main, :].astype(BF16)
    k_tail = jnp.concatenate([k_ref[n_main:n_cache, :], kn_ref[...]], axis=0).astype(BF16)
    v_tail = jnp.concatenate([v_ref[n_main:n_cache, :], vn_ref[...]], axis=0).astype(BF16)
    bias = jnp.concatenate([b_ref[0], b_ref[0]], axis=0)
    s1 = _dot_nt(qst, k_main) + bias[:, 0:n_main]
    s2 = _dot_nt(qst, k_tail) + bias[:, n_main:]
    m = jnp.maximum(jnp.max(s1, axis=-1, keepdims=True), jnp.max(s2, axis=-1, keepdims=True))
    p1 = jnp.exp2(s1 - m)
    p2 = jnp.exp2(s2 - m)
    l = jnp.sum(p1, axis=-1, keepdims=True) + jnp.sum(p2, axis=-1, keepdims=True)
    res = (_dot(p1.astype(BF16), v_main) + _dot(p2.astype(BF16), v_tail)) / l
    o = res[0:nq] - lam * res[nq:]
    o_ref[...] = _sub_layer_norm(o, g_ref[...], out_scale).astype(BF16)


def _attn_decode(lam, q_s, cache_k, cache_v, layer, k_s, v_s, bias, subln_g, out_scale, nq):
    _, b, n_cache, n_h, _ = cache_k.shape
    n_main = (n_cache // LANES) * LANES
    n_keys = n_cache + nq
    smem = pl.BlockSpec(memory_space=pltpu.SMEM)
    hbm = pl.BlockSpec(memory_space=pl.ANY)
    new = pl.BlockSpec((nq, LANES), lambda bi, h: (bi, h))
    return pl.pallas_call(
        functools.partial(_attn_decode_kernel, layer=layer, n_main=n_main, out_scale=out_scale),
        grid=(b, n_h),
        in_specs=[smem, new, hbm, hbm, new, new,
                  pl.BlockSpec((1, nq, n_keys), lambda bi, h: (h, 0, 0)), _const_spec((1, LANES))],
        out_specs=new,
        out_shape=jax.ShapeDtypeStruct((b * nq, DA_WIDTH), BF16),
        scratch_shapes=[pltpu.VMEM((2, n_cache, LANES), F32), pltpu.VMEM((2, n_cache, LANES), F32),
                        pltpu.SemaphoreType.DMA((2, 2))],
        compiler_params=_cparams(("arbitrary", "arbitrary")),
        name="attn_decode",
    )(lam, q_s, cache_k, cache_v, k_s, v_s, bias, subln_g)


def _sigmoid(x):
    return 1.0 / (1.0 + jnp.exp(-x))


def _softplus(x):
    return jnp.maximum(x, 0.0) + jnp.log(1.0 + jnp.exp(-jnp.abs(x)))


def _split3(x):
    x1 = x.astype(BF16)
    r1 = x - x1.astype(F32)
    x2 = r1.astype(BF16)
    x3 = (r1 - x2.astype(F32)).astype(BF16)
    return x1, x2, x3


def _stack_heads(x):
    lo = lax.broadcasted_iota(jnp.int32, x.shape, 1) < HEAD_DIM
    zero = jnp.zeros_like(x)
    return jnp.concatenate([jnp.where(lo, x, zero), jnp.where(lo, zero, x)], axis=0)


def _rwkv_kernel(pr_ref, sh0_ref, h0_ref, mu_ref, w0_ref, a0_ref, kk_ref, ka_ref, rk_ref,
                 lw_ref, lb_ref, w2_ref, a2_ref, g2_ref, gsum_ref, tri_ref,
                 o_ref, hn_ref,
                 xbuf, carry, at_s, rt_s, bt_s, kt_s, v_s, ep_s, y_s,
                 t_s, tav_s, lrb_s, lrkv_s, zbt_s, zkv_s, ar_s, pc_s,
                 *, tr, t_valid):
    c_len = RW_CHUNK
    n_ch = tr // c_len
    ti = pl.program_id(1)

    @pl.when(ti == 0)
    def _():
        carry[...] = sh0_ref[0]
        hn_ref[...] = h0_ref[...]

    x = pr_ref[0]
    xbuf[8:8 + tr, :] = x
    xbuf[7:8, :] = carry[...]
    prev = xbuf[7:7 + tr, :]
    carry[...] = pr_ref[0, tr - 1:tr, :]
    xm = x + (prev - x) * mu_ref[...]
    r = xm[:, 0:RW_WIDTH]
    kr = xm[:, RW_WIDTH:2 * RW_WIDTH]
    vr = xm[:, 2 * RW_WIDTH:3 * RW_WIDTH]
    wa = xm[:, 3 * RW_WIDTH:3 * RW_WIDTH + W_LORA + A_LORA]
    gd = xm[:, 3 * RW_WIDTH + W_LORA + A_LORA:]
    lane_wa = lax.broadcasted_iota(jnp.int32, wa.shape, 1)
    twa = jnp.where(lane_wa < W_LORA, jnp.tanh(wa), wa).astype(BF16)
    w_log = -_softplus(-(w0_ref[...] + _dot(twa, w2_ref[...]))) - 0.5
    logw = -jnp.exp(w_log)
    a_sig = _sigmoid(a0_ref[...] + _dot(twa, a2_ref[...]))
    g = _dot(_sigmoid(gd).astype(BF16), g2_ref[...])
    gsum = gsum_ref[...]
    kk = kr * kk_ref[...]
    kk_ss = _dot((kk * kk).astype(BF16), gsum)
    kk = kk / jnp.maximum(jnp.sqrt(kk_ss), 1e-12)
    kr2 = kr * (1.0 + (a_sig - 1.0) * ka_ref[...])
    a_vec = -kk
    b_vec = kk * a_sig
    bonus = _dot((r * kr2 * rk_ref[...]).astype(BF16), gsum) * vr
    if t_valid % tr != 0:
        row = lax.broadcasted_iota(jnp.int32, (tr, 1), 0) + ti * tr
        valid = row < t_valid
        logw = jnp.where(valid, logw, 0.0)
        a_vec = jnp.where(valid, a_vec, 0.0)
        b_vec = jnp.where(valid, b_vec, 0.0)
        kr2 = jnp.where(valid, kr2, 0.0)
        vr = jnp.where(valid, vr, 0.0)
        bonus = jnp.where(valid, bonus, 0.0)
    l1, l2, l3 = _split3(logw)
    tri = tri_ref[...]
    cs = _dot(tri, l1) + _dot(tri, l2) + _dot(tri, l3)
    e_pos = jnp.exp(cs)
    e_neg = jnp.exp(-cs)
    at_s[...] = a_vec * jnp.exp(cs - logw)
    rt_s[...] = r * e_pos
    bt_s[...] = b_vec * e_neg
    kt_s[...] = kr2 * e_neg
    v_s[...] = vr
    ep_s[...] = e_pos

    idx_r = lax.broadcasted_iota(jnp.int32, (PAIR, PAIR), 0)
    idx_c = lax.broadcasted_iota(jnp.int32, (PAIR, PAIR), 1)
    same = (idx_r // c_len) == (idx_c // c_len)
    strict = same & ((idx_r % c_len) > (idx_c % c_len))
    incl = same & ((idx_r % c_len) >= (idx_c % c_len))
    eye = idx_r == idx_c
    eye_f = jnp.where(eye, 1.0, 0.0).astype(F32)

    pairs = range(N_PAIR)
    zero = jnp.zeros((PAIR, PAIR), F32)

    def phase1(c, carry_):
        rows = pl.ds(pl.multiple_of(c * c_len, c_len), c_len)
        cols = [slice(PAIR * p, PAIR * (p + 1)) for p in pairs]
        at = [at_s[rows, cl] for cl in cols]
        rt = [rt_s[rows, cl] for cl in cols]
        yb = [_stack_heads(bt_s[rows, cl]) for cl in cols]
        yk = [_stack_heads(kt_s[rows, cl]) for cl in cols]
        vst = [_stack_heads(v_s[rows, cl]).astype(BF16) for cl in cols]
        pc = [ep_s[rows, cl][c_len - 1:c_len] for cl in cols]
        gmat = [_dot_nt(jnp.concatenate([_stack_heads(at[p]), _stack_heads(rt[p])], axis=0).astype(BF16),
                        jnp.concatenate([yb[p], yk[p]], axis=0).astype(BF16)) for p in pairs]
        aab = [jnp.where(strict, gmat[p][0:PAIR, 0:PAIR], zero) for p in pairs]
        aak = [jnp.where(strict, gmat[p][0:PAIR, PAIR:], zero).astype(BF16) for p in pairs]
        lrb = [jnp.where(incl, gmat[p][PAIR:, 0:PAIR], zero).astype(BF16) for p in pairs]
        lrk = [jnp.where(incl, gmat[p][PAIR:, PAIR:], zero).astype(BF16) for p in pairs]
        tinv = [eye_f + aab[p] for p in pairs]
        lp = aab
        n = 1
        while 2 * n < c_len:
            lpb = [x.astype(BF16) for x in lp]
            lp = [_dot(x, x) for x in lpb]
            tinv = [tinv[p] + _dot(tinv[p].astype(BF16), lp[p].astype(BF16)) for p in pairs]
            n *= 2
        tb = [x.astype(BF16) for x in tinv]
        av = [_dot(aak[p], vst[p]).astype(BF16) for p in pairs]
        tav = [_dot(tb[p], av[p]) for p in pairs]
        lrkv = [_dot(lrk[p], vst[p]) for p in pairs]
        zbt = [jnp.transpose(yb[p] * pc[p]).astype(BF16) for p in pairs]
        zkv = [_dot(jnp.transpose(yk[p] * pc[p]).astype(BF16), vst[p]) for p in pairs]
        pcc = [jnp.sum(jnp.where(eye, jnp.broadcast_to(pc[p], (PAIR, PAIR)), zero), axis=-1, keepdims=True)
               for p in pairs]
        for p in pairs:
            t_s[c, p] = tb[p]
            tav_s[c, p] = tav[p]
            lrb_s[c, p] = lrb[p]
            lrkv_s[c, p] = lrkv[p]
            zbt_s[c, p] = zbt[p]
            zkv_s[c, p] = zkv[p]
            ar_s[c, p] = jnp.concatenate([at[p], rt[p]], axis=0).astype(BF16)
            pc_s[c, p] = pcc[p]
        return carry_

    lax.fori_loop(0, n_ch, phase1, 0)

    def phase2(c, carry_):
        rows = pl.ds(pl.multiple_of(c * c_len, c_len), c_len)
        hbd = [hn_ref[0, p] for p in pairs]
        ar = [ar_s[c, p] for p in pairs]
        tb = [t_s[c, p] for p in pairs]
        tav = [tav_s[c, p] for p in pairs]
        lrb = [lrb_s[c, p] for p in pairs]
        lrkv = [lrkv_s[c, p] for p in pairs]
        zbt = [zbt_s[c, p] for p in pairs]
        zkv = [zkv_s[c, p] for p in pairs]
        pcc = [pc_s[c, p] for p in pairs]
        arh = [_dot(ar[p], hbd[p].astype(BF16)) for p in pairs]
        ub = [(_dot(tb[p], _stack_heads(arh[p][0:c_len]).astype(BF16)) + tav[p]).astype(BF16)
              for p in pairs]
        yst = [_dot(lrb[p], ub[p]) + lrkv[p] for p in pairs]
        hn = [pcc[p] * hbd[p] + _dot(zbt[p], ub[p]) + zkv[p] for p in pairs]
        for p in pairs:
            y_s[rows, PAIR * p:PAIR * (p + 1)] = arh[p][c_len:] + yst[p][0:c_len] + yst[p][c_len:]
            hn_ref[0, p] = hn[p]
        return carry_

    lax.fori_loop(0, n_ch, phase2, 0)

    y = y_s[...]
    inv_n = 1.0 / HEAD_DIM
    mean = _dot(y.astype(BF16), gsum) * inv_n
    d = y - mean
    var = _dot((d * d).astype(BF16), gsum) * inv_n
    yn = d * lax.rsqrt(var + GN_EPS) * lw_ref[...] + lb_ref[...]
    o_ref[0] = ((yn + bonus) * g).astype(BF16)


def _rwkv(pr, shift0, h0, prm, tr, t_valid):
    b, t_pad, _ = pr.shape
    assert t_pad % tr == 0 and tr % RW_CHUNK == 0
    n_ch = tr // RW_CHUNK
    tri = np.zeros((tr, tr), np.float32)
    for c in range(n_ch):
        tri[c * RW_CHUNK:(c + 1) * RW_CHUNK, c * RW_CHUNK:(c + 1) * RW_CHUNK] = np.tril(
            np.ones((RW_CHUNK, RW_CHUNK), np.float32))
    tri = jnp.asarray(tri, BF16)
    vec = _const_spec((1, RW_WIDTH))
    mat = lambda dt: pltpu.VMEM((n_ch, N_PAIR, PAIR, PAIR), dt)
    tile = lambda: pltpu.VMEM((tr, RW_WIDTH), F32)
    return pl.pallas_call(
        functools.partial(_rwkv_kernel, tr=tr, t_valid=t_valid),
        grid=(b, t_pad // tr),
        in_specs=[pl.BlockSpec((1, tr, RW_PROJ), lambda bi, ti: (bi, ti, 0)),
                  pl.BlockSpec((1, 1, RW_PROJ), lambda bi, ti: (bi, 0, 0)),
                  pl.BlockSpec((1, N_PAIR, PAIR, PAIR), lambda bi, ti: (bi, 0, 0, 0)),
                  _const_spec((1, RW_PROJ)), vec, vec, vec, vec, vec, vec, vec,
                  _const_spec((W_LORA + A_LORA, RW_WIDTH)), _const_spec((W_LORA + A_LORA, RW_WIDTH)),
                  _const_spec((G_LORA, RW_WIDTH)), _const_spec((RW_WIDTH, RW_WIDTH)),
                  _const_spec((tr, tr))],
        out_specs=[pl.BlockSpec((1, tr, RW_WIDTH), lambda bi, ti: (bi, ti, 0)),
                   pl.BlockSpec((1, N_PAIR, PAIR, PAIR), lambda bi, ti: (bi, 0, 0, 0))],
        out_shape=[jax.ShapeDtypeStruct((b, t_pad, RW_WIDTH), BF16),
                   jax.ShapeDtypeStruct((b, N_PAIR, PAIR, PAIR), F32)],
        scratch_shapes=[pltpu.VMEM((8 + tr, RW_PROJ), F32), pltpu.VMEM((1, RW_PROJ), F32),
                        tile(), tile(), tile(), tile(), tile(), tile(), tile(),
                        mat(BF16), mat(F32), mat(BF16), mat(F32), mat(BF16), mat(F32), mat(BF16),
                        pltpu.VMEM((n_ch, N_PAIR, PAIR, 1), F32)],
        compiler_params=_cparams(("arbitrary", "arbitrary")),
        name="rwkv",
    )(pr, shift0, h0, prm["mu"], prm["w0"], prm["a0"], prm["k_k"], prm["k_a"], prm["r_k"],
      prm["lnx_w"], prm["lnx_b"], prm["w2p"], prm["a2p"], prm["g2"], prm["gsum"], tri)


FF_COLS = 256


def _ffn_kernel(x_ref, oa_ref, orw_ref, c0_ref, wo_ref, g_ref, wu_ref, cw_ref, cb_ref, wd_ref,
                y_ref, cn_ref, zbuf, cbuf, h_ref, x1_ref, *, tm, ts, d_ff):
    off = zbuf.shape[1] - tm
    da = oa_ref.shape[1]

    @pl.when(pl.program_id(0) == 0)
    def _():
        cbuf[...] = c0_ref[...]

    x1 = x_ref[...] + _dot(oa_ref[...], wo_ref[0:da, :]) + _dot(orw_ref[...], wo_ref[da:, :])
    x1_ref[...] = x1
    ms = jnp.mean(x1 * x1, axis=-1, keepdims=True)
    h_ref[...] = (x1 * lax.rsqrt(ms + NORM_EPS) * g_ref[...]).astype(BF16)

    def up_proj(c0, zb):
        cols = slice(c0, c0 + FF_COLS)
        zb[off - 2 * ts:off, :] = cbuf[:, cols]
        zb[off:off + tm, :] = _dot(h_ref[...], wu_ref[:, cols])
        cbuf[:, cols] = zb[off + tm - 2 * ts:off + tm, :]

    def conv_cols(c0, zb):
        cols = slice(c0, c0 + FF_COLS)
        z2 = zb[off - 2 * ts:off - 2 * ts + tm, :]
        z1 = zb[off - ts:off - ts + tm, :]
        z = zb[off:off + tm, :]
        return (cb_ref[:, cols] + z2 * cw_ref[0:1, cols] + z1 * cw_ref[1:2, cols]
                + z * cw_ref[2:3, cols])

    def stage(c):
        up_proj(c * FF_COLS, zbuf.at[2 * (c % 2)])
        up_proj(d_ff + c * FF_COLS, zbuf.at[2 * (c % 2) + 1])

    n_chunks = d_ff // FF_COLS
    acc = None
    stage(0)
    for c in range(n_chunks):
        if c + 1 < n_chunks:
            stage(c + 1)
        gate = conv_cols(c * FF_COLS, zbuf.at[2 * (c % 2)])
        up = conv_cols(d_ff + c * FF_COLS, zbuf.at[2 * (c % 2) + 1])
        act = (gate * _sigmoid(gate) * up).astype(BF16)
        part = _dot(act, wd_ref[c * FF_COLS:(c + 1) * FF_COLS, :])
        acc = part if acc is None else acc + part

    y_ref[...] = x1_ref[...] + acc
    cn_ref[...] = cbuf[...]


def _ffn(x, oa, orw, conv0, prm, tm, ts):
    rows, d = x.shape
    d_ff = prm["w_down"].shape[0]
    assert rows % tm == 0 and d_ff % FF_COLS == 0 and (ts == 1 or ts % 8 == 0)
    off = -(-2 * ts // 8) * 8
    row = lambda w: pl.BlockSpec((tm, w), lambda i: (i, 0))
    return pl.pallas_call(
        functools.partial(_ffn_kernel, tm=tm, ts=ts, d_ff=d_ff),
        grid=(rows // tm,),
        in_specs=[row(d), row(DA_WIDTH), row(RW_WIDTH), _const_spec((2 * ts, 2 * d_ff)),
                  _resident_spec((DA_WIDTH + RW_WIDTH, d)), _const_spec((1, d)),
                  _resident_spec((d, 2 * d_ff)), _const_spec((CONV_W, 2 * d_ff)),
                  _const_spec((1, 2 * d_ff)), _resident_spec((d_ff, d))],
        out_specs=[row(d), _const_spec((2 * ts, 2 * d_ff))],
        out_shape=[jax.ShapeDtypeStruct((rows, d), F32),
                   jax.ShapeDtypeStruct((2 * ts, 2 * d_ff), F32)],
        scratch_shapes=[pltpu.VMEM((4, off + tm, FF_COLS), F32), pltpu.VMEM((2 * ts, 2 * d_ff), F32),
                        pltpu.VMEM((tm, d), BF16), pltpu.VMEM((tm, d), F32)],
        compiler_params=_cparams(("arbitrary",)),
        name="ffn",
    )(x, oa, orw, conv0, prm["w_out"], prm["ln2_g"], prm["w_up"], prm["conv_w"], prm["conv_b"],
      prm["w_down"])


def _rel_bucket(rel):
    nb = N_BUCKETS // 2
    max_exact = nb // 2
    bucket = jnp.where(rel > 0, nb, 0)
    n = jnp.abs(rel)
    nf = jnp.maximum(n, 1).astype(F32)
    large = max_exact + (jnp.log(nf / max_exact) / math.log(MAX_DISTANCE / max_exact)
                         * (nb - max_exact)).astype(jnp.int32)
    large = jnp.minimum(large, nb - 1)
    return bucket + jnp.where(n < max_exact, n, large)


def _bias_table(rel_bias, q_pos, k_pos, mask):
    rel = jnp.asarray(k_pos[None, :] - q_pos[:, None], jnp.int32)
    onehot = _rel_bucket(rel)[None, :, :, None] == jnp.arange(N_BUCKETS, dtype=jnp.int32)
    table = jnp.transpose(rel_bias).astype(F32)[:, None, None, :]
    bias = jnp.sum(jnp.where(onehot, table, 0.0), axis=-1) * LOG2E
    return jnp.where(jnp.asarray(mask)[None], bias, NEG_INF)


def _ext_chunk(pos):
    return np.where(pos < N_META, -1, (pos - N_META) // CHUNK)


def _prompt_bias(rel_bias):
    tq = ATT_T
    fr = np.arange(tq) + N_META
    meta = np.arange(N_META)
    causal = _ext_chunk(fr)[None, :] <= _ext_chunk(fr)[:, None]
    tr = lambda b: jnp.swapaxes(b, 1, 2)
    b0 = tr(_bias_table(rel_bias, fr, fr, causal))
    bm1 = tr(_bias_table(rel_bias, fr + tq, fr, np.ones((tq, tq), bool)))
    bq0 = tr(_bias_table(rel_bias, fr, meta, np.ones((tq, N_META), bool)))
    bmm = _bias_table(rel_bias, meta, meta, np.ones((N_META, N_META), bool))
    assert tq + 1 >= MAX_DISTANCE
    cfar = rel_bias[_rel_bucket(jnp.asarray(-(tq + 1), jnp.int32))].astype(F32) * LOG2E
    return cfar, b0, bm1, bq0, bmm


def _decode_bias(rel_bias, n_cache, nq):
    k_pos = np.arange(n_cache + nq)
    q_pos = k_pos[n_cache:]
    mask = _ext_chunk(k_pos)[None, :] <= _ext_chunk(q_pos)[:, None]
    return _bias_table(rel_bias, q_pos, k_pos, mask)


def _state_to_pairs(s):
    b = s.shape[0]
    st = jnp.swapaxes(s, -1, -2).reshape(b, N_PAIR, 2, HEAD_DIM, HEAD_DIM)
    eye2 = jnp.eye(2, dtype=s.dtype)
    return jnp.einsum('bphkv,hg->bphkgv', st, eye2).reshape(b, N_PAIR, PAIR, PAIR)


def _pairs_to_state(hp):
    b = hp.shape[0]
    h6 = hp.reshape(b, N_PAIR, 2, HEAD_DIM, 2, HEAD_DIM)
    diag = jnp.stack([h6[:, :, 0, :, 0, :], h6[:, :, 1, :, 1, :]], axis=2)
    return jnp.swapaxes(diag, -1, -2).reshape(b, RW_HEADS, HEAD_DIM, HEAD_DIM)


def _block_ones(n, blk, dtype):
    idx = np.arange(n) // blk
    return jnp.asarray((idx[:, None] == idx[None, :]).astype(np.float32), dtype)


def kernel(x_prompt, x_sample, cache_k, cache_v, state_rwkv, state_shift, state_conv, meta_tokens,
           rel_bias, ln1_g, w_in, q_norm_g, k_norm_g, lam_q1, lam_k1, lam_q2, lam_k2, subln_g,
           mu_shift, w0, w2, a0, a2, g2, k_k, k_a, r_k, lnx_w, lnx_b, w_out, ln2_g, w_up, conv_w,
           conv_b, w_down):
    bp, seq, d = x_prompt.shape
    db, dt, _ = x_sample.shape
    depth = w_in.shape[0]
    d_ff = w_down.shape[1]
    n_cache = cache_k.shape[2]
    assert bp == 1 and dt == N_META, "the meta stream rides with the decode streams"
    assert cache_k.shape[3] == DA_HEADS and cache_k.shape[4] == 2 * HEAD_DIM
    nb = db + 1
    nb_pad = -(-nb // 8) * 8

    cfar, b0, bm1, bq0, bmm = _prompt_bias(rel_bias)
    bias_dec = _decode_bias(rel_bias, n_cache, dt)
    gmean = _block_ones(DA_WIDTH, HEAD_DIM, BF16) * (1.0 / HEAD_DIM)
    gsum = _block_ones(RW_WIDTH, HEAD_DIM, BF16)
    zrow = lambda n: jnp.zeros((n, RW_WIDTH), BF16)

    x_f = x_prompt[0]
    x_s = jnp.concatenate([x_sample, meta_tokens.astype(x_sample.dtype)[None]], axis=0)
    outs = [[] for _ in range(10)]
    for l in range(depth):
        lam_init = 0.8 - 0.6 * math.exp(-0.3 * l)
        lam = (jnp.exp(jnp.sum(lam_q1[l].astype(F32) * lam_k1[l].astype(F32)))
               - jnp.exp(jnp.sum(lam_q2[l].astype(F32) * lam_k2[l].astype(F32))) + lam_init).reshape(1)
        out_scale = 1.0 - lam_init
        tile128 = lambda g_: jnp.tile(g_.reshape(1, -1), (1, DA_WIDTH // g_.shape[-1]))
        qg, kg = tile128(q_norm_g[l]), tile128(k_norm_g[l])
        sg = subln_g[l].reshape(1, LANES)
        w_in_bf = w_in[l].astype(BF16)
        rw = {
            "mu": mu_shift[l].reshape(1, -1), "w0": w0[l].reshape(1, -1), "a0": a0[l].reshape(1, -1),
            "k_k": k_k[l].reshape(1, -1), "k_a": k_a[l].reshape(1, -1), "r_k": r_k[l].reshape(1, -1),
            "lnx_w": lnx_w[l].reshape(1, -1), "lnx_b": lnx_b[l].reshape(1, -1),
            "w2p": jnp.concatenate([w2[l].astype(BF16), zrow(A_LORA)], axis=0),
            "a2p": jnp.concatenate([zrow(W_LORA), a2[l].astype(BF16)], axis=0),
            "g2": g2[l].astype(BF16), "gsum": gsum,
        }
        ff = {
            "w_out": w_out[l].astype(BF16), "ln2_g": ln2_g[l].reshape(1, -1),
            "w_up": w_up[l].astype(BF16), "conv_w": conv_w[l], "conv_b": conv_b[l].reshape(1, -1),
            "w_down": w_down[l].astype(BF16),
        }

        q_f, k_f, v_f, pr_f, kb_t, vt_t = _proj(x_f, ln1_g[l].reshape(1, -1), w_in_bf, qg, kg, gmean,
                                                 512, True)
        q_s, k_s, v_s, pr_s = _proj(x_s.reshape(nb * dt, d), ln1_g[l].reshape(1, -1), w_in_bf, qg, kg,
                                    gmean, nb * dt, False)
        m0 = db * dt

        kb_m = k_s[m0:].astype(BF16)
        vb_m = v_s[m0:].astype(BF16)
        o_f, o_m = _attn_prompt(cfar, lam, q_f, kb_t, vt_t, kb_m, vb_m, jnp.transpose(vb_m), q_s[m0:],
                                b0, bm1, bq0, bmm, sg, out_scale)
        o_d = _attn_decode(lam, q_s, cache_k, cache_v, l, k_s, v_s, bias_dec, sg, out_scale, dt)
        o_s = jnp.concatenate([o_d, o_m], axis=0)

        pr_s3 = pr_s.reshape(nb, dt, RW_PROJ)
        pr_pad = jnp.pad(pr_s3, ((0, 0), (0, RW_CHUNK - dt), (0, 0)))
        shift_s = jnp.concatenate([state_shift[l], jnp.zeros((1, RW_PROJ), F32)], axis=0)[:, None, :]
        h_s = _state_to_pairs(jnp.concatenate(
            [state_rwkv[l], jnp.zeros((1,) + state_rwkv.shape[2:], F32)], axis=0))
        orw_s, hn_s = _rwkv(pr_pad, shift_s, h_s, rw, RW_CHUNK, dt)
        orw_f, hn_f = _rwkv(pr_f[None], pr_s3[db:, dt - 1:dt, :], hn_s[db:], rw, 512, seq)

        def time_major(a):
            a = jnp.pad(a.reshape(nb, dt, -1), ((0, nb_pad - nb), (0, 0), (0, 0)))
            return jnp.swapaxes(a, 0, 1).reshape(dt * nb_pad, -1)

        conv_s = jnp.concatenate([state_conv[l], jnp.zeros((1, CONV_W - 1, 2 * d_ff), F32)], axis=0)
        conv_s = jnp.pad(conv_s, ((0, nb_pad - nb), (0, 0), (0, 0)))
        conv_s = jnp.swapaxes(conv_s, 0, 1).reshape(2 * nb_pad, 2 * d_ff)
        y_s, cn_s = _ffn(time_major(x_s), time_major(o_s), time_major(orw_s[:, :dt]), conv_s, ff,
                         dt * nb_pad, nb_pad)
        cn_s = jnp.swapaxes(cn_s.reshape(2, nb_pad, 2 * d_ff), 0, 1)
        y_f, cn_f = _ffn(x_f, o_f, orw_f[0], cn_s[db], ff, 512, 1)
        y_s = jnp.swapaxes(y_s.reshape(dt, nb_pad, d), 0, 1)[:nb]

        hw = (DA_HEADS, 2 * HEAD_DIM)
        outs[0].append(jnp.concatenate([k_s[m0:], k_f], axis=0).reshape(bp, N_META + seq, *hw))
        outs[1].append(jnp.concatenate([v_s[m0:], v_f], axis=0).reshape(bp, N_META + seq, *hw))
        outs[2].append(_pairs_to_state(hn_f))
        outs[3].append(pr_f[seq - 1:seq])
        outs[4].append(cn_f[None])
        outs[5].append(k_s[:m0].reshape(db, dt, *hw))
        outs[6].append(v_s[:m0].reshape(db, dt, *hw))
        outs[7].append(_pairs_to_state(hn_s[:db]))
        outs[8].append(pr_s3[:db, dt - 1])
        outs[9].append(cn_s[:db])
        x_f, x_s = y_f, y_s

    return (x_f[None], x_s[:db], *[jnp.stack(o) for o in outs])
```

```python
import functools
import math

import numpy as np
import jax
import jax.numpy as jnp
from jax import lax
from jax.experimental import pallas as pl
from jax.experimental.pallas import tpu as pltpu

F32 = jnp.float32
BF16 = jnp.bfloat16

CHUNK = 64
N_META = 16
HEAD_DIM = 64
DA_HEADS = 4
RW_HEADS = 8
W_LORA = 64
A_LORA = 64
G_LORA = 128
CONV_W = 3
N_BUCKETS = 32
MAX_DISTANCE = 128
NORM_EPS = 1e-6
GN_EPS = 64e-5
NEG_INF = -1e30
LOG2E = math.log2(math.e)
DECAY_SCALE = math.exp(-0.5)

DA_WIDTH = DA_HEADS * 2 * HEAD_DIM
RW_WIDTH = RW_HEADS * HEAD_DIM
RW_PROJ = 3 * RW_WIDTH + W_LORA + A_LORA + G_LORA
LANES = 128
PAIR = 2 * HEAD_DIM
N_PAIR = RW_WIDTH // PAIR
RW_CHUNK = 64
ATT_T = 512
VMEM_LIMIT = 56 * 1024 * 1024


def _dot(a, b):
    return jnp.dot(a, b, preferred_element_type=F32)


def _dot_nt(a, b):
    return lax.dot_general(a, b, (((1,), (1,)), ((), ())), preferred_element_type=F32)


def _group_sum(x, blk):
    xb = x.astype(BF16)
    return jnp.concatenate([_dot(xb[:, LANES * p:LANES * (p + 1)], blk)
                            for p in range(x.shape[1] // LANES)], axis=1)


def _cparams(sem):
    return pltpu.CompilerParams(dimension_semantics=sem, vmem_limit_bytes=VMEM_LIMIT)


def _const_spec(shape):
    nd = len(shape)
    return pl.BlockSpec(shape, lambda *_: (0,) * nd)


def _resident_spec(shape):
    nd = len(shape)
    return pl.BlockSpec(shape, lambda *_: (0,) * nd, pipeline_mode=pl.Buffered(1))


def _proj_kernel(x_ref, g_ref, w_ref, qg_ref, kg_ref, gm_ref,
                 q_ref, k_ref, v_ref, pr_ref, *tile_refs, tm):
    x = x_ref[...]
    ms = jnp.mean(x * x, axis=-1, keepdims=True)
    h = (x * lax.rsqrt(ms + NORM_EPS) * g_ref[...]).astype(BF16)
    gm = gm_ref[...]

    def group_norm(t, g):
        ms_g = _group_sum(t * t, gm)
        return t * lax.rsqrt(ms_g + NORM_EPS) * g

    q = _dot(h, w_ref[:, 0:DA_WIDTH])
    q_ref[...] = (group_norm(q, qg_ref[...]) * (HEAD_DIM ** -0.5 * LOG2E)).astype(BF16)
    k = group_norm(_dot(h, w_ref[:, DA_WIDTH:2 * DA_WIDTH]), kg_ref[...])
    v = _dot(h, w_ref[:, 2 * DA_WIDTH:3 * DA_WIDTH])
    if len(k_ref.shape) == 3:
        for hd in range(DA_HEADS):
            k_ref[:, hd, :] = k[:, hd * LANES:(hd + 1) * LANES]
            v_ref[:, hd, :] = v[:, hd * LANES:(hd + 1) * LANES]
    else:
        k_ref[...] = k
        v_ref[...] = v
    pr_ref[...] = _dot(h, w_ref[:, 3 * DA_WIDTH:])
    if tile_refs:
        kb_ref, vt_ref = tile_refs
        kb = k.astype(BF16)
        vt = jnp.transpose(v).astype(BF16)
        for hd in range(DA_HEADS):
            for jj in range(tm // ATT_T):
                kb_ref[hd, jj] = kb[jj * ATT_T:(jj + 1) * ATT_T, hd * LANES:(hd + 1) * LANES]
                vt_ref[hd, jj] = vt[hd * LANES:(hd + 1) * LANES, jj * ATT_T:(jj + 1) * ATT_T]


def _proj(x, ln1_g, w_in_bf, qg, kg, gmean, tm, emit_tiles):
    rows, d = x.shape
    n_in = w_in_bf.shape[1]
    assert rows % tm == 0
    row = lambda w: pl.BlockSpec((tm, w), lambda i: (i, 0))
    out_specs = [row(DA_WIDTH), row(DA_WIDTH), row(DA_WIDTH), row(RW_PROJ)]
    out_shape = [jax.ShapeDtypeStruct((rows, DA_WIDTH), BF16),
                 jax.ShapeDtypeStruct((rows, DA_WIDTH), F32),
                 jax.ShapeDtypeStruct((rows, DA_WIDTH), F32),
                 jax.ShapeDtypeStruct((rows, RW_PROJ), F32)]
    if emit_tiles:
        assert tm % ATT_T == 0
        tpt = tm // ATT_T
        for o in (1, 2):
            out_specs[o] = pl.BlockSpec((tm, DA_HEADS, LANES), lambda i: (i, 0, 0))
            out_shape[o] = jax.ShapeDtypeStruct((rows, DA_HEADS, LANES), F32)
        out_specs += [pl.BlockSpec((DA_HEADS, tpt, ATT_T, LANES), lambda i: (0, i, 0, 0)),
                      pl.BlockSpec((DA_HEADS, tpt, LANES, ATT_T), lambda i: (0, i, 0, 0))]
        out_shape += [jax.ShapeDtypeStruct((DA_HEADS, rows // ATT_T, ATT_T, LANES), BF16),
                      jax.ShapeDtypeStruct((DA_HEADS, rows // ATT_T, LANES, ATT_T), BF16)]
    return pl.pallas_call(
        functools.partial(_proj_kernel, tm=tm),
        grid=(rows // tm,),
        in_specs=[row(d), _const_spec((1, d)), _resident_spec((d, n_in)),
                  _const_spec((1, DA_WIDTH)), _const_spec((1, DA_WIDTH)),
                  _const_spec((LANES, LANES))],
        out_specs=out_specs,
        out_shape=out_shape,
        compiler_params=_cparams(("arbitrary",)),
        name="proj",
    )(x, ln1_g, w_in_bf, qg, kg, gmean)


def _stack_components(q):
    lo = lax.broadcasted_iota(jnp.int32, q.shape, 1) < HEAD_DIM
    zero = jnp.zeros_like(q)
    return jnp.concatenate([jnp.where(lo, q, zero), jnp.where(lo, zero, q)], axis=0)


def _sub_layer_norm(o, g, out_scale):
    ms = jnp.mean(o * o, axis=-1, keepdims=True)
    return o * lax.rsqrt(ms + NORM_EPS) * g * out_scale


def _attn_prompt_kernel(cfar_ref, lam_ref, q_ref, k_ref, vt_ref, km_ref, vm_ref, vmt_ref, qm_ref,
                        b0_ref, bm1_ref, bq0_ref, bmm_ref, g_ref, o_ref, om_ref,
                        acc_ref, m_ref, l_ref, s_buf, p_buf, a_buf, *, out_scale):
    h = pl.program_id(0)
    i = pl.program_id(1)
    tq = ATT_T
    cf = cfar_ref[h]
    lam = lam_ref[0]
    qst = _stack_components(q_ref[...])

    def both(b):
        return jnp.concatenate([b, b], axis=1)

    s = _dot_nt(km_ref[...], qst)
    s = s + both(jnp.where(i == 0, bq0_ref[0], cf))
    m0 = jnp.max(s, axis=0, keepdims=True)
    p = jnp.exp2(s - m0)
    m_ref[...] = m0
    l_ref[...] = jnp.sum(p, axis=0, keepdims=True)
    acc_ref[...] = _dot(vmt_ref[...], p.astype(BF16))

    def scores(idx, slot):
        s_buf[slot] = _dot_nt(k_ref[0, idx], qst)

    def softmax(slot, bias):
        s = s_buf[slot]
        m_prev = m_ref[...]
        if bias.ndim == 0:
            m_new = jnp.maximum(m_prev, jnp.max(s, axis=0, keepdims=True) + bias)
            p = jnp.exp2(s - (m_new - bias))
        else:
            s = s + both(bias)
            m_new = jnp.maximum(m_prev, jnp.max(s, axis=0, keepdims=True))
            p = jnp.exp2(s - m_new)
        alpha = jnp.exp2(m_prev - m_new)
        l_ref[...] = alpha * l_ref[...] + jnp.sum(p, axis=0, keepdims=True)
        m_ref[...] = m_new
        p_buf[slot] = p.astype(BF16)
        a_buf[slot] = alpha

    def accumulate(idx, slot):
        acc_ref[...] = a_buf[slot] * acc_ref[...] + _dot(vt_ref[0, idx], p_buf[slot])

    n_far = jnp.maximum(i - 1, 0)
    off = n_far % 2
    p_buf[1] = jnp.zeros(p_buf.shape[1:], BF16)
    a_buf[1] = jnp.ones(a_buf.shape[1:], F32)
    scores(0, 0)

    def far_body(t, carry):
        u = 2 * t - off
        scores(u + 1, 1)
        softmax(0, jnp.where(u >= 0, cf, NEG_INF))
        accumulate(jnp.maximum(u - 1, 0), 1)
        scores(u + 2, 0)
        softmax(1, cf)
        accumulate(jnp.maximum(u, 0), 0)
        return carry

    lax.fori_loop(0, (n_far + 1) // 2, far_body, 0)

    scores(i, 1)
    softmax(0, jnp.where(i >= 1, bm1_ref[0], NEG_INF))
    accumulate(jnp.maximum(n_far - 1, 0), 1)
    softmax(1, b0_ref[0])
    accumulate(n_far, 0)
    accumulate(i, 1)

    accn = acc_ref[...] / l_ref[...]
    o_t = accn[:, 0:tq] - lam * accn[:, tq:]
    o_ref[...] = _sub_layer_norm(jnp.transpose(o_t), g_ref[...], out_scale).astype(BF16)

    @pl.when(i == 0)
    def _():
        qm = _stack_components(qm_ref[...])
        bmm = bmm_ref[0]
        sm = _dot_nt(qm, km_ref[...]) + jnp.concatenate([bmm, bmm], axis=0)
        mm = jnp.max(sm, axis=-1, keepdims=True)
        pm = jnp.exp2(sm - mm)
        accm = _dot(pm.astype(BF16), vm_ref[...]) / jnp.sum(pm, axis=-1, keepdims=True)
        om = accm[0:N_META] - lam * accm[N_META:]
        om_ref[...] = _sub_layer_norm(om, g_ref[...], out_scale).astype(BF16)


def _attn_prompt(cfar, lam, q_f, kb_t, vt_t, kb_m, vb_m, vbt_m, q_m, b0, bm1, bq0, bmm, subln_g,
                 out_scale):
    tf = q_f.shape[0]
    tq = ATT_T
    assert tf % tq == 0
    n_t = tf // tq
    smem = pl.BlockSpec(memory_space=pltpu.SMEM)
    head_col = lambda rows: pl.BlockSpec((rows, LANES), lambda h, i: (0, h))
    head_tile = lambda a, b: pl.BlockSpec((1, a, b), lambda h, i: (h, 0, 0))
    return pl.pallas_call(
        functools.partial(_attn_prompt_kernel, out_scale=out_scale),
        grid=(DA_HEADS, n_t),
        in_specs=[smem, smem,
                  pl.BlockSpec((tq, LANES), lambda h, i: (i, h)),
                  pl.BlockSpec((1, n_t, tq, LANES), lambda h, i: (h, 0, 0, 0)),
                  pl.BlockSpec((1, n_t, LANES, tq), lambda h, i: (h, 0, 0, 0)),
                  head_col(N_META), head_col(N_META),
                  pl.BlockSpec((LANES, N_META), lambda h, i: (h, 0)),
                  head_col(N_META),
                  head_tile(tq, tq), head_tile(tq, tq), head_tile(N_META, tq),
                  head_tile(N_META, N_META), _const_spec((1, LANES))],
        out_specs=[pl.BlockSpec((tq, LANES), lambda h, i: (i, h)), head_col(N_META)],
        out_shape=[jax.ShapeDtypeStruct((tf, DA_WIDTH), BF16),
                   jax.ShapeDtypeStruct((N_META, DA_WIDTH), BF16)],
        scratch_shapes=[pltpu.VMEM((LANES, 2 * tq), F32), pltpu.VMEM((1, 2 * tq), F32),
                        pltpu.VMEM((1, 2 * tq), F32), pltpu.VMEM((2, tq, 2 * tq), F32),
                        pltpu.VMEM((2, tq, 2 * tq), BF16), pltpu.VMEM((2, 1, 2 * tq), F32)],
        compiler_params=_cparams(("arbitrary", "arbitrary")),
        name="attn_prompt",
    )(cfar, lam, q_f, kb_t, vt_t, kb_m, vb_m, vbt_m, q_m, b0, bm1, bq0, bmm, subln_g)


def _attn_decode_kernel(lam_ref, q_ref, k_hbm, v_hbm, kn_ref, vn_ref, b_ref, g_ref, o_ref,
                        kbuf, vbuf, sem, *, layer, n_main, out_scale):
    bi = pl.program_id(0)
    h = pl.program_id(1)
    n_h = pl.num_programs(1)
    step = bi * n_h + h
    slot = step % 2

    def cache_copies(b_, h_, slot_):
        return (pltpu.make_async_copy(k_hbm.at[layer, b_, :, h_, :], kbuf.at[slot_], sem.at[0, slot_]),
                pltpu.make_async_copy(v_hbm.at[layer, b_, :, h_, :], vbuf.at[slot_], sem.at[1, slot_]))

    @pl.when(step == 0)
    def _():
        for cp in cache_copies(bi, h, slot):
            cp.start()

    @pl.when(step + 1 < pl.num_programs(0) * n_h)
    def _():
        nxt = step + 1
        for cp in cache_copies(nxt // n_h, nxt % n_h, 1 - slot):
            cp.start()

    for cp in cache_copies(bi, h, slot):
        cp.wait()

    lam = lam_ref[0]
    nq = q_ref.shape[0]
    n_cache = kbuf.shape[1]
    k_ref = kbuf.at[slot]
    v_ref = vbuf.at[slot]
    qst = _stack_components(q_ref[...])
    k_main = k_ref[0:n_main, :].astype(BF16)
    v_main = v_ref[0:n_main, :].astype(BF16)
    k_tail = jnp.concatenate([k_ref[n_main:n_cache, :], kn_ref[...]], axis=0).astype(BF16)
    v_tail = jnp.concatenate([v_ref[n_main:n_cache, :], vn_ref[...]], axis=0).astype(BF16)
    bias = jnp.concatenate([b_ref[0], b_ref[0]], axis=0)
    s1 = _dot_nt(qst, k_main) + bias[:, 0:n_main]
    s2 = _dot_nt(qst, k_tail) + bias[:, n_main:]
    m = jnp.maximum(jnp.max(s1, axis=-1, keepdims=True), jnp.max(s2, axis=-1, keepdims=True))
    p1 = jnp.exp2(s1 - m)
    p2 = jnp.exp2(s2 - m)
    l = jnp.sum(p1, axis=-1, keepdims=True) + jnp.sum(p2, axis=-1, keepdims=True)
    res = (_dot(p1.astype(BF16), v_main) + _dot(p2.astype(BF16), v_tail)) / l
    o = res[0:nq] - lam * res[nq:]
    o_ref[...] = _sub_layer_norm(o, g_ref[...], out_scale).astype(BF16)


def _attn_decode(lam, q_s, cache_k, cache_v, layer, k_s, v_s, bias, subln_g, out_scale, nq):
    _, b, n_cache, n_h, _ = cache_k.shape
    n_main = (n_cache // LANES) * LANES
    n_keys = n_cache + nq
    smem = pl.BlockSpec(memory_space=pltpu.SMEM)
    hbm = pl.BlockSpec(memory_space=pl.ANY)
    new = pl.BlockSpec((nq, LANES), lambda bi, h: (bi, h))
    return pl.pallas_call(
        functools.partial(_attn_decode_kernel, layer=layer, n_main=n_main, out_scale=out_scale),
        grid=(b, n_h),
        in_specs=[smem, new, hbm, hbm, new, new,
                  pl.BlockSpec((1, nq, n_keys), lambda bi, h: (h, 0, 0)), _const_spec((1, LANES))],
        out_specs=new,
        out_shape=jax.ShapeDtypeStruct((b * nq, DA_WIDTH), BF16),
        scratch_shapes=[pltpu.VMEM((2, n_cache, LANES), F32), pltpu.VMEM((2, n_cache, LANES), F32),
                        pltpu.SemaphoreType.DMA((2, 2))],
        compiler_params=_cparams(("arbitrary", "arbitrary")),
        name="attn_decode",
    )(lam, q_s, cache_k, cache_v, k_s, v_s, bias, subln_g)


def _sigmoid(x):
    return 1.0 / (1.0 + jnp.exp(-x))


def _split2(x):
    x1 = x.astype(BF16)
    x2 = (x - x1.astype(F32)).astype(BF16)
    return x1, x2


def _stack_heads(x):
    lo = lax.broadcasted_iota(jnp.int32, x.shape, 1) < HEAD_DIM
    zero = jnp.zeros_like(x)
    return jnp.concatenate([jnp.where(lo, x, zero), jnp.where(lo, zero, x)], axis=0)


def _rwkv_kernel(pr_ref, sh0_ref, h0_ref, mu_ref, w0_ref, a0_ref, kk_ref, ka_ref, rk_ref,
                 lw_ref, lb_ref, w2_ref, a2_ref, g2_ref, gsum_ref, tri_ref,
                 o_ref, hn_ref,
                 xbuf, carry, at_s, rt_s, bt_s, kt_s, v_s, ep_s, y_s,
                 t_s, tav_s, lrb_s, lrkv_s, zbt_s, zkv_s, ar_s, pc_s,
                 *, tr, t_valid):
    c_len = RW_CHUNK
    n_ch = tr // c_len
    ti = pl.program_id(1)

    @pl.when(ti == 0)
    def _():
        carry[...] = sh0_ref[0]
        hn_ref[...] = h0_ref[...]

    x = pr_ref[0]
    xbuf[8:8 + tr, :] = x
    xbuf[7:8, :] = carry[...]
    prev = xbuf[7:7 + tr, :]
    carry[...] = pr_ref[0, tr - 1:tr, :]
    xm = x + (prev - x) * mu_ref[...]
    r = xm[:, 0:RW_WIDTH]
    kr = xm[:, RW_WIDTH:2 * RW_WIDTH]
    vr = xm[:, 2 * RW_WIDTH:3 * RW_WIDTH]
    wa = xm[:, 3 * RW_WIDTH:3 * RW_WIDTH + W_LORA + A_LORA]
    gd = xm[:, 3 * RW_WIDTH + W_LORA + A_LORA:]
    lane_wa = lax.broadcasted_iota(jnp.int32, wa.shape, 1)
    twa = jnp.where(lane_wa < W_LORA, jnp.tanh(wa), wa).astype(BF16)
    logw = -DECAY_SCALE * _sigmoid(w0_ref[...] + _dot(twa, w2_ref[...]))
    a_sig = _sigmoid(a0_ref[...] + _dot(twa, a2_ref[...]))
    g = _dot(_sigmoid(gd).astype(BF16), g2_ref[...])
    gsum = gsum_ref[...]
    kk = kr * kk_ref[...]
    kk = kk * lax.rsqrt(jnp.maximum(_group_sum(kk * kk, gsum), 1e-24))
    kr2 = kr * (1.0 + (a_sig - 1.0) * ka_ref[...])
    a_vec = -kk
    b_vec = kk * a_sig
    bonus = _group_sum(r * kr2 * rk_ref[...], gsum) * vr
    if t_valid % tr != 0:
        row = lax.broadcasted_iota(jnp.int32, (tr, 1), 0) + ti * tr
        valid = row < t_valid
        logw = jnp.where(valid, logw, 0.0)
        a_vec = jnp.where(valid, a_vec, 0.0)
        b_vec = jnp.where(valid, b_vec, 0.0)
        kr2 = jnp.where(valid, kr2, 0.0)
        vr = jnp.where(valid, vr, 0.0)
        bonus = jnp.where(valid, bonus, 0.0)
    l1, l2 = _split2(logw)
    tri = tri_ref[...]
    cs = _dot(tri, l1) + _dot(tri, l2)
    e_pos = jnp.exp(cs)
    e_neg = jnp.exp(-cs)
    at_s[...] = a_vec * jnp.exp(cs - logw)
    rt_s[...] = r * e_pos
    bt_s[...] = b_vec * e_neg
    kt_s[...] = kr2 * e_neg
    v_s[...] = vr
    ep_s[...] = e_pos

    idx_r = lax.broadcasted_iota(jnp.int32, (PAIR, PAIR), 0)
    idx_c = lax.broadcasted_iota(jnp.int32, (PAIR, PAIR), 1)
    same = (idx_r // c_len) == (idx_c // c_len)
    strict = same & ((idx_r % c_len) > (idx_c % c_len))
    incl = same & ((idx_r % c_len) >= (idx_c % c_len))
    eye = idx_r == idx_c
    eye_f = jnp.where(eye, 1.0, 0.0).astype(F32)

    zero = jnp.zeros((PAIR, PAIR), F32)
    n_par = 2 if n_ch % 2 == 0 else 1
    items = [(dc, p) for dc in range(n_par) for p in range(N_PAIR)]
    pairs = range(len(items))

    def phase1(c2, carry_):
        sel = [(pl.ds(pl.multiple_of((c2 * n_par + dc) * c_len, c_len), c_len),
                slice(PAIR * p, PAIR * (p + 1))) for dc, p in items]
        at = [at_s[rw, cl] for rw, cl in sel]
        rt = [rt_s[rw, cl] for rw, cl in sel]
        yb = [_stack_heads(bt_s[rw, cl]) for rw, cl in sel]
        yk = [_stack_heads(kt_s[rw, cl]) for rw, cl in sel]
        vst = [_stack_heads(v_s[rw, cl]).astype(BF16) for rw, cl in sel]
        pc = [ep_s[rw, cl][c_len - 1:c_len] for rw, cl in sel]
        gmat = [_dot_nt(jnp.concatenate([_stack_heads(at[p]), _stack_heads(rt[p])], axis=0).astype(BF16),
                        jnp.concatenate([yb[p], yk[p]], axis=0).astype(BF16)) for p in pairs]
        aab = [jnp.where(strict, gmat[p][0:PAIR, 0:PAIR], zero) for p in pairs]
        aak = [jnp.where(strict, gmat[p][0:PAIR, PAIR:], zero).astype(BF16) for p in pairs]
        lrb = [jnp.where(incl, gmat[p][PAIR:, 0:PAIR], zero).astype(BF16) for p in pairs]
        lrk = [jnp.where(incl, gmat[p][PAIR:, PAIR:], zero).astype(BF16) for p in pairs]
        tinv = [eye_f + aab[p] for p in pairs]
        lp = aab
        n = 1
        while 2 * n < c_len:
            lpb = [x.astype(BF16) for x in lp]
            lp = [_dot(x, x) for x in lpb]
            tinv = [tinv[p] + _dot(tinv[p].astype(BF16), lp[p].astype(BF16)) for p in pairs]
            n *= 2
        tb = [x.astype(BF16) for x in tinv]
        av = [_dot(aak[p], vst[p]).astype(BF16) for p in pairs]
        tav = [_dot(tb[p], av[p]) for p in pairs]
        lrkv = [_dot(lrk[p], vst[p]) for p in pairs]
        zbt = [jnp.transpose(yb[p] * pc[p]).astype(BF16) for p in pairs]
        zkv = [_dot(jnp.transpose(yk[p] * pc[p]).astype(BF16), vst[p]) for p in pairs]
        pcc = [jnp.sum(jnp.where(eye, jnp.broadcast_to(pc[p], (PAIR, PAIR)), zero), axis=-1, keepdims=True)
               for p in pairs]
        for q, (dc, p) in enumerate(items):
            c = c2 * n_par + dc
            t_s[c, p] = tb[q]
            tav_s[c, p] = tav[q]
            lrb_s[c, p] = lrb[q]
            lrkv_s[c, p] = lrkv[q]
            zbt_s[c, p] = zbt[q]
            zkv_s[c, p] = zkv[q]
            ar_s[c, p] = jnp.concatenate([at[q], rt[q]], axis=0).astype(BF16)
            pc_s[c, p] = pcc[q]
        return carry_

    lax.fori_loop(0, n_ch // n_par, phase1, 0)

    pairs = range(N_PAIR)

    def phase2(c, carry_):
        rows = pl.ds(pl.multiple_of(c * c_len, c_len), c_len)
        hbd = [hn_ref[0, p] for p in pairs]
        ar = [ar_s[c, p] for p in pairs]
        tb = [t_s[c, p] for p in pairs]
        tav = [tav_s[c, p] for p in pairs]
        lrb = [lrb_s[c, p] for p in pairs]
        lrkv = [lrkv_s[c, p] for p in pairs]
        zbt = [zbt_s[c, p] for p in pairs]
        zkv = [zkv_s[c, p] for p in pairs]
        pcc = [pc_s[c, p] for p in pairs]
        arh = [_dot(ar[p], hbd[p].astype(BF16)) for p in pairs]
        ub = [(_dot(tb[p], _stack_heads(arh[p][0:c_len]).astype(BF16)) + tav[p]).astype(BF16)
              for p in pairs]
        yst = [_dot(lrb[p], ub[p]) + lrkv[p] for p in pairs]
        hn = [pcc[p] * hbd[p] + _dot(zbt[p], ub[p]) + zkv[p] for p in pairs]
        for p in pairs:
            y_s[rows, PAIR * p:PAIR * (p + 1)] = arh[p][c_len:] + yst[p][0:c_len] + yst[p][c_len:]
            hn_ref[0, p] = hn[p]
        return carry_

    lax.fori_loop(0, n_ch, phase2, 0)

    y = y_s[...]
    inv_n = 1.0 / HEAD_DIM
    mean = _group_sum(y, gsum) * inv_n
    d = y - mean
    var = _group_sum(d * d, gsum) * inv_n
    yn = d * lax.rsqrt(var + GN_EPS) * lw_ref[...] + lb_ref[...]
    o_ref[0] = ((yn + bonus) * g).astype(BF16)


def _rwkv(pr, shift0, h0, prm, tr, t_valid):
    b, t_pad, _ = pr.shape
    assert t_pad % tr == 0 and tr % RW_CHUNK == 0
    n_ch = tr // RW_CHUNK
    tri = np.zeros((tr, tr), np.float32)
    for c in range(n_ch):
        tri[c * RW_CHUNK:(c + 1) * RW_CHUNK, c * RW_CHUNK:(c + 1) * RW_CHUNK] = np.tril(
            np.ones((RW_CHUNK, RW_CHUNK), np.float32))
    tri = jnp.asarray(tri, BF16)
    vec = _const_spec((1, RW_WIDTH))
    mat = lambda dt: pltpu.VMEM((n_ch, N_PAIR, PAIR, PAIR), dt)
    tile = lambda: pltpu.VMEM((tr, RW_WIDTH), F32)
    return pl.pallas_call(
        functools.partial(_rwkv_kernel, tr=tr, t_valid=t_valid),
        grid=(b, t_pad // tr),
        in_specs=[pl.BlockSpec((1, tr, RW_PROJ), lambda bi, ti: (bi, ti, 0)),
                  pl.BlockSpec((1, 1, RW_PROJ), lambda bi, ti: (bi, 0, 0)),
                  pl.BlockSpec((1, N_PAIR, PAIR, PAIR), lambda bi, ti: (bi, 0, 0, 0)),
                  _const_spec((1, RW_PROJ)), vec, vec, vec, vec, vec, vec, vec,
                  _const_spec((W_LORA + A_LORA, RW_WIDTH)), _const_spec((W_LORA + A_LORA, RW_WIDTH)),
                  _const_spec((G_LORA, RW_WIDTH)), _const_spec((LANES, LANES)),
                  _const_spec((tr, tr))],
        out_specs=[pl.BlockSpec((1, tr, RW_WIDTH), lambda bi, ti: (bi, ti, 0)),
                   pl.BlockSpec((1, N_PAIR, PAIR, PAIR), lambda bi, ti: (bi, 0, 0, 0))],
        out_shape=[jax.ShapeDtypeStruct((b, t_pad, RW_WIDTH), BF16),
                   jax.ShapeDtypeStruct((b, N_PAIR, PAIR, PAIR), F32)],
        scratch_shapes=[pltpu.VMEM((8 + tr, RW_PROJ), F32), pltpu.VMEM((1, RW_PROJ), F32),
                        tile(), tile(), tile(), tile(), tile(), tile(), tile(),
                        mat(BF16), mat(F32), mat(BF16), mat(F32), mat(BF16), mat(F32), mat(BF16),
                        pltpu.VMEM((n_ch, N_PAIR, PAIR, 1), F32)],
        compiler_params=_cparams(("arbitrary", "arbitrary")),
        name="rwkv",
    )(pr, shift0, h0, prm["mu"], prm["w0"], prm["a0"], prm["k_k"], prm["k_a"], prm["r_k"],
      prm["lnx_w"], prm["lnx_b"], prm["w2p"], prm["a2p"], prm["g2"], prm["gsum"], tri)


FF_COLS = 256


def _ffn_kernel(x_ref, oa_ref, orw_ref, c0_ref, wo_ref, g_ref, wu_ref, cw_ref, cb_ref, wd_ref,
                y_ref, cn_ref, zbuf, cbuf, h_ref, x1_ref, *, tm, ts, d_ff):
    off = zbuf.shape[1] - tm
    da = oa_ref.shape[1]

    @pl.when(pl.program_id(0) == 0)
    def _():
        cbuf[...] = c0_ref[...]

    x1 = x_ref[...] + _dot(oa_ref[...], wo_ref[0:da, :]) + _dot(orw_ref[...], wo_ref[da:, :])
    x1_ref[...] = x1
    ms = jnp.mean(x1 * x1, axis=-1, keepdims=True)
    h_ref[...] = (x1 * lax.rsqrt(ms + NORM_EPS) * g_ref[...]).astype(BF16)

    def up_proj(c0, zb):
        cols = slice(c0, c0 + FF_COLS)
        zb[off - 2 * ts:off, :] = cbuf[:, cols]
        zb[off:off + tm, :] = _dot(h_ref[...], wu_ref[:, cols])
        cbuf[:, cols] = zb[off + tm - 2 * ts:off + tm, :]

    def conv_cols(c0, zb):
        cols = slice(c0, c0 + FF_COLS)
        z2 = zb[off - 2 * ts:off - 2 * ts + tm, :]
        z1 = zb[off - ts:off - ts + tm, :]
        z = zb[off:off + tm, :]
        return (cb_ref[:, cols] + z2 * cw_ref[0:1, cols] + z1 * cw_ref[1:2, cols]
                + z * cw_ref[2:3, cols])

    def stage(c):
        up_proj(c * FF_COLS, zbuf.at[2 * (c % 2)])
        up_proj(d_ff + c * FF_COLS, zbuf.at[2 * (c % 2) + 1])

    n_chunks = d_ff // FF_COLS
    acc = None
    stage(0)
    for c in range(n_chunks):
        if c + 1 < n_chunks:
            stage(c + 1)
        gate = conv_cols(c * FF_COLS, zbuf.at[2 * (c % 2)])
        up = conv_cols(d_ff + c * FF_COLS, zbuf.at[2 * (c % 2) + 1])
        act = (gate * _sigmoid(gate) * up).astype(BF16)
        part = _dot(act, wd_ref[c * FF_COLS:(c + 1) * FF_COLS, :])
        acc = part if acc is None else acc + part

    y_ref[...] = x1_ref[...] + acc
    cn_ref[...] = cbuf[...]


def _ffn(x, oa, orw, conv0, prm, tm, ts):
    rows, d = x.shape
    d_ff = prm["w_down"].shape[0]
    assert rows % tm == 0 and d_ff % FF_COLS == 0 and (ts == 1 or ts % 8 == 0)
    off = -(-2 * ts // 8) * 8
    row = lambda w: pl.BlockSpec((tm, w), lambda i: (i, 0))
    return pl.pallas_call(
        functools.partial(_ffn_kernel, tm=tm, ts=ts, d_ff=d_ff),
        grid=(rows // tm,),
        in_specs=[row(d), row(DA_WIDTH), row(RW_WIDTH), _const_spec((2 * ts, 2 * d_ff)),
                  _resident_spec((DA_WIDTH + RW_WIDTH, d)), _const_spec((1, d)),
                  _resident_spec((d, 2 * d_ff)), _const_spec((CONV_W, 2 * d_ff)),
                  _const_spec((1, 2 * d_ff)), _resident_spec((d_ff, d))],
        out_specs=[row(d), _const_spec((2 * ts, 2 * d_ff))],
        out_shape=[jax.ShapeDtypeStruct((rows, d), F32),
                   jax.ShapeDtypeStruct((2 * ts, 2 * d_ff), F32)],
        scratch_shapes=[pltpu.VMEM((4, off + tm, FF_COLS), F32), pltpu.VMEM((2 * ts, 2 * d_ff), F32),
                        pltpu.VMEM((tm, d), BF16), pltpu.VMEM((tm, d), F32)],
        compiler_params=_cparams(("arbitrary",)),
        name="ffn",
    )(x, oa, orw, conv0, prm["w_out"], prm["ln2_g"], prm["w_up"], prm["conv_w"], prm["conv_b"],
      prm["w_down"])


def _rel_bucket(rel):
    nb = N_BUCKETS // 2
    max_exact = nb // 2
    bucket = jnp.where(rel > 0, nb, 0)
    n = jnp.abs(rel)
    nf = jnp.maximum(n, 1).astype(F32)
    large = max_exact + (jnp.log(nf / max_exact) / math.log(MAX_DISTANCE / max_exact)
                         * (nb - max_exact)).astype(jnp.int32)
    large = jnp.minimum(large, nb - 1)
    return bucket + jnp.where(n < max_exact, n, large)


def _bias_table(rel_bias, q_pos, k_pos, mask):
    n_q, n_k = len(q_pos), len(k_pos)
    assert np.all(np.diff(q_pos) == 1) and np.all(np.diff(k_pos) == 1)
    span = n_q + n_k - 1
    rel = jnp.asarray(int(k_pos[0]) - int(q_pos[0]) - (n_q - 1) + np.arange(span), jnp.int32)
    onehot = _rel_bucket(rel)[None, :, None] == jnp.arange(N_BUCKETS, dtype=jnp.int32)
    table = jnp.transpose(rel_bias).astype(F32)[:, None, :]
    w = jnp.sum(jnp.where(onehot, table, 0.0), axis=-1) * LOG2E
    n_h = w.shape[0]
    x = jnp.pad(w, ((0, 0), (0, 1)))
    skew = jnp.tile(x, (1, n_q))[:, :n_q * span].reshape(n_h, n_q, span)
    bias = skew[:, :, n_q - 1:n_q - 1 + n_k]
    return jnp.where(jnp.asarray(mask)[None], bias, NEG_INF)


def _ext_chunk(pos):
    return np.where(pos < N_META, -1, (pos - N_META) // CHUNK)


def _prompt_bias(rel_bias):
    tq = ATT_T
    fr = np.arange(tq) + N_META
    meta = np.arange(N_META)
    causal = _ext_chunk(fr)[None, :] <= _ext_chunk(fr)[:, None]
    tr = lambda b: jnp.swapaxes(b, 1, 2)
    b0 = tr(_bias_table(rel_bias, fr, fr, causal))
    bm1 = tr(_bias_table(rel_bias, fr + tq, fr, np.ones((tq, tq), bool)))
    bq0 = tr(_bias_table(rel_bias, fr, meta, np.ones((tq, N_META), bool)))
    bmm = _bias_table(rel_bias, meta, meta, np.ones((N_META, N_META), bool))
    assert tq + 1 >= MAX_DISTANCE
    cfar = rel_bias[_rel_bucket(jnp.asarray(-(tq + 1), jnp.int32))].astype(F32) * LOG2E
    return cfar, b0, bm1, bq0, bmm


def _decode_bias(rel_bias, n_cache, nq):
    k_pos = np.arange(n_cache + nq)
    q_pos = k_pos[n_cache:]
    mask = _ext_chunk(k_pos)[None, :] <= _ext_chunk(q_pos)[:, None]
    return _bias_table(rel_bias, q_pos, k_pos, mask)


def _state_to_pairs(s):
    b = s.shape[0]
    st = jnp.swapaxes(s, -1, -2).reshape(b, N_PAIR, 2, HEAD_DIM, HEAD_DIM)
    eye2 = jnp.eye(2, dtype=s.dtype)
    return jnp.einsum('bphkv,hg->bphkgv', st, eye2).reshape(b, N_PAIR, PAIR, PAIR)


def _pairs_to_state(hp):
    b = hp.shape[0]
    h6 = hp.reshape(b, N_PAIR, 2, HEAD_DIM, 2, HEAD_DIM)
    diag = jnp.stack([h6[:, :, 0, :, 0, :], h6[:, :, 1, :, 1, :]], axis=2)
    return jnp.swapaxes(diag, -1, -2).reshape(b, RW_HEADS, HEAD_DIM, HEAD_DIM)


def _block_ones(n, blk, dtype):
    idx = np.arange(n) // blk
    return jnp.asarray((idx[:, None] == idx[None, :]).astype(np.float32), dtype)


def kernel(x_prompt, x_sample, cache_k, cache_v, state_rwkv, state_shift, state_conv, meta_tokens,
           rel_bias, ln1_g, w_in, q_norm_g, k_norm_g, lam_q1, lam_k1, lam_q2, lam_k2, subln_g,
           mu_shift, w0, w2, a0, a2, g2, k_k, k_a, r_k, lnx_w, lnx_b, w_out, ln2_g, w_up, conv_w,
           conv_b, w_down):
    bp, seq, d = x_prompt.shape
    db, dt, _ = x_sample.shape
    depth = w_in.shape[0]
    d_ff = w_down.shape[1]
    n_cache = cache_k.shape[2]
    assert bp == 1 and dt == N_META, "the meta stream rides with the decode streams"
    assert cache_k.shape[3] == DA_HEADS and cache_k.shape[4] == 2 * HEAD_DIM
    nb = db + 1
    nb_pad = -(-nb // 8) * 8

    cfar, b0, bm1, bq0, bmm = _prompt_bias(rel_bias)
    bias_dec = _decode_bias(rel_bias, n_cache, dt)
    gsum = _block_ones(LANES, HEAD_DIM, BF16)
    gmean = gsum * (1.0 / HEAD_DIM)
    zrow = lambda n: jnp.zeros((n, RW_WIDTH), BF16)

    x_f = x_prompt[0]
    x_s = jnp.concatenate([x_sample, meta_tokens.astype(x_sample.dtype)[None]], axis=0)
    outs = [[] for _ in range(10)]
    for l in range(depth):
        lam_init = 0.8 - 0.6 * math.exp(-0.3 * l)
        lam = (jnp.exp(jnp.sum(lam_q1[l].astype(F32) * lam_k1[l].astype(F32)))
               - jnp.exp(jnp.sum(lam_q2[l].astype(F32) * lam_k2[l].astype(F32))) + lam_init).reshape(1)
        out_scale = 1.0 - lam_init
        tile128 = lambda g_: jnp.tile(g_.reshape(1, -1), (1, DA_WIDTH // g_.shape[-1]))
        qg, kg = tile128(q_norm_g[l]), tile128(k_norm_g[l])
        sg = subln_g[l].reshape(1, LANES)
        w_in_bf = w_in[l].astype(BF16)
        rw = {
            "mu": mu_shift[l].reshape(1, -1), "w0": w0[l].reshape(1, -1), "a0": a0[l].reshape(1, -1),
            "k_k": k_k[l].reshape(1, -1), "k_a": k_a[l].reshape(1, -1), "r_k": r_k[l].reshape(1, -1),
            "lnx_w": lnx_w[l].reshape(1, -1), "lnx_b": lnx_b[l].reshape(1, -1),
            "w2p": jnp.concatenate([w2[l].astype(BF16), zrow(A_LORA)], axis=0),
            "a2p": jnp.concatenate([zrow(W_LORA), a2[l].astype(BF16)], axis=0),
            "g2": g2[l].astype(BF16), "gsum": gsum,
        }
        ff = {
            "w_out": w_out[l].astype(BF16), "ln2_g": ln2_g[l].reshape(1, -1),
            "w_up": w_up[l].astype(BF16), "conv_w": conv_w[l], "conv_b": conv_b[l].reshape(1, -1),
            "w_down": w_down[l].astype(BF16),
        }

        q_f, k_f, v_f, pr_f, kb_t, vt_t = _proj(x_f, ln1_g[l].reshape(1, -1), w_in_bf, qg, kg, gmean,
                                                 512, True)
        q_s, k_s, v_s, pr_s = _proj(x_s.reshape(nb * dt, d), ln1_g[l].reshape(1, -1), w_in_bf, qg, kg,
                                    gmean, nb * dt, False)
        m0 = db * dt

        kb_m = k_s[m0:].astype(BF16)
        vb_m = v_s[m0:].astype(BF16)
        o_f, o_m = _attn_prompt(cfar, lam, q_f, kb_t, vt_t, kb_m, vb_m, jnp.transpose(vb_m), q_s[m0:],
                                b0, bm1, bq0, bmm, sg, out_scale)
        o_d = _attn_decode(lam, q_s, cache_k, cache_v, l, k_s, v_s, bias_dec, sg, out_scale, dt)
        o_s = jnp.concatenate([o_d, o_m], axis=0)

        pr_s3 = pr_s.reshape(nb, dt, RW_PROJ)
        pr_pad = jnp.pad(pr_s3, ((0, 0), (0, RW_CHUNK - dt), (0, 0)))
        shift_s = jnp.concatenate([state_shift[l], jnp.zeros((1, RW_PROJ), F32)], axis=0)[:, None, :]
        h_s = _state_to_pairs(jnp.concatenate(
            [state_rwkv[l], jnp.zeros((1,) + state_rwkv.shape[2:], F32)], axis=0))
        orw_s, hn_s = _rwkv(pr_pad, shift_s, h_s, rw, RW_CHUNK, dt)
        orw_f, hn_f = _rwkv(pr_f[None], pr_s3[db:, dt - 1:dt, :], hn_s[db:], rw, 512, seq)

        def time_major(a):
            a = jnp.pad(a.reshape(nb, dt, -1), ((0, nb_pad - nb), (0, 0), (0, 0)))
            return jnp.swapaxes(a, 0, 1).reshape(dt * nb_pad, -1)

        conv_s = jnp.concatenate([state_conv[l], jnp.zeros((1, CONV_W - 1, 2 * d_ff), F32)], axis=0)
        conv_s = jnp.pad(conv_s, ((0, nb_pad - nb), (0, 0), (0, 0)))
        conv_s = jnp.swapaxes(conv_s, 0, 1).reshape(2 * nb_pad, 2 * d_ff)
        y_s, cn_s = _ffn(time_major(x_s), time_major(o_s), time_major(orw_s[:, :dt]), conv_s, ff,
                         dt * nb_pad, nb_pad)
        cn_s = jnp.swapaxes(cn_s.reshape(2, nb_pad, 2 * d_ff), 0, 1)
        y_f, cn_f = _ffn(x_f, o_f, orw_f[0], cn_s[db], ff, 512, 1)
        y_s = jnp.swapaxes(y_s.reshape(dt, nb_pad, d), 0, 1)[:nb]

        hw = (DA_HEADS, 2 * HEAD_DIM)
        outs[0].append(jnp.concatenate([k_s[m0:].reshape(N_META, *hw), k_f], axis=0)[None])
        outs[1].append(jnp.concatenate([v_s[m0:].reshape(N_META, *hw), v_f], axis=0)[None])
        outs[2].append(_pairs_to_state(hn_f))
        outs[3].append(pr_f[seq - 1:seq])
        outs[4].append(cn_f[None])
        outs[5].append(k_s[:m0].reshape(db, dt, *hw))
        outs[6].append(v_s[:m0].reshape(db, dt, *hw))
        outs[7].append(_pairs_to_state(hn_s[:db]))
        outs[8].append(pr_s3[:db, dt - 1])
        outs[9].append(cn_s[:db])
        x_f, x_s = y_f, y_s

    return (x_f[None], x_s[:db], *[jnp.stack(o) for o in outs])
```

```python
import functools
import math

import numpy as np
import jax
import jax.numpy as jnp
from jax import lax
from jax.experimental import pallas as pl
from jax.experimental.pallas import tpu as pltpu

F32 = jnp.float32
BF16 = jnp.bfloat16

CHUNK = 64
N_META = 16
HEAD_DIM = 64
DA_HEADS = 4
RW_HEADS = 8
W_LORA = 64
A_LORA = 64
G_LORA = 128
CONV_W = 3
N_BUCKETS = 32
MAX_DISTANCE = 128
NORM_EPS = 1e-6
GN_EPS = 64e-5
NEG_INF = -1e30
LOG2E = math.log2(math.e)
DECAY_SCALE = math.exp(-0.5)

DA_WIDTH = DA_HEADS * 2 * HEAD_DIM
RW_WIDTH = RW_HEADS * HEAD_DIM
RW_PROJ = 3 * RW_WIDTH + W_LORA + A_LORA + G_LORA
LANES = 128
PAIR = 2 * HEAD_DIM
N_PAIR = RW_WIDTH // PAIR
RW_CHUNK = 64
ATT_T = 512
VMEM_LIMIT = 56 * 1024 * 1024


def _dot(a, b):
    return jnp.dot(a, b, preferred_element_type=F32)


def _dot_nt(a, b):
    return lax.dot_general(a, b, (((1,), (1,)), ((), ())), preferred_element_type=F32)


def _group_sum(x, blk):
    xb = x.astype(BF16)
    return jnp.concatenate([_dot(xb[:, LANES * p:LANES * (p + 1)], blk)
                            for p in range(x.shape[1] // LANES)], axis=1)


def _cparams(sem):
    return pltpu.CompilerParams(dimension_semantics=sem, vmem_limit_bytes=VMEM_LIMIT)


def _const_spec(shape):
    nd = len(shape)
    return pl.BlockSpec(shape, lambda *_: (0,) * nd)


def _resident_spec(shape):
    nd = len(shape)
    return pl.BlockSpec(shape, lambda *_: (0,) * nd, pipeline_mode=pl.Buffered(1))


def _proj_kernel(x_ref, g_ref, w_ref, qg_ref, kg_ref, gm_ref,
                 q_ref, k_ref, v_ref, pr_ref, *tile_refs, tm):
    x = x_ref[...]
    ms = jnp.mean(x * x, axis=-1, keepdims=True)
    h = (x * lax.rsqrt(ms + NORM_EPS) * g_ref[...]).astype(BF16)
    gm = gm_ref[...]

    def group_norm(t, g):
        ms_g = _group_sum(t * t, gm)
        return t * lax.rsqrt(ms_g + NORM_EPS) * g

    q = _dot(h, w_ref[:, 0:DA_WIDTH])
    q_ref[...] = (group_norm(q, qg_ref[...]) * (HEAD_DIM ** -0.5 * LOG2E)).astype(BF16)
    k = group_norm(_dot(h, w_ref[:, DA_WIDTH:2 * DA_WIDTH]), kg_ref[...])
    v = _dot(h, w_ref[:, 2 * DA_WIDTH:3 * DA_WIDTH])
    if len(k_ref.shape) == 3:
        for hd in range(DA_HEADS):
            k_ref[:, hd, :] = k[:, hd * LANES:(hd + 1) * LANES]
            v_ref[:, hd, :] = v[:, hd * LANES:(hd + 1) * LANES]
    else:
        k_ref[...] = k
        v_ref[...] = v
    pr_ref[...] = _dot(h, w_ref[:, 3 * DA_WIDTH:])
    if tile_refs:
        kb_ref, vt_ref = tile_refs
        kb = k.astype(BF16)
        vt = jnp.transpose(v).astype(BF16)
        for hd in range(DA_HEADS):
            for jj in range(tm // ATT_T):
                kb_ref[hd, jj] = kb[jj * ATT_T:(jj + 1) * ATT_T, hd * LANES:(hd + 1) * LANES]
                vt_ref[hd, jj] = vt[hd * LANES:(hd + 1) * LANES, jj * ATT_T:(jj + 1) * ATT_T]


def _proj(x, ln1_g, w_in_bf, qg, kg, gmean, tm, emit_tiles):
    rows, d = x.shape
    n_in = w_in_bf.shape[1]
    assert rows % tm == 0
    row = lambda w: pl.BlockSpec((tm, w), lambda i: (i, 0))
    out_specs = [row(DA_WIDTH), row(DA_WIDTH), row(DA_WIDTH), row(RW_PROJ)]
    out_shape = [jax.ShapeDtypeStruct((rows, DA_WIDTH), BF16),
                 jax.ShapeDtypeStruct((rows, DA_WIDTH), F32),
                 jax.ShapeDtypeStruct((rows, DA_WIDTH), F32),
                 jax.ShapeDtypeStruct((rows, RW_PROJ), F32)]
    if emit_tiles:
        assert tm % ATT_T == 0
        tpt = tm // ATT_T
        for o in (1, 2):
            out_specs[o] = pl.BlockSpec((tm, DA_HEADS, LANES), lambda i: (i, 0, 0))
            out_shape[o] = jax.ShapeDtypeStruct((rows, DA_HEADS, LANES), F32)
        out_specs += [pl.BlockSpec((DA_HEADS, tpt, ATT_T, LANES), lambda i: (0, i, 0, 0)),
                      pl.BlockSpec((DA_HEADS, tpt, LANES, ATT_T), lambda i: (0, i, 0, 0))]
        out_shape += [jax.ShapeDtypeStruct((DA_HEADS, rows // ATT_T, ATT_T, LANES), BF16),
                      jax.ShapeDtypeStruct((DA_HEADS, rows // ATT_T, LANES, ATT_T), BF16)]
    return pl.pallas_call(
        functools.partial(_proj_kernel, tm=tm),
        grid=(rows // tm,),
        in_specs=[row(d), _const_spec((1, d)), _resident_spec((d, n_in)),
                  _const_spec((1, DA_WIDTH)), _const_spec((1, DA_WIDTH)),
                  _const_spec((LANES, LANES))],
        out_specs=out_specs,
        out_shape=out_shape,
        compiler_params=_cparams(("arbitrary",)),
        name="proj",
    )(x, ln1_g, w_in_bf, qg, kg, gmean)


def _stack_components(q):
    lo = lax.broadcasted_iota(jnp.int32, q.shape, 1) < HEAD_DIM
    zero = jnp.zeros_like(q)
    return jnp.concatenate([jnp.where(lo, q, zero), jnp.where(lo, zero, q)], axis=0)


def _sub_layer_norm(o, g, out_scale):
    ms = jnp.mean(o * o, axis=-1, keepdims=True)
    return o * lax.rsqrt(ms + NORM_EPS) * g * out_scale


def _attn_prompt_kernel(cfar_ref, lam_ref, fix_ref, q_ref, k_ref, vt_ref, km_ref, vm_ref, vmt_ref, qm_ref,
                        b0_ref, bm1_ref, bq0_ref, bmm_ref, g_ref, o_ref, om_ref,
                        acc_ref, m_ref, l_ref, s_buf, p_buf, a_buf, *, out_scale):
    h = pl.program_id(0)
    i = pl.program_id(1)
    tq = ATT_T
    cf = cfar_ref[h]
    lam = lam_ref[0]
    qst = _stack_components(q_ref[...])

    def both(b):
        return jnp.concatenate([b, b], axis=1)

    s = _dot_nt(km_ref[...], qst)
    s = s + both(jnp.where(i == 0, bq0_ref[0], cf))
    use_bound = fix_ref[0] > 0.5
    bound = fix_ref[1]
    m0 = jnp.where(use_bound, bound, jnp.max(s, axis=0, keepdims=True))
    p = jnp.exp2(s - m0)
    m_ref[...] = m0
    l_ref[...] = jnp.sum(p, axis=0, keepdims=True)
    acc_ref[...] = _dot(vmt_ref[...], p.astype(BF16))

    def scores(idx, slot):
        s_buf[slot] = _dot_nt(k_ref[0, idx], qst)

    def softmax(slot, bias):
        s = s_buf[slot]
        m_prev = m_ref[...]
        if bias.ndim == 0:
            m_new = jnp.maximum(m_prev, jnp.max(s, axis=0, keepdims=True) + bias)
            p = jnp.exp2(s - (m_new - bias))
        else:
            s = s + both(bias)
            m_new = jnp.maximum(m_prev, jnp.max(s, axis=0, keepdims=True))
            p = jnp.exp2(s - m_new)
        alpha = jnp.exp2(m_prev - m_new)
        l_ref[...] = alpha * l_ref[...] + jnp.sum(p, axis=0, keepdims=True)
        m_ref[...] = m_new
        p_buf[slot] = p.astype(BF16)
        a_buf[slot] = alpha

    def accumulate(idx, slot):
        acc_ref[...] = a_buf[slot] * acc_ref[...] + _dot(vt_ref[0, idx], p_buf[slot])

    n_far = jnp.maximum(i - 1, 0)
    off = n_far % 2
    p_buf[1] = jnp.zeros(p_buf.shape[1:], BF16)
    a_buf[1] = jnp.ones(a_buf.shape[1:], F32)
    scores(0, 0)

    def far_body(t, carry):
        u = 2 * t - off
        scores(u + 1, 1)
        softmax(0, jnp.where(u >= 0, cf, NEG_INF))
        accumulate(jnp.maximum(u - 1, 0), 1)
        scores(u + 2, 0)
        softmax(1, cf)
        accumulate(jnp.maximum(u, 0), 0)
        return carry

    def bound_softmax(slot, bias):
        p = jnp.exp2(s_buf[slot] - (bound - bias))
        l_ref[...] += jnp.sum(p, axis=0, keepdims=True)
        p_buf[slot] = p.astype(BF16)

    def bound_accumulate(idx, slot):
        acc_ref[...] += _dot(vt_ref[0, idx], p_buf[slot])

    def far_body_bound(t, carry):
        u = 2 * t - off
        scores(u + 1, 1)
        bound_softmax(0, jnp.where(u >= 0, cf, NEG_INF))
        bound_accumulate(jnp.maximum(u - 1, 0), 1)
        scores(u + 2, 0)
        bound_softmax(1, cf)
        bound_accumulate(jnp.maximum(u, 0), 0)
        return carry

    @pl.when(use_bound)
    def _():
        lax.fori_loop(0, (n_far + 1) // 2, far_body_bound, 0)

    @pl.when(jnp.logical_not(use_bound))
    def _():
        lax.fori_loop(0, (n_far + 1) // 2, far_body, 0)

    scores(i, 1)
    softmax(0, jnp.where(i >= 1, bm1_ref[0], NEG_INF))
    accumulate(jnp.maximum(n_far - 1, 0), 1)
    softmax(1, b0_ref[0])
    accumulate(n_far, 0)
    accumulate(i, 1)

    accn = acc_ref[...] / l_ref[...]
    o_t = accn[:, 0:tq] - lam * accn[:, tq:]
    o_ref[...] = _sub_layer_norm(jnp.transpose(o_t), g_ref[...], out_scale).astype(BF16)

    @pl.when(i == 0)
    def _():
        qm = _stack_components(qm_ref[...])
        bmm = bmm_ref[0]
        sm = _dot_nt(qm, km_ref[...]) + jnp.concatenate([bmm, bmm], axis=0)
        mm = jnp.max(sm, axis=-1, keepdims=True)
        pm = jnp.exp2(sm - mm)
        accm = _dot(pm.astype(BF16), vm_ref[...]) / jnp.sum(pm, axis=-1, keepdims=True)
        om = accm[0:N_META] - lam * accm[N_META:]
        om_ref[...] = _sub_layer_norm(om, g_ref[...], out_scale).astype(BF16)


def _attn_prompt(cfar, lam, fix, q_f, kb_t, vt_t, kb_m, vb_m, vbt_m, q_m, b0, bm1, bq0, bmm, subln_g,
                 out_scale):
    tf = q_f.shape[0]
    tq = ATT_T
    assert tf % tq == 0
    n_t = tf // tq
    smem = pl.BlockSpec(memory_space=pltpu.SMEM)
    head_col = lambda rows: pl.BlockSpec((rows, LANES), lambda h, i: (0, h))
    head_tile = lambda a, b: pl.BlockSpec((1, a, b), lambda h, i: (h, 0, 0))
    return pl.pallas_call(
        functools.partial(_attn_prompt_kernel, out_scale=out_scale),
        grid=(DA_HEADS, n_t),
        in_specs=[smem, smem, smem,
                  pl.BlockSpec((tq, LANES), lambda h, i: (i, h)),
                  pl.BlockSpec((1, n_t, tq, LANES), lambda h, i: (h, 0, 0, 0)),
                  pl.BlockSpec((1, n_t, LANES, tq), lambda h, i: (h, 0, 0, 0)),
                  head_col(N_META), head_col(N_META),
                  pl.BlockSpec((LANES, N_META), lambda h, i: (h, 0)),
                  head_col(N_META),
                  head_tile(tq, tq), head_tile(tq, tq), head_tile(N_META, tq),
                  head_tile(N_META, N_META), _const_spec((1, LANES))],
        out_specs=[pl.BlockSpec((tq, LANES), lambda h, i: (i, h)), head_col(N_META)],
        out_shape=[jax.ShapeDtypeStruct((tf, DA_WIDTH), BF16),
                   jax.ShapeDtypeStruct((N_META, DA_WIDTH), BF16)],
        scratch_shapes=[pltpu.VMEM((LANES, 2 * tq), F32), pltpu.VMEM((1, 2 * tq), F32),
                        pltpu.VMEM((1, 2 * tq), F32), pltpu.VMEM((2, tq, 2 * tq), F32),
                        pltpu.VMEM((2, tq, 2 * tq), BF16), pltpu.VMEM((2, 1, 2 * tq), F32)],
        compiler_params=_cparams(("arbitrary", "arbitrary")),
        name="attn_prompt",
    )(cfar, lam, fix, q_f, kb_t, vt_t, kb_m, vb_m, vbt_m, q_m, b0, bm1, bq0, bmm, subln_g)


def _attn_decode_kernel(lam_ref, q_ref, k_hbm, v_hbm, kn_ref, vn_ref, b_ref, g_ref, o_ref,
                        kbuf, vbuf, sem, *, layer, n_main, out_scale):
    bi = pl.program_id(0)
    h = pl.program_id(1)
    n_h = pl.num_programs(1)
    step = bi * n_h + h
    slot = step % 2

    def cache_copies(b_, h_, slot_):
        return (pltpu.make_async_copy(k_hbm.at[layer, b_, :, h_, :], kbuf.at[slot_], sem.at[0, slot_]),
                pltpu.make_async_copy(v_hbm.at[layer, b_, :, h_, :], vbuf.at[slot_], sem.at[1, slot_]))

    @pl.when(step == 0)
    def _():
        for cp in cache_copies(bi, h, slot):
            cp.start()

    @pl.when(step + 1 < pl.num_programs(0) * n_h)
    def _():
        nxt = step + 1
        for cp in cache_copies(nxt // n_h, nxt % n_h, 1 - slot):
            cp.start()

    for cp in cache_copies(bi, h, slot):
        cp.wait()

    lam = lam_ref[0]
    nq = q_ref.shape[0]
    n_cache = kbuf.shape[1]
    k_ref = kbuf.at[slot]
    v_ref = vbuf.at[slot]
    qst = _stack_components(q_ref[...])
    k_main = k_ref[0:n_main, :].astype(BF16)
    v_main = v_ref[0:n_main, :].astype(BF16)
    k_tail = jnp.concatenate([k_ref[n_main:n_cache, :], kn_ref[...]], axis=0).astype(BF16)
    v_tail = jnp.concatenate([v_ref[n_main:n_cache, :], vn_ref[...]], axis=0).astype(BF16)
    bias = jnp.concatenate([b_ref[0], b_ref[0]], axis=0)
    s1 = _dot_nt(qst, k_main) + bias[:, 0:n_main]
    s2 = _dot_nt(qst, k_tail) + bias[:, n_main:]
    m = jnp.maximum(jnp.max(s1, axis=-1, keepdims=True), jnp.max(s2, axis=-1, keepdims=True))
    p1 = jnp.exp2(s1 - m)
    p2 = jnp.exp2(s2 - m)
    l = jnp.sum(p1, axis=-1, keepdims=True) + jnp.sum(p2, axis=-1, keepdims=True)
    res = (_dot(p1.astype(BF16), v_main) + _dot(p2.astype(BF16), v_tail)) / l
    o = res[0:nq] - lam * res[nq:]
    o_ref[...] = _sub_layer_norm(o, g_ref[...], out_scale).astype(BF16)


def _attn_decode(lam, q_s, cache_k, cache_v, layer, k_s, v_s, bias, subln_g, out_scale, nq):
    _, b, n_cache, n_h, _ = cache_k.shape
    n_main = (n_cache // LANES) * LANES
    n_keys = n_cache + nq
    smem = pl.BlockSpec(memory_space=pltpu.SMEM)
    hbm = pl.BlockSpec(memory_space=pl.ANY)
    new = pl.BlockSpec((nq, LANES), lambda bi, h: (bi, h))
    return pl.pallas_call(
        functools.partial(_attn_decode_kernel, layer=layer, n_main=n_main, out_scale=out_scale),
        grid=(b, n_h),
        in_specs=[smem, new, hbm, hbm, new, new,
                  pl.BlockSpec((1, nq, n_keys), lambda bi, h: (h, 0, 0)), _const_spec((1, LANES))],
        out_specs=new,
        out_shape=jax.ShapeDtypeStruct((b * nq, DA_WIDTH), BF16),
        scratch_shapes=[pltpu.VMEM((2, n_cache, LANES), F32), pltpu.VMEM((2, n_cache, LANES), F32),
                        pltpu.SemaphoreType.DMA((2, 2))],
        compiler_params=_cparams(("arbitrary", "arbitrary")),
        name="attn_decode",
    )(lam, q_s, cache_k, cache_v, k_s, v_s, bias, subln_g)


def _sigmoid(x):
    return 1.0 / (1.0 + jnp.exp(-x))


def _split2(x):
    x1 = x.astype(BF16)
    x2 = (x - x1.astype(F32)).astype(BF16)
    return x1, x2


def _stack_heads(x):
    lo = lax.broadcasted_iota(jnp.int32, x.shape, 1) < HEAD_DIM
    zero = jnp.zeros_like(x)
    return jnp.concatenate([jnp.where(lo, x, zero), jnp.where(lo, zero, x)], axis=0)


def _rwkv_kernel(pr_ref, sh0_ref, h0_ref, mu_ref, w0_ref, a0_ref, kk_ref, ka_ref, rk_ref,
                 lw_ref, lb_ref, w2_ref, a2_ref, g2_ref, gsum_ref, tri_ref,
                 o_ref, hn_ref,
                 xbuf, carry, at_s, rt_s, bt_s, kt_s, v_s, ep_s, y_s,
                 t_s, tav_s, lrb_s, lrkv_s, zbt_s, zkv_s, ar_s, pc_s,
                 *, tr, t_valid):
    c_len = RW_CHUNK
    n_ch = tr // c_len
    ti = pl.program_id(1)

    @pl.when(ti == 0)
    def _():
        carry[...] = sh0_ref[0]
        hn_ref[...] = h0_ref[...]

    x = pr_ref[0]
    xbuf[8:8 + tr, :] = x
    xbuf[7:8, :] = carry[...]
    prev = xbuf[7:7 + tr, :]
    carry[...] = pr_ref[0, tr - 1:tr, :]
    xm = x + (prev - x) * mu_ref[...]
    r = xm[:, 0:RW_WIDTH]
    kr = xm[:, RW_WIDTH:2 * RW_WIDTH]
    vr = xm[:, 2 * RW_WIDTH:3 * RW_WIDTH]
    wa = xm[:, 3 * RW_WIDTH:3 * RW_WIDTH + W_LORA + A_LORA]
    gd = xm[:, 3 * RW_WIDTH + W_LORA + A_LORA:]
    lane_wa = lax.broadcasted_iota(jnp.int32, wa.shape, 1)
    twa = jnp.where(lane_wa < W_LORA, jnp.tanh(wa), wa).astype(BF16)
    logw = -DECAY_SCALE * _sigmoid(w0_ref[...] + _dot(twa, w2_ref[...]))
    a_sig = _sigmoid(a0_ref[...] + _dot(twa, a2_ref[...]))
    g = _dot(_sigmoid(gd).astype(BF16), g2_ref[...])
    gsum = gsum_ref[...]
    kk = kr * kk_ref[...]
    kk = kk * lax.rsqrt(jnp.maximum(_group_sum(kk * kk, gsum), 1e-24))
    kr2 = kr * (1.0 + (a_sig - 1.0) * ka_ref[...])
    a_vec = -kk
    b_vec = kk * a_sig
    bonus = _group_sum(r * kr2 * rk_ref[...], gsum) * vr
    if t_valid % tr != 0:
        row = lax.broadcasted_iota(jnp.int32, (tr, 1), 0) + ti * tr
        valid = row < t_valid
        logw = jnp.where(valid, logw, 0.0)
        a_vec = jnp.where(valid, a_vec, 0.0)
        b_vec = jnp.where(valid, b_vec, 0.0)
        kr2 = jnp.where(valid, kr2, 0.0)
        vr = jnp.where(valid, vr, 0.0)
        bonus = jnp.where(valid, bonus, 0.0)
    l1, l2 = _split2(logw)
    tri = tri_ref[...]
    cs = _dot(tri, l1) + _dot(tri, l2)
    e_pos = jnp.exp(cs)
    e_neg = jnp.exp(-cs)
    at_s[...] = a_vec * jnp.exp(cs - logw)
    rt_s[...] = r * e_pos
    bt_s[...] = b_vec * e_neg
    kt_s[...] = kr2 * e_neg
    v_s[...] = vr
    ep_s[...] = e_pos

    idx_r = lax.broadcasted_iota(jnp.int32, (PAIR, PAIR), 0)
    idx_c = lax.broadcasted_iota(jnp.int32, (PAIR, PAIR), 1)
    same = (idx_r // c_len) == (idx_c // c_len)
    strict = same & ((idx_r % c_len) > (idx_c % c_len))
    incl = same & ((idx_r % c_len) >= (idx_c % c_len))
    eye = idx_r == idx_c
    eye_f = jnp.where(eye, 1.0, 0.0).astype(F32)

    zero = jnp.zeros((PAIR, PAIR), F32)
    n_par = 2 if n_ch % 2 == 0 else 1
    items = [(dc, p) for dc in range(n_par) for p in range(N_PAIR)]
    pairs = range(len(items))

    def phase1(c2, carry_):
        sel = [(pl.ds(pl.multiple_of((c2 * n_par + dc) * c_len, c_len), c_len),
                slice(PAIR * p, PAIR * (p + 1))) for dc, p in items]
        at = [at_s[rw, cl] for rw, cl in sel]
        rt = [rt_s[rw, cl] for rw, cl in sel]
        yb = [_stack_heads(bt_s[rw, cl]) for rw, cl in sel]
        yk = [_stack_heads(kt_s[rw, cl]) for rw, cl in sel]
        vst = [_stack_heads(v_s[rw, cl]).astype(BF16) for rw, cl in sel]
        pc = [ep_s[rw, cl][c_len - 1:c_len] for rw, cl in sel]
        gmat = [_dot_nt(jnp.concatenate([_stack_heads(at[p]), _stack_heads(rt[p])], axis=0).astype(BF16),
                        jnp.concatenate([yb[p], yk[p]], axis=0).astype(BF16)) for p in pairs]
        aab = [jnp.where(strict, gmat[p][0:PAIR, 0:PAIR], zero) for p in pairs]
        aak = [jnp.where(strict, gmat[p][0:PAIR, PAIR:], zero).astype(BF16) for p in pairs]
        lrb = [jnp.where(incl, gmat[p][PAIR:, 0:PAIR], zero).astype(BF16) for p in pairs]
        lrk = [jnp.where(incl, gmat[p][PAIR:, PAIR:], zero).astype(BF16) for p in pairs]
        tinv = [eye_f + aab[p] for p in pairs]
        lp = aab
        n = 1
        while 2 * n < c_len:
            lpb = [x.astype(BF16) for x in lp]
            lp = [_dot(x, x) for x in lpb]
            tinv = [tinv[p] + _dot(tinv[p].astype(BF16), lp[p].astype(BF16)) for p in pairs]
            n *= 2
        tb = [x.astype(BF16) for x in tinv]
        av = [_dot(aak[p], vst[p]).astype(BF16) for p in pairs]
        tav = [_dot(tb[p], av[p]) for p in pairs]
        lrkv = [_dot(lrk[p], vst[p]) for p in pairs]
        zbt = [jnp.transpose(yb[p] * pc[p]).astype(BF16) for p in pairs]
        zkv = [_dot(jnp.transpose(yk[p] * pc[p]).astype(BF16), vst[p]) for p in pairs]
        pcc = [jnp.sum(jnp.where(eye, jnp.broadcast_to(pc[p], (PAIR, PAIR)), zero), axis=-1, keepdims=True)
               for p in pairs]
        for q, (dc, p) in enumerate(items):
            c = c2 * n_par + dc
            t_s[c, p] = tb[q]
            tav_s[c, p] = tav[q]
            lrb_s[c, p] = lrb[q]
            lrkv_s[c, p] = lrkv[q]
            zbt_s[c, p] = zbt[q]
            zkv_s[c, p] = zkv[q]
            ar_s[c, p] = jnp.concatenate([at[q], rt[q]], axis=0).astype(BF16)
            pc_s[c, p] = pcc[q]
        return carry_

    lax.fori_loop(0, n_ch // n_par, phase1, 0)

    pairs = range(N_PAIR)

    def phase2(c, carry_):
        rows = pl.ds(pl.multiple_of(c * c_len, c_len), c_len)
        hbd = [hn_ref[0, p] for p in pairs]
        ar = [ar_s[c, p] for p in pairs]
        tb = [t_s[c, p] for p in pairs]
        tav = [tav_s[c, p] for p in pairs]
        lrb = [lrb_s[c, p] for p in pairs]
        lrkv = [lrkv_s[c, p] for p in pairs]
        zbt = [zbt_s[c, p] for p in pairs]
        zkv = [zkv_s[c, p] for p in pairs]
        pcc = [pc_s[c, p] for p in pairs]
        arh = [_dot(ar[p], hbd[p].astype(BF16)) for p in pairs]
        ub = [(_dot(tb[p], _stack_heads(arh[p][0:c_len]).astype(BF16)) + tav[p]).astype(BF16)
              for p in pairs]
        yst = [_dot(lrb[p], ub[p]) + lrkv[p] for p in pairs]
        hn = [pcc[p] * hbd[p] + _dot(zbt[p], ub[p]) + zkv[p] for p in pairs]
        for p in pairs:
            y_s[rows, PAIR * p:PAIR * (p + 1)] = arh[p][c_len:] + yst[p][0:c_len] + yst[p][c_len:]
            hn_ref[0, p] = hn[p]
        return carry_

    lax.fori_loop(0, n_ch, phase2, 0)

    y = y_s[...]
    inv_n = 1.0 / HEAD_DIM
    mean = _group_sum(y, gsum) * inv_n
    d = y - mean
    var = _group_sum(d * d, gsum) * inv_n
    yn = d * lax.rsqrt(var + GN_EPS) * lw_ref[...] + lb_ref[...]
    o_ref[0] = ((yn + bonus) * g).astype(BF16)


def _rwkv(pr, shift0, h0, prm, tr, t_valid):
    b, t_pad, _ = pr.shape
    assert t_pad % tr == 0 and tr % RW_CHUNK == 0
    n_ch = tr // RW_CHUNK
    tri = np.zeros((tr, tr), np.float32)
    for c in range(n_ch):
        tri[c * RW_CHUNK:(c + 1) * RW_CHUNK, c * RW_CHUNK:(c + 1) * RW_CHUNK] = np.tril(
            np.ones((RW_CHUNK, RW_CHUNK), np.float32))
    tri = jnp.asarray(tri, BF16)
    vec = _const_spec((1, RW_WIDTH))
    mat = lambda dt: pltpu.VMEM((n_ch, N_PAIR, PAIR, PAIR), dt)
    tile = lambda: pltpu.VMEM((tr, RW_WIDTH), F32)
    return pl.pallas_call(
        functools.partial(_rwkv_kernel, tr=tr, t_valid=t_valid),
        grid=(b, t_pad // tr),
        in_specs=[pl.BlockSpec((1, tr, RW_PROJ), lambda bi, ti: (bi, ti, 0)),
                  pl.BlockSpec((1, 1, RW_PROJ), lambda bi, ti: (bi, 0, 0)),
                  pl.BlockSpec((1, N_PAIR, PAIR, PAIR), lambda bi, ti: (bi, 0, 0, 0)),
                  _const_spec((1, RW_PROJ)), vec, vec, vec, vec, vec, vec, vec,
                  _const_spec((W_LORA + A_LORA, RW_WIDTH)), _const_spec((W_LORA + A_LORA, RW_WIDTH)),
                  _const_spec((G_LORA, RW_WIDTH)), _const_spec((LANES, LANES)),
                  _const_spec((tr, tr))],
        out_specs=[pl.BlockSpec((1, tr, RW_WIDTH), lambda bi, ti: (bi, ti, 0)),
                   pl.BlockSpec((1, N_PAIR, PAIR, PAIR), lambda bi, ti: (bi, 0, 0, 0))],
        out_shape=[jax.ShapeDtypeStruct((b, t_pad, RW_WIDTH), BF16),
                   jax.ShapeDtypeStruct((b, N_PAIR, PAIR, PAIR), F32)],
        scratch_shapes=[pltpu.VMEM((8 + tr, RW_PROJ), F32), pltpu.VMEM((1, RW_PROJ), F32),
                        tile(), tile(), tile(), tile(), tile(), tile(), tile(),
                        mat(BF16), mat(F32), mat(BF16), mat(F32), mat(BF16), mat(F32), mat(BF16),
                        pltpu.VMEM((n_ch, N_PAIR, PAIR, 1), F32)],
        compiler_params=_cparams(("arbitrary", "arbitrary")),
        name="rwkv",
    )(pr, shift0, h0, prm["mu"], prm["w0"], prm["a0"], prm["k_k"], prm["k_a"], prm["r_k"],
      prm["lnx_w"], prm["lnx_b"], prm["w2p"], prm["a2p"], prm["g2"], prm["gsum"], tri)


FF_COLS = 256


def _ffn_kernel(x_ref, oa_ref, orw_ref, c0_ref, wo_ref, g_ref, wu_ref, cw_ref, cb_ref, wd_ref,
                y_ref, cn_ref, zbuf, cbuf, h_ref, x1_ref, *, tm, ts, d_ff):
    off = zbuf.shape[1] - tm
    da = oa_ref.shape[1]

    @pl.when(pl.program_id(0) == 0)
    def _():
        cbuf[...] = c0_ref[...]

    x1 = x_ref[...] + _dot(oa_ref[...], wo_ref[0:da, :]) + _dot(orw_ref[...], wo_ref[da:, :])
    x1_ref[...] = x1
    ms = jnp.mean(x1 * x1, axis=-1, keepdims=True)
    h_ref[...] = (x1 * lax.rsqrt(ms + NORM_EPS) * g_ref[...]).astype(BF16)

    def up_proj(c0, zb):
        cols = slice(c0, c0 + FF_COLS)
        zb[off - 2 * ts:off, :] = cbuf[:, cols]
        zb[off:off + tm, :] = _dot(h_ref[...], wu_ref[:, cols])
        cbuf[:, cols] = zb[off + tm - 2 * ts:off + tm, :]

    def conv_cols(c0, zb):
        cols = slice(c0, c0 + FF_COLS)
        z2 = zb[off - 2 * ts:off - 2 * ts + tm, :]
        z1 = zb[off - ts:off - ts + tm, :]
        z = zb[off:off + tm, :]
        return (cb_ref[:, cols] + z2 * cw_ref[0:1, cols] + z1 * cw_ref[1:2, cols]
                + z * cw_ref[2:3, cols])

    def stage(c):
        up_proj(c * FF_COLS, zbuf.at[2 * (c % 2)])
        up_proj(d_ff + c * FF_COLS, zbuf.at[2 * (c % 2) + 1])

    n_chunks = d_ff // FF_COLS
    acc = None
    stage(0)
    for c in range(n_chunks):
        if c + 1 < n_chunks:
            stage(c + 1)
        gate = conv_cols(c * FF_COLS, zbuf.at[2 * (c % 2)])
        up = conv_cols(d_ff + c * FF_COLS, zbuf.at[2 * (c % 2) + 1])
        act = (gate * _sigmoid(gate) * up).astype(BF16)
        part = _dot(act, wd_ref[c * FF_COLS:(c + 1) * FF_COLS, :])
        acc = part if acc is None else acc + part

    y_ref[...] = x1_ref[...] + acc
    cn_ref[...] = cbuf[...]


def _ffn(x, oa, orw, conv0, prm, tm, ts):
    rows, d = x.shape
    d_ff = prm["w_down"].shape[0]
    assert rows % tm == 0 and d_ff % FF_COLS == 0 and (ts == 1 or ts % 8 == 0)
    off = -(-2 * ts // 8) * 8
    row = lambda w: pl.BlockSpec((tm, w), lambda i: (i, 0))
    return pl.pallas_call(
        functools.partial(_ffn_kernel, tm=tm, ts=ts, d_ff=d_ff),
        grid=(rows // tm,),
        in_specs=[row(d), row(DA_WIDTH), row(RW_WIDTH), _const_spec((2 * ts, 2 * d_ff)),
                  _resident_spec((DA_WIDTH + RW_WIDTH, d)), _const_spec((1, d)),
                  _resident_spec((d, 2 * d_ff)), _const_spec((CONV_W, 2 * d_ff)),
                  _const_spec((1, 2 * d_ff)), _resident_spec((d_ff, d))],
        out_specs=[row(d), _const_spec((2 * ts, 2 * d_ff))],
        out_shape=[jax.ShapeDtypeStruct((rows, d), F32),
                   jax.ShapeDtypeStruct((2 * ts, 2 * d_ff), F32)],
        scratch_shapes=[pltpu.VMEM((4, off + tm, FF_COLS), F32), pltpu.VMEM((2 * ts, 2 * d_ff), F32),
                        pltpu.VMEM((tm, d), BF16), pltpu.VMEM((tm, d), F32)],
        compiler_params=_cparams(("arbitrary",)),
        name="ffn",
    )(x, oa, orw, conv0, prm["w_out"], prm["ln2_g"], prm["w_up"], prm["conv_w"], prm["conv_b"],
      prm["w_down"])


def _rel_bucket(rel):
    nb = N_BUCKETS // 2
    max_exact = nb // 2
    bucket = jnp.where(rel > 0, nb, 0)
    n = jnp.abs(rel)
    nf = jnp.maximum(n, 1).astype(F32)
    large = max_exact + (jnp.log(nf / max_exact) / math.log(MAX_DISTANCE / max_exact)
                         * (nb - max_exact)).astype(jnp.int32)
    large = jnp.minimum(large, nb - 1)
    return bucket + jnp.where(n < max_exact, n, large)


def _bias_table(rel_bias, q_pos, k_pos, mask):
    n_q, n_k = len(q_pos), len(k_pos)
    assert np.all(np.diff(q_pos) == 1) and np.all(np.diff(k_pos) == 1)
    span = n_q + n_k - 1
    rel = jnp.asarray(int(k_pos[0]) - int(q_pos[0]) - (n_q - 1) + np.arange(span), jnp.int32)
    onehot = _rel_bucket(rel)[None, :, None] == jnp.arange(N_BUCKETS, dtype=jnp.int32)
    table = jnp.transpose(rel_bias).astype(F32)[:, None, :]
    w = jnp.sum(jnp.where(onehot, table, 0.0), axis=-1) * LOG2E
    n_h = w.shape[0]
    x = jnp.pad(w, ((0, 0), (0, 1)))
    skew = jnp.tile(x, (1, n_q))[:, :n_q * span].reshape(n_h, n_q, span)
    bias = skew[:, :, n_q - 1:n_q - 1 + n_k]
    return jnp.where(jnp.asarray(mask)[None], bias, NEG_INF)


BOUND_MAX_SPREAD = 100.0


def _score_bound(q_g, k_g, rel_bias):
    s_max = (HEAD_DIM ** 0.5) * LOG2E * 1.01 * jnp.max(jnp.abs(q_g)) * jnp.max(jnp.abs(k_g))
    b_hi = jnp.max(rel_bias) * LOG2E
    b_lo = jnp.min(rel_bias) * LOG2E
    use = (2.0 * s_max + (b_hi - b_lo)) <= BOUND_MAX_SPREAD
    return jnp.stack([use.astype(F32), (s_max + b_hi).astype(F32)])


def _ext_chunk(pos):
    return np.where(pos < N_META, -1, (pos - N_META) // CHUNK)


def _prompt_bias(rel_bias):
    tq = ATT_T
    fr = np.arange(tq) + N_META
    meta = np.arange(N_META)
    causal = _ext_chunk(fr)[None, :] <= _ext_chunk(fr)[:, None]
    tr = lambda b: jnp.swapaxes(b, 1, 2)
    b0 = tr(_bias_table(rel_bias, fr, fr, causal))
    bm1 = tr(_bias_table(rel_bias, fr + tq, fr, np.ones((tq, tq), bool)))
    bq0 = tr(_bias_table(rel_bias, fr, meta, np.ones((tq, N_META), bool)))
    bmm = _bias_table(rel_bias, meta, meta, np.ones((N_META, N_META), bool))
    assert tq + 1 >= MAX_DISTANCE
    cfar = rel_bias[_rel_bucket(jnp.asarray(-(tq + 1), jnp.int32))].astype(F32) * LOG2E
    return cfar, b0, bm1, bq0, bmm


def _decode_bias(rel_bias, n_cache, nq):
    k_pos = np.arange(n_cache + nq)
    q_pos = k_pos[n_cache:]
    mask = _ext_chunk(k_pos)[None, :] <= _ext_chunk(q_pos)[:, None]
    return _bias_table(rel_bias, q_pos, k_pos, mask)


def _state_to_pairs(s):
    b = s.shape[0]
    st = jnp.swapaxes(s, -1, -2).reshape(b, N_PAIR, 2, HEAD_DIM, HEAD_DIM)
    eye2 = jnp.eye(2, dtype=s.dtype)
    return jnp.einsum('bphkv,hg->bphkgv', st, eye2).reshape(b, N_PAIR, PAIR, PAIR)


def _pairs_to_state(hp):
    b = hp.shape[0]
    h6 = hp.reshape(b, N_PAIR, 2, HEAD_DIM, 2, HEAD_DIM)
    diag = jnp.stack([h6[:, :, 0, :, 0, :], h6[:, :, 1, :, 1, :]], axis=2)
    return jnp.swapaxes(diag, -1, -2).reshape(b, RW_HEADS, HEAD_DIM, HEAD_DIM)


def _block_ones(n, blk, dtype):
    idx = np.arange(n) // blk
    return jnp.asarray((idx[:, None] == idx[None, :]).astype(np.float32), dtype)


def kernel(x_prompt, x_sample, cache_k, cache_v, state_rwkv, state_shift, state_conv, meta_tokens,
           rel_bias, ln1_g, w_in, q_norm_g, k_norm_g, lam_q1, lam_k1, lam_q2, lam_k2, subln_g,
           mu_shift, w0, w2, a0, a2, g2, k_k, k_a, r_k, lnx_w, lnx_b, w_out, ln2_g, w_up, conv_w,
           conv_b, w_down):
    bp, seq, d = x_prompt.shape
    db, dt, _ = x_sample.shape
    depth = w_in.shape[0]
    d_ff = w_down.shape[1]
    n_cache = cache_k.shape[2]
    assert bp == 1 and dt == N_META, "the meta stream rides with the decode streams"
    assert cache_k.shape[3] == DA_HEADS and cache_k.shape[4] == 2 * HEAD_DIM
    nb = db + 1
    nb_pad = -(-nb // 8) * 8

    cfar, b0, bm1, bq0, bmm = _prompt_bias(rel_bias)
    bias_dec = _decode_bias(rel_bias, n_cache, dt)
    gsum = _block_ones(LANES, HEAD_DIM, BF16)
    gmean = gsum * (1.0 / HEAD_DIM)
    zrow = lambda n: jnp.zeros((n, RW_WIDTH), BF16)

    x_f = x_prompt[0]
    x_s = jnp.concatenate([x_sample, meta_tokens.astype(x_sample.dtype)[None]], axis=0)
    outs = [[] for _ in range(10)]
    for l in range(depth):
        lam_init = 0.8 - 0.6 * math.exp(-0.3 * l)
        lam = (jnp.exp(jnp.sum(lam_q1[l].astype(F32) * lam_k1[l].astype(F32)))
               - jnp.exp(jnp.sum(lam_q2[l].astype(F32) * lam_k2[l].astype(F32))) + lam_init).reshape(1)
        out_scale = 1.0 - lam_init
        tile128 = lambda g_: jnp.tile(g_.reshape(1, -1), (1, DA_WIDTH // g_.shape[-1]))
        qg, kg = tile128(q_norm_g[l]), tile128(k_norm_g[l])
        sg = subln_g[l].reshape(1, LANES)
        w_in_bf = w_in[l].astype(BF16)
        rw = {
            "mu": mu_shift[l].reshape(1, -1), "w0": w0[l].reshape(1, -1), "a0": a0[l].reshape(1, -1),
            "k_k": k_k[l].reshape(1, -1), "k_a": k_a[l].reshape(1, -1), "r_k": r_k[l].reshape(1, -1),
            "lnx_w": lnx_w[l].reshape(1, -1), "lnx_b": lnx_b[l].reshape(1, -1),
            "w2p": jnp.concatenate([w2[l].astype(BF16), zrow(A_LORA)], axis=0),
            "a2p": jnp.concatenate([zrow(W_LORA), a2[l].astype(BF16)], axis=0),
            "g2": g2[l].astype(BF16), "gsum": gsum,
        }
        ff = {
            "w_out": w_out[l].astype(BF16), "ln2_g": ln2_g[l].reshape(1, -1),
            "w_up": w_up[l].astype(BF16), "conv_w": conv_w[l], "conv_b": conv_b[l].reshape(1, -1),
            "w_down": w_down[l].astype(BF16),
        }

        q_f, k_f, v_f, pr_f, kb_t, vt_t = _proj(x_f, ln1_g[l].reshape(1, -1), w_in_bf, qg, kg, gmean,
                                                 512, True)
        q_s, k_s, v_s, pr_s = _proj(x_s.reshape(nb * dt, d), ln1_g[l].reshape(1, -1), w_in_bf, qg, kg,
                                    gmean, nb * dt, False)
        m0 = db * dt

        kb_m = k_s[m0:].astype(BF16)
        vb_m = v_s[m0:].astype(BF16)
        o_f, o_m = _attn_prompt(cfar, lam, _score_bound(q_norm_g[l], k_norm_g[l], rel_bias), q_f, kb_t, vt_t, kb_m, vb_m, jnp.transpose(vb_m), q_s[m0:],
                                b0, bm1, bq0, bmm, sg, out_scale)
        o_d = _attn_decode(lam, q_s, cache_k, cache_v, l, k_s, v_s, bias_dec, sg, out_scale, dt)
        o_s = jnp.concatenate([o_d, o_m], axis=0)

        pr_s3 = pr_s.reshape(nb, dt, RW_PROJ)
        pr_pad = jnp.pad(pr_s3, ((0, 0), (0, RW_CHUNK - dt), (0, 0)))
        shift_s = jnp.concatenate([state_shift[l], jnp.zeros((1, RW_PROJ), F32)], axis=0)[:, None, :]
        h_s = _state_to_pairs(jnp.concatenate(
            [state_rwkv[l], jnp.zeros((1,) + state_rwkv.shape[2:], F32)], axis=0))
        orw_s, hn_s = _rwkv(pr_pad, shift_s, h_s, rw, RW_CHUNK, dt)
        orw_f, hn_f = _rwkv(pr_f[None], pr_s3[db:, dt - 1:dt, :], hn_s[db:], rw, 512, seq)

        def time_major(a):
            a = jnp.pad(a.reshape(nb, dt, -1), ((0, nb_pad - nb), (0, 0), (0, 0)))
            return jnp.swapaxes(a, 0, 1).reshape(dt * nb_pad, -1)

        conv_s = jnp.concatenate([state_conv[l], jnp.zeros((1, CONV_W - 1, 2 * d_ff), F32)], axis=0)
        conv_s = jnp.pad(conv_s, ((0, nb_pad - nb), (0, 0), (0, 0)))
        conv_s = jnp.swapaxes(conv_s, 0, 1).reshape(2 * nb_pad, 2 * d_ff)
        y_s, cn_s = _ffn(time_major(x_s), time_major(o_s), time_major(orw_s[:, :dt]), conv_s, ff,
                         dt * nb_pad, nb_pad)
        cn_s = jnp.swapaxes(cn_s.reshape(2, nb_pad, 2 * d_ff), 0, 1)
        y_f, cn_f = _ffn(x_f, o_f, orw_f[0], cn_s[db], ff, 512, 1)
        y_s = jnp.swapaxes(y_s.reshape(dt, nb_pad, d), 0, 1)[:nb]

        hw = (DA_HEADS, 2 * HEAD_DIM)
        outs[0].append(jnp.concatenate([k_s[m0:].reshape(N_META, *hw), k_f], axis=0)[None])
        outs[1].append(jnp.concatenate([v_s[m0:].reshape(N_META, *hw), v_f], axis=0)[None])
        outs[2].append(_pairs_to_state(hn_f))
        outs[3].append(pr_f[seq - 1:seq])
        outs[4].append(cn_f[None])
        outs[5].append(k_s[:m0].reshape(db, dt, *hw))
        outs[6].append(v_s[:m0].reshape(db, dt, *hw))
        outs[7].append(_pairs_to_state(hn_s[:db]))
        outs[8].append(pr_s3[:db, dt - 1])
        outs[9].append(cn_s[:db])
        x_f, x_s = y_f, y_s

    return (x_f[None], x_s[:db], *[jnp.stack(o) for o in outs])
```

```python
import functools
import math

import numpy as np
import jax
import jax.numpy as jnp
from jax import lax
from jax.experimental import pallas as pl
from jax.experimental.pallas import tpu as pltpu

F32 = jnp.float32
BF16 = jnp.bfloat16

CHUNK = 64
N_META = 16
HEAD_DIM = 64
DA_HEADS = 4
RW_HEADS = 8
W_LORA = 64
A_LORA = 64
G_LORA = 128
CONV_W = 3
N_BUCKETS = 32
MAX_DISTANCE = 128
NORM_EPS = 1e-6
GN_EPS = 64e-5
NEG_INF = -1e30
LOG2E = math.log2(math.e)
DECAY_SCALE = math.exp(-0.5)

DA_WIDTH = DA_HEADS * 2 * HEAD_DIM
RW_WIDTH = RW_HEADS * HEAD_DIM
RW_PROJ = 3 * RW_WIDTH + W_LORA + A_LORA + G_LORA
LANES = 128
PAIR = 2 * HEAD_DIM
N_PAIR = RW_WIDTH // PAIR
RW_CHUNK = 64
ATT_T = 512
VMEM_LIMIT = 56 * 1024 * 1024


def _dot(a, b):
    return jnp.dot(a, b, preferred_element_type=F32)


def _dot_nt(a, b):
    return lax.dot_general(a, b, (((1,), (1,)), ((), ())), preferred_element_type=F32)


def _group_sum(x, blk):
    xb = x.astype(BF16)
    return jnp.concatenate([_dot(xb[:, LANES * p:LANES * (p + 1)], blk)
                            for p in range(x.shape[1] // LANES)], axis=1)


def _cparams(sem):
    return pltpu.CompilerParams(dimension_semantics=sem, vmem_limit_bytes=VMEM_LIMIT)


def _const_spec(shape):
    nd = len(shape)
    return pl.BlockSpec(shape, lambda *_: (0,) * nd)


def _resident_spec(shape):
    nd = len(shape)
    return pl.BlockSpec(shape, lambda *_: (0,) * nd, pipeline_mode=pl.Buffered(1))


def _proj_kernel(x_ref, g_ref, w_ref, qg_ref, kg_ref, gm_ref,
                 q_ref, k_ref, v_ref, pr_ref, *tile_refs, tm):
    x = x_ref[...]
    ms = jnp.mean(x * x, axis=-1, keepdims=True)
    h = (x * lax.rsqrt(ms + NORM_EPS) * g_ref[...]).astype(BF16)
    gm = gm_ref[...]

    def group_norm(t, g):
        ms_g = _group_sum(t * t, gm)
        return t * lax.rsqrt(ms_g + NORM_EPS) * g

    q = _dot(h, w_ref[:, 0:DA_WIDTH])
    q_ref[...] = (group_norm(q, qg_ref[...]) * (HEAD_DIM ** -0.5 * LOG2E)).astype(BF16)
    k = group_norm(_dot(h, w_ref[:, DA_WIDTH:2 * DA_WIDTH]), kg_ref[...])
    v = _dot(h, w_ref[:, 2 * DA_WIDTH:3 * DA_WIDTH])
    if len(k_ref.shape) == 3:
        for hd in range(DA_HEADS):
            k_ref[:, hd, :] = k[:, hd * LANES:(hd + 1) * LANES]
            v_ref[:, hd, :] = v[:, hd * LANES:(hd + 1) * LANES]
    else:
        k_ref[...] = k
        v_ref[...] = v
    pr_ref[...] = _dot(h, w_ref[:, 3 * DA_WIDTH:])
    if tile_refs:
        kb_ref, vt_ref = tile_refs
        kb = k.astype(BF16)
        vt = jnp.transpose(v).astype(BF16)
        for hd in range(DA_HEADS):
            for jj in range(tm // ATT_T):
                kb_ref[hd, jj] = kb[jj * ATT_T:(jj + 1) * ATT_T, hd * LANES:(hd + 1) * LANES]
                vt_ref[hd, jj] = vt[hd * LANES:(hd + 1) * LANES, jj * ATT_T:(jj + 1) * ATT_T]


def _proj(x, ln1_g, w_in_bf, qg, kg, gmean, tm, emit_tiles):
    rows, d = x.shape
    n_in = w_in_bf.shape[1]
    assert rows % tm == 0
    row = lambda w: pl.BlockSpec((tm, w), lambda i: (i, 0))
    out_specs = [row(DA_WIDTH), row(DA_WIDTH), row(DA_WIDTH), row(RW_PROJ)]
    out_shape = [jax.ShapeDtypeStruct((rows, DA_WIDTH), BF16),
                 jax.ShapeDtypeStruct((rows, DA_WIDTH), F32),
                 jax.ShapeDtypeStruct((rows, DA_WIDTH), F32),
                 jax.ShapeDtypeStruct((rows, RW_PROJ), F32)]
    if emit_tiles:
        assert tm % ATT_T == 0
        tpt = tm // ATT_T
        for o in (1, 2):
            out_specs[o] = pl.BlockSpec((tm, DA_HEADS, LANES), lambda i: (i, 0, 0))
            out_shape[o] = jax.ShapeDtypeStruct((rows, DA_HEADS, LANES), F32)
        out_specs += [pl.BlockSpec((DA_HEADS, tpt, ATT_T, LANES), lambda i: (0, i, 0, 0)),
                      pl.BlockSpec((DA_HEADS, tpt, LANES, ATT_T), lambda i: (0, i, 0, 0))]
        out_shape += [jax.ShapeDtypeStruct((DA_HEADS, rows // ATT_T, ATT_T, LANES), BF16),
                      jax.ShapeDtypeStruct((DA_HEADS, rows // ATT_T, LANES, ATT_T), BF16)]
    return pl.pallas_call(
        functools.partial(_proj_kernel, tm=tm),
        grid=(rows // tm,),
        in_specs=[row(d), _const_spec((1, d)), _resident_spec((d, n_in)),
                  _const_spec((1, DA_WIDTH)), _const_spec((1, DA_WIDTH)),
                  _const_spec((LANES, LANES))],
        out_specs=out_specs,
        out_shape=out_shape,
        compiler_params=_cparams(("arbitrary",)),
        name="proj",
    )(x, ln1_g, w_in_bf, qg, kg, gmean)


def _stack_components(q):
    lo = lax.broadcasted_iota(jnp.int32, q.shape, 1) < HEAD_DIM
    zero = jnp.zeros_like(q)
    return jnp.concatenate([jnp.where(lo, q, zero), jnp.where(lo, zero, q)], axis=0)


def _sub_layer_norm(o, g, out_scale):
    ms = jnp.mean(o * o, axis=-1, keepdims=True)
    return o * lax.rsqrt(ms + NORM_EPS) * g * out_scale


def _attn_prompt_kernel(cfar_ref, lam_ref, fix_ref, q_ref, k_ref, vt_ref, km_ref, vm_ref, vmt_ref, qm_ref,
                        b0_ref, bm1_ref, bq0_ref, bmm_ref, g_ref, o_ref, om_ref,
                        acc_ref, m_ref, l_ref, s_buf, p_buf, a_buf, *, out_scale):
    h = pl.program_id(0)
    i = pl.program_id(1)
    tq = ATT_T
    cf = cfar_ref[h]
    lam = lam_ref[0]
    qst = _stack_components(q_ref[...])

    def both(b):
        return jnp.concatenate([b, b], axis=1)

    s = _dot_nt(km_ref[...], qst)
    s = s + both(jnp.where(i == 0, bq0_ref[0], cf))
    use_bound = fix_ref[0] > 0.5
    bound = fix_ref[1]
    m0 = jnp.where(use_bound, bound, jnp.max(s, axis=0, keepdims=True))
    p = jnp.exp2(s - m0)
    m_ref[...] = m0
    l_ref[...] = jnp.sum(p, axis=0, keepdims=True)
    acc_ref[...] = _dot(vmt_ref[...], p.astype(BF16))

    def scores(idx, slot):
        s_buf[slot] = _dot_nt(k_ref[0, idx], qst)

    def softmax(slot, bias):
        s = s_buf[slot]
        m_prev = m_ref[...]
        if bias.ndim == 0:
            m_new = jnp.maximum(m_prev, jnp.max(s, axis=0, keepdims=True) + bias)
            p = jnp.exp2(s - (m_new - bias))
        else:
            s = s + both(bias)
            m_new = jnp.maximum(m_prev, jnp.max(s, axis=0, keepdims=True))
            p = jnp.exp2(s - m_new)
        alpha = jnp.exp2(m_prev - m_new)
        l_ref[...] = alpha * l_ref[...] + jnp.sum(p, axis=0, keepdims=True)
        m_ref[...] = m_new
        p_buf[slot] = p.astype(BF16)
        a_buf[slot] = alpha

    def accumulate(idx, slot):
        acc_ref[...] = a_buf[slot] * acc_ref[...] + _dot(vt_ref[0, idx], p_buf[slot])

    n_far = jnp.maximum(i - 1, 0)
    off = n_far % 2
    p_buf[1] = jnp.zeros(p_buf.shape[1:], BF16)
    a_buf[1] = jnp.ones(a_buf.shape[1:], F32)
    scores(0, 0)

    def far_body(t, carry):
        u = 2 * t - off
        scores(u + 1, 1)
        softmax(0, jnp.where(u >= 0, cf, NEG_INF))
        accumulate(jnp.maximum(u - 1, 0), 1)
        scores(u + 2, 0)
        softmax(1, cf)
        accumulate(jnp.maximum(u, 0), 0)
        return carry

    def bound_softmax(slot, bias):
        p = jnp.exp2(s_buf[slot] - (bound - bias))
        l_ref[...] += jnp.sum(p, axis=0, keepdims=True)
        p_buf[slot] = p.astype(BF16)

    def bound_accumulate(idx, slot):
        acc_ref[...] += _dot(vt_ref[0, idx], p_buf[slot])

    def far_body_bound(t, carry):
        u = 2 * t - off
        scores(u + 1, 1)
        bound_softmax(0, jnp.where(u >= 0, cf, NEG_INF))
        bound_accumulate(jnp.maximum(u - 1, 0), 1)
        scores(u + 2, 0)
        bound_softmax(1, cf)
        bound_accumulate(jnp.maximum(u, 0), 0)
        return carry

    @pl.when(use_bound)
    def _():
        lax.fori_loop(0, (n_far + 1) // 2, far_body_bound, 0)

    @pl.when(jnp.logical_not(use_bound))
    def _():
        lax.fori_loop(0, (n_far + 1) // 2, far_body, 0)

    scores(i, 1)
    softmax(0, jnp.where(i >= 1, bm1_ref[0], NEG_INF))
    accumulate(jnp.maximum(n_far - 1, 0), 1)
    softmax(1, b0_ref[0])
    accumulate(n_far, 0)
    accumulate(i, 1)

    accn = acc_ref[...] / l_ref[...]
    o_t = accn[:, 0:tq] - lam * accn[:, tq:]
    o_ref[...] = _sub_layer_norm(jnp.transpose(o_t), g_ref[...], out_scale).astype(BF16)

    @pl.when(i == 0)
    def _():
        qm = _stack_components(qm_ref[...])
        bmm = bmm_ref[0]
        sm = _dot_nt(qm, km_ref[...]) + jnp.concatenate([bmm, bmm], axis=0)
        mm = jnp.max(sm, axis=-1, keepdims=True)
        pm = jnp.exp2(sm - mm)
        accm = _dot(pm.astype(BF16), vm_ref[...]) / jnp.sum(pm, axis=-1, keepdims=True)
        om = accm[0:N_META] - lam * accm[N_META:]
        om_ref[...] = _sub_layer_norm(om, g_ref[...], out_scale).astype(BF16)


def _attn_prompt(cfar, lam, fix, q_f, kb_t, vt_t, kb_m, vb_m, vbt_m, q_m, b0, bm1, bq0, bmm, subln_g,
                 out_scale):
    tf = q_f.shape[0]
    tq = ATT_T
    assert tf % tq == 0
    n_t = tf // tq
    smem = pl.BlockSpec(memory_space=pltpu.SMEM)
    head_col = lambda rows: pl.BlockSpec((rows, LANES), lambda h, i: (0, h))
    head_tile = lambda a, b: pl.BlockSpec((1, a, b), lambda h, i: (h, 0, 0))
    return pl.pallas_call(
        functools.partial(_attn_prompt_kernel, out_scale=out_scale),
        grid=(DA_HEADS, n_t),
        in_specs=[smem, smem, smem,
                  pl.BlockSpec((tq, LANES), lambda h, i: (i, h)),
                  pl.BlockSpec((1, n_t, tq, LANES), lambda h, i: (h, 0, 0, 0)),
                  pl.BlockSpec((1, n_t, LANES, tq), lambda h, i: (h, 0, 0, 0)),
                  head_col(N_META), head_col(N_META),
                  pl.BlockSpec((LANES, N_META), lambda h, i: (h, 0)),
                  head_col(N_META),
                  head_tile(tq, tq), head_tile(tq, tq), head_tile(N_META, tq),
                  head_tile(N_META, N_META), _const_spec((1, LANES))],
        out_specs=[pl.BlockSpec((tq, LANES), lambda h, i: (i, h)), head_col(N_META)],
        out_shape=[jax.ShapeDtypeStruct((tf, DA_WIDTH), BF16),
                   jax.ShapeDtypeStruct((N_META, DA_WIDTH), BF16)],
        scratch_shapes=[pltpu.VMEM((LANES, 2 * tq), F32), pltpu.VMEM((1, 2 * tq), F32),
                        pltpu.VMEM((1, 2 * tq), F32), pltpu.VMEM((2, tq, 2 * tq), F32),
                        pltpu.VMEM((2, tq, 2 * tq), BF16), pltpu.VMEM((2, 1, 2 * tq), F32)],
        compiler_params=_cparams(("arbitrary", "arbitrary")),
        name="attn_prompt",
    )(cfar, lam, fix, q_f, kb_t, vt_t, kb_m, vb_m, vbt_m, q_m, b0, bm1, bq0, bmm, subln_g)


def _attn_decode_kernel(lam_ref, q_ref, k_hbm, v_hbm, kn_ref, vn_ref, b_ref, g_ref, o_ref,
                        kbuf, vbuf, sem, *, layer, n_main, out_scale):
    bi = pl.program_id(0)
    h = pl.program_id(1)
    n_h = pl.num_programs(1)
    step = bi * n_h + h
    slot = step % 2

    def cache_copies(b_, h_, slot_):
        return (pltpu.make_async_copy(k_hbm.at[layer, b_, :, h_, :], kbuf.at[slot_], sem.at[0, slot_]),
                pltpu.make_async_copy(v_hbm.at[layer, b_, :, h_, :], vbuf.at[slot_], sem.at[1, slot_]))

    @pl.when(step == 0)
    def _():
        for cp in cache_copies(bi, h, slot):
            cp.start()

    @pl.when(step + 1 < pl.num_programs(0) * n_h)
    def _():
        nxt = step + 1
        for cp in cache_copies(nxt // n_h, nxt % n_h, 1 - slot):
            cp.start()

    for cp in cache_copies(bi, h, slot):
        cp.wait()

    lam = lam_ref[0]
    nq = q_ref.shape[0]
    n_cache = kbuf.shape[1]
    k_ref = kbuf.at[slot]
    v_ref = vbuf.at[slot]
    qst = _stack_components(q_ref[...])
    k_main = k_ref[0:n_main, :].astype(BF16)
    v_main = v_ref[0:n_main, :].astype(BF16)
    k_tail = jnp.concatenate([k_ref[n_main:n_cache, :], kn_ref[...]], axis=0).astype(BF16)
    v_tail = jnp.concatenate([v_ref[n_main:n_cache, :], vn_ref[...]], axis=0).astype(BF16)
    bias = jnp.concatenate([b_ref[0], b_ref[0]], axis=0)
    s1 = _dot_nt(qst, k_main) + bias[:, 0:n_main]
    s2 = _dot_nt(qst, k_tail) + bias[:, n_main:]
    m = jnp.maximum(jnp.max(s1, axis=-1, keepdims=True), jnp.max(s2, axis=-1, keepdims=True))
    p1 = jnp.exp2(s1 - m)
    p2 = jnp.exp2(s2 - m)
    l = jnp.sum(p1, axis=-1, keepdims=True) + jnp.sum(p2, axis=-1, keepdims=True)
    res = (_dot(p1.astype(BF16), v_main) + _dot(p2.astype(BF16), v_tail)) / l
    o = res[0:nq] - lam * res[nq:]
    o_ref[...] = _sub_layer_norm(o, g_ref[...], out_scale).astype(BF16)


def _attn_decode(lam, q_s, cache_k, cache_v, layer, k_s, v_s, bias, subln_g, out_scale, nq):
    _, b, n_cache, n_h, _ = cache_k.shape
    n_main = (n_cache // LANES) * LANES
    n_keys = n_cache + nq
    smem = pl.BlockSpec(memory_space=pltpu.SMEM)
    hbm = pl.BlockSpec(memory_space=pl.ANY)
    new = pl.BlockSpec((nq, LANES), lambda bi, h: (bi, h))
    return pl.pallas_call(
        functools.partial(_attn_decode_kernel, layer=layer, n_main=n_main, out_scale=out_scale),
        grid=(b, n_h),
        in_specs=[smem, new, hbm, hbm, new, new,
                  pl.BlockSpec((1, nq, n_keys), lambda bi, h: (h, 0, 0)), _const_spec((1, LANES))],
        out_specs=new,
        out_shape=jax.ShapeDtypeStruct((b * nq, DA_WIDTH), BF16),
        scratch_shapes=[pltpu.VMEM((2, n_cache, LANES), F32), pltpu.VMEM((2, n_cache, LANES), F32),
                        pltpu.SemaphoreType.DMA((2, 2))],
        compiler_params=_cparams(("arbitrary", "arbitrary")),
        name="attn_decode",
    )(lam, q_s, cache_k, cache_v, k_s, v_s, bias, subln_g)


def _sigmoid(x):
    return 1.0 / (1.0 + jnp.exp(-x))


def _split2(x):
    x1 = x.astype(BF16)
    x2 = (x - x1.astype(F32)).astype(BF16)
    return x1, x2


def _stack_heads(x):
    lo = lax.broadcasted_iota(jnp.int32, x.shape, 1) < HEAD_DIM
    zero = jnp.zeros_like(x)
    return jnp.concatenate([jnp.where(lo, x, zero), jnp.where(lo, zero, x)], axis=0)


def _rwkv_kernel(pr_ref, sh0_ref, h0_ref, mu_ref, w0_ref, a0_ref, kk_ref, ka_ref, rk_ref,
                 lw_ref, lb_ref, w2_ref, a2_ref, g2_ref, gsum_ref, tri_ref,
                 o_ref, sn_ref,
                 h_st, xbuf, carry, at_s, rt_s, bt_s, kt_s, v_s, ep_s, y_s,
                 t_s, tav_s, lrb_s, lrkv_s, zbt_s, zkv_s, ar_s, pc_s,
                 *, tr, t_valid, n_state):
    c_len = RW_CHUNK
    n_ch = tr // c_len
    ti = pl.program_id(1)

    @pl.when(ti == 0)
    def _():
        has_state = pl.program_id(0) < n_state
        carry[...] = jnp.where(has_state, sh0_ref[0], 0.0)
        hd_r = lax.broadcasted_iota(jnp.int32, (PAIR, PAIR), 0) // HEAD_DIM
        hd_c = lax.broadcasted_iota(jnp.int32, (PAIR, PAIR), 1) // HEAD_DIM
        for p in range(N_PAIR):
            s_p = h0_ref[0, p]
            blk = jnp.where((hd_r == hd_c) & has_state, jnp.concatenate([s_p, s_p], axis=1), 0.0)
            h_st[p] = jnp.transpose(blk)

    x = pr_ref[0]
    xbuf[8:8 + tr, :] = x
    xbuf[7:8, :] = carry[...]
    prev = xbuf[7:7 + tr, :]
    carry[...] = pr_ref[0, tr - 1:tr, :]
    xm = x + (prev - x) * mu_ref[...]
    r = xm[:, 0:RW_WIDTH]
    kr = xm[:, RW_WIDTH:2 * RW_WIDTH]
    vr = xm[:, 2 * RW_WIDTH:3 * RW_WIDTH]
    wa = xm[:, 3 * RW_WIDTH:3 * RW_WIDTH + W_LORA + A_LORA]
    gd = xm[:, 3 * RW_WIDTH + W_LORA + A_LORA:]
    lane_wa = lax.broadcasted_iota(jnp.int32, wa.shape, 1)
    twa = jnp.where(lane_wa < W_LORA, jnp.tanh(wa), wa).astype(BF16)
    logw = -DECAY_SCALE * _sigmoid(w0_ref[...] + _dot(twa, w2_ref[...]))
    a_sig = _sigmoid(a0_ref[...] + _dot(twa, a2_ref[...]))
    g = _dot(_sigmoid(gd).astype(BF16), g2_ref[...])
    gsum = gsum_ref[...]
    kk = kr * kk_ref[...]
    kk = kk * lax.rsqrt(jnp.maximum(_group_sum(kk * kk, gsum), 1e-24))
    kr2 = kr * (1.0 + (a_sig - 1.0) * ka_ref[...])
    a_vec = -kk
    b_vec = kk * a_sig
    bonus = _group_sum(r * kr2 * rk_ref[...], gsum) * vr
    if t_valid % tr != 0:
        row = lax.broadcasted_iota(jnp.int32, (tr, 1), 0) + ti * tr
        valid = row < t_valid
        logw = jnp.where(valid, logw, 0.0)
        a_vec = jnp.where(valid, a_vec, 0.0)
        b_vec = jnp.where(valid, b_vec, 0.0)
        kr2 = jnp.where(valid, kr2, 0.0)
        vr = jnp.where(valid, vr, 0.0)
        bonus = jnp.where(valid, bonus, 0.0)
    l1, l2 = _split2(logw)
    tri = tri_ref[...]
    cs = _dot(tri, l1) + _dot(tri, l2)
    e_pos = jnp.exp(cs)
    e_neg = jnp.exp(-cs)
    at_s[...] = a_vec * jnp.exp(cs - logw)
    rt_s[...] = r * e_pos
    bt_s[...] = b_vec * e_neg
    kt_s[...] = kr2 * e_neg
    v_s[...] = vr
    ep_s[...] = e_pos

    idx_r = lax.broadcasted_iota(jnp.int32, (PAIR, PAIR), 0)
    idx_c = lax.broadcasted_iota(jnp.int32, (PAIR, PAIR), 1)
    same = (idx_r // c_len) == (idx_c // c_len)
    strict = same & ((idx_r % c_len) > (idx_c % c_len))
    incl = same & ((idx_r % c_len) >= (idx_c % c_len))
    eye = idx_r == idx_c
    eye_f = jnp.where(eye, 1.0, 0.0).astype(F32)

    zero = jnp.zeros((PAIR, PAIR), F32)
    n_par = 2 if n_ch % 2 == 0 else 1
    items = [(dc, p) for dc in range(n_par) for p in range(N_PAIR)]
    pairs = range(len(items))

    def phase1(c2, carry_):
        sel = [(pl.ds(pl.multiple_of((c2 * n_par + dc) * c_len, c_len), c_len),
                slice(PAIR * p, PAIR * (p + 1))) for dc, p in items]
        at = [at_s[rw, cl] for rw, cl in sel]
        rt = [rt_s[rw, cl] for rw, cl in sel]
        yb = [_stack_heads(bt_s[rw, cl]) for rw, cl in sel]
        yk = [_stack_heads(kt_s[rw, cl]) for rw, cl in sel]
        vst = [_stack_heads(v_s[rw, cl]).astype(BF16) for rw, cl in sel]
        pc = [ep_s[rw, cl][c_len - 1:c_len] for rw, cl in sel]
        gmat = [_dot_nt(jnp.concatenate([_stack_heads(at[p]), _stack_heads(rt[p])], axis=0).astype(BF16),
                        jnp.concatenate([yb[p], yk[p]], axis=0).astype(BF16)) for p in pairs]
        aab = [jnp.where(strict, gmat[p][0:PAIR, 0:PAIR], zero) for p in pairs]
        aak = [jnp.where(strict, gmat[p][0:PAIR, PAIR:], zero).astype(BF16) for p in pairs]
        lrb = [jnp.where(incl, gmat[p][PAIR:, 0:PAIR], zero).astype(BF16) for p in pairs]
        lrk = [jnp.where(incl, gmat[p][PAIR:, PAIR:], zero).astype(BF16) for p in pairs]
        tinv = [eye_f + aab[p] for p in pairs]
        lp = aab
        n = 1
        while 2 * n < c_len:
            lpb = [x.astype(BF16) for x in lp]
            lp = [_dot(x, x) for x in lpb]
            tinv = [tinv[p] + _dot(tinv[p].astype(BF16), lp[p].astype(BF16)) for p in pairs]
            n *= 2
        tb = [x.astype(BF16) for x in tinv]
        av = [_dot(aak[p], vst[p]).astype(BF16) for p in pairs]
        tav = [_dot(tb[p], av[p]) for p in pairs]
        lrkv = [_dot(lrk[p], vst[p]) for p in pairs]
        zbt = [jnp.transpose(yb[p] * pc[p]).astype(BF16) for p in pairs]
        zkv = [_dot(jnp.transpose(yk[p] * pc[p]).astype(BF16), vst[p]) for p in pairs]
        pcc = [jnp.sum(jnp.where(eye, jnp.broadcast_to(pc[p], (PAIR, PAIR)), zero), axis=-1, keepdims=True)
               for p in pairs]
        for q, (dc, p) in enumerate(items):
            c = c2 * n_par + dc
            t_s[c, p] = tb[q]
            tav_s[c, p] = tav[q]
            lrb_s[c, p] = lrb[q]
            lrkv_s[c, p] = lrkv[q]
            zbt_s[c, p] = zbt[q]
            zkv_s[c, p] = zkv[q]
            ar_s[c, p] = jnp.concatenate([at[q], rt[q]], axis=0).astype(BF16)
            pc_s[c, p] = pcc[q]
        return carry_

    lax.fori_loop(0, n_ch // n_par, phase1, 0)

    pairs = range(N_PAIR)

    def phase2(c, carry_):
        rows = pl.ds(pl.multiple_of(c * c_len, c_len), c_len)
        hbd = [h_st[p] for p in pairs]
        ar = [ar_s[c, p] for p in pairs]
        tb = [t_s[c, p] for p in pairs]
        tav = [tav_s[c, p] for p in pairs]
        lrb = [lrb_s[c, p] for p in pairs]
        lrkv = [lrkv_s[c, p] for p in pairs]
        zbt = [zbt_s[c, p] for p in pairs]
        zkv = [zkv_s[c, p] for p in pairs]
        pcc = [pc_s[c, p] for p in pairs]
        arh = [_dot(ar[p], hbd[p].astype(BF16)) for p in pairs]
        ub = [(_dot(tb[p], _stack_heads(arh[p][0:c_len]).astype(BF16)) + tav[p]).astype(BF16)
              for p in pairs]
        yst = [_dot(lrb[p], ub[p]) + lrkv[p] for p in pairs]
        hn = [pcc[p] * hbd[p] + _dot(zbt[p], ub[p]) + zkv[p] for p in pairs]
        for p in pairs:
            y_s[rows, PAIR * p:PAIR * (p + 1)] = arh[p][c_len:] + yst[p][0:c_len] + yst[p][c_len:]
            h_st[p] = hn[p]
        return carry_

    lax.fori_loop(0, n_ch, phase2, 0)

    @pl.when(ti == pl.num_programs(1) - 1)
    def _():
        first = lax.broadcasted_iota(jnp.int32, (PAIR, HEAD_DIM), 0) < HEAD_DIM
        for p in range(N_PAIR):
            s_t = jnp.transpose(h_st[p])
            sn_ref[0, p] = jnp.where(first, s_t[:, 0:HEAD_DIM], s_t[:, HEAD_DIM:])

    y = y_s[...]
    inv_n = 1.0 / HEAD_DIM
    mean = _group_sum(y, gsum) * inv_n
    d = y - mean
    var = _group_sum(d * d, gsum) * inv_n
    yn = d * lax.rsqrt(var + GN_EPS) * lw_ref[...] + lb_ref[...]
    o_ref[0] = ((yn + bonus) * g).astype(BF16)


def _rwkv(pr, shift0, s0, prm, tr, t_valid):
    b, t_pad, _ = pr.shape
    n_state = s0.shape[0]
    assert shift0.shape[0] == n_state
    last = n_state - 1
    assert t_pad % tr == 0 and tr % RW_CHUNK == 0
    n_ch = tr // RW_CHUNK
    tri = np.zeros((tr, tr), np.float32)
    for c in range(n_ch):
        tri[c * RW_CHUNK:(c + 1) * RW_CHUNK, c * RW_CHUNK:(c + 1) * RW_CHUNK] = np.tril(
            np.ones((RW_CHUNK, RW_CHUNK), np.float32))
    tri = jnp.asarray(tri, BF16)
    vec = _const_spec((1, RW_WIDTH))
    mat = lambda dt: pltpu.VMEM((n_ch, N_PAIR, PAIR, PAIR), dt)
    tile = lambda: pltpu.VMEM((tr, RW_WIDTH), F32)
    return pl.pallas_call(
        functools.partial(_rwkv_kernel, tr=tr, t_valid=t_valid, n_state=n_state),
        grid=(b, t_pad // tr),
        in_specs=[pl.BlockSpec((1, tr, RW_PROJ), lambda bi, ti: (bi, ti, 0)),
                  pl.BlockSpec((1, 1, RW_PROJ), lambda bi, ti: (jnp.minimum(bi, last), 0, 0)),
                  pl.BlockSpec((1, N_PAIR, PAIR, HEAD_DIM),
                               lambda bi, ti: (jnp.minimum(bi, last), 0, 0, 0)),
                  _const_spec((1, RW_PROJ)), vec, vec, vec, vec, vec, vec, vec,
                  _const_spec((W_LORA + A_LORA, RW_WIDTH)), _const_spec((W_LORA + A_LORA, RW_WIDTH)),
                  _const_spec((G_LORA, RW_WIDTH)), _const_spec((LANES, LANES)),
                  _const_spec((tr, tr))],
        out_specs=[pl.BlockSpec((1, tr, RW_WIDTH), lambda bi, ti: (bi, ti, 0)),
                   pl.BlockSpec((1, N_PAIR, PAIR, HEAD_DIM), lambda bi, ti: (bi, 0, 0, 0))],
        out_shape=[jax.ShapeDtypeStruct((b, t_pad, RW_WIDTH), BF16),
                   jax.ShapeDtypeStruct((b, N_PAIR, PAIR, HEAD_DIM), F32)],
        scratch_shapes=[pltpu.VMEM((N_PAIR, PAIR, PAIR), F32),
                        pltpu.VMEM((8 + tr, RW_PROJ), F32), pltpu.VMEM((1, RW_PROJ), F32),
                        tile(), tile(), tile(), tile(), tile(), tile(), tile(),
                        mat(BF16), mat(F32), mat(BF16), mat(F32), mat(BF16), mat(F32), mat(BF16),
                        pltpu.VMEM((n_ch, N_PAIR, PAIR, 1), F32)],
        compiler_params=_cparams(("arbitrary", "arbitrary")),
        name="rwkv",
    )(pr, shift0, s0, prm["mu"], prm["w0"], prm["a0"], prm["k_k"], prm["k_a"], prm["r_k"],
      prm["lnx_w"], prm["lnx_b"], prm["w2p"], prm["a2p"], prm["g2"], prm["gsum"], tri)


FF_COLS = 256


def _ffn_kernel(x_ref, oa_ref, orw_ref, c0_ref, wo_ref, g_ref, wu_ref, cw_ref, cb_ref, wd_ref,
                y_ref, cn_ref, zbuf, cbuf, h_ref, x1_ref, *, tm, ts, d_ff):
    off = zbuf.shape[1] - tm
    da = oa_ref.shape[1]

    @pl.when(pl.program_id(0) == 0)
    def _():
        cbuf[...] = c0_ref[...]

    x1 = x_ref[...] + _dot(oa_ref[...], wo_ref[0:da, :]) + _dot(orw_ref[...], wo_ref[da:, :])
    x1_ref[...] = x1
    ms = jnp.mean(x1 * x1, axis=-1, keepdims=True)
    h_ref[...] = (x1 * lax.rsqrt(ms + NORM_EPS) * g_ref[...]).astype(BF16)

    def up_proj(c0, zb):
        cols = slice(c0, c0 + FF_COLS)
        zb[off - 2 * ts:off, :] = cbuf[:, cols]
        zb[off:off + tm, :] = _dot(h_ref[...], wu_ref[:, cols])
        cbuf[:, cols] = zb[off + tm - 2 * ts:off + tm, :]

    def conv_cols(c0, zb):
        cols = slice(c0, c0 + FF_COLS)
        z2 = zb[off - 2 * ts:off - 2 * ts + tm, :]
        z1 = zb[off - ts:off - ts + tm, :]
        z = zb[off:off + tm, :]
        return (cb_ref[:, cols] + z2 * cw_ref[0:1, cols] + z1 * cw_ref[1:2, cols]
                + z * cw_ref[2:3, cols])

    def stage(c):
        up_proj(c * FF_COLS, zbuf.at[2 * (c % 2)])
        up_proj(d_ff + c * FF_COLS, zbuf.at[2 * (c % 2) + 1])

    n_chunks = d_ff // FF_COLS
    acc = None
    stage(0)
    for c in range(n_chunks):
        if c + 1 < n_chunks:
            stage(c + 1)
        gate = conv_cols(c * FF_COLS, zbuf.at[2 * (c % 2)])
        up = conv_cols(d_ff + c * FF_COLS, zbuf.at[2 * (c % 2) + 1])
        act = (gate * _sigmoid(gate) * up).astype(BF16)
        part = _dot(act, wd_ref[c * FF_COLS:(c + 1) * FF_COLS, :])
        acc = part if acc is None else acc + part

    y_ref[...] = x1_ref[...] + acc
    cn_ref[...] = cbuf[...]


def _ffn(x, oa, orw, conv0, prm, tm, ts):
    rows, d = x.shape
    d_ff = prm["w_down"].shape[0]
    assert rows % tm == 0 and d_ff % FF_COLS == 0 and (ts == 1 or ts % 8 == 0)
    off = -(-2 * ts // 8) * 8
    row = lambda w: pl.BlockSpec((tm, w), lambda i: (i, 0))
    return pl.pallas_call(
        functools.partial(_ffn_kernel, tm=tm, ts=ts, d_ff=d_ff),
        grid=(rows // tm,),
        in_specs=[row(d), row(DA_WIDTH), row(RW_WIDTH), _const_spec((2 * ts, 2 * d_ff)),
                  _resident_spec((DA_WIDTH + RW_WIDTH, d)), _const_spec((1, d)),
                  _resident_spec((d, 2 * d_ff)), _const_spec((CONV_W, 2 * d_ff)),
                  _const_spec((1, 2 * d_ff)), _resident_spec((d_ff, d))],
        out_specs=[row(d), _const_spec((2 * ts, 2 * d_ff))],
        out_shape=[jax.ShapeDtypeStruct((rows, d), F32),
                   jax.ShapeDtypeStruct((2 * ts, 2 * d_ff), F32)],
        scratch_shapes=[pltpu.VMEM((4, off + tm, FF_COLS), F32), pltpu.VMEM((2 * ts, 2 * d_ff), F32),
                        pltpu.VMEM((tm, d), BF16), pltpu.VMEM((tm, d), F32)],
        compiler_params=_cparams(("arbitrary",)),
        name="ffn",
    )(x, oa, orw, conv0, prm["w_out"], prm["ln2_g"], prm["w_up"], prm["conv_w"], prm["conv_b"],
      prm["w_down"])


def _rel_bucket(rel):
    nb = N_BUCKETS // 2
    max_exact = nb // 2
    bucket = jnp.where(rel > 0, nb, 0)
    n = jnp.abs(rel)
    nf = jnp.maximum(n, 1).astype(F32)
    large = max_exact + (jnp.log(nf / max_exact) / math.log(MAX_DISTANCE / max_exact)
                         * (nb - max_exact)).astype(jnp.int32)
    large = jnp.minimum(large, nb - 1)
    return bucket + jnp.where(n < max_exact, n, large)


def _bias_table(rel_bias, q_pos, k_pos, mask):
    n_q, n_k = len(q_pos), len(k_pos)
    assert np.all(np.diff(q_pos) == 1) and np.all(np.diff(k_pos) == 1)
    span = n_q + n_k - 1
    rel = jnp.asarray(int(k_pos[0]) - int(q_pos[0]) - (n_q - 1) + np.arange(span), jnp.int32)
    onehot = _rel_bucket(rel)[None, :, None] == jnp.arange(N_BUCKETS, dtype=jnp.int32)
    table = jnp.transpose(rel_bias).astype(F32)[:, None, :]
    w = jnp.sum(jnp.where(onehot, table, 0.0), axis=-1) * LOG2E
    return jnp.where(jnp.asarray(mask)[None], _toeplitz(w, n_q, n_k), NEG_INF)


def _skew(w, n_q, n_k):
    span = n_q + n_k - 1
    x = jnp.pad(w[:, :span], ((0, 0), (0, 1)))
    rows = jnp.tile(x, (1, n_q))[:, :n_q * span].reshape(w.shape[0], n_q, span)
    return rows[:, :, n_q - 1:n_q - 1 + n_k]


def _toeplitz(w, n_q, n_k):
    blk = LANES
    if n_q % blk or n_k % blk or n_q * n_k <= blk * blk:
        return _skew(w, n_q, n_k)
    nbq, nbk = n_q // blk, n_k // blk
    block = {d: _skew(w[:, (d + nbq - 1) * blk:(d + nbq + 1) * blk - 1], blk, blk)
             for d in range(-(nbq - 1), nbk)}
    return jnp.concatenate([jnp.concatenate([block[c - r] for c in range(nbk)], axis=2)
                            for r in range(nbq)], axis=1)


BOUND_MAX_SPREAD = 100.0


def _score_bound(q_g, k_g, rel_bias):
    s_max = (HEAD_DIM ** 0.5) * LOG2E * 1.01 * jnp.max(jnp.abs(q_g)) * jnp.max(jnp.abs(k_g))
    b_hi = jnp.max(rel_bias) * LOG2E
    b_lo = jnp.min(rel_bias) * LOG2E
    use = (2.0 * s_max + (b_hi - b_lo)) <= BOUND_MAX_SPREAD
    return jnp.stack([use.astype(F32), (s_max + b_hi).astype(F32)])


def _ext_chunk(pos):
    return np.where(pos < N_META, -1, (pos - N_META) // CHUNK)


def _prompt_bias(rel_bias):
    tq = ATT_T
    fr = np.arange(tq) + N_META
    meta = np.arange(N_META)
    causal = _ext_chunk(fr)[None, :] <= _ext_chunk(fr)[:, None]
    tr = lambda b: jnp.swapaxes(b, 1, 2)
    b0 = tr(_bias_table(rel_bias, fr, fr, causal))
    bm1 = tr(_bias_table(rel_bias, fr + tq, fr, np.ones((tq, tq), bool)))
    bq0 = tr(_bias_table(rel_bias, fr, meta, np.ones((tq, N_META), bool)))
    bmm = _bias_table(rel_bias, meta, meta, np.ones((N_META, N_META), bool))
    assert tq + 1 >= MAX_DISTANCE
    cfar = rel_bias[_rel_bucket(jnp.asarray(-(tq + 1), jnp.int32))].astype(F32) * LOG2E
    return cfar, b0, bm1, bq0, bmm


def _decode_bias(rel_bias, n_cache, nq):
    k_pos = np.arange(n_cache + nq)
    q_pos = k_pos[n_cache:]
    mask = _ext_chunk(k_pos)[None, :] <= _ext_chunk(q_pos)[:, None]
    return _bias_table(rel_bias, q_pos, k_pos, mask)


def _block_ones(n, blk, dtype):
    idx = np.arange(n) // blk
    return jnp.asarray((idx[:, None] == idx[None, :]).astype(np.float32), dtype)


def kernel(x_prompt, x_sample, cache_k, cache_v, state_rwkv, state_shift, state_conv, meta_tokens,
           rel_bias, ln1_g, w_in, q_norm_g, k_norm_g, lam_q1, lam_k1, lam_q2, lam_k2, subln_g,
           mu_shift, w0, w2, a0, a2, g2, k_k, k_a, r_k, lnx_w, lnx_b, w_out, ln2_g, w_up, conv_w,
           conv_b, w_down):
    bp, seq, d = x_prompt.shape
    db, dt, _ = x_sample.shape
    depth = w_in.shape[0]
    d_ff = w_down.shape[1]
    n_cache = cache_k.shape[2]
    assert bp == 1 and dt == N_META, "the meta stream rides with the decode streams"
    assert cache_k.shape[3] == DA_HEADS and cache_k.shape[4] == 2 * HEAD_DIM
    nb = db + 1
    nb_pad = -(-nb // 8) * 8

    cfar, b0, bm1, bq0, bmm = _prompt_bias(rel_bias)
    bias_dec = _decode_bias(rel_bias, n_cache, dt)
    gsum = _block_ones(LANES, HEAD_DIM, BF16)
    gmean = gsum * (1.0 / HEAD_DIM)
    zrow = lambda n: jnp.zeros((n, RW_WIDTH), BF16)

    x_f = x_prompt[0]
    x_s = jnp.concatenate([x_sample, meta_tokens.astype(x_sample.dtype)[None]], axis=0)
    outs = [[] for _ in range(10)]
    for l in range(depth):
        lam_init = 0.8 - 0.6 * math.exp(-0.3 * l)
        lam = (jnp.exp(jnp.sum(lam_q1[l].astype(F32) * lam_k1[l].astype(F32)))
               - jnp.exp(jnp.sum(lam_q2[l].astype(F32) * lam_k2[l].astype(F32))) + lam_init).reshape(1)
        out_scale = 1.0 - lam_init
        tile128 = lambda g_: jnp.tile(g_.reshape(1, -1), (1, DA_WIDTH // g_.shape[-1]))
        qg, kg = tile128(q_norm_g[l]), tile128(k_norm_g[l])
        sg = subln_g[l].reshape(1, LANES)
        w_in_bf = w_in[l].astype(BF16)
        rw = {
            "mu": mu_shift[l].reshape(1, -1), "w0": w0[l].reshape(1, -1), "a0": a0[l].reshape(1, -1),
            "k_k": k_k[l].reshape(1, -1), "k_a": k_a[l].reshape(1, -1), "r_k": r_k[l].reshape(1, -1),
            "lnx_w": lnx_w[l].reshape(1, -1), "lnx_b": lnx_b[l].reshape(1, -1),
            "w2p": jnp.concatenate([w2[l].astype(BF16), zrow(A_LORA)], axis=0),
            "a2p": jnp.concatenate([zrow(W_LORA), a2[l].astype(BF16)], axis=0),
            "g2": g2[l].astype(BF16), "gsum": gsum,
        }
        ff = {
            "w_out": w_out[l].astype(BF16), "ln2_g": ln2_g[l].reshape(1, -1),
            "w_up": w_up[l].astype(BF16), "conv_w": conv_w[l], "conv_b": conv_b[l].reshape(1, -1),
            "w_down": w_down[l].astype(BF16),
        }

        q_f, k_f, v_f, pr_f, kb_t, vt_t = _proj(x_f, ln1_g[l].reshape(1, -1), w_in_bf, qg, kg, gmean,
                                                 512, True)
        q_s, k_s, v_s, pr_s = _proj(x_s.reshape(nb * dt, d), ln1_g[l].reshape(1, -1), w_in_bf, qg, kg,
                                    gmean, nb * dt, False)
        m0 = db * dt

        kb_m = k_s[m0:].astype(BF16)
        vb_m = v_s[m0:].astype(BF16)
        o_f, o_m = _attn_prompt(cfar, lam, _score_bound(q_norm_g[l], k_norm_g[l], rel_bias), q_f, kb_t, vt_t, kb_m, vb_m, jnp.transpose(vb_m), q_s[m0:],
                                b0, bm1, bq0, bmm, sg, out_scale)
        o_d = _attn_decode(lam, q_s, cache_k, cache_v, l, k_s, v_s, bias_dec, sg, out_scale, dt)
        o_s = jnp.concatenate([o_d, o_m], axis=0)

        pr_s3 = pr_s.reshape(nb, dt, RW_PROJ)
        pr_pad = jnp.pad(pr_s3, ((0, 0), (0, RW_CHUNK - dt), (0, 0)))
        pair_view = (N_PAIR, PAIR, HEAD_DIM)
        orw_s, sn_s = _rwkv(pr_pad, state_shift[l][:, None, :], state_rwkv[l].reshape(db, *pair_view),
                            rw, RW_CHUNK, dt)
        orw_f, sn_f = _rwkv(pr_f[None], pr_s3[db:, dt - 1:dt, :], sn_s[db:], rw, 512, seq)

        def time_major(a):
            a = jnp.pad(a.reshape(nb, dt, -1), ((0, nb_pad - nb), (0, 0), (0, 0)))
            return jnp.swapaxes(a, 0, 1).reshape(dt * nb_pad, -1)

        conv_s = jnp.concatenate([state_conv[l], jnp.zeros((1, CONV_W - 1, 2 * d_ff), F32)], axis=0)
        conv_s = jnp.pad(conv_s, ((0, nb_pad - nb), (0, 0), (0, 0)))
        conv_s = jnp.swapaxes(conv_s, 0, 1).reshape(2 * nb_pad, 2 * d_ff)
        y_s, cn_s = _ffn(time_major(x_s), time_major(o_s), time_major(orw_s[:, :dt]), conv_s, ff,
                         dt * nb_pad, nb_pad)
        cn_s = jnp.swapaxes(cn_s.reshape(2, nb_pad, 2 * d_ff), 0, 1)
        y_f, cn_f = _ffn(x_f, o_f, orw_f[0], cn_s[db], ff, 512, 1)
        y_s = jnp.swapaxes(y_s.reshape(dt, nb_pad, d), 0, 1)[:nb]

        hw = (DA_HEADS, 2 * HEAD_DIM)
        outs[0].append(jnp.concatenate([k_s[m0:].reshape(N_META, *hw), k_f], axis=0)[None])
        outs[1].append(jnp.concatenate([v_s[m0:].reshape(N_META, *hw), v_f], axis=0)[None])
        outs[2].append(sn_f.reshape(bp, RW_HEADS, HEAD_DIM, HEAD_DIM))
        outs[3].append(pr_f[seq - 1:seq])
        outs[4].append(cn_f[None])
        outs[5].append(k_s[:m0].reshape(db, dt, *hw))
        outs[6].append(v_s[:m0].reshape(db, dt, *hw))
        outs[7].append(sn_s[:db].reshape(db, RW_HEADS, HEAD_DIM, HEAD_DIM))
        outs[8].append(pr_s3[:db, dt - 1])
        outs[9].append(cn_s[:db])
        x_f, x_s = y_f, y_s

    return (x_f[None], x_s[:db], *[jnp.stack(o) for o in outs])
```

```python
import functools
import math

import numpy as np
import jax
import jax.numpy as jnp
from jax import lax
from jax.experimental import pallas as pl
from jax.experimental.pallas import tpu as pltpu

F32 = jnp.float32
BF16 = jnp.bfloat16

CHUNK = 64
N_META = 16
HEAD_DIM = 64
DA_HEADS = 4
RW_HEADS = 8
W_LORA = 64
A_LORA = 64
G_LORA = 128
CONV_W = 3
N_BUCKETS = 32
MAX_DISTANCE = 128
NORM_EPS = 1e-6
GN_EPS = 64e-5
NEG_INF = -1e30
LOG2E = math.log2(math.e)
DECAY_SCALE = math.exp(-0.5)

DA_WIDTH = DA_HEADS * 2 * HEAD_DIM
RW_WIDTH = RW_HEADS * HEAD_DIM
RW_PROJ = 3 * RW_WIDTH + W_LORA + A_LORA + G_LORA
LANES = 128
PAIR = 2 * HEAD_DIM
N_PAIR = RW_WIDTH // PAIR
RW_CHUNK = 64
ATT_T = 512
VMEM_LIMIT = 56 * 1024 * 1024


def _dot(a, b):
    return jnp.dot(a, b, preferred_element_type=F32)


def _dot_nt(a, b):
    return lax.dot_general(a, b, (((1,), (1,)), ((), ())), preferred_element_type=F32)


def _group_sum(x, blk):
    xb = x.astype(BF16)
    return jnp.concatenate([_dot(xb[:, LANES * p:LANES * (p + 1)], blk)
                            for p in range(x.shape[1] // LANES)], axis=1)


def _cparams(sem):
    return pltpu.CompilerParams(dimension_semantics=sem, vmem_limit_bytes=VMEM_LIMIT)


def _const_spec(shape):
    nd = len(shape)
    return pl.BlockSpec(shape, lambda *_: (0,) * nd)


def _resident_spec(shape):
    nd = len(shape)
    return pl.BlockSpec(shape, lambda *_: (0,) * nd, pipeline_mode=pl.Buffered(1))


def _proj_kernel(x_ref, g_ref, w_ref, qg_ref, kg_ref, gm_ref,
                 q_ref, k_ref, v_ref, pr_ref, *tile_refs, tm):
    x = x_ref[...]
    ms = jnp.mean(x * x, axis=-1, keepdims=True)
    h = (x * lax.rsqrt(ms + NORM_EPS) * g_ref[...]).astype(BF16)
    gm = gm_ref[...]

    def group_norm(t, g):
        ms_g = _group_sum(t * t, gm)
        return t * lax.rsqrt(ms_g + NORM_EPS) * g

    q = _dot(h, w_ref[:, 0:DA_WIDTH])
    q_ref[...] = (group_norm(q, qg_ref[...]) * (HEAD_DIM ** -0.5 * LOG2E)).astype(BF16)
    k = group_norm(_dot(h, w_ref[:, DA_WIDTH:2 * DA_WIDTH]), kg_ref[...])
    v = _dot(h, w_ref[:, 2 * DA_WIDTH:3 * DA_WIDTH])
    if len(k_ref.shape) == 3:
        for hd in range(DA_HEADS):
            k_ref[:, hd, :] = k[:, hd * LANES:(hd + 1) * LANES]
            v_ref[:, hd, :] = v[:, hd * LANES:(hd + 1) * LANES]
    else:
        k_ref[...] = k
        v_ref[...] = v
    pr_ref[...] = _dot(h, w_ref[:, 3 * DA_WIDTH:])
    if tile_refs:
        kb_ref, vt_ref = tile_refs
        kb = k.astype(BF16)
        vt = jnp.transpose(v).astype(BF16)
        for hd in range(DA_HEADS):
            for jj in range(tm // ATT_T):
                kb_ref[hd, jj] = kb[jj * ATT_T:(jj + 1) * ATT_T, hd * LANES:(hd + 1) * LANES]
                vt_ref[hd, jj] = vt[hd * LANES:(hd + 1) * LANES, jj * ATT_T:(jj + 1) * ATT_T]


def _proj(x, ln1_g, w_in_bf, qg, kg, gmean, tm, emit_tiles):
    rows, d = x.shape
    n_in = w_in_bf.shape[1]
    assert rows % tm == 0
    row = lambda w: pl.BlockSpec((tm, w), lambda i: (i, 0))
    out_specs = [row(DA_WIDTH), row(DA_WIDTH), row(DA_WIDTH), row(RW_PROJ)]
    out_shape = [jax.ShapeDtypeStruct((rows, DA_WIDTH), BF16),
                 jax.ShapeDtypeStruct((rows, DA_WIDTH), F32),
                 jax.ShapeDtypeStruct((rows, DA_WIDTH), F32),
                 jax.ShapeDtypeStruct((rows, RW_PROJ), F32)]
    if emit_tiles:
        assert tm % ATT_T == 0
        tpt = tm // ATT_T
        for o in (1, 2):
            out_specs[o] = pl.BlockSpec((tm, DA_HEADS, LANES), lambda i: (i, 0, 0))
            out_shape[o] = jax.ShapeDtypeStruct((rows, DA_HEADS, LANES), F32)
        out_specs += [pl.BlockSpec((DA_HEADS, tpt, ATT_T, LANES), lambda i: (0, i, 0, 0)),
                      pl.BlockSpec((DA_HEADS, tpt, LANES, ATT_T), lambda i: (0, i, 0, 0))]
        out_shape += [jax.ShapeDtypeStruct((DA_HEADS, rows // ATT_T, ATT_T, LANES), BF16),
                      jax.ShapeDtypeStruct((DA_HEADS, rows // ATT_T, LANES, ATT_T), BF16)]
    return pl.pallas_call(
        functools.partial(_proj_kernel, tm=tm),
        grid=(rows // tm,),
        in_specs=[row(d), _const_spec((1, d)), _resident_spec((d, n_in)),
                  _const_spec((1, DA_WIDTH)), _const_spec((1, DA_WIDTH)),
                  _const_spec((LANES, LANES))],
        out_specs=out_specs,
        out_shape=out_shape,
        compiler_params=_cparams(("arbitrary",)),
        name="proj",
    )(x, ln1_g, w_in_bf, qg, kg, gmean)


def _stack_components(q):
    lo = lax.broadcasted_iota(jnp.int32, q.shape, 1) < HEAD_DIM
    zero = jnp.zeros_like(q)
    return jnp.concatenate([jnp.where(lo, q, zero), jnp.where(lo, zero, q)], axis=0)


def _sub_layer_norm(o, g, out_scale):
    ms = jnp.mean(o * o, axis=-1, keepdims=True)
    return o * lax.rsqrt(ms + NORM_EPS) * g * out_scale


def _attn_prompt_kernel(cfar_ref, lam_ref, fix_ref, q_ref, k_ref, vt_ref, km_ref, vm_ref, vmt_ref, qm_ref,
                        b0_ref, bm1_ref, bq0_ref, bmm_ref, g_ref, o_ref, om_ref,
                        acc_ref, m_ref, l_ref, s_buf, p_buf, a_buf, *, out_scale):
    h = pl.program_id(0)
    i = pl.program_id(1)
    tq = ATT_T
    cf = cfar_ref[h]
    lam = lam_ref[0]
    qst = _stack_components(q_ref[...])

    def both(b):
        return jnp.concatenate([b, b], axis=1)

    s = _dot_nt(km_ref[...], qst)
    s = s + both(jnp.where(i == 0, bq0_ref[0], cf))
    use_bound = fix_ref[0] > 0.5
    bound = fix_ref[1]
    m0 = jnp.where(use_bound, bound, jnp.max(s, axis=0, keepdims=True))
    p = jnp.exp2(s - m0)
    m_ref[...] = m0
    l_ref[...] = jnp.sum(p, axis=0, keepdims=True)
    acc_ref[...] = _dot(vmt_ref[...], p.astype(BF16))

    def scores(idx, slot):
        s_buf[slot] = _dot_nt(k_ref[0, idx], qst)

    def softmax(slot, bias):
        s = s_buf[slot]
        m_prev = m_ref[...]
        if bias.ndim == 0:
            m_new = jnp.maximum(m_prev, jnp.max(s, axis=0, keepdims=True) + bias)
            p = jnp.exp2(s - (m_new - bias))
        else:
            s = s + both(bias)
            m_new = jnp.maximum(m_prev, jnp.max(s, axis=0, keepdims=True))
            p = jnp.exp2(s - m_new)
        alpha = jnp.exp2(m_prev - m_new)
        l_ref[...] = alpha * l_ref[...] + jnp.sum(p, axis=0, keepdims=True)
        m_ref[...] = m_new
        p_buf[slot] = p.astype(BF16)
        a_buf[slot] = alpha

    def accumulate(idx, slot):
        acc_ref[...] = a_buf[slot] * acc_ref[...] + _dot(vt_ref[0, idx], p_buf[slot])

    n_far = jnp.maximum(i - 1, 0)
    off = n_far % 2
    p_buf[1] = jnp.zeros(p_buf.shape[1:], BF16)
    a_buf[1] = jnp.ones(a_buf.shape[1:], F32)
    scores(0, 0)

    def far_body(t, carry):
        u = 2 * t - off
        scores(u + 1, 1)
        softmax(0, jnp.where(u >= 0, cf, NEG_INF))
        accumulate(jnp.maximum(u - 1, 0), 1)
        scores(u + 2, 0)
        softmax(1, cf)
        accumulate(jnp.maximum(u, 0), 0)
        return carry

    def bound_softmax(slot, bias):
        p = jnp.exp2(s_buf[slot] - (bound - bias))
        l_ref[...] += jnp.sum(p, axis=0, keepdims=True)
        p_buf[slot] = p.astype(BF16)

    def bound_accumulate(idx, slot):
        acc_ref[...] += _dot(vt_ref[0, idx], p_buf[slot])

    def far_body_bound(t, carry):
        u = 2 * t - off
        scores(u + 1, 1)
        bound_softmax(0, jnp.where(u >= 0, cf, NEG_INF))
        bound_accumulate(jnp.maximum(u - 1, 0), 1)
        scores(u + 2, 0)
        bound_softmax(1, cf)
        bound_accumulate(jnp.maximum(u, 0), 0)
        return carry

    @pl.when(use_bound)
    def _():
        lax.fori_loop(0, (n_far + 1) // 2, far_body_bound, 0)

    @pl.when(jnp.logical_not(use_bound))
    def _():
        lax.fori_loop(0, (n_far + 1) // 2, far_body, 0)

    scores(i, 1)
    softmax(0, jnp.where(i >= 1, bm1_ref[0], NEG_INF))
    accumulate(jnp.maximum(n_far - 1, 0), 1)
    softmax(1, b0_ref[0])
    accumulate(n_far, 0)
    accumulate(i, 1)

    accn = acc_ref[...] / l_ref[...]
    o_t = accn[:, 0:tq] - lam * accn[:, tq:]
    o_ref[...] = _sub_layer_norm(jnp.transpose(o_t), g_ref[...], out_scale).astype(BF16)

    @pl.when(i == 0)
    def _():
        qm = _stack_components(qm_ref[...])
        bmm = bmm_ref[0]
        sm = _dot_nt(qm, km_ref[...]) + jnp.concatenate([bmm, bmm], axis=0)
        mm = jnp.max(sm, axis=-1, keepdims=True)
        pm = jnp.exp2(sm - mm)
        accm = _dot(pm.astype(BF16), vm_ref[...]) / jnp.sum(pm, axis=-1, keepdims=True)
        om = accm[0:N_META] - lam * accm[N_META:]
        om_ref[...] = _sub_layer_norm(om, g_ref[...], out_scale).astype(BF16)


def _attn_prompt(cfar, lam, fix, q_f, kb_t, vt_t, kb_m, vb_m, vbt_m, q_m, b0, bm1, bq0, bmm, subln_g,
                 out_scale):
    tf = q_f.shape[0]
    tq = ATT_T
    assert tf % tq == 0
    n_t = tf // tq
    smem = pl.BlockSpec(memory_space=pltpu.SMEM)
    head_col = lambda rows: pl.BlockSpec((rows, LANES), lambda h, i: (0, h))
    head_tile = lambda a, b: pl.BlockSpec((1, a, b), lambda h, i: (h, 0, 0))
    return pl.pallas_call(
        functools.partial(_attn_prompt_kernel, out_scale=out_scale),
        grid=(DA_HEADS, n_t),
        in_specs=[smem, smem, smem,
                  pl.BlockSpec((tq, LANES), lambda h, i: (i, h)),
                  pl.BlockSpec((1, n_t, tq, LANES), lambda h, i: (h, 0, 0, 0)),
                  pl.BlockSpec((1, n_t, LANES, tq), lambda h, i: (h, 0, 0, 0)),
                  head_col(N_META), head_col(N_META),
                  pl.BlockSpec((LANES, N_META), lambda h, i: (h, 0)),
                  head_col(N_META),
                  head_tile(tq, tq), head_tile(tq, tq), head_tile(N_META, tq),
                  head_tile(N_META, N_META), _const_spec((1, LANES))],
        out_specs=[pl.BlockSpec((tq, LANES), lambda h, i: (i, h)), head_col(N_META)],
        out_shape=[jax.ShapeDtypeStruct((tf, DA_WIDTH), BF16),
                   jax.ShapeDtypeStruct((N_META, DA_WIDTH), BF16)],
        scratch_shapes=[pltpu.VMEM((LANES, 2 * tq), F32), pltpu.VMEM((1, 2 * tq), F32),
                        pltpu.VMEM((1, 2 * tq), F32), pltpu.VMEM((2, tq, 2 * tq), F32),
                        pltpu.VMEM((2, tq, 2 * tq), BF16), pltpu.VMEM((2, 1, 2 * tq), F32)],
        compiler_params=_cparams(("arbitrary", "arbitrary")),
        name="attn_prompt",
    )(cfar, lam, fix, q_f, kb_t, vt_t, kb_m, vb_m, vbt_m, q_m, b0, bm1, bq0, bmm, subln_g)


def _attn_decode_kernel(lam_ref, q_ref, k_hbm, v_hbm, kn_ref, vn_ref, b_ref, g_ref, o_ref,
                        kbuf, vbuf, sem, *, layer, n_main, out_scale):
    bi = pl.program_id(0)
    h = pl.program_id(1)
    n_h = pl.num_programs(1)
    step = bi * n_h + h
    slot = step % 2

    def cache_copies(b_, h_, slot_):
        return (pltpu.make_async_copy(k_hbm.at[layer, b_, :, h_, :], kbuf.at[slot_], sem.at[0, slot_]),
                pltpu.make_async_copy(v_hbm.at[layer, b_, :, h_, :], vbuf.at[slot_], sem.at[1, slot_]))

    @pl.when(step == 0)
    def _():
        for cp in cache_copies(bi, h, slot):
            cp.start()

    @pl.when(step + 1 < pl.num_programs(0) * n_h)
    def _():
        nxt = step + 1
        for cp in cache_copies(nxt // n_h, nxt % n_h, 1 - slot):
            cp.start()

    for cp in cache_copies(bi, h, slot):
        cp.wait()

    lam = lam_ref[0]
    nq = q_ref.shape[0]
    n_cache = kbuf.shape[1]
    k_ref = kbuf.at[slot]
    v_ref = vbuf.at[slot]
    qst = _stack_components(q_ref[...])
    k_main = k_ref[0:n_main, :].astype(BF16)
    v_main = v_ref[0:n_main, :].astype(BF16)
    k_tail = jnp.concatenate([k_ref[n_main:n_cache, :], kn_ref[...]], axis=0).astype(BF16)
    v_tail = jnp.concatenate([v_ref[n_main:n_cache, :], vn_ref[...]], axis=0).astype(BF16)
    bias = jnp.concatenate([b_ref[0], b_ref[0]], axis=0)
    s1 = _dot_nt(qst, k_main) + bias[:, 0:n_main]
    s2 = _dot_nt(qst, k_tail) + bias[:, n_main:]
    m = jnp.maximum(jnp.max(s1, axis=-1, keepdims=True), jnp.max(s2, axis=-1, keepdims=True))
    p1 = jnp.exp2(s1 - m)
    p2 = jnp.exp2(s2 - m)
    l = jnp.sum(p1, axis=-1, keepdims=True) + jnp.sum(p2, axis=-1, keepdims=True)
    res = (_dot(p1.astype(BF16), v_main) + _dot(p2.astype(BF16), v_tail)) / l
    o = res[0:nq] - lam * res[nq:]
    o_ref[...] = _sub_layer_norm(o, g_ref[...], out_scale).astype(BF16)


def _attn_decode(lam, q_s, cache_k, cache_v, layer, k_s, v_s, bias, subln_g, out_scale, nq):
    _, b, n_cache, n_h, _ = cache_k.shape
    n_main = (n_cache // LANES) * LANES
    n_keys = n_cache + nq
    smem = pl.BlockSpec(memory_space=pltpu.SMEM)
    hbm = pl.BlockSpec(memory_space=pl.ANY)
    new = pl.BlockSpec((nq, LANES), lambda bi, h: (bi, h))
    return pl.pallas_call(
        functools.partial(_attn_decode_kernel, layer=layer, n_main=n_main, out_scale=out_scale),
        grid=(b, n_h),
        in_specs=[smem, new, hbm, hbm, new, new,
                  pl.BlockSpec((1, nq, n_keys), lambda bi, h: (h, 0, 0)), _const_spec((1, LANES))],
        out_specs=new,
        out_shape=jax.ShapeDtypeStruct((b * nq, DA_WIDTH), BF16),
        scratch_shapes=[pltpu.VMEM((2, n_cache, LANES), F32), pltpu.VMEM((2, n_cache, LANES), F32),
                        pltpu.SemaphoreType.DMA((2, 2))],
        compiler_params=_cparams(("arbitrary", "arbitrary")),
        name="attn_decode",
    )(lam, q_s, cache_k, cache_v, k_s, v_s, bias, subln_g)


def _sigmoid(x):
    return 1.0 / (1.0 + jnp.exp(-x))


def _split2(x):
    x1 = x.astype(BF16)
    x2 = (x - x1.astype(F32)).astype(BF16)
    return x1, x2


def _stack_heads(x):
    lo = lax.broadcasted_iota(jnp.int32, x.shape, 1) < HEAD_DIM
    zero = jnp.zeros_like(x)
    return jnp.concatenate([jnp.where(lo, x, zero), jnp.where(lo, zero, x)], axis=0)


def _rwkv_kernel(pr_ref, sh0_ref, h0_ref, mu_ref, w0_ref, a0_ref, kk_ref, ka_ref, rk_ref,
                 lw_ref, lb_ref, w2_ref, a2_ref, g2_ref, gsum_ref, tri_ref,
                 o_ref, sn_ref,
                 h_st, xbuf, carry, at_s, rt_s, bt_s, kt_s, v_s, ep_s, y_s,
                 t_s, tav_s, lrb_s, lrkv_s, zbt_s, zkv_s, ar_s, pc_s,
                 *, tr, t_valid, n_state):
    c_len = RW_CHUNK
    n_ch = tr // c_len
    ti = pl.program_id(1)

    @pl.when(ti == 0)
    def _():
        has_state = pl.program_id(0) < n_state
        carry[...] = jnp.where(has_state, sh0_ref[0], 0.0)
        hd_r = lax.broadcasted_iota(jnp.int32, (PAIR, PAIR), 0) // HEAD_DIM
        hd_c = lax.broadcasted_iota(jnp.int32, (PAIR, PAIR), 1) // HEAD_DIM
        for p in range(N_PAIR):
            s_p = h0_ref[0, p]
            blk = jnp.where((hd_r == hd_c) & has_state, jnp.concatenate([s_p, s_p], axis=1), 0.0)
            h_st[p] = jnp.transpose(blk)

    x = pr_ref[0]
    xbuf[8:8 + tr, :] = x
    xbuf[7:8, :] = carry[...]
    prev = xbuf[7:7 + tr, :]
    carry[...] = pr_ref[0, tr - 1:tr, :]
    xm = x + (prev - x) * mu_ref[...]
    r = xm[:, 0:RW_WIDTH]
    kr = xm[:, RW_WIDTH:2 * RW_WIDTH]
    vr = xm[:, 2 * RW_WIDTH:3 * RW_WIDTH]
    wa = xm[:, 3 * RW_WIDTH:3 * RW_WIDTH + W_LORA + A_LORA]
    gd = xm[:, 3 * RW_WIDTH + W_LORA + A_LORA:]
    lane_wa = lax.broadcasted_iota(jnp.int32, wa.shape, 1)
    twa = jnp.where(lane_wa < W_LORA, jnp.tanh(wa), wa).astype(BF16)
    logw = -DECAY_SCALE * _sigmoid(w0_ref[...] + _dot(twa, w2_ref[...]))
    a_sig = _sigmoid(a0_ref[...] + _dot(twa, a2_ref[...]))
    g = _dot(_sigmoid(gd).astype(BF16), g2_ref[...])
    gsum = gsum_ref[...]
    kk = kr * kk_ref[...]
    kk = kk * lax.rsqrt(jnp.maximum(_group_sum(kk * kk, gsum), 1e-24))
    kr2 = kr * (1.0 + (a_sig - 1.0) * ka_ref[...])
    a_vec = -kk
    b_vec = kk * a_sig
    bonus = _group_sum(r * kr2 * rk_ref[...], gsum) * vr
    if t_valid % tr != 0:
        row = lax.broadcasted_iota(jnp.int32, (tr, 1), 0) + ti * tr
        valid = row < t_valid
        logw = jnp.where(valid, logw, 0.0)
        a_vec = jnp.where(valid, a_vec, 0.0)
        b_vec = jnp.where(valid, b_vec, 0.0)
        kr2 = jnp.where(valid, kr2, 0.0)
        vr = jnp.where(valid, vr, 0.0)
        bonus = jnp.where(valid, bonus, 0.0)
    l1, l2 = _split2(logw)
    tri = tri_ref[...]
    cs = _dot(tri, l1) + _dot(tri, l2)
    e_pos = jnp.exp(cs)
    e_neg = jnp.exp(-cs)
    at_s[...] = a_vec * jnp.exp(cs - logw)
    rt_s[...] = r * e_pos
    bt_s[...] = b_vec * e_neg
    kt_s[...] = kr2 * e_neg
    v_s[...] = vr
    ep_s[...] = e_pos

    idx_r = lax.broadcasted_iota(jnp.int32, (PAIR, PAIR), 0)
    idx_c = lax.broadcasted_iota(jnp.int32, (PAIR, PAIR), 1)
    same = (idx_r // c_len) == (idx_c // c_len)
    strict = same & ((idx_r % c_len) > (idx_c % c_len))
    incl = same & ((idx_r % c_len) >= (idx_c % c_len))
    eye = idx_r == idx_c
    eye_f = jnp.where(eye, 1.0, 0.0).astype(F32)

    zero = jnp.zeros((PAIR, PAIR), F32)
    n_par = 2 if n_ch % 2 == 0 else 1
    items = [(dc, p) for dc in range(n_par) for p in range(N_PAIR)]
    pairs = range(len(items))

    def phase1(c2, carry_):
        sel = [(pl.ds(pl.multiple_of((c2 * n_par + dc) * c_len, c_len), c_len),
                slice(PAIR * p, PAIR * (p + 1))) for dc, p in items]
        at = [at_s[rw, cl] for rw, cl in sel]
        rt = [rt_s[rw, cl] for rw, cl in sel]
        yb = [_stack_heads(bt_s[rw, cl]) for rw, cl in sel]
        yk = [_stack_heads(kt_s[rw, cl]) for rw, cl in sel]
        vst = [_stack_heads(v_s[rw, cl]).astype(BF16) for rw, cl in sel]
        pc = [ep_s[rw, cl][c_len - 1:c_len] for rw, cl in sel]
        gmat = [_dot_nt(jnp.concatenate([_stack_heads(at[p]), _stack_heads(rt[p])], axis=0).astype(BF16),
                        jnp.concatenate([yb[p], yk[p]], axis=0).astype(BF16)) for p in pairs]
        aab = [jnp.where(strict, gmat[p][0:PAIR, 0:PAIR], zero) for p in pairs]
        aak = [jnp.where(strict, gmat[p][0:PAIR, PAIR:], zero).astype(BF16) for p in pairs]
        lrb = [jnp.where(incl, gmat[p][PAIR:, 0:PAIR], zero).astype(BF16) for p in pairs]
        lrk = [jnp.where(incl, gmat[p][PAIR:, PAIR:], zero).astype(BF16) for p in pairs]
        tinv = [eye_f + aab[p] for p in pairs]
        lp = aab
        n = 1
        while 2 * n < c_len:
            lpb = [x.astype(BF16) for x in lp]
            lp = [_dot(x, x) for x in lpb]
            tinv = [tinv[p] + _dot(tinv[p].astype(BF16), lp[p].astype(BF16)) for p in pairs]
            n *= 2
        tb = [x.astype(BF16) for x in tinv]
        av = [_dot(aak[p], vst[p]).astype(BF16) for p in pairs]
        tav = [_dot(tb[p], av[p]) for p in pairs]
        lrkv = [_dot(lrk[p], vst[p]) for p in pairs]
        zbt = [jnp.transpose(yb[p] * pc[p]).astype(BF16) for p in pairs]
        zkv = [_dot(jnp.transpose(yk[p] * pc[p]).astype(BF16), vst[p]) for p in pairs]
        pcc = [jnp.sum(jnp.where(eye, jnp.broadcast_to(pc[p], (PAIR, PAIR)), zero), axis=-1, keepdims=True)
               for p in pairs]
        for q, (dc, p) in enumerate(items):
            c = c2 * n_par + dc
            t_s[c, p] = tb[q]
            tav_s[c, p] = tav[q]
            lrb_s[c, p] = lrb[q]
            lrkv_s[c, p] = lrkv[q]
            zbt_s[c, p] = zbt[q]
            zkv_s[c, p] = zkv[q]
            ar_s[c, p] = jnp.concatenate([at[q], rt[q]], axis=0).astype(BF16)
            pc_s[c, p] = pcc[q]
        return carry_

    lax.fori_loop(0, n_ch // n_par, phase1, 0)

    pairs = range(N_PAIR)

    def phase2(c, carry_):
        rows = pl.ds(pl.multiple_of(c * c_len, c_len), c_len)
        hbd = [h_st[p] for p in pairs]
        ar = [ar_s[c, p] for p in pairs]
        tb = [t_s[c, p] for p in pairs]
        tav = [tav_s[c, p] for p in pairs]
        lrb = [lrb_s[c, p] for p in pairs]
        lrkv = [lrkv_s[c, p] for p in pairs]
        zbt = [zbt_s[c, p] for p in pairs]
        zkv = [zkv_s[c, p] for p in pairs]
        pcc = [pc_s[c, p] for p in pairs]
        arh = [_dot(ar[p], hbd[p].astype(BF16)) for p in pairs]
        ub = [(_dot(tb[p], _stack_heads(arh[p][0:c_len]).astype(BF16)) + tav[p]).astype(BF16)
              for p in pairs]
        yst = [_dot(lrb[p], ub[p]) + lrkv[p] for p in pairs]
        hn = [pcc[p] * hbd[p] + _dot(zbt[p], ub[p]) + zkv[p] for p in pairs]
        for p in pairs:
            y_s[rows, PAIR * p:PAIR * (p + 1)] = arh[p][c_len:] + yst[p][0:c_len] + yst[p][c_len:]
            h_st[p] = hn[p]
        return carry_

    lax.fori_loop(0, n_ch, phase2, 0)

    @pl.when(ti == pl.num_programs(1) - 1)
    def _():
        first = lax.broadcasted_iota(jnp.int32, (PAIR, HEAD_DIM), 0) < HEAD_DIM
        for p in range(N_PAIR):
            s_t = jnp.transpose(h_st[p])
            sn_ref[0, p] = jnp.where(first, s_t[:, 0:HEAD_DIM], s_t[:, HEAD_DIM:])

    y = y_s[...]
    inv_n = 1.0 / HEAD_DIM
    mean = _group_sum(y, gsum) * inv_n
    d = y - mean
    var = _group_sum(d * d, gsum) * inv_n
    yn = d * lax.rsqrt(var + GN_EPS) * lw_ref[...] + lb_ref[...]
    o_ref[0] = ((yn + bonus) * g).astype(BF16)


def _rwkv(pr, shift0, s0, prm, tr, t_valid):
    b, t_pad, _ = pr.shape
    n_state = s0.shape[0]
    assert shift0.shape[0] == n_state
    last = n_state - 1
    assert t_pad % tr == 0 and tr % RW_CHUNK == 0
    n_ch = tr // RW_CHUNK
    tri = np.zeros((tr, tr), np.float32)
    for c in range(n_ch):
        tri[c * RW_CHUNK:(c + 1) * RW_CHUNK, c * RW_CHUNK:(c + 1) * RW_CHUNK] = np.tril(
            np.ones((RW_CHUNK, RW_CHUNK), np.float32))
    tri = jnp.asarray(tri, BF16)
    vec = _const_spec((1, RW_WIDTH))
    mat = lambda dt: pltpu.VMEM((n_ch, N_PAIR, PAIR, PAIR), dt)
    tile = lambda: pltpu.VMEM((tr, RW_WIDTH), F32)
    return pl.pallas_call(
        functools.partial(_rwkv_kernel, tr=tr, t_valid=t_valid, n_state=n_state),
        grid=(b, t_pad // tr),
        in_specs=[pl.BlockSpec((1, tr, RW_PROJ), lambda bi, ti: (bi, ti, 0)),
                  pl.BlockSpec((1, 1, RW_PROJ), lambda bi, ti: (jnp.minimum(bi, last), 0, 0)),
                  pl.BlockSpec((1, N_PAIR, PAIR, HEAD_DIM),
                               lambda bi, ti: (jnp.minimum(bi, last), 0, 0, 0)),
                  _const_spec((1, RW_PROJ)), vec, vec, vec, vec, vec, vec, vec,
                  _const_spec((W_LORA + A_LORA, RW_WIDTH)), _const_spec((W_LORA + A_LORA, RW_WIDTH)),
                  _const_spec((G_LORA, RW_WIDTH)), _const_spec((LANES, LANES)),
                  _const_spec((tr, tr))],
        out_specs=[pl.BlockSpec((1, tr, RW_WIDTH), lambda bi, ti: (bi, ti, 0)),
                   pl.BlockSpec((1, N_PAIR, PAIR, HEAD_DIM), lambda bi, ti: (bi, 0, 0, 0))],
        out_shape=[jax.ShapeDtypeStruct((b, t_pad, RW_WIDTH), BF16),
                   jax.ShapeDtypeStruct((b, N_PAIR, PAIR, HEAD_DIM), F32)],
        scratch_shapes=[pltpu.VMEM((N_PAIR, PAIR, PAIR), F32),
                        pltpu.VMEM((8 + tr, RW_PROJ), F32), pltpu.VMEM((1, RW_PROJ), F32),
                        tile(), tile(), tile(), tile(), tile(), tile(), tile(),
                        mat(BF16), mat(F32), mat(BF16), mat(F32), mat(BF16), mat(F32), mat(BF16),
                        pltpu.VMEM((n_ch, N_PAIR, PAIR, 1), F32)],
        compiler_params=_cparams(("arbitrary", "arbitrary")),
        name="rwkv",
    )(pr, shift0, s0, prm["mu"], prm["w0"], prm["a0"], prm["k_k"], prm["k_a"], prm["r_k"],
      prm["lnx_w"], prm["lnx_b"], prm["w2p"], prm["a2p"], prm["g2"], prm["gsum"], tri)


FF_COLS = 256


def _ffn_kernel(x_ref, oa_ref, orw_ref, c0_ref, wo_ref, g_ref, wu_ref, cw_ref, cb_ref, wd_ref,
                y_ref, cn_ref, zbuf, cbuf, h_ref, x1_ref, *, tm, ts, d_ff):
    off = zbuf.shape[1] - tm
    da = oa_ref.shape[1]

    @pl.when(pl.program_id(0) == 0)
    def _():
        cbuf[...] = c0_ref[...]

    x1 = x_ref[...] + _dot(oa_ref[...], wo_ref[0:da, :]) + _dot(orw_ref[...], wo_ref[da:, :])
    x1_ref[...] = x1
    ms = jnp.mean(x1 * x1, axis=-1, keepdims=True)
    h_ref[...] = (x1 * lax.rsqrt(ms + NORM_EPS) * g_ref[...]).astype(BF16)

    def up_proj(c0, zb):
        cols = slice(c0, c0 + FF_COLS)
        zb[off - 2 * ts:off, :] = cbuf[:, cols]
        zb[off:off + tm, :] = _dot(h_ref[...], wu_ref[:, cols])
        cbuf[:, cols] = zb[off + tm - 2 * ts:off + tm, :]

    def conv_cols(c0, zb):
        cols = slice(c0, c0 + FF_COLS)
        z2 = zb[off - 2 * ts:off - 2 * ts + tm, :]
        z1 = zb[off - ts:off - ts + tm, :]
        z = zb[off:off + tm, :]
        return (cb_ref[:, cols] + z2 * cw_ref[0:1, cols] + z1 * cw_ref[1:2, cols]
                + z * cw_ref[2:3, cols])

    def stage(c):
        up_proj(c * FF_COLS, zbuf.at[2 * (c % 2)])
        up_proj(d_ff + c * FF_COLS, zbuf.at[2 * (c % 2) + 1])

    n_chunks = d_ff // FF_COLS
    acc = None
    stage(0)
    for c in range(n_chunks):
        if c + 1 < n_chunks:
            stage(c + 1)
        gate = conv_cols(c * FF_COLS, zbuf.at[2 * (c % 2)])
        up = conv_cols(d_ff + c * FF_COLS, zbuf.at[2 * (c % 2) + 1])
        act = (gate * _sigmoid(gate) * up).astype(BF16)
        part = _dot(act, wd_ref[c * FF_COLS:(c + 1) * FF_COLS, :])
        acc = part if acc is None else acc + part

    y_ref[...] = x1_ref[...] + acc
    cn_ref[...] = cbuf[...]


def _ffn(x, oa, orw, conv0, prm, tm, ts):
    rows, d = x.shape
    d_ff = prm["w_down"].shape[0]
    assert rows % tm == 0 and d_ff % FF_COLS == 0 and (ts == 1 or ts % 8 == 0)
    off = -(-2 * ts // 8) * 8
    row = lambda w: pl.BlockSpec((tm, w), lambda i: (i, 0))
    return pl.pallas_call(
        functools.partial(_ffn_kernel, tm=tm, ts=ts, d_ff=d_ff),
        grid=(rows // tm,),
        in_specs=[row(d), row(DA_WIDTH), row(RW_WIDTH), _const_spec((2 * ts, 2 * d_ff)),
                  _resident_spec((DA_WIDTH + RW_WIDTH, d)), _const_spec((1, d)),
                  _resident_spec((d, 2 * d_ff)), _const_spec((CONV_W, 2 * d_ff)),
                  _const_spec((1, 2 * d_ff)), _resident_spec((d_ff, d))],
        out_specs=[row(d), _const_spec((2 * ts, 2 * d_ff))],
        out_shape=[jax.ShapeDtypeStruct((rows, d), F32),
                   jax.ShapeDtypeStruct((2 * ts, 2 * d_ff), F32)],
        scratch_shapes=[pltpu.VMEM((4, off + tm, FF_COLS), F32), pltpu.VMEM((2 * ts, 2 * d_ff), F32),
                        pltpu.VMEM((tm, d), BF16), pltpu.VMEM((tm, d), F32)],
        compiler_params=_cparams(("arbitrary",)),
        name="ffn",
    )(x, oa, orw, conv0, prm["w_out"], prm["ln2_g"], prm["w_up"], prm["conv_w"], prm["conv_b"],
      prm["w_down"])


def _rel_bucket(rel):
    nb = N_BUCKETS // 2
    max_exact = nb // 2
    bucket = jnp.where(rel > 0, nb, 0)
    n = jnp.abs(rel)
    nf = jnp.maximum(n, 1).astype(F32)
    large = max_exact + (jnp.log(nf / max_exact) / math.log(MAX_DISTANCE / max_exact)
                         * (nb - max_exact)).astype(jnp.int32)
    large = jnp.minimum(large, nb - 1)
    return bucket + jnp.where(n < max_exact, n, large)


def _bias_table(rel_bias, q_pos, k_pos, mask):
    n_q, n_k = len(q_pos), len(k_pos)
    assert np.all(np.diff(q_pos) == 1) and np.all(np.diff(k_pos) == 1)
    span = n_q + n_k - 1
    rel = jnp.asarray(int(k_pos[0]) - int(q_pos[0]) - (n_q - 1) + np.arange(span), jnp.int32)
    w = jnp.transpose(rel_bias[_rel_bucket(rel)]).astype(F32) * LOG2E
    return jnp.where(jnp.asarray(mask)[None], _toeplitz(w, n_q, n_k), NEG_INF)


def _skew(w, n_q, n_k):
    span = n_q + n_k - 1
    x = jnp.pad(w[:, :span], ((0, 0), (0, 1)))
    rows = jnp.tile(x, (1, n_q))[:, :n_q * span].reshape(w.shape[0], n_q, span)
    return rows[:, :, n_q - 1:n_q - 1 + n_k]


def _toeplitz(w, n_q, n_k):
    blk = LANES
    if n_q % blk or n_k % blk or n_q * n_k <= blk * blk:
        return _skew(w, n_q, n_k)
    nbq, nbk = n_q // blk, n_k // blk
    block = {d: _skew(w[:, (d + nbq - 1) * blk:(d + nbq + 1) * blk - 1], blk, blk)
             for d in range(-(nbq - 1), nbk)}
    return jnp.concatenate([jnp.concatenate([block[c - r] for c in range(nbk)], axis=2)
                            for r in range(nbq)], axis=1)


BOUND_MAX_SPREAD = 100.0


def _score_bound(q_g, k_g, rel_bias):
    s_max = (HEAD_DIM ** 0.5) * LOG2E * 1.01 * jnp.max(jnp.abs(q_g)) * jnp.max(jnp.abs(k_g))
    b_hi = jnp.max(rel_bias) * LOG2E
    b_lo = jnp.min(rel_bias) * LOG2E
    use = (2.0 * s_max + (b_hi - b_lo)) <= BOUND_MAX_SPREAD
    return jnp.stack([use.astype(F32), (s_max + b_hi).astype(F32)])


def _ext_chunk(pos):
    return np.where(pos < N_META, -1, (pos - N_META) // CHUNK)


def _prompt_bias(rel_bias):
    tq = ATT_T
    fr = np.arange(tq) + N_META
    meta = np.arange(N_META)
    causal = _ext_chunk(fr)[None, :] <= _ext_chunk(fr)[:, None]
    tr = lambda b: jnp.swapaxes(b, 1, 2)
    b0 = tr(_bias_table(rel_bias, fr, fr, causal))
    bm1 = tr(_bias_table(rel_bias, fr + tq, fr, np.ones((tq, tq), bool)))
    bq0 = tr(_bias_table(rel_bias, fr, meta, np.ones((tq, N_META), bool)))
    bmm = _bias_table(rel_bias, meta, meta, np.ones((N_META, N_META), bool))
    assert tq + 1 >= MAX_DISTANCE
    cfar = rel_bias[_rel_bucket(jnp.asarray(-(tq + 1), jnp.int32))].astype(F32) * LOG2E
    return cfar, b0, bm1, bq0, bmm


def _decode_bias(rel_bias, n_cache, nq):
    k_pos = np.arange(n_cache + nq)
    q_pos = k_pos[n_cache:]
    mask = _ext_chunk(k_pos)[None, :] <= _ext_chunk(q_pos)[:, None]
    return _bias_table(rel_bias, q_pos, k_pos, mask)


def _block_ones(n, blk, dtype):
    idx = np.arange(n) // blk
    return jnp.asarray((idx[:, None] == idx[None, :]).astype(np.float32), dtype)


def kernel(x_prompt, x_sample, cache_k, cache_v, state_rwkv, state_shift, state_conv, meta_tokens,
           rel_bias, ln1_g, w_in, q_norm_g, k_norm_g, lam_q1, lam_k1, lam_q2, lam_k2, subln_g,
           mu_shift, w0, w2, a0, a2, g2, k_k, k_a, r_k, lnx_w, lnx_b, w_out, ln2_g, w_up, conv_w,
           conv_b, w_down):
    bp, seq, d = x_prompt.shape
    db, dt, _ = x_sample.shape
    depth = w_in.shape[0]
    d_ff = w_down.shape[1]
    n_cache = cache_k.shape[2]
    assert bp == 1 and dt == N_META, "the meta stream rides with the decode streams"
    assert cache_k.shape[3] == DA_HEADS and cache_k.shape[4] == 2 * HEAD_DIM
    nb = db + 1
    nb_pad = -(-nb // 8) * 8

    cfar, b0, bm1, bq0, bmm = _prompt_bias(rel_bias)
    bias_dec = _decode_bias(rel_bias, n_cache, dt)
    gsum = _block_ones(LANES, HEAD_DIM, BF16)
    gmean = gsum * (1.0 / HEAD_DIM)
    zrow = lambda n: jnp.zeros((n, RW_WIDTH), BF16)

    x_f = x_prompt[0]
    x_s = jnp.concatenate([x_sample, meta_tokens.astype(x_sample.dtype)[None]], axis=0)
    outs = [[] for _ in range(10)]
    for l in range(depth):
        lam_init = 0.8 - 0.6 * math.exp(-0.3 * l)
        lam = (jnp.exp(jnp.sum(lam_q1[l].astype(F32) * lam_k1[l].astype(F32)))
               - jnp.exp(jnp.sum(lam_q2[l].astype(F32) * lam_k2[l].astype(F32))) + lam_init).reshape(1)
        out_scale = 1.0 - lam_init
        tile128 = lambda g_: jnp.tile(g_.reshape(1, -1), (1, DA_WIDTH // g_.shape[-1]))
        qg, kg = tile128(q_norm_g[l]), tile128(k_norm_g[l])
        sg = subln_g[l].reshape(1, LANES)
        w_in_bf = w_in[l].astype(BF16)
        rw = {
            "mu": mu_shift[l].reshape(1, -1), "w0": w0[l].reshape(1, -1), "a0": a0[l].reshape(1, -1),
            "k_k": k_k[l].reshape(1, -1), "k_a": k_a[l].reshape(1, -1), "r_k": r_k[l].reshape(1, -1),
            "lnx_w": lnx_w[l].reshape(1, -1), "lnx_b": lnx_b[l].reshape(1, -1),
            "w2p": jnp.concatenate([w2[l].astype(BF16), zrow(A_LORA)], axis=0),
            "a2p": jnp.concatenate([zrow(W_LORA), a2[l].astype(BF16)], axis=0),
            "g2": g2[l].astype(BF16), "gsum": gsum,
        }
        ff = {
            "w_out": w_out[l].astype(BF16), "ln2_g": ln2_g[l].reshape(1, -1),
            "w_up": w_up[l].astype(BF16), "conv_w": conv_w[l], "conv_b": conv_b[l].reshape(1, -1),
            "w_down": w_down[l].astype(BF16),
        }

        q_f, k_f, v_f, pr_f, kb_t, vt_t = _proj(x_f, ln1_g[l].reshape(1, -1), w_in_bf, qg, kg, gmean,
                                                 512, True)
        q_s, k_s, v_s, pr_s = _proj(x_s.reshape(nb * dt, d), ln1_g[l].reshape(1, -1), w_in_bf, qg, kg,
                                    gmean, nb * dt, False)
        m0 = db * dt

        kb_m = k_s[m0:].astype(BF16)
        vb_m = v_s[m0:].astype(BF16)
        o_f, o_m = _attn_prompt(cfar, lam, _score_bound(q_norm_g[l], k_norm_g[l], rel_bias), q_f, kb_t, vt_t, kb_m, vb_m, jnp.transpose(vb_m), q_s[m0:],
                                b0, bm1, bq0, bmm, sg, out_scale)
        o_d = _attn_decode(lam, q_s, cache_k, cache_v, l, k_s, v_s, bias_dec, sg, out_scale, dt)
        o_s = jnp.concatenate([o_d, o_m], axis=0)

        pr_s3 = pr_s.reshape(nb, dt, RW_PROJ)
        pr_pad = jnp.pad(pr_s3, ((0, 0), (0, RW_CHUNK - dt), (0, 0)))
        pair_view = (N_PAIR, PAIR, HEAD_DIM)
        orw_s, sn_s = _rwkv(pr_pad, state_shift[l][:, None, :], state_rwkv[l].reshape(db, *pair_view),
                            rw, RW_CHUNK, dt)
        orw_f, sn_f = _rwkv(pr_f[None], pr_s3[db:, dt - 1:dt, :], sn_s[db:], rw, 512, seq)

        def time_major(a):
            a = jnp.pad(a.reshape(nb, dt, -1), ((0, nb_pad - nb), (0, 0), (0, 0)))
            return jnp.swapaxes(a, 0, 1).reshape(dt * nb_pad, -1)

        conv_s = jnp.concatenate([state_conv[l], jnp.zeros((1, CONV_W - 1, 2 * d_ff), F32)], axis=0)
        conv_s = jnp.pad(conv_s, ((0, nb_pad - nb), (0, 0), (0, 0)))
        conv_s = jnp.swapaxes(conv_s, 0, 1).reshape(2 * nb_pad, 2 * d_ff)
        y_s, cn_s = _ffn(time_major(x_s), time_major(o_s), time_major(orw_s[:, :dt]), conv_s, ff,
                         dt * nb_pad, nb_pad)
        cn_s = jnp.swapaxes(cn_s.reshape(2, nb_pad, 2 * d_ff), 0, 1)
        y_f, cn_f = _ffn(x_f, o_f, orw_f[0], cn_s[db], ff, 512, 1)
        y_s = jnp.swapaxes(y_s.reshape(dt, nb_pad, d), 0, 1)[:nb]

        hw = (DA_HEADS, 2 * HEAD_DIM)
        outs[0].append(jnp.concatenate([k_s[m0:].reshape(N_META, *hw), k_f], axis=0)[None])
        outs[1].append(jnp.concatenate([v_s[m0:].reshape(N_META, *hw), v_f], axis=0)[None])
        outs[2].append(sn_f.reshape(bp, RW_HEADS, HEAD_DIM, HEAD_DIM))
        outs[3].append(pr_f[seq - 1:seq])
        outs[4].append(cn_f[None])
        outs[5].append(k_s[:m0].reshape(db, dt, *hw))
        outs[6].append(v_s[:m0].reshape(db, dt, *hw))
        outs[7].append(sn_s[:db].reshape(db, RW_HEADS, HEAD_DIM, HEAD_DIM))
        outs[8].append(pr_s3[:db, dt - 1])
        outs[9].append(cn_s[:db])
        x_f, x_s = y_f, y_s

    return (x_f[None], x_s[:db], *[jnp.stack(o) for o in outs])
```

```python
import functools
import math

import numpy as np
import jax
import jax.numpy as jnp
from jax import lax
from jax.experimental import pallas as pl
from jax.experimental.pallas import tpu as pltpu

F32 = jnp.float32
BF16 = jnp.bfloat16

CHUNK = 64
N_META = 16
HEAD_DIM = 64
DA_HEADS = 4
RW_HEADS = 8
W_LORA = 64
A_LORA = 64
G_LORA = 128
CONV_W = 3
N_BUCKETS = 32
MAX_DISTANCE = 128
NORM_EPS = 1e-6
GN_EPS = 64e-5
NEG_INF = -1e30
LOG2E = math.log2(math.e)
DECAY_SCALE = math.exp(-0.5)

DA_WIDTH = DA_HEADS * 2 * HEAD_DIM
RW_WIDTH = RW_HEADS * HEAD_DIM
RW_PROJ = 3 * RW_WIDTH + W_LORA + A_LORA + G_LORA
LANES = 128
PAIR = 2 * HEAD_DIM
N_PAIR = RW_WIDTH // PAIR
RW_CHUNK = 64
ATT_T = 512
VMEM_LIMIT = 56 * 1024 * 1024


def _dot(a, b):
    return jnp.dot(a, b, preferred_element_type=F32)


def _dot_nt(a, b):
    return lax.dot_general(a, b, (((1,), (1,)), ((), ())), preferred_element_type=F32)


def _group_sum(x, blk):
    xb = x.astype(BF16)
    return jnp.concatenate([_dot(xb[:, LANES * p:LANES * (p + 1)], blk)
                            for p in range(x.shape[1] // LANES)], axis=1)


def _cparams(sem):
    return pltpu.CompilerParams(dimension_semantics=sem, vmem_limit_bytes=VMEM_LIMIT)


def _const_spec(shape):
    nd = len(shape)
    return pl.BlockSpec(shape, lambda *_: (0,) * nd)


def _resident_spec(shape):
    nd = len(shape)
    return pl.BlockSpec(shape, lambda *_: (0,) * nd, pipeline_mode=pl.Buffered(1))


def _proj_kernel(x_ref, g_ref, w_ref, qg_ref, kg_ref, gm_ref,
                 q_ref, k_ref, v_ref, pr_ref, *tile_refs, tm):
    x = x_ref[...]
    ms = jnp.mean(x * x, axis=-1, keepdims=True)
    h = (x * lax.rsqrt(ms + NORM_EPS) * g_ref[...]).astype(BF16)
    gm = gm_ref[...]

    def group_norm(t, g):
        ms_g = _group_sum(t * t, gm)
        return t * lax.rsqrt(ms_g + NORM_EPS) * g

    q = _dot(h, w_ref[:, 0:DA_WIDTH])
    k = _dot(h, w_ref[:, DA_WIDTH:2 * DA_WIDTH])
    pr_ref[...] = _dot(h, w_ref[:, 3 * DA_WIDTH:])
    v = _dot(h, w_ref[:, 2 * DA_WIDTH:3 * DA_WIDTH])
    q_ref[...] = (group_norm(q, qg_ref[...]) * (HEAD_DIM ** -0.5 * LOG2E)).astype(BF16)
    k = group_norm(k, kg_ref[...])
    if len(k_ref.shape) == 3:
        for hd in range(DA_HEADS):
            k_ref[:, hd, :] = k[:, hd * LANES:(hd + 1) * LANES]
            v_ref[:, hd, :] = v[:, hd * LANES:(hd + 1) * LANES]
    else:
        k_ref[...] = k
        v_ref[...] = v
    if tile_refs:
        kb_ref, vt_ref = tile_refs
        kb = k.astype(BF16)
        vt = jnp.transpose(v).astype(BF16)
        for hd in range(DA_HEADS):
            for jj in range(tm // ATT_T):
                kb_ref[hd, jj] = kb[jj * ATT_T:(jj + 1) * ATT_T, hd * LANES:(hd + 1) * LANES]
                vt_ref[hd, jj] = vt[hd * LANES:(hd + 1) * LANES, jj * ATT_T:(jj + 1) * ATT_T]


def _proj(x, ln1_g, w_in_bf, qg, kg, gmean, tm, emit_tiles):
    rows, d = x.shape
    n_in = w_in_bf.shape[1]
    assert rows % tm == 0
    row = lambda w: pl.BlockSpec((tm, w), lambda i: (i, 0))
    out_specs = [row(DA_WIDTH), row(DA_WIDTH), row(DA_WIDTH), row(RW_PROJ)]
    out_shape = [jax.ShapeDtypeStruct((rows, DA_WIDTH), BF16),
                 jax.ShapeDtypeStruct((rows, DA_WIDTH), F32),
                 jax.ShapeDtypeStruct((rows, DA_WIDTH), F32),
                 jax.ShapeDtypeStruct((rows, RW_PROJ), F32)]
    if emit_tiles:
        assert tm % ATT_T == 0
        tpt = tm // ATT_T
        for o in (1, 2):
            out_specs[o] = pl.BlockSpec((tm, DA_HEADS, LANES), lambda i: (i, 0, 0))
            out_shape[o] = jax.ShapeDtypeStruct((rows, DA_HEADS, LANES), F32)
        out_specs += [pl.BlockSpec((DA_HEADS, tpt, ATT_T, LANES), lambda i: (0, i, 0, 0)),
                      pl.BlockSpec((DA_HEADS, tpt, LANES, ATT_T), lambda i: (0, i, 0, 0))]
        out_shape += [jax.ShapeDtypeStruct((DA_HEADS, rows // ATT_T, ATT_T, LANES), BF16),
                      jax.ShapeDtypeStruct((DA_HEADS, rows // ATT_T, LANES, ATT_T), BF16)]
    return pl.pallas_call(
        functools.partial(_proj_kernel, tm=tm),
        grid=(rows // tm,),
        in_specs=[row(d), _const_spec((1, d)), _resident_spec((d, n_in)),
                  _const_spec((1, DA_WIDTH)), _const_spec((1, DA_WIDTH)),
                  _const_spec((LANES, LANES))],
        out_specs=out_specs,
        out_shape=out_shape,
        compiler_params=_cparams(("arbitrary",)),
        name="proj",
    )(x, ln1_g, w_in_bf, qg, kg, gmean)


def _stack_components(q):
    lo = lax.broadcasted_iota(jnp.int32, q.shape, 1) < HEAD_DIM
    zero = jnp.zeros_like(q)
    return jnp.concatenate([jnp.where(lo, q, zero), jnp.where(lo, zero, q)], axis=0)


def _sub_layer_norm(o, g, out_scale):
    ms = jnp.mean(o * o, axis=-1, keepdims=True)
    return o * lax.rsqrt(ms + NORM_EPS) * g * out_scale


def _attn_prompt_kernel(cfar_ref, lam_ref, fix_ref, q_ref, k_ref, vt_ref, km_ref, vm_ref, vmt_ref, qm_ref,
                        b0_ref, bm1_ref, bq0_ref, bmm_ref, g_ref, o_ref, om_ref,
                        acc_ref, m_ref, l_ref, s_buf, p_buf, a_buf, *, out_scale):
    h = pl.program_id(0)
    i = pl.program_id(1)
    tq = ATT_T
    cf = cfar_ref[h]
    lam = lam_ref[0]
    qst = _stack_components(q_ref[...])

    def both(b):
        return jnp.concatenate([b, b], axis=1)

    s = _dot_nt(km_ref[...], qst)
    s = s + both(jnp.where(i == 0, bq0_ref[0], cf))
    use_bound = fix_ref[0] > 0.5
    bound = fix_ref[1]
    m0 = jnp.where(use_bound, bound, jnp.max(s, axis=0, keepdims=True))
    p = jnp.exp2(s - m0)
    m_ref[...] = m0
    l_ref[...] = jnp.sum(p, axis=0, keepdims=True)
    acc_ref[...] = _dot(vmt_ref[...], p.astype(BF16))

    def scores(idx, slot):
        s_buf[slot] = _dot_nt(k_ref[0, idx], qst)

    def softmax(slot, bias):
        s = s_buf[slot]
        m_prev = m_ref[...]
        if bias.ndim == 0:
            m_new = jnp.maximum(m_prev, jnp.max(s, axis=0, keepdims=True) + bias)
            p = jnp.exp2(s - (m_new - bias))
        else:
            s = s + both(bias)
            m_new = jnp.maximum(m_prev, jnp.max(s, axis=0, keepdims=True))
            p = jnp.exp2(s - m_new)
        alpha = jnp.exp2(m_prev - m_new)
        l_ref[...] = alpha * l_ref[...] + jnp.sum(p, axis=0, keepdims=True)
        m_ref[...] = m_new
        p_buf[slot] = p.astype(BF16)
        a_buf[slot] = alpha

    def accumulate(idx, slot):
        acc_ref[...] = a_buf[slot] * acc_ref[...] + _dot(vt_ref[0, idx], p_buf[slot])

    n_far = jnp.maximum(i - 1, 0)
    off = n_far % 2
    p_buf[1] = jnp.zeros(p_buf.shape[1:], BF16)
    a_buf[1] = jnp.ones(a_buf.shape[1:], F32)
    scores(0, 0)

    def far_body(t, carry):
        u = 2 * t - off
        scores(u + 1, 1)
        softmax(0, jnp.where(u >= 0, cf, NEG_INF))
        accumulate(jnp.maximum(u - 1, 0), 1)
        scores(u + 2, 0)
        softmax(1, cf)
        accumulate(jnp.maximum(u, 0), 0)
        return carry

    def bound_softmax(slot, bias):
        p = jnp.exp2(s_buf[slot] - (bound - bias))
        l_ref[...] += jnp.sum(p, axis=0, keepdims=True)
        p_buf[slot] = p.astype(BF16)

    def bound_accumulate(idx, slot):
        acc_ref[...] += _dot(vt_ref[0, idx], p_buf[slot])

    def far_body_bound(t, carry):
        u = 2 * t - off
        scores(u + 1, 1)
        bound_softmax(0, jnp.where(u >= 0, cf, NEG_INF))
        bound_accumulate(jnp.maximum(u - 1, 0), 1)
        scores(u + 2, 0)
        bound_softmax(1, cf)
        bound_accumulate(jnp.maximum(u, 0), 0)
        return carry

    @pl.when(use_bound)
    def _():
        lax.fori_loop(0, (n_far + 1) // 2, far_body_bound, 0)

    @pl.when(jnp.logical_not(use_bound))
    def _():
        lax.fori_loop(0, (n_far + 1) // 2, far_body, 0)

    scores(i, 1)
    softmax(0, jnp.where(i >= 1, bm1_ref[0], NEG_INF))
    accumulate(jnp.maximum(n_far - 1, 0), 1)
    softmax(1, b0_ref[0])
    accumulate(n_far, 0)
    accumulate(i, 1)

    accn = acc_ref[...] * (1.0 / l_ref[...])
    o_t = accn[:, 0:tq] - lam * accn[:, tq:]
    o_ref[...] = _sub_layer_norm(jnp.transpose(o_t), g_ref[...], out_scale).astype(BF16)

    @pl.when(i == 0)
    def _():
        qm = _stack_components(qm_ref[...])
        bmm = bmm_ref[0]
        sm = _dot_nt(qm, km_ref[...]) + jnp.concatenate([bmm, bmm], axis=0)
        mm = jnp.max(sm, axis=-1, keepdims=True)
        pm = jnp.exp2(sm - mm)
        accm = _dot(pm.astype(BF16), vm_ref[...]) / jnp.sum(pm, axis=-1, keepdims=True)
        om = accm[0:N_META] - lam * accm[N_META:]
        om_ref[...] = _sub_layer_norm(om, g_ref[...], out_scale).astype(BF16)


def _attn_prompt(cfar, lam, fix, q_f, kb_t, vt_t, kb_m, vb_m, vbt_m, q_m, b0, bm1, bq0, bmm, subln_g,
                 out_scale):
    tf = q_f.shape[0]
    tq = ATT_T
    assert tf % tq == 0
    n_t = tf // tq
    smem = pl.BlockSpec(memory_space=pltpu.SMEM)
    head_col = lambda rows: pl.BlockSpec((rows, LANES), lambda h, i: (0, h))
    head_tile = lambda a, b: pl.BlockSpec((1, a, b), lambda h, i: (h, 0, 0))
    return pl.pallas_call(
        functools.partial(_attn_prompt_kernel, out_scale=out_scale),
        grid=(DA_HEADS, n_t),
        in_specs=[smem, smem, smem,
                  pl.BlockSpec((tq, LANES), lambda h, i: (i, h)),
                  pl.BlockSpec((1, n_t, tq, LANES), lambda h, i: (h, 0, 0, 0)),
                  pl.BlockSpec((1, n_t, LANES, tq), lambda h, i: (h, 0, 0, 0)),
                  head_col(N_META), head_col(N_META),
                  pl.BlockSpec((LANES, N_META), lambda h, i: (h, 0)),
                  head_col(N_META),
                  head_tile(tq, tq), head_tile(tq, tq), head_tile(N_META, tq),
                  head_tile(N_META, N_META), _const_spec((1, LANES))],
        out_specs=[pl.BlockSpec((tq, LANES), lambda h, i: (i, h)), head_col(N_META)],
        out_shape=[jax.ShapeDtypeStruct((tf, DA_WIDTH), BF16),
                   jax.ShapeDtypeStruct((N_META, DA_WIDTH), BF16)],
        scratch_shapes=[pltpu.VMEM((LANES, 2 * tq), F32), pltpu.VMEM((1, 2 * tq), F32),
                        pltpu.VMEM((1, 2 * tq), F32), pltpu.VMEM((2, tq, 2 * tq), F32),
                        pltpu.VMEM((2, tq, 2 * tq), BF16), pltpu.VMEM((2, 1, 2 * tq), F32)],
        compiler_params=_cparams(("arbitrary", "arbitrary")),
        name="attn_prompt",
    )(cfar, lam, fix, q_f, kb_t, vt_t, kb_m, vb_m, vbt_m, q_m, b0, bm1, bq0, bmm, subln_g)


def _attn_decode_kernel(lam_ref, q_ref, k_hbm, v_hbm, kn_ref, vn_ref, b_ref, g_ref, o_ref,
                        kbuf, vbuf, sem, *, layer, n_main, out_scale):
    bi = pl.program_id(0)
    h = pl.program_id(1)
    n_h = pl.num_programs(1)
    step = bi * n_h + h
    slot = step % 2

    def cache_copies(b_, h_, slot_):
        return (pltpu.make_async_copy(k_hbm.at[layer, b_, :, h_, :], kbuf.at[slot_], sem.at[0, slot_]),
                pltpu.make_async_copy(v_hbm.at[layer, b_, :, h_, :], vbuf.at[slot_], sem.at[1, slot_]))

    @pl.when(step == 0)
    def _():
        for cp in cache_copies(bi, h, slot):
            cp.start()

    @pl.when(step + 1 < pl.num_programs(0) * n_h)
    def _():
        nxt = step + 1
        for cp in cache_copies(nxt // n_h, nxt % n_h, 1 - slot):
            cp.start()

    for cp in cache_copies(bi, h, slot):
        cp.wait()

    lam = lam_ref[0]
    nq = q_ref.shape[0]
    n_cache = kbuf.shape[1]
    k_ref = kbuf.at[slot]
    v_ref = vbuf.at[slot]
    qst = _stack_components(q_ref[...])
    k_main = k_ref[0:n_main, :].astype(BF16)
    v_main = v_ref[0:n_main, :].astype(BF16)
    k_tail = jnp.concatenate([k_ref[n_main:n_cache, :], kn_ref[...]], axis=0).astype(BF16)
    v_tail = jnp.concatenate([v_ref[n_main:n_cache, :], vn_ref[...]], axis=0).astype(BF16)
    bias = jnp.concatenate([b_ref[0], b_ref[0]], axis=0)
    s1 = _dot_nt(qst, k_main) + bias[:, 0:n_main]
    s2 = _dot_nt(qst, k_tail) + bias[:, n_main:]
    m = jnp.maximum(jnp.max(s1, axis=-1, keepdims=True), jnp.max(s2, axis=-1, keepdims=True))
    p1 = jnp.exp2(s1 - m)
    p2 = jnp.exp2(s2 - m)
    l = jnp.sum(p1, axis=-1, keepdims=True) + jnp.sum(p2, axis=-1, keepdims=True)
    res = (_dot(p1.astype(BF16), v_main) + _dot(p2.astype(BF16), v_tail)) / l
    o = res[0:nq] - lam * res[nq:]
    o_ref[...] = _sub_layer_norm(o, g_ref[...], out_scale).astype(BF16)


def _attn_decode(lam, q_s, cache_k, cache_v, layer, k_s, v_s, bias, subln_g, out_scale, nq):
    _, b, n_cache, n_h, _ = cache_k.shape
    n_main = (n_cache // LANES) * LANES
    n_keys = n_cache + nq
    smem = pl.BlockSpec(memory_space=pltpu.SMEM)
    hbm = pl.BlockSpec(memory_space=pl.ANY)
    new = pl.BlockSpec((nq, LANES), lambda bi, h: (bi, h))
    return pl.pallas_call(
        functools.partial(_attn_decode_kernel, layer=layer, n_main=n_main, out_scale=out_scale),
        grid=(b, n_h),
        in_specs=[smem, new, hbm, hbm, new, new,
                  pl.BlockSpec((1, nq, n_keys), lambda bi, h: (h, 0, 0)), _const_spec((1, LANES))],
        out_specs=new,
        out_shape=jax.ShapeDtypeStruct((b * nq, DA_WIDTH), BF16),
        scratch_shapes=[pltpu.VMEM((2, n_cache, LANES), F32), pltpu.VMEM((2, n_cache, LANES), F32),
                        pltpu.SemaphoreType.DMA((2, 2))],
        compiler_params=_cparams(("arbitrary", "arbitrary")),
        name="attn_decode",
    )(lam, q_s, cache_k, cache_v, k_s, v_s, bias, subln_g)


def _sigmoid(x):
    return 1.0 / (1.0 + jnp.exp(-x))


def _split2(x):
    x1 = x.astype(BF16)
    x2 = (x - x1.astype(F32)).astype(BF16)
    return x1, x2


def _stack_heads(x):
    lo = lax.broadcasted_iota(jnp.int32, x.shape, 1) < HEAD_DIM
    zero = jnp.zeros_like(x)
    return jnp.concatenate([jnp.where(lo, x, zero), jnp.where(lo, zero, x)], axis=0)


def _rwkv_kernel(pr_ref, sh0_ref, h0_ref, mu_ref, w0_ref, a0_ref, kk_ref, ka_ref, rk_ref,
                 lw_ref, lb_ref, w2_ref, a2_ref, g2_ref, gsum_ref, tri_ref,
                 o_ref, sn_ref,
                 h_st, xbuf, carry, at_s, rt_s, bt_s, kt_s, v_s, ep_s, y_s,
                 t_s, tav_s, lrb_s, lrkv_s, zbt_s, zkv_s, ar_s, pc_s,
                 *, tr, t_valid, n_state):
    c_len = RW_CHUNK
    n_ch = tr // c_len
    ti = pl.program_id(1)

    @pl.when(ti == 0)
    def _():
        has_state = pl.program_id(0) < n_state
        carry[...] = jnp.where(has_state, sh0_ref[0], 0.0)
        hd_r = lax.broadcasted_iota(jnp.int32, (PAIR, PAIR), 0) // HEAD_DIM
        hd_c = lax.broadcasted_iota(jnp.int32, (PAIR, PAIR), 1) // HEAD_DIM
        for p in range(N_PAIR):
            s_p = h0_ref[0, p]
            blk = jnp.where((hd_r == hd_c) & has_state, jnp.concatenate([s_p, s_p], axis=1), 0.0)
            h_st[p] = jnp.transpose(blk)

    x = pr_ref[0]
    xbuf[8:8 + tr, :] = x
    xbuf[7:8, :] = carry[...]
    prev = xbuf[7:7 + tr, :]
    carry[...] = pr_ref[0, tr - 1:tr, :]
    xm = x + (prev - x) * mu_ref[...]
    r = xm[:, 0:RW_WIDTH]
    kr = xm[:, RW_WIDTH:2 * RW_WIDTH]
    vr = xm[:, 2 * RW_WIDTH:3 * RW_WIDTH]
    wa = xm[:, 3 * RW_WIDTH:3 * RW_WIDTH + W_LORA + A_LORA]
    gd = xm[:, 3 * RW_WIDTH + W_LORA + A_LORA:]
    lane_wa = lax.broadcasted_iota(jnp.int32, wa.shape, 1)
    twa = jnp.where(lane_wa < W_LORA, jnp.tanh(wa), wa).astype(BF16)
    logw = -DECAY_SCALE * _sigmoid(w0_ref[...] + _dot(twa, w2_ref[...]))
    a_sig = _sigmoid(a0_ref[...] + _dot(twa, a2_ref[...]))
    g = _dot(_sigmoid(gd).astype(BF16), g2_ref[...])
    gsum = gsum_ref[...]
    kk = kr * kk_ref[...]
    kk = kk * lax.rsqrt(jnp.maximum(_group_sum(kk * kk, gsum), 1e-24))
    kr2 = kr * (1.0 + (a_sig - 1.0) * ka_ref[...])
    a_vec = -kk
    b_vec = kk * a_sig
    bonus = _group_sum(r * kr2 * rk_ref[...], gsum) * vr
    if t_valid % tr != 0:
        row = lax.broadcasted_iota(jnp.int32, (tr, 1), 0) + ti * tr
        valid = row < t_valid
        logw = jnp.where(valid, logw, 0.0)
        a_vec = jnp.where(valid, a_vec, 0.0)
        b_vec = jnp.where(valid, b_vec, 0.0)
        kr2 = jnp.where(valid, kr2, 0.0)
        vr = jnp.where(valid, vr, 0.0)
        bonus = jnp.where(valid, bonus, 0.0)
    l1, l2 = _split2(logw)
    tri = tri_ref[...]
    cs = _dot(tri, l1) + _dot(tri, l2)
    e_pos = jnp.exp(cs)
    e_neg = jnp.exp(-cs)
    at_s[...] = a_vec * jnp.exp(cs - logw)
    rt_s[...] = r * e_pos
    bt_s[...] = b_vec * e_neg
    kt_s[...] = kr2 * e_neg
    v_s[...] = vr
    ep_s[...] = e_pos

    idx_r = lax.broadcasted_iota(jnp.int32, (PAIR, PAIR), 0)
    idx_c = lax.broadcasted_iota(jnp.int32, (PAIR, PAIR), 1)
    same = (idx_r // c_len) == (idx_c // c_len)
    strict = same & ((idx_r % c_len) > (idx_c % c_len))
    incl = same & ((idx_r % c_len) >= (idx_c % c_len))
    eye = idx_r == idx_c
    eye_f = jnp.where(eye, 1.0, 0.0).astype(F32)

    zero = jnp.zeros((PAIR, PAIR), F32)
    n_par = 2 if n_ch % 2 == 0 else 1
    n_grp = n_ch // n_par
    items = [(dc, p) for dc in range(n_par) for p in range(N_PAIR)]
    pairs = range(N_PAIR)

    def s1_load(g):
        sel = [(pl.ds(pl.multiple_of((g * n_par + dc) * c_len, c_len), c_len),
                slice(PAIR * p, PAIR * (p + 1))) for dc, p in items]
        return dict(at=[at_s[rw, cl] for rw, cl in sel], rt=[rt_s[rw, cl] for rw, cl in sel],
                    bt=[bt_s[rw, cl] for rw, cl in sel], kt=[kt_s[rw, cl] for rw, cl in sel],
                    v=[v_s[rw, cl] for rw, cl in sel], ep=[ep_s[rw, cl] for rw, cl in sel])

    def s1_compute(ld):
        q_all = range(len(items))
        at, rt = ld["at"], ld["rt"]
        yb = [_stack_heads(x) for x in ld["bt"]]
        yk = [_stack_heads(x) for x in ld["kt"]]
        vst = [_stack_heads(x).astype(BF16) for x in ld["v"]]
        pc = [x[c_len - 1:c_len] for x in ld["ep"]]
        gmat = [_dot_nt(jnp.concatenate([_stack_heads(at[q]), _stack_heads(rt[q])], axis=0).astype(BF16),
                        jnp.concatenate([yb[q], yk[q]], axis=0).astype(BF16)) for q in q_all]
        aab = [jnp.where(strict, gmat[q][0:PAIR, 0:PAIR], zero) for q in q_all]
        aak = [jnp.where(strict, gmat[q][0:PAIR, PAIR:], zero).astype(BF16) for q in q_all]
        lrb = [jnp.where(incl, gmat[q][PAIR:, 0:PAIR], zero).astype(BF16) for q in q_all]
        lrk = [jnp.where(incl, gmat[q][PAIR:, PAIR:], zero).astype(BF16) for q in q_all]
        tinv = [eye_f + aab[q] for q in q_all]
        lp = aab
        n = 1
        while 2 * n < c_len:
            lpb = [x.astype(BF16) for x in lp]
            lp = [_dot(x, x) for x in lpb]
            tinv = [tinv[q] + _dot(tinv[q].astype(BF16), lp[q].astype(BF16)) for q in q_all]
            n *= 2
        tb = [x.astype(BF16) for x in tinv]
        av = [_dot(aak[q], vst[q]).astype(BF16) for q in q_all]
        return dict(
            t=tb, tav=[_dot(tb[q], av[q]) for q in q_all], lrb=lrb,
            lrkv=[_dot(lrk[q], vst[q]) for q in q_all],
            zbt=[jnp.transpose(yb[q] * pc[q]).astype(BF16) for q in q_all],
            zkv=[_dot(jnp.transpose(yk[q] * pc[q]).astype(BF16), vst[q]) for q in q_all],
            ar=[jnp.concatenate([at[q], rt[q]], axis=0).astype(BF16) for q in q_all],
            pc=[jnp.sum(jnp.where(eye, jnp.broadcast_to(pc[q], (PAIR, PAIR)), zero), axis=-1, keepdims=True)
                for q in q_all])

    stage_bufs = dict(t=t_s, tav=tav_s, lrb=lrb_s, lrkv=lrkv_s, zbt=zbt_s, zkv=zkv_s, ar=ar_s, pc=pc_s)

    def s1_store(g, res):
        for q, (dc, p) in enumerate(items):
            for name, buf in stage_bufs.items():
                buf[g * n_par + dc, p] = res[name][q]

    def s2_load(g):
        return [{name: [buf[g * n_par + dc, p] for p in pairs] for name, buf in stage_bufs.items()}
                for dc in range(n_par)]

    def s2_compute(ld_group, hbd):
        ys = []
        for ld in ld_group:
            arh = [_dot(ld["ar"][p], hbd[p].astype(BF16)) for p in pairs]
            ub = [(_dot(ld["t"][p], _stack_heads(arh[p][0:c_len]).astype(BF16)) + ld["tav"][p]).astype(BF16)
                  for p in pairs]
            yst = [_dot(ld["lrb"][p], ub[p]) + ld["lrkv"][p] for p in pairs]
            ys.append([arh[p][c_len:] + yst[p][0:c_len] + yst[p][c_len:] for p in pairs])
            hbd = [ld["pc"][p] * hbd[p] + _dot(ld["zbt"][p], ub[p]) + ld["zkv"][p] for p in pairs]
        return ys, hbd

    def s2_store(g, ys, hbd):
        for dc in range(n_par):
            rows = pl.ds(pl.multiple_of((g * n_par + dc) * c_len, c_len), c_len)
            for p in pairs:
                y_s[rows, PAIR * p:PAIR * (p + 1)] = ys[dc][p]
        for p in pairs:
            h_st[p] = hbd[p]

    s1_store(0, s1_compute(s1_load(0)))

    def body(g, carry_):
        ld2 = s2_load(g)
        ld1 = s1_load(g + 1)
        hbd = [h_st[p] for p in pairs]
        res1 = s1_compute(ld1)
        ys, hbd = s2_compute(ld2, hbd)
        s2_store(g, ys, hbd)
        s1_store(g + 1, res1)
        return carry_

    lax.fori_loop(0, n_grp - 1, body, 0)
    ys_last, h_last = s2_compute(s2_load(n_grp - 1), [h_st[p] for p in pairs])
    s2_store(n_grp - 1, ys_last, h_last)

    @pl.when(ti == pl.num_programs(1) - 1)
    def _():
        first = lax.broadcasted_iota(jnp.int32, (PAIR, HEAD_DIM), 0) < HEAD_DIM
        for p in range(N_PAIR):
            s_t = jnp.transpose(h_st[p])
            sn_ref[0, p] = jnp.where(first, s_t[:, 0:HEAD_DIM], s_t[:, HEAD_DIM:])

    y = y_s[...]
    inv_n = 1.0 / HEAD_DIM
    mean = _group_sum(y, gsum) * inv_n
    d = y - mean
    var = _group_sum(d * d, gsum) * inv_n
    yn = d * lax.rsqrt(var + GN_EPS) * lw_ref[...] + lb_ref[...]
    o_ref[0] = ((yn + bonus) * g).astype(BF16)


def _rwkv(pr, shift0, s0, prm, tr, t_valid):
    b, t_pad, _ = pr.shape
    n_state = s0.shape[0]
    assert shift0.shape[0] == n_state
    last = n_state - 1
    assert t_pad % tr == 0 and tr % RW_CHUNK == 0
    n_ch = tr // RW_CHUNK
    tri = np.zeros((tr, tr), np.float32)
    for c in range(n_ch):
        tri[c * RW_CHUNK:(c + 1) * RW_CHUNK, c * RW_CHUNK:(c + 1) * RW_CHUNK] = np.tril(
            np.ones((RW_CHUNK, RW_CHUNK), np.float32))
    tri = jnp.asarray(tri, BF16)
    vec = _const_spec((1, RW_WIDTH))
    mat = lambda dt: pltpu.VMEM((n_ch, N_PAIR, PAIR, PAIR), dt)
    tile = lambda: pltpu.VMEM((tr, RW_WIDTH), F32)
    return pl.pallas_call(
        functools.partial(_rwkv_kernel, tr=tr, t_valid=t_valid, n_state=n_state),
        grid=(b, t_pad // tr),
        in_specs=[pl.BlockSpec((1, tr, RW_PROJ), lambda bi, ti: (bi, ti, 0)),
                  pl.BlockSpec((1, 1, RW_PROJ), lambda bi, ti: (jnp.minimum(bi, last), 0, 0)),
                  pl.BlockSpec((1, N_PAIR, PAIR, HEAD_DIM),
                               lambda bi, ti: (jnp.minimum(bi, last), 0, 0, 0)),
                  _const_spec((1, RW_PROJ)), vec, vec, vec, vec, vec, vec, vec,
                  _const_spec((W_LORA + A_LORA, RW_WIDTH)), _const_spec((W_LORA + A_LORA, RW_WIDTH)),
                  _const_spec((G_LORA, RW_WIDTH)), _const_spec((LANES, LANES)),
                  _const_spec((tr, tr))],
        out_specs=[pl.BlockSpec((1, tr, RW_WIDTH), lambda bi, ti: (bi, ti, 0)),
                   pl.BlockSpec((1, N_PAIR, PAIR, HEAD_DIM), lambda bi, ti: (bi, 0, 0, 0))],
        out_shape=[jax.ShapeDtypeStruct((b, t_pad, RW_WIDTH), BF16),
                   jax.ShapeDtypeStruct((b, N_PAIR, PAIR, HEAD_DIM), F32)],
        scratch_shapes=[pltpu.VMEM((N_PAIR, PAIR, PAIR), F32),
                        pltpu.VMEM((8 + tr, RW_PROJ), F32), pltpu.VMEM((1, RW_PROJ), F32),
                        tile(), tile(), tile(), tile(), tile(), tile(), tile(),
                        mat(BF16), mat(F32), mat(BF16), mat(F32), mat(BF16), mat(F32), mat(BF16),
                        pltpu.VMEM((n_ch, N_PAIR, PAIR, 1), F32)],
        compiler_params=_cparams(("arbitrary", "arbitrary")),
        name="rwkv",
    )(pr, shift0, s0, prm["mu"], prm["w0"], prm["a0"], prm["k_k"], prm["k_a"], prm["r_k"],
      prm["lnx_w"], prm["lnx_b"], prm["w2p"], prm["a2p"], prm["g2"], prm["gsum"], tri)


FF_COLS = 256


def _ffn_kernel(x_ref, oa_ref, orw_ref, c0_ref, wo_ref, g_ref, wu_ref, cw_ref, cb_ref, wd_ref,
                y_ref, cn_ref, zbuf, cbuf, h_ref, x1_ref, *, tm, ts, d_ff):
    off = zbuf.shape[1] - tm
    da = oa_ref.shape[1]

    @pl.when(pl.program_id(0) == 0)
    def _():
        cbuf[...] = c0_ref[...]

    x1 = x_ref[...] + _dot(oa_ref[...], wo_ref[0:da, :]) + _dot(orw_ref[...], wo_ref[da:, :])
    x1_ref[...] = x1
    ms = jnp.mean(x1 * x1, axis=-1, keepdims=True)
    h_ref[...] = (x1 * lax.rsqrt(ms + NORM_EPS) * g_ref[...]).astype(BF16)

    def up_proj(c0, zb):
        cols = slice(c0, c0 + FF_COLS)
        zb[off - 2 * ts:off, :] = cbuf[:, cols]
        zb[off:off + tm, :] = _dot(h_ref[...], wu_ref[:, cols])
        cbuf[:, cols] = zb[off + tm - 2 * ts:off + tm, :]

    def conv_cols(c0, zb):
        cols = slice(c0, c0 + FF_COLS)
        z2 = zb[off - 2 * ts:off - 2 * ts + tm, :]
        z1 = zb[off - ts:off - ts + tm, :]
        z = zb[off:off + tm, :]
        return (cb_ref[:, cols] + z2 * cw_ref[0:1, cols] + z1 * cw_ref[1:2, cols]
                + z * cw_ref[2:3, cols])

    def stage(c):
        up_proj(c * FF_COLS, zbuf.at[2 * (c % 2)])
        up_proj(d_ff + c * FF_COLS, zbuf.at[2 * (c % 2) + 1])

    n_chunks = d_ff // FF_COLS
    acc = None
    stage(0)
    for c in range(n_chunks):
        if c + 1 < n_chunks:
            stage(c + 1)
        gate = conv_cols(c * FF_COLS, zbuf.at[2 * (c % 2)])
        up = conv_cols(d_ff + c * FF_COLS, zbuf.at[2 * (c % 2) + 1])
        act = (gate * _sigmoid(gate) * up).astype(BF16)
        part = _dot(act, wd_ref[c * FF_COLS:(c + 1) * FF_COLS, :])
        acc = part if acc is None else acc + part

    y_ref[...] = x1_ref[...] + acc
    cn_ref[...] = cbuf[...]


def _ffn(x, oa, orw, conv0, prm, tm, ts):
    rows, d = x.shape
    d_ff = prm["w_down"].shape[0]
    assert rows % tm == 0 and d_ff % FF_COLS == 0 and (ts == 1 or ts % 8 == 0)
    off = -(-2 * ts // 8) * 8
    row = lambda w: pl.BlockSpec((tm, w), lambda i: (i, 0))
    return pl.pallas_call(
        functools.partial(_ffn_kernel, tm=tm, ts=ts, d_ff=d_ff),
        grid=(rows // tm,),
        in_specs=[row(d), row(DA_WIDTH), row(RW_WIDTH), _const_spec((2 * ts, 2 * d_ff)),
                  _resident_spec((DA_WIDTH + RW_WIDTH, d)), _const_spec((1, d)),
                  _resident_spec((d, 2 * d_ff)), _const_spec((CONV_W, 2 * d_ff)),
                  _const_spec((1, 2 * d_ff)), _resident_spec((d_ff, d))],
        out_specs=[row(d), _const_spec((2 * ts, 2 * d_ff))],
        out_shape=[jax.ShapeDtypeStruct((rows, d), F32),
                   jax.ShapeDtypeStruct((2 * ts, 2 * d_ff), F32)],
        scratch_shapes=[pltpu.VMEM((4, off + tm, FF_COLS), F32), pltpu.VMEM((2 * ts, 2 * d_ff), F32),
                        pltpu.VMEM((tm, d), BF16), pltpu.VMEM((tm, d), F32)],
        compiler_params=_cparams(("arbitrary",)),
        name="ffn",
    )(x, oa, orw, conv0, prm["w_out"], prm["ln2_g"], prm["w_up"], prm["conv_w"], prm["conv_b"],
      prm["w_down"])


def _rel_bucket(rel):
    nb = N_BUCKETS // 2
    max_exact = nb // 2
    bucket = jnp.where(rel > 0, nb, 0)
    n = jnp.abs(rel)
    nf = jnp.maximum(n, 1).astype(F32)
    large = max_exact + (jnp.log(nf / max_exact) / math.log(MAX_DISTANCE / max_exact)
                         * (nb - max_exact)).astype(jnp.int32)
    large = jnp.minimum(large, nb - 1)
    return bucket + jnp.where(n < max_exact, n, large)


def _bias_table(rel_bias, q_pos, k_pos, mask):
    n_q, n_k = len(q_pos), len(k_pos)
    assert np.all(np.diff(q_pos) == 1) and np.all(np.diff(k_pos) == 1)
    span = n_q + n_k - 1
    rel = jnp.asarray(int(k_pos[0]) - int(q_pos[0]) - (n_q - 1) + np.arange(span), jnp.int32)
    w = jnp.transpose(rel_bias[_rel_bucket(rel)]).astype(F32) * LOG2E
    return jnp.where(jnp.asarray(mask)[None], _toeplitz(w, n_q, n_k), NEG_INF)


def _skew(w, n_q, n_k):
    span = n_q + n_k - 1
    x = jnp.pad(w[:, :span], ((0, 0), (0, 1)))
    rows = jnp.tile(x, (1, n_q))[:, :n_q * span].reshape(w.shape[0], n_q, span)
    return rows[:, :, n_q - 1:n_q - 1 + n_k]


def _toeplitz(w, n_q, n_k):
    blk = LANES
    if n_q % blk or n_k % blk or n_q * n_k <= blk * blk:
        return _skew(w, n_q, n_k)
    nbq, nbk = n_q // blk, n_k // blk
    block = {d: _skew(w[:, (d + nbq - 1) * blk:(d + nbq + 1) * blk - 1], blk, blk)
             for d in range(-(nbq - 1), nbk)}
    return jnp.concatenate([jnp.concatenate([block[c - r] for c in range(nbk)], axis=2)
                            for r in range(nbq)], axis=1)


BOUND_MAX_SPREAD = 100.0


def _score_bound(q_g, k_g, rel_bias):
    s_max = (HEAD_DIM ** 0.5) * LOG2E * 1.01 * jnp.max(jnp.abs(q_g)) * jnp.max(jnp.abs(k_g))
    b_hi = jnp.max(rel_bias) * LOG2E
    b_lo = jnp.min(rel_bias) * LOG2E
    use = (2.0 * s_max + (b_hi - b_lo)) <= BOUND_MAX_SPREAD
    return jnp.stack([use.astype(F32), (s_max + b_hi).astype(F32)])


def _ext_chunk(pos):
    return np.where(pos < N_META, -1, (pos - N_META) // CHUNK)


def _prompt_bias(rel_bias):
    tq = ATT_T
    fr = np.arange(tq) + N_META
    meta = np.arange(N_META)
    causal = _ext_chunk(fr)[None, :] <= _ext_chunk(fr)[:, None]
    tr = lambda b: jnp.swapaxes(b, 1, 2)
    b0 = tr(_bias_table(rel_bias, fr, fr, causal))
    bm1 = tr(_bias_table(rel_bias, fr + tq, fr, np.ones((tq, tq), bool)))
    bq0 = tr(_bias_table(rel_bias, fr, meta, np.ones((tq, N_META), bool)))
    bmm = _bias_table(rel_bias, meta, meta, np.ones((N_META, N_META), bool))
    assert tq + 1 >= MAX_DISTANCE
    cfar = rel_bias[_rel_bucket(jnp.asarray(-(tq + 1), jnp.int32))].astype(F32) * LOG2E
    return cfar, b0, bm1, bq0, bmm


def _decode_bias(rel_bias, n_cache, nq):
    k_pos = np.arange(n_cache + nq)
    q_pos = k_pos[n_cache:]
    mask = _ext_chunk(k_pos)[None, :] <= _ext_chunk(q_pos)[:, None]
    return _bias_table(rel_bias, q_pos, k_pos, mask)


def _block_ones(n, blk, dtype):
    idx = np.arange(n) // blk
    return jnp.asarray((idx[:, None] == idx[None, :]).astype(np.float32), dtype)


def kernel(x_prompt, x_sample, cache_k, cache_v, state_rwkv, state_shift, state_conv, meta_tokens,
           rel_bias, ln1_g, w_in, q_norm_g, k_norm_g, lam_q1, lam_k1, lam_q2, lam_k2, subln_g,
           mu_shift, w0, w2, a0, a2, g2, k_k, k_a, r_k, lnx_w, lnx_b, w_out, ln2_g, w_up, conv_w,
           conv_b, w_down):
    bp, seq, d = x_prompt.shape
    db, dt, _ = x_sample.shape
    depth = w_in.shape[0]
    d_ff = w_down.shape[1]
    n_cache = cache_k.shape[2]
    assert bp == 1 and dt == N_META, "the meta stream rides with the decode streams"
    assert cache_k.shape[3] == DA_HEADS and cache_k.shape[4] == 2 * HEAD_DIM
    nb = db + 1
    nb_pad = -(-nb // 8) * 8

    cfar, b0, bm1, bq0, bmm = _prompt_bias(rel_bias)
    bias_dec = _decode_bias(rel_bias, n_cache, dt)
    gsum = _block_ones(LANES, HEAD_DIM, BF16)
    gmean = gsum * (1.0 / HEAD_DIM)
    zrow = lambda n: jnp.zeros((n, RW_WIDTH), BF16)

    x_f = x_prompt[0]
    x_s = jnp.concatenate([x_sample, meta_tokens.astype(x_sample.dtype)[None]], axis=0)
    outs = [[] for _ in range(10)]
    for l in range(depth):
        lam_init = 0.8 - 0.6 * math.exp(-0.3 * l)
        lam = (jnp.exp(jnp.sum(lam_q1[l].astype(F32) * lam_k1[l].astype(F32)))
               - jnp.exp(jnp.sum(lam_q2[l].astype(F32) * lam_k2[l].astype(F32))) + lam_init).reshape(1)
        out_scale = 1.0 - lam_init
        tile128 = lambda g_: jnp.tile(g_.reshape(1, -1), (1, DA_WIDTH // g_.shape[-1]))
        qg, kg = tile128(q_norm_g[l]), tile128(k_norm_g[l])
        sg = subln_g[l].reshape(1, LANES)
        w_in_bf = w_in[l].astype(BF16)
        rw = {
            "mu": mu_shift[l].reshape(1, -1), "w0": w0[l].reshape(1, -1), "a0": a0[l].reshape(1, -1),
            "k_k": k_k[l].reshape(1, -1), "k_a": k_a[l].reshape(1, -1), "r_k": r_k[l].reshape(1, -1),
            "lnx_w": lnx_w[l].reshape(1, -1), "lnx_b": lnx_b[l].reshape(1, -1),
            "w2p": jnp.concatenate([w2[l].astype(BF16), zrow(A_LORA)], axis=0),
            "a2p": jnp.concatenate([zrow(W_LORA), a2[l].astype(BF16)], axis=0),
            "g2": g2[l].astype(BF16), "gsum": gsum,
        }
        ff = {
            "w_out": w_out[l].astype(BF16), "ln2_g": ln2_g[l].reshape(1, -1),
            "w_up": w_up[l].astype(BF16), "conv_w": conv_w[l], "conv_b": conv_b[l].reshape(1, -1),
            "w_down": w_down[l].astype(BF16),
        }

        q_f, k_f, v_f, pr_f, kb_t, vt_t = _proj(x_f, ln1_g[l].reshape(1, -1), w_in_bf, qg, kg, gmean,
                                                 512, True)
        q_s, k_s, v_s, pr_s = _proj(x_s.reshape(nb * dt, d), ln1_g[l].reshape(1, -1), w_in_bf, qg, kg,
                                    gmean, nb * dt, False)
        m0 = db * dt

        kb_m = k_s[m0:].astype(BF16)
        vb_m = v_s[m0:].astype(BF16)
        o_f, o_m = _attn_prompt(cfar, lam, _score_bound(q_norm_g[l], k_norm_g[l], rel_bias), q_f, kb_t, vt_t, kb_m, vb_m, jnp.transpose(vb_m), q_s[m0:],
                                b0, bm1, bq0, bmm, sg, out_scale)
        o_d = _attn_decode(lam, q_s, cache_k, cache_v, l, k_s, v_s, bias_dec, sg, out_scale, dt)
        o_s = jnp.concatenate([o_d, o_m], axis=0)

        pr_s3 = pr_s.reshape(nb, dt, RW_PROJ)
        pr_pad = jnp.pad(pr_s3, ((0, 0), (0, RW_CHUNK - dt), (0, 0)))
        pair_view = (N_PAIR, PAIR, HEAD_DIM)
        orw_s, sn_s = _rwkv(pr_pad, state_shift[l][:, None, :], state_rwkv[l].reshape(db, *pair_view),
                            rw, RW_CHUNK, dt)
        orw_f, sn_f = _rwkv(pr_f[None], pr_s3[db:, dt - 1:dt, :], sn_s[db:], rw, 512, seq)

        def time_major(a):
            a = jnp.pad(a.reshape(nb, dt, -1), ((0, nb_pad - nb), (0, 0), (0, 0)))
            return jnp.swapaxes(a, 0, 1).reshape(dt * nb_pad, -1)

        conv_s = jnp.concatenate([state_conv[l], jnp.zeros((1, CONV_W - 1, 2 * d_ff), F32)], axis=0)
        conv_s = jnp.pad(conv_s, ((0, nb_pad - nb), (0, 0), (0, 0)))
        conv_s = jnp.swapaxes(conv_s, 0, 1).reshape(2 * nb_pad, 2 * d_ff)
        y_s, cn_s = _ffn(time_major(x_s), time_major(o_s), time_major(orw_s[:, :dt]), conv_s, ff,
                         dt * nb_pad, nb_pad)
        cn_s = jnp.swapaxes(cn_s.reshape(2, nb_pad, 2 * d_ff), 0, 1)
        y_f, cn_f = _ffn(x_f, o_f, orw_f[0], cn_s[db], ff, 512, 1)
        y_s = jnp.swapaxes(y_s.reshape(dt, nb_pad, d), 0, 1)[:nb]

        hw = (DA_HEADS, 2 * HEAD_DIM)
        outs[0].append(jnp.concatenate([k_s[m0:].reshape(N_META, *hw), k_f], axis=0)[None])
        outs[1].append(jnp.concatenate([v_s[m0:].reshape(N_META, *hw), v_f], axis=0)[None])
        outs[2].append(sn_f.reshape(bp, RW_HEADS, HEAD_DIM, HEAD_DIM))
        outs[3].append(pr_f[seq - 1:seq])
        outs[4].append(cn_f[None])
        outs[5].append(k_s[:m0].reshape(db, dt, *hw))
        outs[6].append(v_s[:m0].reshape(db, dt, *hw))
        outs[7].append(sn_s[:db].reshape(db, RW_HEADS, HEAD_DIM, HEAD_DIM))
        outs[8].append(pr_s3[:db, dt - 1])
        outs[9].append(cn_s[:db])
        x_f, x_s = y_f, y_s

    return (x_f[None], x_s[:db], *[jnp.stack(o) for o in outs])
```

```python
import functools
import math

import numpy as np
import jax
import jax.numpy as jnp
from jax import lax
from jax.experimental import pallas as pl
from jax.experimental.pallas import tpu as pltpu

F32 = jnp.float32
BF16 = jnp.bfloat16

CHUNK = 64
N_META = 16
HEAD_DIM = 64
DA_HEADS = 4
RW_HEADS = 8
W_LORA = 64
A_LORA = 64
G_LORA = 128
CONV_W = 3
N_BUCKETS = 32
MAX_DISTANCE = 128
NORM_EPS = 1e-6
GN_EPS = 64e-5
NEG_INF = -1e30
LOG2E = math.log2(math.e)
DECAY_SCALE = math.exp(-0.5)

DA_WIDTH = DA_HEADS * 2 * HEAD_DIM
RW_WIDTH = RW_HEADS * HEAD_DIM
RW_PROJ = 3 * RW_WIDTH + W_LORA + A_LORA + G_LORA
LANES = 128
PAIR = 2 * HEAD_DIM
N_PAIR = RW_WIDTH // PAIR
RW_CHUNK = 64
ATT_T = 512
VMEM_LIMIT = 56 * 1024 * 1024


def _dot(a, b):
    return jnp.dot(a, b, preferred_element_type=F32)


def _dot_nt(a, b):
    return lax.dot_general(a, b, (((1,), (1,)), ((), ())), preferred_element_type=F32)


def _group_sum(x, blk):
    xb = x.astype(BF16)
    return jnp.concatenate([_dot(xb[:, LANES * p:LANES * (p + 1)], blk)
                            for p in range(x.shape[1] // LANES)], axis=1)


def _cparams(sem):
    return pltpu.CompilerParams(dimension_semantics=sem, vmem_limit_bytes=VMEM_LIMIT)


def _const_spec(shape):
    nd = len(shape)
    return pl.BlockSpec(shape, lambda *_: (0,) * nd)


def _resident_spec(shape):
    nd = len(shape)
    return pl.BlockSpec(shape, lambda *_: (0,) * nd, pipeline_mode=pl.Buffered(1))


def _proj_kernel(x_ref, g_ref, w_ref, qg_ref, kg_ref, gm_ref,
                 q_ref, k_ref, v_ref, pr_ref, *tile_refs, tm):
    x = x_ref[...]
    ms = jnp.mean(x * x, axis=-1, keepdims=True)
    h = (x * lax.rsqrt(ms + NORM_EPS) * g_ref[...]).astype(BF16)
    gm = gm_ref[...]

    def group_norm(t, g):
        ms_g = _group_sum(t * t, gm)
        return t * lax.rsqrt(ms_g + NORM_EPS) * g

    q = _dot(h, w_ref[:, 0:DA_WIDTH])
    k = _dot(h, w_ref[:, DA_WIDTH:2 * DA_WIDTH])
    pr_ref[...] = _dot(h, w_ref[:, 3 * DA_WIDTH:])
    v = _dot(h, w_ref[:, 2 * DA_WIDTH:3 * DA_WIDTH])
    q_ref[...] = (group_norm(q, qg_ref[...]) * (HEAD_DIM ** -0.5 * LOG2E)).astype(BF16)
    k = group_norm(k, kg_ref[...])
    if len(k_ref.shape) == 3:
        for hd in range(DA_HEADS):
            k_ref[:, hd, :] = k[:, hd * LANES:(hd + 1) * LANES]
            v_ref[:, hd, :] = v[:, hd * LANES:(hd + 1) * LANES]
    else:
        k_ref[...] = k
        v_ref[...] = v
    if tile_refs:
        kb_ref, vt_ref = tile_refs
        kb = k.astype(BF16)
        vt = jnp.transpose(v).astype(BF16)
        for hd in range(DA_HEADS):
            for jj in range(tm // ATT_T):
                kb_ref[hd, jj] = kb[jj * ATT_T:(jj + 1) * ATT_T, hd * LANES:(hd + 1) * LANES]
                vt_ref[hd, jj] = vt[hd * LANES:(hd + 1) * LANES, jj * ATT_T:(jj + 1) * ATT_T]


def _proj(x, ln1_g, w_in_bf, qg, kg, gmean, tm, emit_tiles):
    rows, d = x.shape
    n_in = w_in_bf.shape[1]
    assert rows % tm == 0
    row = lambda w: pl.BlockSpec((tm, w), lambda i: (i, 0))
    out_specs = [row(DA_WIDTH), row(DA_WIDTH), row(DA_WIDTH), row(RW_PROJ)]
    out_shape = [jax.ShapeDtypeStruct((rows, DA_WIDTH), BF16),
                 jax.ShapeDtypeStruct((rows, DA_WIDTH), F32),
                 jax.ShapeDtypeStruct((rows, DA_WIDTH), F32),
                 jax.ShapeDtypeStruct((rows, RW_PROJ), F32)]
    if emit_tiles:
        assert tm % ATT_T == 0
        tpt = tm // ATT_T
        for o in (1, 2):
            out_specs[o] = pl.BlockSpec((tm, DA_HEADS, LANES), lambda i: (i, 0, 0))
            out_shape[o] = jax.ShapeDtypeStruct((rows, DA_HEADS, LANES), F32)
        out_specs += [pl.BlockSpec((DA_HEADS, tpt, ATT_T, LANES), lambda i: (0, i, 0, 0)),
                      pl.BlockSpec((DA_HEADS, tpt, LANES, ATT_T), lambda i: (0, i, 0, 0))]
        out_shape += [jax.ShapeDtypeStruct((DA_HEADS, rows // ATT_T, ATT_T, LANES), BF16),
                      jax.ShapeDtypeStruct((DA_HEADS, rows // ATT_T, LANES, ATT_T), BF16)]
    return pl.pallas_call(
        functools.partial(_proj_kernel, tm=tm),
        grid=(rows // tm,),
        in_specs=[row(d), _const_spec((1, d)), _resident_spec((d, n_in)),
                  _const_spec((1, DA_WIDTH)), _const_spec((1, DA_WIDTH)),
                  _const_spec((LANES, LANES))],
        out_specs=out_specs,
        out_shape=out_shape,
        compiler_params=_cparams(("arbitrary",)),
        name="proj",
    )(x, ln1_g, w_in_bf, qg, kg, gmean)


def _stack_components(q):
    lo = lax.broadcasted_iota(jnp.int32, q.shape, 1) < HEAD_DIM
    zero = jnp.zeros_like(q)
    return jnp.concatenate([jnp.where(lo, q, zero), jnp.where(lo, zero, q)], axis=0)


def _sub_layer_norm(o, g, out_scale):
    ms = jnp.mean(o * o, axis=-1, keepdims=True)
    return o * lax.rsqrt(ms + NORM_EPS) * g * out_scale


def _attn_prompt_kernel(cfar_ref, lam_ref, fix_ref, q_ref, k_ref, vt_ref, km_ref, vm_ref, vmt_ref, qm_ref,
                        b0_ref, bm1_ref, bq0_ref, bmm_ref, g_ref, o_ref, om_ref,
                        acc_ref, m_ref, l_ref, s_buf, p_buf, a_buf, *, out_scale):
    h = pl.program_id(0)
    i = pl.program_id(1)
    tq = ATT_T
    cf = cfar_ref[h]
    lam = lam_ref[0]
    qst = _stack_components(q_ref[...])

    def both(b):
        return jnp.concatenate([b, b], axis=1)

    s = _dot_nt(km_ref[...], qst)
    s = s + both(jnp.where(i == 0, bq0_ref[0], cf))
    use_bound = fix_ref[0] > 0.5
    bound = fix_ref[1]
    m0 = jnp.where(use_bound, bound, jnp.max(s, axis=0, keepdims=True))
    p = jnp.exp2(s - m0)
    m_ref[...] = m0
    l_ref[...] = jnp.sum(p, axis=0, keepdims=True)
    acc_ref[...] = _dot(vmt_ref[...], p.astype(BF16))

    def scores(idx, slot):
        s_buf[slot] = _dot_nt(k_ref[0, idx], qst)

    def softmax(slot, bias):
        s = s_buf[slot]
        m_prev = m_ref[...]
        if bias.ndim == 0:
            m_new = jnp.maximum(m_prev, jnp.max(s, axis=0, keepdims=True) + bias)
            p = jnp.exp2(s - (m_new - bias))
        else:
            s = s + both(bias)
            m_new = jnp.maximum(m_prev, jnp.max(s, axis=0, keepdims=True))
            p = jnp.exp2(s - m_new)
        alpha = jnp.exp2(m_prev - m_new)
        l_ref[...] = alpha * l_ref[...] + jnp.sum(p, axis=0, keepdims=True)
        m_ref[...] = m_new
        p_buf[slot] = p.astype(BF16)
        a_buf[slot] = alpha

    def accumulate(idx, slot):
        acc_ref[...] = a_buf[slot] * acc_ref[...] + _dot(vt_ref[0, idx], p_buf[slot])

    n_far = jnp.maximum(i - 1, 0)
    off = n_far % 2
    p_buf[1] = jnp.zeros(p_buf.shape[1:], BF16)
    a_buf[1] = jnp.ones(a_buf.shape[1:], F32)
    scores(0, 0)

    def far_body(t, carry):
        u = 2 * t - off
        scores(u + 1, 1)
        softmax(0, jnp.where(u >= 0, cf, NEG_INF))
        accumulate(jnp.maximum(u - 1, 0), 1)
        scores(u + 2, 0)
        softmax(1, cf)
        accumulate(jnp.maximum(u, 0), 0)
        return carry

    def bound_softmax(slot, bias):
        p = jnp.exp2(s_buf[slot] - (bound - bias))
        l_ref[...] += jnp.sum(p, axis=0, keepdims=True)
        p_buf[slot] = p.astype(BF16)

    def bound_accumulate(idx, slot):
        acc_ref[...] += _dot(vt_ref[0, idx], p_buf[slot])

    def far_body_bound(t, carry):
        u = 2 * t - off
        scores(u + 1, 1)
        bound_softmax(0, jnp.where(u >= 0, cf, NEG_INF))
        bound_accumulate(jnp.maximum(u - 1, 0), 1)
        scores(u + 2, 0)
        bound_softmax(1, cf)
        bound_accumulate(jnp.maximum(u, 0), 0)
        return carry

    @pl.when(use_bound)
    def _():
        lax.fori_loop(0, (n_far + 1) // 2, far_body_bound, 0)

    @pl.when(jnp.logical_not(use_bound))
    def _():
        lax.fori_loop(0, (n_far + 1) // 2, far_body, 0)

    scores(i, 1)
    softmax(0, jnp.where(i >= 1, bm1_ref[0], NEG_INF))
    accumulate(jnp.maximum(n_far - 1, 0), 1)
    softmax(1, b0_ref[0])
    accumulate(n_far, 0)
    accumulate(i, 1)

    accn = acc_ref[...] * (1.0 / l_ref[...])
    o_t = accn[:, 0:tq] - lam * accn[:, tq:]
    o_ref[...] = _sub_layer_norm(jnp.transpose(o_t), g_ref[...], out_scale).astype(BF16)

    @pl.when(i == 0)
    def _():
        qm = _stack_components(qm_ref[...])
        bmm = bmm_ref[0]
        sm = _dot_nt(qm, km_ref[...]) + jnp.concatenate([bmm, bmm], axis=0)
        mm = jnp.max(sm, axis=-1, keepdims=True)
        pm = jnp.exp2(sm - mm)
        accm = _dot(pm.astype(BF16), vm_ref[...]) / jnp.sum(pm, axis=-1, keepdims=True)
        om = accm[0:N_META] - lam * accm[N_META:]
        om_ref[...] = _sub_layer_norm(om, g_ref[...], out_scale).astype(BF16)


def _attn_prompt(cfar, lam, fix, q_f, kb_t, vt_t, kb_m, vb_m, vbt_m, q_m, b0, bm1, bq0, bmm, subln_g,
                 out_scale):
    tf = q_f.shape[0]
    tq = ATT_T
    assert tf % tq == 0
    n_t = tf // tq
    smem = pl.BlockSpec(memory_space=pltpu.SMEM)
    head_col = lambda rows: pl.BlockSpec((rows, LANES), lambda h, i: (0, h))
    head_tile = lambda a, b: pl.BlockSpec((1, a, b), lambda h, i: (h, 0, 0))
    return pl.pallas_call(
        functools.partial(_attn_prompt_kernel, out_scale=out_scale),
        grid=(DA_HEADS, n_t),
        in_specs=[smem, smem, smem,
                  pl.BlockSpec((tq, LANES), lambda h, i: (i, h)),
                  pl.BlockSpec((1, n_t, tq, LANES), lambda h, i: (h, 0, 0, 0)),
                  pl.BlockSpec((1, n_t, LANES, tq), lambda h, i: (h, 0, 0, 0)),
                  head_col(N_META), head_col(N_META),
                  pl.BlockSpec((LANES, N_META), lambda h, i: (h, 0)),
                  head_col(N_META),
                  head_tile(tq, tq), head_tile(tq, tq), head_tile(N_META, tq),
                  head_tile(N_META, N_META), _const_spec((1, LANES))],
        out_specs=[pl.BlockSpec((tq, LANES), lambda h, i: (i, h)), head_col(N_META)],
        out_shape=[jax.ShapeDtypeStruct((tf, DA_WIDTH), BF16),
                   jax.ShapeDtypeStruct((N_META, DA_WIDTH), BF16)],
        scratch_shapes=[pltpu.VMEM((LANES, 2 * tq), F32), pltpu.VMEM((1, 2 * tq), F32),
                        pltpu.VMEM((1, 2 * tq), F32), pltpu.VMEM((2, tq, 2 * tq), F32),
                        pltpu.VMEM((2, tq, 2 * tq), BF16), pltpu.VMEM((2, 1, 2 * tq), F32)],
        compiler_params=_cparams(("arbitrary", "arbitrary")),
        name="attn_prompt",
    )(cfar, lam, fix, q_f, kb_t, vt_t, kb_m, vb_m, vbt_m, q_m, b0, bm1, bq0, bmm, subln_g)


DEC_HEADS = 2


def _attn_decode_kernel(lam_ref, q_ref, k_hbm, v_hbm, kn_ref, vn_ref, b_ref, g_ref, o_ref,
                        kbuf, vbuf, sem, *, layer, n_main, out_scale):
    bi = pl.program_id(0)
    hg = pl.program_id(1)
    n_hg = pl.num_programs(1)
    step = bi * n_hg + hg
    slot = step % 2

    def cache_copies(b_, hg_, slot_):
        cps = []
        for hh in range(DEC_HEADS):
            h_ = hg_ * DEC_HEADS + hh
            cps.append(pltpu.make_async_copy(k_hbm.at[layer, b_, :, h_, :], kbuf.at[slot_, hh],
                                             sem.at[0, slot_, hh]))
            cps.append(pltpu.make_async_copy(v_hbm.at[layer, b_, :, h_, :], vbuf.at[slot_, hh],
                                             sem.at[1, slot_, hh]))
        return cps

    @pl.when(step == 0)
    def _():
        for cp in cache_copies(bi, hg, slot):
            cp.start()

    @pl.when(step + 1 < pl.num_programs(0) * n_hg)
    def _():
        nxt = step + 1
        for cp in cache_copies(nxt // n_hg, nxt % n_hg, 1 - slot):
            cp.start()

    for cp in cache_copies(bi, hg, slot):
        cp.wait()

    lam = lam_ref[0]
    nq = q_ref.shape[0]
    n_cache = kbuf.shape[2]
    for hh in range(DEC_HEADS):
        cols = slice(LANES * hh, LANES * (hh + 1))
        k_ref = kbuf.at[slot, hh]
        v_ref = vbuf.at[slot, hh]
        qst = _stack_components(q_ref[:, cols])
        k_main = k_ref[0:n_main, :].astype(BF16)
        v_main = v_ref[0:n_main, :].astype(BF16)
        k_tail = jnp.concatenate([k_ref[n_main:n_cache, :], kn_ref[:, cols]], axis=0).astype(BF16)
        v_tail = jnp.concatenate([v_ref[n_main:n_cache, :], vn_ref[:, cols]], axis=0).astype(BF16)
        bias = jnp.concatenate([b_ref[hh], b_ref[hh]], axis=0)
        s1 = _dot_nt(qst, k_main) + bias[:, 0:n_main]
        s2 = _dot_nt(qst, k_tail) + bias[:, n_main:]
        m = jnp.maximum(jnp.max(s1, axis=-1, keepdims=True), jnp.max(s2, axis=-1, keepdims=True))
        p1 = jnp.exp2(s1 - m)
        p2 = jnp.exp2(s2 - m)
        l = jnp.sum(p1, axis=-1, keepdims=True) + jnp.sum(p2, axis=-1, keepdims=True)
        res = (_dot(p1.astype(BF16), v_main) + _dot(p2.astype(BF16), v_tail)) / l
        o = res[0:nq] - lam * res[nq:]
        o_ref[:, cols] = _sub_layer_norm(o, g_ref[...], out_scale).astype(BF16)


def _attn_decode(lam, q_s, cache_k, cache_v, layer, k_s, v_s, bias, subln_g, out_scale, nq):
    _, b, n_cache, n_h, _ = cache_k.shape
    assert n_h % DEC_HEADS == 0
    n_main = (n_cache // LANES) * LANES
    n_keys = n_cache + nq
    smem = pl.BlockSpec(memory_space=pltpu.SMEM)
    hbm = pl.BlockSpec(memory_space=pl.ANY)
    new = pl.BlockSpec((nq, DEC_HEADS * LANES), lambda bi, hg: (bi, hg))
    buf = pltpu.VMEM((2, DEC_HEADS, n_cache, LANES), F32)
    return pl.pallas_call(
        functools.partial(_attn_decode_kernel, layer=layer, n_main=n_main, out_scale=out_scale),
        grid=(b, n_h // DEC_HEADS),
        in_specs=[smem, new, hbm, hbm, new, new,
                  pl.BlockSpec((DEC_HEADS, nq, n_keys), lambda bi, hg: (hg, 0, 0)),
                  _const_spec((1, LANES))],
        out_specs=new,
        out_shape=jax.ShapeDtypeStruct((b * nq, DA_WIDTH), BF16),
        scratch_shapes=[buf, buf, pltpu.SemaphoreType.DMA((2, 2, DEC_HEADS))],
        compiler_params=_cparams(("arbitrary", "arbitrary")),
        name="attn_decode",
    )(lam, q_s, cache_k, cache_v, k_s, v_s, bias, subln_g)


def _sigmoid(x):
    return 1.0 / (1.0 + jnp.exp(-x))


def _split2(x):
    x1 = x.astype(BF16)
    x2 = (x - x1.astype(F32)).astype(BF16)
    return x1, x2


def _stack_heads(x):
    lo = lax.broadcasted_iota(jnp.int32, x.shape, 1) < HEAD_DIM
    zero = jnp.zeros_like(x)
    return jnp.concatenate([jnp.where(lo, x, zero), jnp.where(lo, zero, x)], axis=0)


def _rwkv_kernel(pr_ref, sh0_ref, h0_ref, mu_ref, w0_ref, a0_ref, kk_ref, ka_ref, rk_ref,
                 lw_ref, lb_ref, w2_ref, a2_ref, g2_ref, gsum_ref, tri_ref,
                 o_ref, sn_ref,
                 h_st, xbuf, carry, at_s, rt_s, bt_s, kt_s, v_s, ep_s, y_s,
                 t_s, tav_s, lrb_s, lrkv_s, zbt_s, zkv_s, ar_s, pc_s,
                 *, tr, t_valid, n_state):
    c_len = RW_CHUNK
    n_ch = tr // c_len
    ti = pl.program_id(1)

    @pl.when(ti == 0)
    def _():
        has_state = pl.program_id(0) < n_state
        carry[...] = jnp.where(has_state, sh0_ref[0], 0.0)
        hd_r = lax.broadcasted_iota(jnp.int32, (PAIR, PAIR), 0) // HEAD_DIM
        hd_c = lax.broadcasted_iota(jnp.int32, (PAIR, PAIR), 1) // HEAD_DIM
        for p in range(N_PAIR):
            s_p = h0_ref[0, p]
            blk = jnp.where((hd_r == hd_c) & has_state, jnp.concatenate([s_p, s_p], axis=1), 0.0)
            h_st[p] = jnp.transpose(blk)

    x = pr_ref[0]
    xbuf[8:8 + tr, :] = x
    xbuf[7:8, :] = carry[...]
    prev = xbuf[7:7 + tr, :]
    carry[...] = pr_ref[0, tr - 1:tr, :]
    xm = x + (prev - x) * mu_ref[...]
    r = xm[:, 0:RW_WIDTH]
    kr = xm[:, RW_WIDTH:2 * RW_WIDTH]
    vr = xm[:, 2 * RW_WIDTH:3 * RW_WIDTH]
    wa = xm[:, 3 * RW_WIDTH:3 * RW_WIDTH + W_LORA + A_LORA]
    gd = xm[:, 3 * RW_WIDTH + W_LORA + A_LORA:]
    lane_wa = lax.broadcasted_iota(jnp.int32, wa.shape, 1)
    twa = jnp.where(lane_wa < W_LORA, jnp.tanh(wa), wa).astype(BF16)
    logw = -DECAY_SCALE * _sigmoid(w0_ref[...] + _dot(twa, w2_ref[...]))
    a_sig = _sigmoid(a0_ref[...] + _dot(twa, a2_ref[...]))
    g = _dot(_sigmoid(gd).astype(BF16), g2_ref[...])
    gsum = gsum_ref[...]
    kk = kr * kk_ref[...]
    kk = kk * lax.rsqrt(jnp.maximum(_group_sum(kk * kk, gsum), 1e-24))
    kr2 = kr * (1.0 + (a_sig - 1.0) * ka_ref[...])
    a_vec = -kk
    b_vec = kk * a_sig
    bonus = _group_sum(r * kr2 * rk_ref[...], gsum) * vr
    if t_valid % tr != 0:
        row = lax.broadcasted_iota(jnp.int32, (tr, 1), 0) + ti * tr
        valid = row < t_valid
        logw = jnp.where(valid, logw, 0.0)
        a_vec = jnp.where(valid, a_vec, 0.0)
        b_vec = jnp.where(valid, b_vec, 0.0)
        kr2 = jnp.where(valid, kr2, 0.0)
        vr = jnp.where(valid, vr, 0.0)
        bonus = jnp.where(valid, bonus, 0.0)
    l1, l2 = _split2(logw)
    tri = tri_ref[...]
    cs = _dot(tri, l1) + _dot(tri, l2)
    e_pos = jnp.exp(cs)
    e_neg = jnp.exp(-cs)
    at_s[...] = a_vec * jnp.exp(cs - logw)
    rt_s[...] = r * e_pos
    bt_s[...] = b_vec * e_neg
    kt_s[...] = kr2 * e_neg
    v_s[...] = vr
    ep_s[...] = e_pos

    idx_r = lax.broadcasted_iota(jnp.int32, (PAIR, PAIR), 0)
    idx_c = lax.broadcasted_iota(jnp.int32, (PAIR, PAIR), 1)
    same = (idx_r // c_len) == (idx_c // c_len)
    strict = same & ((idx_r % c_len) > (idx_c % c_len))
    incl = same & ((idx_r % c_len) >= (idx_c % c_len))
    eye = idx_r == idx_c
    eye_f = jnp.where(eye, 1.0, 0.0).astype(F32)

    zero = jnp.zeros((PAIR, PAIR), F32)
    n_par = 2 if n_ch % 2 == 0 else 1
    n_grp = n_ch // n_par
    items = [(dc, p) for dc in range(n_par) for p in range(N_PAIR)]
    pairs = range(N_PAIR)

    def s1_load(g):
        sel = [(pl.ds(pl.multiple_of((g * n_par + dc) * c_len, c_len), c_len),
                slice(PAIR * p, PAIR * (p + 1))) for dc, p in items]
        return dict(at=[at_s[rw, cl] for rw, cl in sel], rt=[rt_s[rw, cl] for rw, cl in sel],
                    bt=[bt_s[rw, cl] for rw, cl in sel], kt=[kt_s[rw, cl] for rw, cl in sel],
                    v=[v_s[rw, cl] for rw, cl in sel], ep=[ep_s[rw, cl] for rw, cl in sel])

    def s1_compute(ld):
        q_all = range(len(items))
        at, rt = ld["at"], ld["rt"]
        yb = [_stack_heads(x) for x in ld["bt"]]
        yk = [_stack_heads(x) for x in ld["kt"]]
        vst = [_stack_heads(x).astype(BF16) for x in ld["v"]]
        pc = [x[c_len - 1:c_len] for x in ld["ep"]]
        gmat = [_dot_nt(jnp.concatenate([_stack_heads(at[q]), _stack_heads(rt[q])], axis=0).astype(BF16),
                        jnp.concatenate([yb[q], yk[q]], axis=0).astype(BF16)) for q in q_all]
        aab = [jnp.where(strict, gmat[q][0:PAIR, 0:PAIR], zero) for q in q_all]
        aak = [jnp.where(strict, gmat[q][0:PAIR, PAIR:], zero).astype(BF16) for q in q_all]
        lrb = [jnp.where(incl, gmat[q][PAIR:, 0:PAIR], zero).astype(BF16) for q in q_all]
        lrk = [jnp.where(incl, gmat[q][PAIR:, PAIR:], zero).astype(BF16) for q in q_all]
        tinv = [eye_f + aab[q] for q in q_all]
        lp = aab
        n = 1
        while 2 * n < c_len:
            lpb = [x.astype(BF16) for x in lp]
            lp = [_dot(x, x) for x in lpb]
            tinv = [tinv[q] + _dot(tinv[q].astype(BF16), lp[q].astype(BF16)) for q in q_all]
            n *= 2
        tb = [x.astype(BF16) for x in tinv]
        av = [_dot(aak[q], vst[q]).astype(BF16) for q in q_all]
        return dict(
            t=tb, tav=[_dot(tb[q], av[q]) for q in q_all], lrb=lrb,
            lrkv=[_dot(lrk[q], vst[q]) for q in q_all],
            zbt=[jnp.transpose(yb[q] * pc[q]).astype(BF16) for q in q_all],
            zkv=[_dot(jnp.transpose(yk[q] * pc[q]).astype(BF16), vst[q]) for q in q_all],
            ar=[jnp.concatenate([at[q], rt[q]], axis=0).astype(BF16) for q in q_all],
            pc=[jnp.sum(jnp.where(eye, jnp.broadcast_to(pc[q], (PAIR, PAIR)), zero), axis=-1, keepdims=True)
                for q in q_all])

    stage_bufs = dict(t=t_s, tav=tav_s, lrb=lrb_s, lrkv=lrkv_s, zbt=zbt_s, zkv=zkv_s, ar=ar_s, pc=pc_s)

    def s1_store(g, res):
        for q, (dc, p) in enumerate(items):
            for name, buf in stage_bufs.items():
                buf[g * n_par + dc, p] = res[name][q]

    def s2_load(g):
        return [{name: [buf[g * n_par + dc, p] for p in pairs] for name, buf in stage_bufs.items()}
                for dc in range(n_par)]

    def s2_compute(ld_group, hbd):
        ys = []
        for ld in ld_group:
            arh = [_dot(ld["ar"][p], hbd[p].astype(BF16)) for p in pairs]
            ub = [(_dot(ld["t"][p], _stack_heads(arh[p][0:c_len]).astype(BF16)) + ld["tav"][p]).astype(BF16)
                  for p in pairs]
            yst = [_dot(ld["lrb"][p], ub[p]) + ld["lrkv"][p] for p in pairs]
            ys.append([arh[p][c_len:] + yst[p][0:c_len] + yst[p][c_len:] for p in pairs])
            hbd = [ld["pc"][p] * hbd[p] + _dot(ld["zbt"][p], ub[p]) + ld["zkv"][p] for p in pairs]
        return ys, hbd

    def s2_store(g, ys, hbd):
        for dc in range(n_par):
            rows = pl.ds(pl.multiple_of((g * n_par + dc) * c_len, c_len), c_len)
            for p in pairs:
                y_s[rows, PAIR * p:PAIR * (p + 1)] = ys[dc][p]
        for p in pairs:
            h_st[p] = hbd[p]

    s1_store(0, s1_compute(s1_load(0)))

    def body(g, carry_):
        ld2 = s2_load(g)
        ld1 = s1_load(g + 1)
        hbd = [h_st[p] for p in pairs]
        res1 = s1_compute(ld1)
        ys, hbd = s2_compute(ld2, hbd)
        s2_store(g, ys, hbd)
        s1_store(g + 1, res1)
        return carry_

    lax.fori_loop(0, n_grp - 1, body, 0)
    ys_last, h_last = s2_compute(s2_load(n_grp - 1), [h_st[p] for p in pairs])
    s2_store(n_grp - 1, ys_last, h_last)

    @pl.when(ti == pl.num_programs(1) - 1)
    def _():
        first = lax.broadcasted_iota(jnp.int32, (PAIR, HEAD_DIM), 0) < HEAD_DIM
        for p in range(N_PAIR):
            s_t = jnp.transpose(h_st[p])
            sn_ref[0, p] = jnp.where(first, s_t[:, 0:HEAD_DIM], s_t[:, HEAD_DIM:])

    y = y_s[...]
    inv_n = 1.0 / HEAD_DIM
    mean = _group_sum(y, gsum) * inv_n
    d = y - mean
    var = _group_sum(d * d, gsum) * inv_n
    yn = d * lax.rsqrt(var + GN_EPS) * lw_ref[...] + lb_ref[...]
    o_ref[0] = ((yn + bonus) * g).astype(BF16)


def _rwkv(pr, shift0, s0, prm, tr, t_valid):
    b, t_pad, _ = pr.shape
    n_state = s0.shape[0]
    assert shift0.shape[0] == n_state
    last = n_state - 1
    assert t_pad % tr == 0 and tr % RW_CHUNK == 0
    n_ch = tr // RW_CHUNK
    tri = np.zeros((tr, tr), np.float32)
    for c in range(n_ch):
        tri[c * RW_CHUNK:(c + 1) * RW_CHUNK, c * RW_CHUNK:(c + 1) * RW_CHUNK] = np.tril(
            np.ones((RW_CHUNK, RW_CHUNK), np.float32))
    tri = jnp.asarray(tri, BF16)
    vec = _const_spec((1, RW_WIDTH))
    mat = lambda dt: pltpu.VMEM((n_ch, N_PAIR, PAIR, PAIR), dt)
    tile = lambda: pltpu.VMEM((tr, RW_WIDTH), F32)
    return pl.pallas_call(
        functools.partial(_rwkv_kernel, tr=tr, t_valid=t_valid, n_state=n_state),
        grid=(b, t_pad // tr),
        in_specs=[pl.BlockSpec((1, tr, RW_PROJ), lambda bi, ti: (bi, ti, 0)),
                  pl.BlockSpec((1, 1, RW_PROJ), lambda bi, ti: (jnp.minimum(bi, last), 0, 0)),
                  pl.BlockSpec((1, N_PAIR, PAIR, HEAD_DIM),
                               lambda bi, ti: (jnp.minimum(bi, last), 0, 0, 0)),
                  _const_spec((1, RW_PROJ)), vec, vec, vec, vec, vec, vec, vec,
                  _const_spec((W_LORA + A_LORA, RW_WIDTH)), _const_spec((W_LORA + A_LORA, RW_WIDTH)),
                  _const_spec((G_LORA, RW_WIDTH)), _const_spec((LANES, LANES)),
                  _const_spec((tr, tr))],
        out_specs=[pl.BlockSpec((1, tr, RW_WIDTH), lambda bi, ti: (bi, ti, 0)),
                   pl.BlockSpec((1, N_PAIR, PAIR, HEAD_DIM), lambda bi, ti: (bi, 0, 0, 0))],
        out_shape=[jax.ShapeDtypeStruct((b, t_pad, RW_WIDTH), BF16),
                   jax.ShapeDtypeStruct((b, N_PAIR, PAIR, HEAD_DIM), F32)],
        scratch_shapes=[pltpu.VMEM((N_PAIR, PAIR, PAIR), F32),
                        pltpu.VMEM((8 + tr, RW_PROJ), F32), pltpu.VMEM((1, RW_PROJ), F32),
                        tile(), tile(), tile(), tile(), tile(), tile(), tile(),
                        mat(BF16), mat(F32), mat(BF16), mat(F32), mat(BF16), mat(F32), mat(BF16),
                        pltpu.VMEM((n_ch, N_PAIR, PAIR, 1), F32)],
        compiler_params=_cparams(("arbitrary", "arbitrary")),
        name="rwkv",
    )(pr, shift0, s0, prm["mu"], prm["w0"], prm["a0"], prm["k_k"], prm["k_a"], prm["r_k"],
      prm["lnx_w"], prm["lnx_b"], prm["w2p"], prm["a2p"], prm["g2"], prm["gsum"], tri)


FF_COLS = 256


def _ffn_kernel(x_ref, oa_ref, orw_ref, c0_ref, wo_ref, g_ref, wu_ref, cw_ref, cb_ref, wd_ref,
                y_ref, cn_ref, zbuf, cbuf, h_ref, x1_ref, *, tm, ts, d_ff):
    off = zbuf.shape[1] - tm
    da = oa_ref.shape[1]

    @pl.when(pl.program_id(0) == 0)
    def _():
        cbuf[...] = c0_ref[...]

    x1 = x_ref[...] + _dot(oa_ref[...], wo_ref[0:da, :]) + _dot(orw_ref[...], wo_ref[da:, :])
    x1_ref[...] = x1
    ms = jnp.mean(x1 * x1, axis=-1, keepdims=True)
    h_ref[...] = (x1 * lax.rsqrt(ms + NORM_EPS) * g_ref[...]).astype(BF16)

    def up_proj(c0, zb):
        cols = slice(c0, c0 + FF_COLS)
        zb[off - 2 * ts:off, :] = cbuf[:, cols]
        zb[off:off + tm, :] = _dot(h_ref[...], wu_ref[:, cols])
        cbuf[:, cols] = zb[off + tm - 2 * ts:off + tm, :]

    def conv_cols(c0, zb):
        cols = slice(c0, c0 + FF_COLS)
        z2 = zb[off - 2 * ts:off - 2 * ts + tm, :]
        z1 = zb[off - ts:off - ts + tm, :]
        z = zb[off:off + tm, :]
        return (cb_ref[:, cols] + z2 * cw_ref[0:1, cols] + z1 * cw_ref[1:2, cols]
                + z * cw_ref[2:3, cols])

    def stage(c):
        up_proj(c * FF_COLS, zbuf.at[2 * (c % 2)])
        up_proj(d_ff + c * FF_COLS, zbuf.at[2 * (c % 2) + 1])

    n_chunks = d_ff // FF_COLS
    acc = None
    stage(0)
    for c in range(n_chunks):
        if c + 1 < n_chunks:
            stage(c + 1)
        gate = conv_cols(c * FF_COLS, zbuf.at[2 * (c % 2)])
        up = conv_cols(d_ff + c * FF_COLS, zbuf.at[2 * (c % 2) + 1])
        act = (gate * _sigmoid(gate) * up).astype(BF16)
        part = _dot(act, wd_ref[c * FF_COLS:(c + 1) * FF_COLS, :])
        acc = part if acc is None else acc + part

    y_ref[...] = x1_ref[...] + acc
    cn_ref[...] = cbuf[...]


def _ffn(x, oa, orw, conv0, prm, tm, ts):
    rows, d = x.shape
    d_ff = prm["w_down"].shape[0]
    assert rows % tm == 0 and d_ff % FF_COLS == 0 and (ts == 1 or ts % 8 == 0)
    off = -(-2 * ts // 8) * 8
    row = lambda w: pl.BlockSpec((tm, w), lambda i: (i, 0))
    return pl.pallas_call(
        functools.partial(_ffn_kernel, tm=tm, ts=ts, d_ff=d_ff),
        grid=(rows // tm,),
        in_specs=[row(d), row(DA_WIDTH), row(RW_WIDTH), _const_spec((2 * ts, 2 * d_ff)),
                  _resident_spec((DA_WIDTH + RW_WIDTH, d)), _const_spec((1, d)),
                  _resident_spec((d, 2 * d_ff)), _const_spec((CONV_W, 2 * d_ff)),
                  _const_spec((1, 2 * d_ff)), _resident_spec((d_ff, d))],
        out_specs=[row(d), _const_spec((2 * ts, 2 * d_ff))],
        out_shape=[jax.ShapeDtypeStruct((rows, d), F32),
                   jax.ShapeDtypeStruct((2 * ts, 2 * d_ff), F32)],
        scratch_shapes=[pltpu.VMEM((4, off + tm, FF_COLS), F32), pltpu.VMEM((2 * ts, 2 * d_ff), F32),
                        pltpu.VMEM((tm, d), BF16), pltpu.VMEM((tm, d), F32)],
        compiler_params=_cparams(("arbitrary",)),
        name="ffn",
    )(x, oa, orw, conv0, prm["w_out"], prm["ln2_g"], prm["w_up"], prm["conv_w"], prm["conv_b"],
      prm["w_down"])


def _rel_bucket(rel):
    nb = N_BUCKETS // 2
    max_exact = nb // 2
    bucket = jnp.where(rel > 0, nb, 0)
    n = jnp.abs(rel)
    nf = jnp.maximum(n, 1).astype(F32)
    large = max_exact + (jnp.log(nf / max_exact) / math.log(MAX_DISTANCE / max_exact)
                         * (nb - max_exact)).astype(jnp.int32)
    large = jnp.minimum(large, nb - 1)
    return bucket + jnp.where(n < max_exact, n, large)


def _bias_table(rel_bias, q_pos, k_pos, mask):
    n_q, n_k = len(q_pos), len(k_pos)
    assert np.all(np.diff(q_pos) == 1) and np.all(np.diff(k_pos) == 1)
    span = n_q + n_k - 1
    rel = jnp.asarray(int(k_pos[0]) - int(q_pos[0]) - (n_q - 1) + np.arange(span), jnp.int32)
    w = jnp.transpose(rel_bias[_rel_bucket(rel)]).astype(F32) * LOG2E
    return jnp.where(jnp.asarray(mask)[None], _toeplitz(w, n_q, n_k), NEG_INF)


def _skew(w, n_q, n_k):
    span = n_q + n_k - 1
    x = jnp.pad(w[:, :span], ((0, 0), (0, 1)))
    rows = jnp.tile(x, (1, n_q))[:, :n_q * span].reshape(w.shape[0], n_q, span)
    return rows[:, :, n_q - 1:n_q - 1 + n_k]


def _toeplitz(w, n_q, n_k):
    blk = LANES
    if n_q % blk or n_k % blk or n_q * n_k <= blk * blk:
        return _skew(w, n_q, n_k)
    nbq, nbk = n_q // blk, n_k // blk
    n_d = nbq + nbk - 1
    n_h = w.shape[0]
    wp = jnp.pad(w[:, :n_q + n_k - 1], ((0, 0), (0, 1)))
    seg = jnp.concatenate([wp[:, :n_d * blk].reshape(n_h, n_d, blk),
                           wp[:, blk:].reshape(n_h, n_d, blk)[..., :blk - 1]], axis=-1)
    x = jnp.pad(seg, ((0, 0), (0, 0), (0, 1)))
    rows = jnp.tile(x, (1, 1, blk))[..., :blk * (2 * blk - 1)].reshape(n_h, n_d, blk, 2 * blk - 1)
    blocks = rows[..., blk - 1:]
    return jnp.concatenate([jnp.concatenate([blocks[:, c - r + nbq - 1] for c in range(nbk)], axis=2)
                            for r in range(nbq)], axis=1)


BOUND_MAX_SPREAD = 100.0


def _score_bound(q_g, k_g, rel_bias):
    s_max = (HEAD_DIM ** 0.5) * LOG2E * 1.01 * jnp.max(jnp.abs(q_g)) * jnp.max(jnp.abs(k_g))
    b_hi = jnp.max(rel_bias) * LOG2E
    b_lo = jnp.min(rel_bias) * LOG2E
    use = (2.0 * s_max + (b_hi - b_lo)) <= BOUND_MAX_SPREAD
    return jnp.stack([use.astype(F32), (s_max + b_hi).astype(F32)])


def _ext_chunk(pos):
    return np.where(pos < N_META, -1, (pos - N_META) // CHUNK)


def _prompt_bias(rel_bias):
    tq = ATT_T
    fr = np.arange(tq) + N_META
    meta = np.arange(N_META)
    causal = _ext_chunk(fr)[None, :] <= _ext_chunk(fr)[:, None]
    tr = lambda b: jnp.swapaxes(b, 1, 2)
    b0 = tr(_bias_table(rel_bias, fr, fr, causal))
    bm1 = tr(_bias_table(rel_bias, fr + tq, fr, np.ones((tq, tq), bool)))
    bq0 = tr(_bias_table(rel_bias, fr, meta, np.ones((tq, N_META), bool)))
    bmm = _bias_table(rel_bias, meta, meta, np.ones((N_META, N_META), bool))
    assert tq + 1 >= MAX_DISTANCE
    cfar = rel_bias[_rel_bucket(jnp.asarray(-(tq + 1), jnp.int32))].astype(F32) * LOG2E
    return cfar, b0, bm1, bq0, bmm


def _decode_bias(rel_bias, n_cache, nq):
    k_pos = np.arange(n_cache + nq)
    q_pos = k_pos[n_cache:]
    mask = _ext_chunk(k_pos)[None, :] <= _ext_chunk(q_pos)[:, None]
    return _bias_table(rel_bias, q_pos, k_pos, mask)


def _block_ones(n, blk, dtype):
    idx = np.arange(n) // blk
    return jnp.asarray((idx[:, None] == idx[None, :]).astype(np.float32), dtype)


def kernel(x_prompt, x_sample, cache_k, cache_v, state_rwkv, state_shift, state_conv, meta_tokens,
           rel_bias, ln1_g, w_in, q_norm_g, k_norm_g, lam_q1, lam_k1, lam_q2, lam_k2, subln_g,
           mu_shift, w0, w2, a0, a2, g2, k_k, k_a, r_k, lnx_w, lnx_b, w_out, ln2_g, w_up, conv_w,
           conv_b, w_down):
    bp, seq, d = x_prompt.shape
    db, dt, _ = x_sample.shape
    depth = w_in.shape[0]
    d_ff = w_down.shape[1]
    n_cache = cache_k.shape[2]
    assert bp == 1 and dt == N_META, "the meta stream rides with the decode streams"
    assert cache_k.shape[3] == DA_HEADS and cache_k.shape[4] == 2 * HEAD_DIM
    nb = db + 1
    nb_pad = -(-nb // 8) * 8

    cfar, b0, bm1, bq0, bmm = _prompt_bias(rel_bias)
    bias_dec = _decode_bias(rel_bias, n_cache, dt)
    gsum = _block_ones(LANES, HEAD_DIM, BF16)
    gmean = gsum * (1.0 / HEAD_DIM)
    zrow = lambda n: jnp.zeros((n, RW_WIDTH), BF16)

    x_f = x_prompt[0]
    x_s = jnp.concatenate([x_sample, meta_tokens.astype(x_sample.dtype)[None]], axis=0)
    outs = [[] for _ in range(10)]
    for l in range(depth):
        lam_init = 0.8 - 0.6 * math.exp(-0.3 * l)
        lam = (jnp.exp(jnp.sum(lam_q1[l].astype(F32) * lam_k1[l].astype(F32)))
               - jnp.exp(jnp.sum(lam_q2[l].astype(F32) * lam_k2[l].astype(F32))) + lam_init).reshape(1)
        out_scale = 1.0 - lam_init
        tile128 = lambda g_: jnp.tile(g_.reshape(1, -1), (1, DA_WIDTH // g_.shape[-1]))
        qg, kg = tile128(q_norm_g[l]), tile128(k_norm_g[l])
        sg = subln_g[l].reshape(1, LANES)
        w_in_bf = w_in[l].astype(BF16)
        rw = {
            "mu": mu_shift[l].reshape(1, -1), "w0": w0[l].reshape(1, -1), "a0": a0[l].reshape(1, -1),
            "k_k": k_k[l].reshape(1, -1), "k_a": k_a[l].reshape(1, -1), "r_k": r_k[l].reshape(1, -1),
            "lnx_w": lnx_w[l].reshape(1, -1), "lnx_b": lnx_b[l].reshape(1, -1),
            "w2p": jnp.concatenate([w2[l].astype(BF16), zrow(A_LORA)], axis=0),
            "a2p": jnp.concatenate([zrow(W_LORA), a2[l].astype(BF16)], axis=0),
            "g2": g2[l].astype(BF16), "gsum": gsum,
        }
        ff = {
            "w_out": w_out[l].astype(BF16), "ln2_g": ln2_g[l].reshape(1, -1),
            "w_up": w_up[l].astype(BF16), "conv_w": conv_w[l], "conv_b": conv_b[l].reshape(1, -1),
            "w_down": w_down[l].astype(BF16),
        }

        q_f, k_f, v_f, pr_f, kb_t, vt_t = _proj(x_f, ln1_g[l].reshape(1, -1), w_in_bf, qg, kg, gmean,
                                                 512, True)
        q_s, k_s, v_s, pr_s = _proj(x_s.reshape(nb * dt, d), ln1_g[l].reshape(1, -1), w_in_bf, qg, kg,
                                    gmean, nb * dt, False)
        m0 = db * dt

        kb_m = k_s[m0:].astype(BF16)
        vb_m = v_s[m0:].astype(BF16)
        o_f, o_m = _attn_prompt(cfar, lam, _score_bound(q_norm_g[l], k_norm_g[l], rel_bias), q_f, kb_t, vt_t, kb_m, vb_m, jnp.transpose(vb_m), q_s[m0:],
                                b0, bm1, bq0, bmm, sg, out_scale)
        o_d = _attn_decode(lam, q_s, cache_k, cache_v, l, k_s, v_s, bias_dec, sg, out_scale, dt)
        o_s = jnp.concatenate([o_d, o_m], axis=0)

        pr_s3 = pr_s.reshape(nb, dt, RW_PROJ)
        pr_pad = jnp.pad(pr_s3, ((0, 0), (0, RW_CHUNK - dt), (0, 0)))
        pair_view = (N_PAIR, PAIR, HEAD_DIM)
        orw_s, sn_s = _rwkv(pr_pad, state_shift[l][:, None, :], state_rwkv[l].reshape(db, *pair_view),
                            rw, RW_CHUNK, dt)
        orw_f, sn_f = _rwkv(pr_f[None], pr_s3[db:, dt - 1:dt, :], sn_s[db:], rw, 512, seq)

        def time_major(a):
            a = jnp.pad(a.reshape(nb, dt, -1), ((0, nb_pad - nb), (0, 0), (0, 0)))
            return jnp.swapaxes(a, 0, 1).reshape(dt * nb_pad, -1)

        conv_s = jnp.concatenate([state_conv[l], jnp.zeros((1, CONV_W - 1, 2 * d_ff), F32)], axis=0)
        conv_s = jnp.pad(conv_s, ((0, nb_pad - nb), (0, 0), (0, 0)))
        conv_s = jnp.swapaxes(conv_s, 0, 1).reshape(2 * nb_pad, 2 * d_ff)
        y_s, cn_s = _ffn(time_major(x_s), time_major(o_s), time_major(orw_s[:, :dt]), conv_s, ff,
                         dt * nb_pad, nb_pad)
        cn_s = jnp.swapaxes(cn_s.reshape(2, nb_pad, 2 * d_ff), 0, 1)
        y_f, cn_f = _ffn(x_f, o_f, orw_f[0], cn_s[db], ff, 512, 1)
        y_s = jnp.swapaxes(y_s.reshape(dt, nb_pad, d), 0, 1)[:nb]

        hw = (DA_HEADS, 2 * HEAD_DIM)
        outs[0].append(jnp.concatenate([k_s[m0:].reshape(N_META, *hw), k_f], axis=0)[None])
        outs[1].append(jnp.concatenate([v_s[m0:].reshape(N_META, *hw), v_f], axis=0)[None])
        outs[2].append(sn_f.reshape(bp, RW_HEADS, HEAD_DIM, HEAD_DIM))
        outs[3].append(pr_f[seq - 1:seq])
        outs[4].append(cn_f[None])
        outs[5].append(k_s[:m0].reshape(db, dt, *hw))
        outs[6].append(v_s[:m0].reshape(db, dt, *hw))
        outs[7].append(sn_s[:db].reshape(db, RW_HEADS, HEAD_DIM, HEAD_DIM))
        outs[8].append(pr_s3[:db, dt - 1])
        outs[9].append(cn_s[:db])
        x_f, x_s = y_f, y_s

    return (x_f[None], x_s[:db], *[jnp.stack(o) for o in outs])
```

```python
import functools
import math

import numpy as np
import jax
import jax.numpy as jnp
from jax import lax
from jax.experimental import pallas as pl
from jax.experimental.pallas import tpu as pltpu

F32 = jnp.float32
BF16 = jnp.bfloat16

CHUNK = 64
N_META = 16
HEAD_DIM = 64
DA_HEADS = 4
RW_HEADS = 8
W_LORA = 64
A_LORA = 64
G_LORA = 128
CONV_W = 3
N_BUCKETS = 32
MAX_DISTANCE = 128
NORM_EPS = 1e-6
GN_EPS = 64e-5
NEG_INF = -1e30
LOG2E = math.log2(math.e)
DECAY_SCALE = math.exp(-0.5)

DA_WIDTH = DA_HEADS * 2 * HEAD_DIM
RW_WIDTH = RW_HEADS * HEAD_DIM
RW_PROJ = 3 * RW_WIDTH + W_LORA + A_LORA + G_LORA
LANES = 128
PAIR = 2 * HEAD_DIM
N_PAIR = RW_WIDTH // PAIR
RW_CHUNK = 64
ATT_T = 512
VMEM_LIMIT = 56 * 1024 * 1024


def _dot(a, b):
    return jnp.dot(a, b, preferred_element_type=F32)


def _dot_nt(a, b):
    return lax.dot_general(a, b, (((1,), (1,)), ((), ())), preferred_element_type=F32)


def _group_sum(x, blk):
    xb = x.astype(BF16)
    return jnp.concatenate([_dot(xb[:, LANES * p:LANES * (p + 1)], blk)
                            for p in range(x.shape[1] // LANES)], axis=1)


def _cparams(sem):
    return pltpu.CompilerParams(dimension_semantics=sem, vmem_limit_bytes=VMEM_LIMIT)


def _const_spec(shape):
    nd = len(shape)
    return pl.BlockSpec(shape, lambda *_: (0,) * nd)


def _resident_spec(shape):
    nd = len(shape)
    return pl.BlockSpec(shape, lambda *_: (0,) * nd, pipeline_mode=pl.Buffered(1))


def _proj_kernel(x_ref, g_ref, w_ref, qg_ref, kg_ref, gm_ref,
                 q_ref, k_ref, v_ref, pr_ref, *tile_refs, tm):
    x = x_ref[...]
    ms = jnp.mean(x * x, axis=-1, keepdims=True)
    h = (x * lax.rsqrt(ms + NORM_EPS) * g_ref[...]).astype(BF16)
    gm = gm_ref[...]

    def group_norm(t, g):
        ms_g = _group_sum(t * t, gm)
        return t * lax.rsqrt(ms_g + NORM_EPS) * g

    q = _dot(h, w_ref[:, 0:DA_WIDTH])
    k = _dot(h, w_ref[:, DA_WIDTH:2 * DA_WIDTH])
    pr_ref[...] = _dot(h, w_ref[:, 3 * DA_WIDTH:])
    v = _dot(h, w_ref[:, 2 * DA_WIDTH:3 * DA_WIDTH])
    q_ref[...] = (group_norm(q, qg_ref[...]) * (HEAD_DIM ** -0.5 * LOG2E)).astype(BF16)
    k = group_norm(k, kg_ref[...])
    if len(k_ref.shape) == 3:
        for hd in range(DA_HEADS):
            k_ref[:, hd, :] = k[:, hd * LANES:(hd + 1) * LANES]
            v_ref[:, hd, :] = v[:, hd * LANES:(hd + 1) * LANES]
    else:
        k_ref[...] = k
        v_ref[...] = v
    if tile_refs:
        kb_ref, vt_ref = tile_refs
        kb = k.astype(BF16)
        vt = jnp.transpose(v).astype(BF16)
        for hd in range(DA_HEADS):
            for jj in range(tm // ATT_T):
                kb_ref[hd, jj] = kb[jj * ATT_T:(jj + 1) * ATT_T, hd * LANES:(hd + 1) * LANES]
                vt_ref[hd, jj] = vt[hd * LANES:(hd + 1) * LANES, jj * ATT_T:(jj + 1) * ATT_T]


def _proj(x, ln1_g, w_in_bf, qg, kg, gmean, tm, emit_tiles):
    rows, d = x.shape
    n_in = w_in_bf.shape[1]
    assert rows % tm == 0
    row = lambda w: pl.BlockSpec((tm, w), lambda i: (i, 0))
    out_specs = [row(DA_WIDTH), row(DA_WIDTH), row(DA_WIDTH), row(RW_PROJ)]
    out_shape = [jax.ShapeDtypeStruct((rows, DA_WIDTH), BF16),
                 jax.ShapeDtypeStruct((rows, DA_WIDTH), F32),
                 jax.ShapeDtypeStruct((rows, DA_WIDTH), F32),
                 jax.ShapeDtypeStruct((rows, RW_PROJ), F32)]
    if emit_tiles:
        assert tm % ATT_T == 0
        tpt = tm // ATT_T
        for o in (1, 2):
            out_specs[o] = pl.BlockSpec((tm, DA_HEADS, LANES), lambda i: (i, 0, 0))
            out_shape[o] = jax.ShapeDtypeStruct((rows, DA_HEADS, LANES), F32)
        out_specs += [pl.BlockSpec((DA_HEADS, tpt, ATT_T, LANES), lambda i: (0, i, 0, 0)),
                      pl.BlockSpec((DA_HEADS, tpt, LANES, ATT_T), lambda i: (0, i, 0, 0))]
        out_shape += [jax.ShapeDtypeStruct((DA_HEADS, rows // ATT_T, ATT_T, LANES), BF16),
                      jax.ShapeDtypeStruct((DA_HEADS, rows // ATT_T, LANES, ATT_T), BF16)]
    return pl.pallas_call(
        functools.partial(_proj_kernel, tm=tm),
        grid=(rows // tm,),
        in_specs=[row(d), _const_spec((1, d)), _resident_spec((d, n_in)),
                  _const_spec((1, DA_WIDTH)), _const_spec((1, DA_WIDTH)),
                  _const_spec((LANES, LANES))],
        out_specs=out_specs,
        out_shape=out_shape,
        compiler_params=_cparams(("arbitrary",)),
        name="proj",
    )(x, ln1_g, w_in_bf, qg, kg, gmean)


def _stack_components(q):
    lo = lax.broadcasted_iota(jnp.int32, q.shape, 1) < HEAD_DIM
    zero = jnp.zeros_like(q)
    return jnp.concatenate([jnp.where(lo, q, zero), jnp.where(lo, zero, q)], axis=0)


def _sub_layer_norm(o, g, out_scale):
    ms = jnp.mean(o * o, axis=-1, keepdims=True)
    return o * lax.rsqrt(ms + NORM_EPS) * g * out_scale


def _attn_prompt_kernel(cfar_ref, lam_ref, fix_ref, q_ref, k_ref, vt_ref, km_ref, vm_ref, vmt_ref, qm_ref,
                        b0_ref, bm1_ref, bq0_ref, bmm_ref, g_ref, o_ref, om_ref,
                        acc_ref, m_ref, l_ref, s_buf, p_buf, a_buf, *, out_scale):
    h = pl.program_id(0)
    i = pl.program_id(1)
    tq = ATT_T
    cf = cfar_ref[h]
    lam = lam_ref[0]
    qst = _stack_components(q_ref[...])

    def both(b):
        return jnp.concatenate([b, b], axis=1)

    s = _dot_nt(km_ref[...], qst)
    s = s + both(jnp.where(i == 0, bq0_ref[0], cf))
    use_bound = fix_ref[0] > 0.5
    bound = fix_ref[1]
    m0 = jnp.where(use_bound, bound, jnp.max(s, axis=0, keepdims=True))
    p = jnp.exp2(s - m0)
    m_ref[...] = m0
    l_ref[...] = jnp.sum(p, axis=0, keepdims=True)
    acc_ref[...] = _dot(vmt_ref[...], p.astype(BF16))

    def scores(idx, slot):
        s_buf[slot] = _dot_nt(k_ref[0, idx], qst)

    def softmax(slot, bias):
        s = s_buf[slot]
        m_prev = m_ref[...]
        if bias.ndim == 0:
            m_new = jnp.maximum(m_prev, jnp.max(s, axis=0, keepdims=True) + bias)
            p = jnp.exp2(s - (m_new - bias))
        else:
            s = s + both(bias)
            m_new = jnp.maximum(m_prev, jnp.max(s, axis=0, keepdims=True))
            p = jnp.exp2(s - m_new)
        alpha = jnp.exp2(m_prev - m_new)
        l_ref[...] = alpha * l_ref[...] + jnp.sum(p, axis=0, keepdims=True)
        m_ref[...] = m_new
        p_buf[slot] = p.astype(BF16)
        a_buf[slot] = alpha

    def accumulate(idx, slot):
        acc_ref[...] = a_buf[slot] * acc_ref[...] + _dot(vt_ref[0, idx], p_buf[slot])

    n_far = jnp.maximum(i - 1, 0)
    off = n_far % 2
    p_buf[1] = jnp.zeros(p_buf.shape[1:], BF16)
    a_buf[1] = jnp.ones(a_buf.shape[1:], F32)
    scores(0, 0)

    def far_body(t, carry):
        u = 2 * t - off
        scores(u + 1, 1)
        softmax(0, jnp.where(u >= 0, cf, NEG_INF))
        accumulate(jnp.maximum(u - 1, 0), 1)
        scores(u + 2, 0)
        softmax(1, cf)
        accumulate(jnp.maximum(u, 0), 0)
        return carry

    def bound_softmax(slot, bias):
        if bias.ndim == 0:
            p = jnp.exp2(s_buf[slot] - (bound - bias))
        else:
            p = jnp.exp2(s_buf[slot] + (both(bias) - bound))
        l_ref[...] += jnp.sum(p, axis=0, keepdims=True)
        p_buf[slot] = p.astype(BF16)

    def bound_accumulate(idx, slot):
        acc_ref[...] += _dot(vt_ref[0, idx], p_buf[slot])

    def far_body_bound(t, carry):
        u = 2 * t - off
        scores(u + 1, 1)
        bound_softmax(0, jnp.where(u >= 0, cf, NEG_INF))
        bound_accumulate(jnp.maximum(u - 1, 0), 1)
        scores(u + 2, 0)
        bound_softmax(1, cf)
        bound_accumulate(jnp.maximum(u, 0), 0)
        return carry

    def tail(softmax_fn, accumulate_fn):
        scores(i, 1)
        softmax_fn(0, jnp.where(i >= 1, bm1_ref[0], NEG_INF))
        accumulate_fn(jnp.maximum(n_far - 1, 0), 1)
        softmax_fn(1, b0_ref[0])
        accumulate_fn(n_far, 0)
        accumulate_fn(i, 1)

    @pl.when(use_bound)
    def _():
        lax.fori_loop(0, (n_far + 1) // 2, far_body_bound, 0)
        tail(bound_softmax, bound_accumulate)

    @pl.when(jnp.logical_not(use_bound))
    def _():
        lax.fori_loop(0, (n_far + 1) // 2, far_body, 0)
        tail(softmax, accumulate)

    accn = acc_ref[...] * (1.0 / l_ref[...])
    o_t = accn[:, 0:tq] - lam * accn[:, tq:]
    o_ref[...] = _sub_layer_norm(jnp.transpose(o_t), g_ref[...], out_scale).astype(BF16)

    @pl.when(i == 0)
    def _():
        qm = _stack_components(qm_ref[...])
        bmm = bmm_ref[0]
        sm = _dot_nt(qm, km_ref[...]) + jnp.concatenate([bmm, bmm], axis=0)
        mm = jnp.max(sm, axis=-1, keepdims=True)
        pm = jnp.exp2(sm - mm)
        accm = _dot(pm.astype(BF16), vm_ref[...]) / jnp.sum(pm, axis=-1, keepdims=True)
        om = accm[0:N_META] - lam * accm[N_META:]
        om_ref[...] = _sub_layer_norm(om, g_ref[...], out_scale).astype(BF16)


def _attn_prompt(cfar, lam, fix, q_f, kb_t, vt_t, kb_m, vb_m, vbt_m, q_m, b0, bm1, bq0, bmm, subln_g,
                 out_scale):
    tf = q_f.shape[0]
    tq = ATT_T
    assert tf % tq == 0
    n_t = tf // tq
    smem = pl.BlockSpec(memory_space=pltpu.SMEM)
    head_col = lambda rows: pl.BlockSpec((rows, LANES), lambda h, i: (0, h))
    head_tile = lambda a, b: pl.BlockSpec((1, a, b), lambda h, i: (h, 0, 0))
    return pl.pallas_call(
        functools.partial(_attn_prompt_kernel, out_scale=out_scale),
        grid=(DA_HEADS, n_t),
        in_specs=[smem, smem, smem,
                  pl.BlockSpec((tq, LANES), lambda h, i: (i, h)),
                  pl.BlockSpec((1, n_t, tq, LANES), lambda h, i: (h, 0, 0, 0)),
                  pl.BlockSpec((1, n_t, LANES, tq), lambda h, i: (h, 0, 0, 0)),
                  head_col(N_META), head_col(N_META),
                  pl.BlockSpec((LANES, N_META), lambda h, i: (h, 0)),
                  head_col(N_META),
                  head_tile(tq, tq), head_tile(tq, tq), head_tile(N_META, tq),
                  head_tile(N_META, N_META), _const_spec((1, LANES))],
        out_specs=[pl.BlockSpec((tq, LANES), lambda h, i: (i, h)), head_col(N_META)],
        out_shape=[jax.ShapeDtypeStruct((tf, DA_WIDTH), BF16),
                   jax.ShapeDtypeStruct((N_META, DA_WIDTH), BF16)],
        scratch_shapes=[pltpu.VMEM((LANES, 2 * tq), F32), pltpu.VMEM((1, 2 * tq), F32),
                        pltpu.VMEM((1, 2 * tq), F32), pltpu.VMEM((2, tq, 2 * tq), F32),
                        pltpu.VMEM((2, tq, 2 * tq), BF16), pltpu.VMEM((2, 1, 2 * tq), F32)],
        compiler_params=_cparams(("arbitrary", "arbitrary")),
        name="attn_prompt",
    )(cfar, lam, fix, q_f, kb_t, vt_t, kb_m, vb_m, vbt_m, q_m, b0, bm1, bq0, bmm, subln_g)


DEC_HEADS = 4


def _attn_decode_kernel(lam_ref, q_ref, k_hbm, v_hbm, kn_ref, vn_ref, b_ref, g_ref, o_ref,
                        kbuf, vbuf, sem, *, layer, n_main, out_scale):
    bi = pl.program_id(0)
    hg = pl.program_id(1)
    n_hg = pl.num_programs(1)
    step = bi * n_hg + hg
    slot = step % 2

    def cache_copies(b_, hg_, slot_):
        cps = []
        for hh in range(DEC_HEADS):
            h_ = hg_ * DEC_HEADS + hh
            cps.append(pltpu.make_async_copy(k_hbm.at[layer, b_, :, h_, :], kbuf.at[slot_, hh],
                                             sem.at[0, slot_, hh]))
            cps.append(pltpu.make_async_copy(v_hbm.at[layer, b_, :, h_, :], vbuf.at[slot_, hh],
                                             sem.at[1, slot_, hh]))
        return cps

    @pl.when(step == 0)
    def _():
        for cp in cache_copies(bi, hg, slot):
            cp.start()

    @pl.when(step + 1 < pl.num_programs(0) * n_hg)
    def _():
        nxt = step + 1
        for cp in cache_copies(nxt // n_hg, nxt % n_hg, 1 - slot):
            cp.start()

    for cp in cache_copies(bi, hg, slot):
        cp.wait()

    lam = lam_ref[0]
    nq = q_ref.shape[0]
    n_cache = kbuf.shape[2]
    for hh in range(DEC_HEADS):
        cols = slice(LANES * hh, LANES * (hh + 1))
        k_ref = kbuf.at[slot, hh]
        v_ref = vbuf.at[slot, hh]
        qst = _stack_components(q_ref[:, cols])
        k_main = k_ref[0:n_main, :].astype(BF16)
        v_main = v_ref[0:n_main, :].astype(BF16)
        k_tail = jnp.concatenate([k_ref[n_main:n_cache, :], kn_ref[:, cols]], axis=0).astype(BF16)
        v_tail = jnp.concatenate([v_ref[n_main:n_cache, :], vn_ref[:, cols]], axis=0).astype(BF16)
        bias = jnp.concatenate([b_ref[hh], b_ref[hh]], axis=0)
        s1 = _dot_nt(qst, k_main) + bias[:, 0:n_main]
        s2 = _dot_nt(qst, k_tail) + bias[:, n_main:]
        m = jnp.maximum(jnp.max(s1, axis=-1, keepdims=True), jnp.max(s2, axis=-1, keepdims=True))
        p1 = jnp.exp2(s1 - m)
        p2 = jnp.exp2(s2 - m)
        l = jnp.sum(p1, axis=-1, keepdims=True) + jnp.sum(p2, axis=-1, keepdims=True)
        res = (_dot(p1.astype(BF16), v_main) + _dot(p2.astype(BF16), v_tail)) / l
        o = res[0:nq] - lam * res[nq:]
        o_ref[:, cols] = _sub_layer_norm(o, g_ref[...], out_scale).astype(BF16)


def _attn_decode(lam, q_s, cache_k, cache_v, layer, k_s, v_s, bias, subln_g, out_scale, nq):
    _, b, n_cache, n_h, _ = cache_k.shape
    assert n_h % DEC_HEADS == 0
    n_main = (n_cache // LANES) * LANES
    n_keys = n_cache + nq
    smem = pl.BlockSpec(memory_space=pltpu.SMEM)
    hbm = pl.BlockSpec(memory_space=pl.ANY)
    new = pl.BlockSpec((nq, DEC_HEADS * LANES), lambda bi, hg: (bi, hg))
    buf = pltpu.VMEM((2, DEC_HEADS, n_cache, LANES), F32)
    return pl.pallas_call(
        functools.partial(_attn_decode_kernel, layer=layer, n_main=n_main, out_scale=out_scale),
        grid=(b, n_h // DEC_HEADS),
        in_specs=[smem, new, hbm, hbm, new, new,
                  pl.BlockSpec((DEC_HEADS, nq, n_keys), lambda bi, hg: (hg, 0, 0)),
                  _const_spec((1, LANES))],
        out_specs=new,
        out_shape=jax.ShapeDtypeStruct((b * nq, DA_WIDTH), BF16),
        scratch_shapes=[buf, buf, pltpu.SemaphoreType.DMA((2, 2, DEC_HEADS))],
        compiler_params=_cparams(("arbitrary", "arbitrary")),
        name="attn_decode",
    )(lam, q_s, cache_k, cache_v, k_s, v_s, bias, subln_g)


def _sigmoid(x):
    return 1.0 / (1.0 + jnp.exp(-x))


def _split2(x):
    x1 = x.astype(BF16)
    x2 = (x - x1.astype(F32)).astype(BF16)
    return x1, x2


def _stack_heads(x):
    lo = lax.broadcasted_iota(jnp.int32, x.shape, 1) < HEAD_DIM
    zero = jnp.zeros_like(x)
    return jnp.concatenate([jnp.where(lo, x, zero), jnp.where(lo, zero, x)], axis=0)


def _rwkv_kernel(pr_ref, sh0_ref, h0_ref, mu_ref, w0_ref, a0_ref, kk_ref, ka_ref, rk_ref,
                 lw_ref, lb_ref, w2_ref, a2_ref, g2_ref, gsum_ref, tri_ref,
                 o_ref, sn_ref,
                 h_st, xbuf, carry, at_s, rt_s, bt_s, kt_s, v_s, ep_s, y_s,
                 t_s, tav_s, lrb_s, lrkv_s, zbt_s, zkv_s, ar_s, pc_s,
                 *, tr, t_valid, n_state):
    c_len = RW_CHUNK
    n_ch = tr // c_len
    ti = pl.program_id(1)

    @pl.when(ti == 0)
    def _():
        has_state = pl.program_id(0) < n_state
        carry[...] = jnp.where(has_state, sh0_ref[0], 0.0)
        hd_r = lax.broadcasted_iota(jnp.int32, (PAIR, PAIR), 0) // HEAD_DIM
        hd_c = lax.broadcasted_iota(jnp.int32, (PAIR, PAIR), 1) // HEAD_DIM
        for p in range(N_PAIR):
            s_p = h0_ref[0, p]
            blk = jnp.where((hd_r == hd_c) & has_state, jnp.concatenate([s_p, s_p], axis=1), 0.0)
            h_st[p] = jnp.transpose(blk)

    x = pr_ref[0]
    xbuf[8:8 + tr, :] = x
    xbuf[7:8, :] = carry[...]
    prev = xbuf[7:7 + tr, :]
    carry[...] = pr_ref[0, tr - 1:tr, :]
    xm = x + (prev - x) * mu_ref[...]
    r = xm[:, 0:RW_WIDTH]
    kr = xm[:, RW_WIDTH:2 * RW_WIDTH]
    vr = xm[:, 2 * RW_WIDTH:3 * RW_WIDTH]
    wa = xm[:, 3 * RW_WIDTH:3 * RW_WIDTH + W_LORA + A_LORA]
    gd = xm[:, 3 * RW_WIDTH + W_LORA + A_LORA:]
    lane_wa = lax.broadcasted_iota(jnp.int32, wa.shape, 1)
    twa = jnp.where(lane_wa < W_LORA, jnp.tanh(wa), wa).astype(BF16)
    logw = -DECAY_SCALE * _sigmoid(w0_ref[...] + _dot(twa, w2_ref[...]))
    a_sig = _sigmoid(a0_ref[...] + _dot(twa, a2_ref[...]))
    g = _dot(_sigmoid(gd).astype(BF16), g2_ref[...])
    gsum = gsum_ref[...]
    kk = kr * kk_ref[...]
    kk = kk * lax.rsqrt(jnp.maximum(_group_sum(kk * kk, gsum), 1e-24))
    kr2 = kr * (1.0 + (a_sig - 1.0) * ka_ref[...])
    a_vec = -kk
    b_vec = kk * a_sig
    bonus = _group_sum(r * kr2 * rk_ref[...], gsum) * vr
    if t_valid % tr != 0:
        row = lax.broadcasted_iota(jnp.int32, (tr, 1), 0) + ti * tr
        valid = row < t_valid
        logw = jnp.where(valid, logw, 0.0)
        a_vec = jnp.where(valid, a_vec, 0.0)
        b_vec = jnp.where(valid, b_vec, 0.0)
        kr2 = jnp.where(valid, kr2, 0.0)
        vr = jnp.where(valid, vr, 0.0)
        bonus = jnp.where(valid, bonus, 0.0)
    l1, l2 = _split2(logw)
    tri = tri_ref[...]
    cs = _dot(tri, l1) + _dot(tri, l2)
    e_pos = jnp.exp(cs)
    e_neg = jnp.exp(-cs)
    at_s[...] = a_vec * jnp.exp(cs - logw)
    rt_s[...] = r * e_pos
    bt_s[...] = b_vec * e_neg
    kt_s[...] = kr2 * e_neg
    v_s[...] = vr
    ep_s[...] = e_pos

    idx_r = lax.broadcasted_iota(jnp.int32, (PAIR, PAIR), 0)
    idx_c = lax.broadcasted_iota(jnp.int32, (PAIR, PAIR), 1)
    same = (idx_r // c_len) == (idx_c // c_len)
    strict = same & ((idx_r % c_len) > (idx_c % c_len))
    incl = same & ((idx_r % c_len) >= (idx_c % c_len))
    eye = idx_r == idx_c
    eye_f = jnp.where(eye, 1.0, 0.0).astype(F32)

    zero = jnp.zeros((PAIR, PAIR), F32)
    n_par = 2 if n_ch % 2 == 0 else 1
    n_grp = n_ch // n_par
    items = [(dc, p) for dc in range(n_par) for p in range(N_PAIR)]
    pairs = range(N_PAIR)

    def s1_load(g):
        sel = [(pl.ds(pl.multiple_of((g * n_par + dc) * c_len, c_len), c_len),
                slice(PAIR * p, PAIR * (p + 1))) for dc, p in items]
        return dict(at=[at_s[rw, cl] for rw, cl in sel], rt=[rt_s[rw, cl] for rw, cl in sel],
                    bt=[bt_s[rw, cl] for rw, cl in sel], kt=[kt_s[rw, cl] for rw, cl in sel],
                    v=[v_s[rw, cl] for rw, cl in sel], ep=[ep_s[rw, cl] for rw, cl in sel])

    def s1_compute(ld):
        q_all = range(len(items))
        at, rt = ld["at"], ld["rt"]
        yb = [_stack_heads(x) for x in ld["bt"]]
        yk = [_stack_heads(x) for x in ld["kt"]]
        vst = [_stack_heads(x).astype(BF16) for x in ld["v"]]
        pc = [x[c_len - 1:c_len] for x in ld["ep"]]
        gmat = [_dot_nt(jnp.concatenate([_stack_heads(at[q]), _stack_heads(rt[q])], axis=0).astype(BF16),
                        jnp.concatenate([yb[q], yk[q]], axis=0).astype(BF16)) for q in q_all]
        aab = [jnp.where(strict, gmat[q][0:PAIR, 0:PAIR], zero) for q in q_all]
        aak = [jnp.where(strict, gmat[q][0:PAIR, PAIR:], zero).astype(BF16) for q in q_all]
        lrb = [jnp.where(incl, gmat[q][PAIR:, 0:PAIR], zero).astype(BF16) for q in q_all]
        lrk = [jnp.where(incl, gmat[q][PAIR:, PAIR:], zero).astype(BF16) for q in q_all]
        tinv = [eye_f + aab[q] for q in q_all]
        lp = aab
        n = 1
        while 2 * n < c_len:
            lpb = [x.astype(BF16) for x in lp]
            lp = [_dot(x, x) for x in lpb]
            tinv = [tinv[q] + _dot(tinv[q].astype(BF16), lp[q].astype(BF16)) for q in q_all]
            n *= 2
        tb = [x.astype(BF16) for x in tinv]
        av = [_dot(aak[q], vst[q]).astype(BF16) for q in q_all]
        return dict(
            t=tb, tav=[_dot(tb[q], av[q]) for q in q_all], lrb=lrb,
            lrkv=[_dot(lrk[q], vst[q]) for q in q_all],
            zbt=[jnp.transpose(yb[q] * pc[q]).astype(BF16) for q in q_all],
            zkv=[_dot(jnp.transpose(yk[q] * pc[q]).astype(BF16), vst[q]) for q in q_all],
            ar=[jnp.concatenate([at[q], rt[q]], axis=0).astype(BF16) for q in q_all],
            pc=[jnp.sum(jnp.where(eye, jnp.broadcast_to(pc[q], (PAIR, PAIR)), zero), axis=-1, keepdims=True)
                for q in q_all])

    stage_bufs = dict(t=t_s, tav=tav_s, lrb=lrb_s, lrkv=lrkv_s, zbt=zbt_s, zkv=zkv_s, ar=ar_s, pc=pc_s)

    def s1_store(g, res):
        for q, (dc, p) in enumerate(items):
            for name, buf in stage_bufs.items():
                buf[g * n_par + dc, p] = res[name][q]

    def s2_load(g):
        return [{name: [buf[g * n_par + dc, p] for p in pairs] for name, buf in stage_bufs.items()}
                for dc in range(n_par)]

    def s2_compute(ld_group, hbd):
        ys = []
        for ld in ld_group:
            arh = [_dot(ld["ar"][p], hbd[p].astype(BF16)) for p in pairs]
            ub = [(_dot(ld["t"][p], _stack_heads(arh[p][0:c_len]).astype(BF16)) + ld["tav"][p]).astype(BF16)
                  for p in pairs]
            yst = [_dot(ld["lrb"][p], ub[p]) + ld["lrkv"][p] for p in pairs]
            ys.append([arh[p][c_len:] + yst[p][0:c_len] + yst[p][c_len:] for p in pairs])
            hbd = [ld["pc"][p] * hbd[p] + _dot(ld["zbt"][p], ub[p]) + ld["zkv"][p] for p in pairs]
        return ys, hbd

    def s2_store(g, ys, hbd):
        for dc in range(n_par):
            rows = pl.ds(pl.multiple_of((g * n_par + dc) * c_len, c_len), c_len)
            for p in pairs:
                y_s[rows, PAIR * p:PAIR * (p + 1)] = ys[dc][p]
        for p in pairs:
            h_st[p] = hbd[p]

    s1_store(0, s1_compute(s1_load(0)))

    def body(g, carry_):
        ld2 = s2_load(g)
        ld1 = s1_load(g + 1)
        hbd = [h_st[p] for p in pairs]
        res1 = s1_compute(ld1)
        ys, hbd = s2_compute(ld2, hbd)
        s2_store(g, ys, hbd)
        s1_store(g + 1, res1)
        return carry_

    lax.fori_loop(0, n_grp - 1, body, 0)
    ys_last, h_last = s2_compute(s2_load(n_grp - 1), [h_st[p] for p in pairs])
    s2_store(n_grp - 1, ys_last, h_last)

    @pl.when(ti == pl.num_programs(1) - 1)
    def _():
        first = lax.broadcasted_iota(jnp.int32, (PAIR, HEAD_DIM), 0) < HEAD_DIM
        for p in range(N_PAIR):
            s_t = jnp.transpose(h_st[p])
            sn_ref[0, p] = jnp.where(first, s_t[:, 0:HEAD_DIM], s_t[:, HEAD_DIM:])

    y = y_s[...]
    inv_n = 1.0 / HEAD_DIM
    mean = _group_sum(y, gsum) * inv_n
    d = y - mean
    var = _group_sum(d * d, gsum) * inv_n
    yn = d * lax.rsqrt(var + GN_EPS) * lw_ref[...] + lb_ref[...]
    o_ref[0] = ((yn + bonus) * g).astype(BF16)


def _rwkv(pr, shift0, s0, prm, tr, t_valid):
    b, t_pad, _ = pr.shape
    n_state = s0.shape[0]
    assert shift0.shape[0] == n_state
    last = n_state - 1
    assert t_pad % tr == 0 and tr % RW_CHUNK == 0
    n_ch = tr // RW_CHUNK
    tri = np.zeros((tr, tr), np.float32)
    for c in range(n_ch):
        tri[c * RW_CHUNK:(c + 1) * RW_CHUNK, c * RW_CHUNK:(c + 1) * RW_CHUNK] = np.tril(
            np.ones((RW_CHUNK, RW_CHUNK), np.float32))
    tri = jnp.asarray(tri, BF16)
    vec = _const_spec((1, RW_WIDTH))
    mat = lambda dt: pltpu.VMEM((n_ch, N_PAIR, PAIR, PAIR), dt)
    tile = lambda: pltpu.VMEM((tr, RW_WIDTH), F32)
    return pl.pallas_call(
        functools.partial(_rwkv_kernel, tr=tr, t_valid=t_valid, n_state=n_state),
        grid=(b, t_pad // tr),
        in_specs=[pl.BlockSpec((1, tr, RW_PROJ), lambda bi, ti: (bi, ti, 0)),
                  pl.BlockSpec((1, 1, RW_PROJ), lambda bi, ti: (jnp.minimum(bi, last), 0, 0)),
                  pl.BlockSpec((1, N_PAIR, PAIR, HEAD_DIM),
                               lambda bi, ti: (jnp.minimum(bi, last), 0, 0, 0)),
                  _const_spec((1, RW_PROJ)), vec, vec, vec, vec, vec, vec, vec,
                  _const_spec((W_LORA + A_LORA, RW_WIDTH)), _const_spec((W_LORA + A_LORA, RW_WIDTH)),
                  _const_spec((G_LORA, RW_WIDTH)), _const_spec((LANES, LANES)),
                  _const_spec((tr, tr))],
        out_specs=[pl.BlockSpec((1, tr, RW_WIDTH), lambda bi, ti: (bi, ti, 0)),
                   pl.BlockSpec((1, N_PAIR, PAIR, HEAD_DIM), lambda bi, ti: (bi, 0, 0, 0))],
        out_shape=[jax.ShapeDtypeStruct((b, t_pad, RW_WIDTH), BF16),
                   jax.ShapeDtypeStruct((b, N_PAIR, PAIR, HEAD_DIM), F32)],
        scratch_shapes=[pltpu.VMEM((N_PAIR, PAIR, PAIR), F32),
                        pltpu.VMEM((8 + tr, RW_PROJ), F32), pltpu.VMEM((1, RW_PROJ), F32),
                        tile(), tile(), tile(), tile(), tile(), tile(), tile(),
                        mat(BF16), mat(F32), mat(BF16), mat(F32), mat(BF16), mat(F32), mat(BF16),
                        pltpu.VMEM((n_ch, N_PAIR, PAIR, 1), F32)],
        compiler_params=_cparams(("arbitrary", "arbitrary")),
        name="rwkv",
    )(pr, shift0, s0, prm["mu"], prm["w0"], prm["a0"], prm["k_k"], prm["k_a"], prm["r_k"],
      prm["lnx_w"], prm["lnx_b"], prm["w2p"], prm["a2p"], prm["g2"], prm["gsum"], tri)


FF_COLS = 256


def _ffn_kernel(x_ref, oa_ref, orw_ref, c0_ref, wo_ref, g_ref, wu_ref, cw_ref, cb_ref, wd_ref,
                y_ref, cn_ref, zbuf, cbuf, h_ref, x1_ref, *, tm, ts, d_ff):
    off = zbuf.shape[1] - tm
    da = oa_ref.shape[1]

    @pl.when(pl.program_id(0) == 0)
    def _():
        cbuf[...] = c0_ref[...]

    x1 = x_ref[...] + _dot(oa_ref[...], wo_ref[0:da, :]) + _dot(orw_ref[...], wo_ref[da:, :])
    x1_ref[...] = x1
    ms = jnp.mean(x1 * x1, axis=-1, keepdims=True)
    h_ref[...] = (x1 * lax.rsqrt(ms + NORM_EPS) * g_ref[...]).astype(BF16)

    def up_proj(c0, zb):
        cols = slice(c0, c0 + FF_COLS)
        zb[off - 2 * ts:off, :] = cbuf[:, cols]
        zb[off:off + tm, :] = _dot(h_ref[...], wu_ref[:, cols])
        cbuf[:, cols] = zb[off + tm - 2 * ts:off + tm, :]

    def conv_cols(c0, zb):
        cols = slice(c0, c0 + FF_COLS)
        z2 = zb[off - 2 * ts:off - 2 * ts + tm, :]
        z1 = zb[off - ts:off - ts + tm, :]
        z = zb[off:off + tm, :]
        return (cb_ref[:, cols] + z2 * cw_ref[0:1, cols] + z1 * cw_ref[1:2, cols]
                + z * cw_ref[2:3, cols])

    def stage(c):
        up_proj(c * FF_COLS, zbuf.at[2 * (c % 2)])
        up_proj(d_ff + c * FF_COLS, zbuf.at[2 * (c % 2) + 1])

    n_chunks = d_ff // FF_COLS
    acc = None
    stage(0)
    for c in range(n_chunks):
        if c + 1 < n_chunks:
            stage(c + 1)
        gate = conv_cols(c * FF_COLS, zbuf.at[2 * (c % 2)])
        up = conv_cols(d_ff + c * FF_COLS, zbuf.at[2 * (c % 2) + 1])
        act = (gate * _sigmoid(gate) * up).astype(BF16)
        part = _dot(act, wd_ref[c * FF_COLS:(c + 1) * FF_COLS, :])
        acc = part if acc is None else acc + part

    y_ref[...] = x1_ref[...] + acc
    cn_ref[...] = cbuf[...]


def _ffn(x, oa, orw, conv0, prm, tm, ts):
    rows, d = x.shape
    d_ff = prm["w_down"].shape[0]
    assert rows % tm == 0 and d_ff % FF_COLS == 0 and (ts == 1 or ts % 8 == 0)
    off = -(-2 * ts // 8) * 8
    row = lambda w: pl.BlockSpec((tm, w), lambda i: (i, 0))
    return pl.pallas_call(
        functools.partial(_ffn_kernel, tm=tm, ts=ts, d_ff=d_ff),
        grid=(rows // tm,),
        in_specs=[row(d), row(DA_WIDTH), row(RW_WIDTH), _const_spec((2 * ts, 2 * d_ff)),
                  _resident_spec((DA_WIDTH + RW_WIDTH, d)), _const_spec((1, d)),
                  _resident_spec((d, 2 * d_ff)), _const_spec((CONV_W, 2 * d_ff)),
                  _const_spec((1, 2 * d_ff)), _resident_spec((d_ff, d))],
        out_specs=[row(d), _const_spec((2 * ts, 2 * d_ff))],
        out_shape=[jax.ShapeDtypeStruct((rows, d), F32),
                   jax.ShapeDtypeStruct((2 * ts, 2 * d_ff), F32)],
        scratch_shapes=[pltpu.VMEM((4, off + tm, FF_COLS), F32), pltpu.VMEM((2 * ts, 2 * d_ff), F32),
                        pltpu.VMEM((tm, d), BF16), pltpu.VMEM((tm, d), F32)],
        compiler_params=_cparams(("arbitrary",)),
        name="ffn",
    )(x, oa, orw, conv0, prm["w_out"], prm["ln2_g"], prm["w_up"], prm["conv_w"], prm["conv_b"],
      prm["w_down"])


def _rel_bucket(rel):
    nb = N_BUCKETS // 2
    max_exact = nb // 2
    bucket = jnp.where(rel > 0, nb, 0)
    n = jnp.abs(rel)
    nf = jnp.maximum(n, 1).astype(F32)
    large = max_exact + (jnp.log(nf / max_exact) / math.log(MAX_DISTANCE / max_exact)
                         * (nb - max_exact)).astype(jnp.int32)
    large = jnp.minimum(large, nb - 1)
    return bucket + jnp.where(n < max_exact, n, large)


def _bias_table(rel_bias, q_pos, k_pos, mask):
    n_q, n_k = len(q_pos), len(k_pos)
    assert np.all(np.diff(q_pos) == 1) and np.all(np.diff(k_pos) == 1)
    span = n_q + n_k - 1
    rel = jnp.asarray(int(k_pos[0]) - int(q_pos[0]) - (n_q - 1) + np.arange(span), jnp.int32)
    w = jnp.transpose(rel_bias[_rel_bucket(rel)]).astype(F32) * LOG2E
    return jnp.where(jnp.asarray(mask)[None], _toeplitz(w, n_q, n_k), NEG_INF)


def _skew(w, n_q, n_k):
    span = n_q + n_k - 1
    x = jnp.pad(w[:, :span], ((0, 0), (0, 1)))
    rows = jnp.tile(x, (1, n_q))[:, :n_q * span].reshape(w.shape[0], n_q, span)
    return rows[:, :, n_q - 1:n_q - 1 + n_k]


def _toeplitz(w, n_q, n_k):
    blk = LANES
    if n_q % blk or n_k % blk or n_q * n_k <= blk * blk:
        return _skew(w, n_q, n_k)
    nbq, nbk = n_q // blk, n_k // blk
    n_d = nbq + nbk - 1
    n_h = w.shape[0]
    wp = jnp.pad(w[:, :n_q + n_k - 1], ((0, 0), (0, 1)))
    seg = jnp.concatenate([wp[:, :n_d * blk].reshape(n_h, n_d, blk),
                           wp[:, blk:].reshape(n_h, n_d, blk)[..., :blk - 1]], axis=-1)
    x = jnp.pad(seg, ((0, 0), (0, 0), (0, 1)))
    rows = jnp.tile(x, (1, 1, blk))[..., :blk * (2 * blk - 1)].reshape(n_h, n_d, blk, 2 * blk - 1)
    blocks = rows[..., blk - 1:]
    return jnp.concatenate([jnp.concatenate([blocks[:, c - r + nbq - 1] for c in range(nbk)], axis=2)
                            for r in range(nbq)], axis=1)


BOUND_MAX_SPREAD = 100.0


def _score_bound(q_g, k_g, rel_bias):
    s_max = (HEAD_DIM ** 0.5) * LOG2E * 1.01 * jnp.max(jnp.abs(q_g)) * jnp.max(jnp.abs(k_g))
    b_hi = jnp.max(rel_bias) * LOG2E
    b_lo = jnp.min(rel_bias) * LOG2E
    use = (2.0 * s_max + (b_hi - b_lo)) <= BOUND_MAX_SPREAD
    return jnp.stack([use.astype(F32), (s_max + b_hi).astype(F32)])


def _ext_chunk(pos):
    return np.where(pos < N_META, -1, (pos - N_META) // CHUNK)


def _prompt_bias(rel_bias):
    tq = ATT_T
    fr = np.arange(tq) + N_META
    meta = np.arange(N_META)
    causal = _ext_chunk(fr)[None, :] <= _ext_chunk(fr)[:, None]
    tr = lambda b: jnp.swapaxes(b, 1, 2)
    b0 = tr(_bias_table(rel_bias, fr, fr, causal))
    bm1 = tr(_bias_table(rel_bias, fr + tq, fr, np.ones((tq, tq), bool)))
    bq0 = tr(_bias_table(rel_bias, fr, meta, np.ones((tq, N_META), bool)))
    bmm = _bias_table(rel_bias, meta, meta, np.ones((N_META, N_META), bool))
    assert tq + 1 >= MAX_DISTANCE
    cfar = rel_bias[_rel_bucket(jnp.asarray(-(tq + 1), jnp.int32))].astype(F32) * LOG2E
    return cfar, b0, bm1, bq0, bmm


def _decode_bias(rel_bias, n_cache, nq):
    k_pos = np.arange(n_cache + nq)
    q_pos = k_pos[n_cache:]
    mask = _ext_chunk(k_pos)[None, :] <= _ext_chunk(q_pos)[:, None]
    return _bias_table(rel_bias, q_pos, k_pos, mask)


def _block_ones(n, blk, dtype):
    idx = np.arange(n) // blk
    return jnp.asarray((idx[:, None] == idx[None, :]).astype(np.float32), dtype)


def kernel(x_prompt, x_sample, cache_k, cache_v, state_rwkv, state_shift, state_conv, meta_tokens,
           rel_bias, ln1_g, w_in, q_norm_g, k_norm_g, lam_q1, lam_k1, lam_q2, lam_k2, subln_g,
           mu_shift, w0, w2, a0, a2, g2, k_k, k_a, r_k, lnx_w, lnx_b, w_out, ln2_g, w_up, conv_w,
           conv_b, w_down):
    bp, seq, d = x_prompt.shape
    db, dt, _ = x_sample.shape
    depth = w_in.shape[0]
    d_ff = w_down.shape[1]
    n_cache = cache_k.shape[2]
    assert bp == 1 and dt == N_META, "the meta stream rides with the decode streams"
    assert cache_k.shape[3] == DA_HEADS and cache_k.shape[4] == 2 * HEAD_DIM
    nb = db + 1
    nb_pad = -(-nb // 8) * 8

    cfar, b0, bm1, bq0, bmm = _prompt_bias(rel_bias)
    bias_dec = _decode_bias(rel_bias, n_cache, dt)
    gsum = _block_ones(LANES, HEAD_DIM, BF16)
    gmean = gsum * (1.0 / HEAD_DIM)
    zrow = lambda n: jnp.zeros((n, RW_WIDTH), BF16)

    x_f = x_prompt[0]
    x_s = jnp.concatenate([x_sample, meta_tokens.astype(x_sample.dtype)[None]], axis=0)
    outs = [[] for _ in range(10)]
    for l in range(depth):
        lam_init = 0.8 - 0.6 * math.exp(-0.3 * l)
        lam = (jnp.exp(jnp.sum(lam_q1[l].astype(F32) * lam_k1[l].astype(F32)))
               - jnp.exp(jnp.sum(lam_q2[l].astype(F32) * lam_k2[l].astype(F32))) + lam_init).reshape(1)
        out_scale = 1.0 - lam_init
        tile128 = lambda g_: jnp.tile(g_.reshape(1, -1), (1, DA_WIDTH // g_.shape[-1]))
        qg, kg = tile128(q_norm_g[l]), tile128(k_norm_g[l])
        sg = subln_g[l].reshape(1, LANES)
        w_in_bf = w_in[l].astype(BF16)
        rw = {
            "mu": mu_shift[l].reshape(1, -1), "w0": w0[l].reshape(1, -1), "a0": a0[l].reshape(1, -1),
            "k_k": k_k[l].reshape(1, -1), "k_a": k_a[l].reshape(1, -1), "r_k": r_k[l].reshape(1, -1),
            "lnx_w": lnx_w[l].reshape(1, -1), "lnx_b": lnx_b[l].reshape(1, -1),
            "w2p": jnp.concatenate([w2[l].astype(BF16), zrow(A_LORA)], axis=0),
            "a2p": jnp.concatenate([zrow(W_LORA), a2[l].astype(BF16)], axis=0),
            "g2": g2[l].astype(BF16), "gsum": gsum,
        }
        ff = {
            "w_out": w_out[l].astype(BF16), "ln2_g": ln2_g[l].reshape(1, -1),
            "w_up": w_up[l].astype(BF16), "conv_w": conv_w[l], "conv_b": conv_b[l].reshape(1, -1),
            "w_down": w_down[l].astype(BF16),
        }

        q_f, k_f, v_f, pr_f, kb_t, vt_t = _proj(x_f, ln1_g[l].reshape(1, -1), w_in_bf, qg, kg, gmean,
                                                 512, True)
        q_s, k_s, v_s, pr_s = _proj(x_s.reshape(nb * dt, d), ln1_g[l].reshape(1, -1), w_in_bf, qg, kg,
                                    gmean, nb * dt, False)
        m0 = db * dt

        kb_m = k_s[m0:].astype(BF16)
        vb_m = v_s[m0:].astype(BF16)
        o_f, o_m = _attn_prompt(cfar, lam, _score_bound(q_norm_g[l], k_norm_g[l], rel_bias), q_f, kb_t, vt_t, kb_m, vb_m, jnp.transpose(vb_m), q_s[m0:],
                                b0, bm1, bq0, bmm, sg, out_scale)
        o_d = _attn_decode(lam, q_s, cache_k, cache_v, l, k_s, v_s, bias_dec, sg, out_scale, dt)
        o_s = jnp.concatenate([o_d, o_m], axis=0)

        pr_s3 = pr_s.reshape(nb, dt, RW_PROJ)
        pr_pad = jnp.pad(pr_s3, ((0, 0), (0, RW_CHUNK - dt), (0, 0)))
        pair_view = (N_PAIR, PAIR, HEAD_DIM)
        orw_s, sn_s = _rwkv(pr_pad, state_shift[l][:, None, :], state_rwkv[l].reshape(db, *pair_view),
                            rw, RW_CHUNK, dt)
        orw_f, sn_f = _rwkv(pr_f[None], pr_s3[db:, dt - 1:dt, :], sn_s[db:], rw, 512, seq)

        def time_major(a):
            a = jnp.pad(a.reshape(nb, dt, -1), ((0, nb_pad - nb), (0, 0), (0, 0)))
            return jnp.swapaxes(a, 0, 1).reshape(dt * nb_pad, -1)

        conv_s = jnp.concatenate([state_conv[l], jnp.zeros((1, CONV_W - 1, 2 * d_ff), F32)], axis=0)
        conv_s = jnp.pad(conv_s, ((0, nb_pad - nb), (0, 0), (0, 0)))
        conv_s = jnp.swapaxes(conv_s, 0, 1).reshape(2 * nb_pad, 2 * d_ff)
        y_s, cn_s = _ffn(time_major(x_s), time_major(o_s), time_major(orw_s[:, :dt]), conv_s, ff,
                         dt * nb_pad, nb_pad)
        cn_s = jnp.swapaxes(cn_s.reshape(2, nb_pad, 2 * d_ff), 0, 1)
        y_f, cn_f = _ffn(x_f, o_f, orw_f[0], cn_s[db], ff, 512, 1)
        y_s = jnp.swapaxes(y_s.reshape(dt, nb_pad, d), 0, 1)[:nb]

        hw = (DA_HEADS, 2 * HEAD_DIM)
        outs[0].append(jnp.concatenate([k_s[m0:].reshape(N_META, *hw), k_f], axis=0)[None])
        outs[1].append(jnp.concatenate([v_s[m0:].reshape(N_META, *hw), v_f], axis=0)[None])
        outs[2].append(sn_f.reshape(bp, RW_HEADS, HEAD_DIM, HEAD_DIM))
        outs[3].append(pr_f[seq - 1:seq])
        outs[4].append(cn_f[None])
        outs[5].append(k_s[:m0].reshape(db, dt, *hw))
        outs[6].append(v_s[:m0].reshape(db, dt, *hw))
        outs[7].append(sn_s[:db].reshape(db, RW_HEADS, HEAD_DIM, HEAD_DIM))
        outs[8].append(pr_s3[:db, dt - 1])
        outs[9].append(cn_s[:db])
        x_f, x_s = y_f, y_s

    return (x_f[None], x_s[:db], *[jnp.stack(o) for o in outs])
```

```python
import functools
import math

import numpy as np
import jax
import jax.numpy as jnp
from jax import lax
from jax.experimental import pallas as pl
from jax.experimental.pallas import tpu as pltpu

F32 = jnp.float32
BF16 = jnp.bfloat16

CHUNK = 64
N_META = 16
HEAD_DIM = 64
DA_HEADS = 4
RW_HEADS = 8
W_LORA = 64
A_LORA = 64
G_LORA = 128
CONV_W = 3
N_BUCKETS = 32
MAX_DISTANCE = 128
NORM_EPS = 1e-6
GN_EPS = 64e-5
NEG_INF = -1e30
LOG2E = math.log2(math.e)
DECAY_SCALE = math.exp(-0.5)

DA_WIDTH = DA_HEADS * 2 * HEAD_DIM
RW_WIDTH = RW_HEADS * HEAD_DIM
RW_PROJ = 3 * RW_WIDTH + W_LORA + A_LORA + G_LORA
LANES = 128
PAIR = 2 * HEAD_DIM
N_PAIR = RW_WIDTH // PAIR
RW_CHUNK = 64
ATT_T = 512
VMEM_LIMIT = 56 * 1024 * 1024


def _dot(a, b):
    return jnp.dot(a, b, preferred_element_type=F32)


def _dot_nt(a, b):
    return lax.dot_general(a, b, (((1,), (1,)), ((), ())), preferred_element_type=F32)


def _group_sum(x, blk):
    xb = x.astype(BF16)
    return jnp.concatenate([_dot(xb[:, LANES * p:LANES * (p + 1)], blk)
                            for p in range(x.shape[1] // LANES)], axis=1)


def _cparams(sem):
    return pltpu.CompilerParams(dimension_semantics=sem, vmem_limit_bytes=VMEM_LIMIT)


def _const_spec(shape):
    nd = len(shape)
    return pl.BlockSpec(shape, lambda *_: (0,) * nd)


def _resident_spec(shape):
    nd = len(shape)
    return pl.BlockSpec(shape, lambda *_: (0,) * nd, pipeline_mode=pl.Buffered(1))


def _proj_kernel(x_ref, g_ref, w_ref, qg_ref, kg_ref, gm_ref,
                 q_ref, k_ref, v_ref, pr_ref, *tile_refs, tm):
    x = x_ref[...]
    ms = jnp.mean(x * x, axis=-1, keepdims=True)
    h = (x * lax.rsqrt(ms + NORM_EPS) * g_ref[...]).astype(BF16)
    gm = gm_ref[...]

    def group_norm(t, g):
        ms_g = _group_sum(t * t, gm)
        return t * lax.rsqrt(ms_g + NORM_EPS) * g

    q = _dot(h, w_ref[:, 0:DA_WIDTH])
    k = _dot(h, w_ref[:, DA_WIDTH:2 * DA_WIDTH])
    pr_ref[...] = _dot(h, w_ref[:, 3 * DA_WIDTH:])
    v = _dot(h, w_ref[:, 2 * DA_WIDTH:3 * DA_WIDTH])
    q_ref[...] = (group_norm(q, qg_ref[...]) * (HEAD_DIM ** -0.5 * LOG2E)).astype(BF16)
    k = group_norm(k, kg_ref[...])
    if len(k_ref.shape) == 3:
        for hd in range(DA_HEADS):
            k_ref[:, hd, :] = k[:, hd * LANES:(hd + 1) * LANES]
            v_ref[:, hd, :] = v[:, hd * LANES:(hd + 1) * LANES]
    else:
        k_ref[...] = k
        v_ref[...] = v
    if tile_refs:
        kb_ref, vt_ref = tile_refs
        kb = k.astype(BF16)
        vt = jnp.transpose(v).astype(BF16)
        for hd in range(DA_HEADS):
            for jj in range(tm // ATT_T):
                kb_ref[hd, jj] = kb[jj * ATT_T:(jj + 1) * ATT_T, hd * LANES:(hd + 1) * LANES]
                vt_ref[hd, jj] = vt[hd * LANES:(hd + 1) * LANES, jj * ATT_T:(jj + 1) * ATT_T]


def _proj(x, ln1_g, w_in_bf, qg, kg, gmean, tm, emit_tiles):
    rows, d = x.shape
    n_in = w_in_bf.shape[1]
    assert rows % tm == 0
    row = lambda w: pl.BlockSpec((tm, w), lambda i: (i, 0))
    out_specs = [row(DA_WIDTH), row(DA_WIDTH), row(DA_WIDTH), row(RW_PROJ)]
    out_shape = [jax.ShapeDtypeStruct((rows, DA_WIDTH), BF16),
                 jax.ShapeDtypeStruct((rows, DA_WIDTH), F32),
                 jax.ShapeDtypeStruct((rows, DA_WIDTH), F32),
                 jax.ShapeDtypeStruct((rows, RW_PROJ), F32)]
    if emit_tiles:
        assert tm % ATT_T == 0
        tpt = tm // ATT_T
        for o in (1, 2):
            out_specs[o] = pl.BlockSpec((tm, DA_HEADS, LANES), lambda i: (i, 0, 0))
            out_shape[o] = jax.ShapeDtypeStruct((rows, DA_HEADS, LANES), F32)
        out_specs += [pl.BlockSpec((DA_HEADS, tpt, ATT_T, LANES), lambda i: (0, i, 0, 0)),
                      pl.BlockSpec((DA_HEADS, tpt, LANES, ATT_T), lambda i: (0, i, 0, 0))]
        out_shape += [jax.ShapeDtypeStruct((DA_HEADS, rows // ATT_T, ATT_T, LANES), BF16),
                      jax.ShapeDtypeStruct((DA_HEADS, rows // ATT_T, LANES, ATT_T), BF16)]
    return pl.pallas_call(
        functools.partial(_proj_kernel, tm=tm),
        grid=(rows // tm,),
        in_specs=[row(d), _const_spec((1, d)), _resident_spec((d, n_in)),
                  _const_spec((1, DA_WIDTH)), _const_spec((1, DA_WIDTH)),
                  _const_spec((LANES, LANES))],
        out_specs=out_specs,
        out_shape=out_shape,
        compiler_params=_cparams(("arbitrary",)),
        name="proj",
    )(x, ln1_g, w_in_bf, qg, kg, gmean)


def _stack_components(q):
    lo = lax.broadcasted_iota(jnp.int32, q.shape, 1) < HEAD_DIM
    zero = jnp.zeros_like(q)
    return jnp.concatenate([jnp.where(lo, q, zero), jnp.where(lo, zero, q)], axis=0)


def _sub_layer_norm(o, g, out_scale):
    ms = jnp.mean(o * o, axis=-1, keepdims=True)
    return o * lax.rsqrt(ms + NORM_EPS) * g * out_scale


def _attn_prompt_kernel(cfar_ref, lam_ref, fix_ref, q_ref, k_ref, vt_ref, km_ref, vm_ref, vmt_ref, qm_ref,
                        b0_ref, bm1_ref, bq0_ref, bmm_ref, g_ref, o_ref, om_ref,
                        acc_ref, m_ref, l_ref, s_buf, p_buf, a_buf, *, out_scale):
    h = pl.program_id(0)
    i = pl.program_id(1)
    tq = ATT_T
    cf = cfar_ref[h]
    lam = lam_ref[0]
    qst = _stack_components(q_ref[...])

    def both(b):
        return jnp.concatenate([b, b], axis=1)

    s = _dot_nt(km_ref[...], qst)
    s = s + both(jnp.where(i == 0, bq0_ref[0], cf))
    use_bound = fix_ref[0] > 0.5
    bound = fix_ref[1]
    m0 = jnp.where(use_bound, bound, jnp.max(s, axis=0, keepdims=True))
    p = jnp.exp2(s - m0)
    m_ref[...] = m0
    l_ref[...] = jnp.sum(p, axis=0, keepdims=True)
    acc_ref[...] = _dot(vmt_ref[...], p.astype(BF16))

    def scores(idx, slot):
        s_buf[slot] = _dot_nt(k_ref[0, idx], qst)

    def softmax(slot, bias):
        s = s_buf[slot]
        m_prev = m_ref[...]
        if bias.ndim == 0:
            m_new = jnp.maximum(m_prev, jnp.max(s, axis=0, keepdims=True) + bias)
            p = jnp.exp2(s - (m_new - bias))
        else:
            s = s + both(bias)
            m_new = jnp.maximum(m_prev, jnp.max(s, axis=0, keepdims=True))
            p = jnp.exp2(s - m_new)
        alpha = jnp.exp2(m_prev - m_new)
        l_ref[...] = alpha * l_ref[...] + jnp.sum(p, axis=0, keepdims=True)
        m_ref[...] = m_new
        p_buf[slot] = p.astype(BF16)
        a_buf[slot] = alpha

    def accumulate(idx, slot):
        acc_ref[...] = a_buf[slot] * acc_ref[...] + _dot(vt_ref[0, idx], p_buf[slot])

    n_far = jnp.maximum(i - 1, 0)
    off = n_far % 2
    p_buf[1] = jnp.zeros(p_buf.shape[1:], BF16)
    a_buf[1] = jnp.ones(a_buf.shape[1:], F32)
    scores(0, 0)

    def far_body(t, carry):
        u = 2 * t - off
        scores(u + 1, 1)
        softmax(0, jnp.where(u >= 0, cf, NEG_INF))
        accumulate(jnp.maximum(u - 1, 0), 1)
        scores(u + 2, 0)
        softmax(1, cf)
        accumulate(jnp.maximum(u, 0), 0)
        return carry

    def bound_softmax(slot, bias):
        if bias.ndim == 0:
            p = jnp.exp2(s_buf[slot] - (bound - bias))
        else:
            p = jnp.exp2(s_buf[slot] + (both(bias) - bound))
        l_ref[...] += jnp.sum(p, axis=0, keepdims=True)
        p_buf[slot] = p.astype(BF16)

    def bound_accumulate(idx, slot):
        acc_ref[...] += _dot(vt_ref[0, idx], p_buf[slot])

    def far_body_bound(t, carry):
        u = 2 * t - off
        scores(u + 1, 1)
        bound_softmax(0, jnp.where(u >= 0, cf, NEG_INF))
        bound_accumulate(jnp.maximum(u - 1, 0), 1)
        scores(u + 2, 0)
        bound_softmax(1, cf)
        bound_accumulate(jnp.maximum(u, 0), 0)
        return carry

    def tail(softmax_fn, accumulate_fn):
        scores(i, 1)
        softmax_fn(0, jnp.where(i >= 1, bm1_ref[0], NEG_INF))
        accumulate_fn(jnp.maximum(n_far - 1, 0), 1)
        softmax_fn(1, b0_ref[0])
        accumulate_fn(n_far, 0)
        accumulate_fn(i, 1)

    @pl.when(use_bound)
    def _():
        lax.fori_loop(0, (n_far + 1) // 2, far_body_bound, 0)
        tail(bound_softmax, bound_accumulate)

    @pl.when(jnp.logical_not(use_bound))
    def _():
        lax.fori_loop(0, (n_far + 1) // 2, far_body, 0)
        tail(softmax, accumulate)

    accn = acc_ref[...] * (1.0 / l_ref[...])
    o_t = accn[:, 0:tq] - lam * accn[:, tq:]
    o_ref[...] = _sub_layer_norm(jnp.transpose(o_t), g_ref[...], out_scale).astype(BF16)

    @pl.when(i == 0)
    def _():
        qm = _stack_components(qm_ref[...])
        bmm = bmm_ref[0]
        sm = _dot_nt(qm, km_ref[...]) + jnp.concatenate([bmm, bmm], axis=0)
        mm = jnp.max(sm, axis=-1, keepdims=True)
        pm = jnp.exp2(sm - mm)
        accm = _dot(pm.astype(BF16), vm_ref[...]) / jnp.sum(pm, axis=-1, keepdims=True)
        om = accm[0:N_META] - lam * accm[N_META:]
        om_ref[...] = _sub_layer_norm(om, g_ref[...], out_scale).astype(BF16)


def _attn_prompt(cfar, lam, fix, q_f, kb_t, vt_t, kb_m, vb_m, vbt_m, q_m, b0, bm1, bq0, bmm, subln_g,
                 out_scale):
    tf = q_f.shape[0]
    tq = ATT_T
    assert tf % tq == 0
    n_t = tf // tq
    smem = pl.BlockSpec(memory_space=pltpu.SMEM)
    head_col = lambda rows: pl.BlockSpec((rows, LANES), lambda h, i: (0, h))
    head_tile = lambda a, b: pl.BlockSpec((1, a, b), lambda h, i: (h, 0, 0))
    return pl.pallas_call(
        functools.partial(_attn_prompt_kernel, out_scale=out_scale),
        grid=(DA_HEADS, n_t),
        in_specs=[smem, smem, smem,
                  pl.BlockSpec((tq, LANES), lambda h, i: (i, h)),
                  pl.BlockSpec((1, n_t, tq, LANES), lambda h, i: (h, 0, 0, 0)),
                  pl.BlockSpec((1, n_t, LANES, tq), lambda h, i: (h, 0, 0, 0)),
                  head_col(N_META), head_col(N_META),
                  pl.BlockSpec((LANES, N_META), lambda h, i: (h, 0)),
                  head_col(N_META),
                  head_tile(tq, tq), head_tile(tq, tq), head_tile(N_META, tq),
                  head_tile(N_META, N_META), _const_spec((1, LANES))],
        out_specs=[pl.BlockSpec((tq, LANES), lambda h, i: (i, h)), head_col(N_META)],
        out_shape=[jax.ShapeDtypeStruct((tf, DA_WIDTH), BF16),
                   jax.ShapeDtypeStruct((N_META, DA_WIDTH), BF16)],
        scratch_shapes=[pltpu.VMEM((LANES, 2 * tq), F32), pltpu.VMEM((1, 2 * tq), F32),
                        pltpu.VMEM((1, 2 * tq), F32), pltpu.VMEM((2, tq, 2 * tq), F32),
                        pltpu.VMEM((2, tq, 2 * tq), BF16), pltpu.VMEM((2, 1, 2 * tq), F32)],
        compiler_params=_cparams(("arbitrary", "arbitrary")),
        name="attn_prompt",
    )(cfar, lam, fix, q_f, kb_t, vt_t, kb_m, vb_m, vbt_m, q_m, b0, bm1, bq0, bmm, subln_g)


DEC_HEADS = 4


def _attn_decode_kernel(lam_ref, q_ref, k_hbm, v_hbm, kn_ref, vn_ref, b_ref, g_ref, o_ref,
                        kbuf, vbuf, sem, *, layer, n_main, out_scale):
    bi = pl.program_id(0)
    hg = pl.program_id(1)
    n_hg = pl.num_programs(1)
    step = bi * n_hg + hg
    slot = step % 2

    def cache_copies(b_, hg_, slot_):
        cps = []
        for hh in range(DEC_HEADS):
            h_ = hg_ * DEC_HEADS + hh
            cps.append(pltpu.make_async_copy(k_hbm.at[layer, b_, :, h_, :], kbuf.at[slot_, hh],
                                             sem.at[0, slot_, hh]))
            cps.append(pltpu.make_async_copy(v_hbm.at[layer, b_, :, h_, :], vbuf.at[slot_, hh],
                                             sem.at[1, slot_, hh]))
        return cps

    @pl.when(step == 0)
    def _():
        for cp in cache_copies(bi, hg, slot):
            cp.start()

    @pl.when(step + 1 < pl.num_programs(0) * n_hg)
    def _():
        nxt = step + 1
        for cp in cache_copies(nxt // n_hg, nxt % n_hg, 1 - slot):
            cp.start()

    for cp in cache_copies(bi, hg, slot):
        cp.wait()

    lam = lam_ref[0]
    nq = q_ref.shape[0]
    n_cache = kbuf.shape[2]
    for hh in range(DEC_HEADS):
        cols = slice(LANES * hh, LANES * (hh + 1))
        k_ref = kbuf.at[slot, hh]
        v_ref = vbuf.at[slot, hh]
        qst = _stack_components(q_ref[:, cols])
        k_main = k_ref[0:n_main, :].astype(BF16)
        v_main = v_ref[0:n_main, :].astype(BF16)
        k_tail = jnp.concatenate([k_ref[n_main:n_cache, :], kn_ref[:, cols]], axis=0).astype(BF16)
        v_tail = jnp.concatenate([v_ref[n_main:n_cache, :], vn_ref[:, cols]], axis=0).astype(BF16)
        bias = jnp.concatenate([b_ref[hh], b_ref[hh]], axis=0)
        s1 = _dot_nt(qst, k_main) + bias[:, 0:n_main]
        s2 = _dot_nt(qst, k_tail) + bias[:, n_main:]
        m = jnp.maximum(jnp.max(s1, axis=-1, keepdims=True), jnp.max(s2, axis=-1, keepdims=True))
        p1 = jnp.exp2(s1 - m)
        p2 = jnp.exp2(s2 - m)
        l = jnp.sum(p1, axis=-1, keepdims=True) + jnp.sum(p2, axis=-1, keepdims=True)
        res = (_dot(p1.astype(BF16), v_main) + _dot(p2.astype(BF16), v_tail)) / l
        o = res[0:nq] - lam * res[nq:]
        o_ref[:, cols] = _sub_layer_norm(o, g_ref[...], out_scale).astype(BF16)


def _attn_decode(lam, q_s, cache_k, cache_v, layer, k_s, v_s, bias, subln_g, out_scale, nq):
    _, b, n_cache, n_h, _ = cache_k.shape
    assert n_h % DEC_HEADS == 0
    n_main = (n_cache // LANES) * LANES
    n_keys = n_cache + nq
    smem = pl.BlockSpec(memory_space=pltpu.SMEM)
    hbm = pl.BlockSpec(memory_space=pl.ANY)
    new = pl.BlockSpec((nq, DEC_HEADS * LANES), lambda bi, hg: (bi, hg))
    buf = pltpu.VMEM((2, DEC_HEADS, n_cache, LANES), F32)
    return pl.pallas_call(
        functools.partial(_attn_decode_kernel, layer=layer, n_main=n_main, out_scale=out_scale),
        grid=(b, n_h // DEC_HEADS),
        in_specs=[smem, new, hbm, hbm, new, new,
                  pl.BlockSpec((DEC_HEADS, nq, n_keys), lambda bi, hg: (hg, 0, 0)),
                  _const_spec((1, LANES))],
        out_specs=new,
        out_shape=jax.ShapeDtypeStruct((b * nq, DA_WIDTH), BF16),
        scratch_shapes=[buf, buf, pltpu.SemaphoreType.DMA((2, 2, DEC_HEADS))],
        compiler_params=_cparams(("arbitrary", "arbitrary")),
        name="attn_decode",
    )(lam, q_s, cache_k, cache_v, k_s, v_s, bias, subln_g)


def _sigmoid(x):
    return 1.0 / (1.0 + jnp.exp(-x))


def _split2(x):
    x1 = x.astype(BF16)
    x2 = (x - x1.astype(F32)).astype(BF16)
    return x1, x2


def _stack_heads(x):
    lo = lax.broadcasted_iota(jnp.int32, x.shape, 1) < HEAD_DIM
    zero = jnp.zeros_like(x)
    return jnp.concatenate([jnp.where(lo, x, zero), jnp.where(lo, zero, x)], axis=0)


def _rwkv_kernel(pr_ref, sh0_ref, h0_ref, mu_ref, w0_ref, a0_ref, kk_ref, ka_ref, rk_ref,
                 lw_ref, lb_ref, w2_ref, a2_ref, g2_ref, gsum_ref, tri_ref,
                 o_ref, sn_ref,
                 h_st, xbuf, carry, at_s, rt_s, bt_s, kt_s, v_s, ep_s, y_s,
                 t_s, tav_s, lrb_s, lrkv_s, zbt_s, zkv_s, ar_s, pc_s,
                 *, tr, t_valid, n_state):
    c_len = RW_CHUNK
    n_ch = tr // c_len
    ti = pl.program_id(1)

    @pl.when(ti == 0)
    def _():
        has_state = pl.program_id(0) < n_state
        carry[...] = jnp.where(has_state, sh0_ref[0], 0.0)
        hd_r = lax.broadcasted_iota(jnp.int32, (PAIR, PAIR), 0) // HEAD_DIM
        hd_c = lax.broadcasted_iota(jnp.int32, (PAIR, PAIR), 1) // HEAD_DIM
        for p in range(N_PAIR):
            s_p = h0_ref[0, p]
            blk = jnp.where((hd_r == hd_c) & has_state, jnp.concatenate([s_p, s_p], axis=1), 0.0)
            h_st[p] = jnp.transpose(blk)

    x = pr_ref[0]
    xbuf[8:8 + tr, :] = x
    xbuf[7:8, :] = carry[...]
    prev = xbuf[7:7 + tr, :]
    carry[...] = pr_ref[0, tr - 1:tr, :]
    xm = x + (prev - x) * mu_ref[...]
    r = xm[:, 0:RW_WIDTH]
    kr = xm[:, RW_WIDTH:2 * RW_WIDTH]
    vr = xm[:, 2 * RW_WIDTH:3 * RW_WIDTH]
    wa = xm[:, 3 * RW_WIDTH:3 * RW_WIDTH + W_LORA + A_LORA]
    gd = xm[:, 3 * RW_WIDTH + W_LORA + A_LORA:]
    lane_wa = lax.broadcasted_iota(jnp.int32, wa.shape, 1)
    twa = jnp.where(lane_wa < W_LORA, jnp.tanh(wa), wa).astype(BF16)
    logw = -DECAY_SCALE * _sigmoid(w0_ref[...] + _dot(twa, w2_ref[...]))
    a_sig = _sigmoid(a0_ref[...] + _dot(twa, a2_ref[...]))
    g = _dot(_sigmoid(gd).astype(BF16), g2_ref[...])
    gsum = gsum_ref[...]
    kk = kr * kk_ref[...]
    kk = kk * lax.rsqrt(jnp.maximum(_group_sum(kk * kk, gsum), 1e-24))
    kr2 = kr * (1.0 + (a_sig - 1.0) * ka_ref[...])
    a_vec = -kk
    b_vec = kk * a_sig
    bonus = _group_sum(r * kr2 * rk_ref[...], gsum) * vr
    if t_valid % tr != 0:
        row = lax.broadcasted_iota(jnp.int32, (tr, 1), 0) + ti * tr
        valid = row < t_valid
        logw = jnp.where(valid, logw, 0.0)
        a_vec = jnp.where(valid, a_vec, 0.0)
        b_vec = jnp.where(valid, b_vec, 0.0)
        kr2 = jnp.where(valid, kr2, 0.0)
        vr = jnp.where(valid, vr, 0.0)
        bonus = jnp.where(valid, bonus, 0.0)
    l1, l2 = _split2(logw)
    tri = tri_ref[...]
    cs = _dot(tri, l1) + _dot(tri, l2)
    e_pos = jnp.exp(cs)
    e_neg = jnp.exp(-cs)
    at_s[...] = a_vec * jnp.exp(cs - logw)
    rt_s[...] = r * e_pos
    bt_s[...] = b_vec * e_neg
    kt_s[...] = kr2 * e_neg
    v_s[...] = vr
    ep_s[...] = e_pos

    idx_r = lax.broadcasted_iota(jnp.int32, (PAIR, PAIR), 0)
    idx_c = lax.broadcasted_iota(jnp.int32, (PAIR, PAIR), 1)
    same = (idx_r // c_len) == (idx_c // c_len)
    strict = same & ((idx_r % c_len) > (idx_c % c_len))
    incl = same & ((idx_r % c_len) >= (idx_c % c_len))
    eye = idx_r == idx_c
    eye_f = jnp.where(eye, 1.0, 0.0).astype(F32)

    zero = jnp.zeros((PAIR, PAIR), F32)
    n_par = next(n for n in (4, 2, 1) if n_ch % n == 0)
    n_grp = n_ch // n_par
    items = [(dc, p) for dc in range(n_par) for p in range(N_PAIR)]
    pairs = range(N_PAIR)

    def s1_load(g):
        sel = [(pl.ds(pl.multiple_of((g * n_par + dc) * c_len, c_len), c_len),
                slice(PAIR * p, PAIR * (p + 1))) for dc, p in items]
        return dict(at=[at_s[rw, cl] for rw, cl in sel], rt=[rt_s[rw, cl] for rw, cl in sel],
                    bt=[bt_s[rw, cl] for rw, cl in sel], kt=[kt_s[rw, cl] for rw, cl in sel],
                    v=[v_s[rw, cl] for rw, cl in sel], ep=[ep_s[rw, cl] for rw, cl in sel])

    def s1_compute(ld):
        q_all = range(len(items))
        at, rt = ld["at"], ld["rt"]
        yb = [_stack_heads(x) for x in ld["bt"]]
        yk = [_stack_heads(x) for x in ld["kt"]]
        vst = [_stack_heads(x).astype(BF16) for x in ld["v"]]
        pc = [x[c_len - 1:c_len] for x in ld["ep"]]
        gmat = [_dot_nt(jnp.concatenate([_stack_heads(at[q]), _stack_heads(rt[q])], axis=0).astype(BF16),
                        jnp.concatenate([yb[q], yk[q]], axis=0).astype(BF16)) for q in q_all]
        aab = [jnp.where(strict, gmat[q][0:PAIR, 0:PAIR], zero) for q in q_all]
        aak = [jnp.where(strict, gmat[q][0:PAIR, PAIR:], zero).astype(BF16) for q in q_all]
        lrb = [jnp.where(incl, gmat[q][PAIR:, 0:PAIR], zero).astype(BF16) for q in q_all]
        lrk = [jnp.where(incl, gmat[q][PAIR:, PAIR:], zero).astype(BF16) for q in q_all]
        tinv = [eye_f + aab[q] for q in q_all]
        lp = aab
        n = 1
        while 2 * n < c_len:
            lpb = [x.astype(BF16) for x in lp]
            lp = [_dot(x, x) for x in lpb]
            tinv = [tinv[q] + _dot(tinv[q].astype(BF16), lp[q].astype(BF16)) for q in q_all]
            n *= 2
        tb = [x.astype(BF16) for x in tinv]
        av = [_dot(aak[q], vst[q]).astype(BF16) for q in q_all]
        return dict(
            t=tb, tav=[_dot(tb[q], av[q]) for q in q_all], lrb=lrb,
            lrkv=[_dot(lrk[q], vst[q]) for q in q_all],
            zbt=[jnp.transpose(yb[q] * pc[q]).astype(BF16) for q in q_all],
            zkv=[_dot(jnp.transpose(yk[q] * pc[q]).astype(BF16), vst[q]) for q in q_all],
            ar=[jnp.concatenate([at[q], rt[q]], axis=0).astype(BF16) for q in q_all],
            pc=[jnp.sum(jnp.where(eye, jnp.broadcast_to(pc[q], (PAIR, PAIR)), zero), axis=-1, keepdims=True)
                for q in q_all])

    stage_bufs = dict(t=t_s, tav=tav_s, lrb=lrb_s, lrkv=lrkv_s, zbt=zbt_s, zkv=zkv_s, ar=ar_s, pc=pc_s)

    def s1_store(g, res):
        for q, (dc, p) in enumerate(items):
            for name, buf in stage_bufs.items():
                buf[g * n_par + dc, p] = res[name][q]

    def s2_load(g):
        return [{name: [buf[g * n_par + dc, p] for p in pairs] for name, buf in stage_bufs.items()}
                for dc in range(n_par)]

    def s2_compute(ld_group, hbd):
        ys = []
        for ld in ld_group:
            arh = [_dot(ld["ar"][p], hbd[p].astype(BF16)) for p in pairs]
            ub = [(_dot(ld["t"][p], _stack_heads(arh[p][0:c_len]).astype(BF16)) + ld["tav"][p]).astype(BF16)
                  for p in pairs]
            yst = [_dot(ld["lrb"][p], ub[p]) + ld["lrkv"][p] for p in pairs]
            ys.append([arh[p][c_len:] + yst[p][0:c_len] + yst[p][c_len:] for p in pairs])
            hbd = [ld["pc"][p] * hbd[p] + _dot(ld["zbt"][p], ub[p]) + ld["zkv"][p] for p in pairs]
        return ys, hbd

    def s2_store(g, ys, hbd):
        for dc in range(n_par):
            rows = pl.ds(pl.multiple_of((g * n_par + dc) * c_len, c_len), c_len)
            for p in pairs:
                y_s[rows, PAIR * p:PAIR * (p + 1)] = ys[dc][p]
        for p in pairs:
            h_st[p] = hbd[p]

    s1_store(0, s1_compute(s1_load(0)))

    def body(g, carry_):
        ld2 = s2_load(g)
        ld1 = s1_load(g + 1)
        hbd = [h_st[p] for p in pairs]
        res1 = s1_compute(ld1)
        ys, hbd = s2_compute(ld2, hbd)
        s2_store(g, ys, hbd)
        s1_store(g + 1, res1)
        return carry_

    lax.fori_loop(0, n_grp - 1, body, 0)
    ys_last, h_last = s2_compute(s2_load(n_grp - 1), [h_st[p] for p in pairs])
    s2_store(n_grp - 1, ys_last, h_last)

    @pl.when(ti == pl.num_programs(1) - 1)
    def _():
        first = lax.broadcasted_iota(jnp.int32, (PAIR, HEAD_DIM), 0) < HEAD_DIM
        for p in range(N_PAIR):
            s_t = jnp.transpose(h_st[p])
            sn_ref[0, p] = jnp.where(first, s_t[:, 0:HEAD_DIM], s_t[:, HEAD_DIM:])

    y = y_s[...]
    inv_n = 1.0 / HEAD_DIM
    mean = _group_sum(y, gsum) * inv_n
    d = y - mean
    var = _group_sum(d * d, gsum) * inv_n
    yn = d * lax.rsqrt(var + GN_EPS) * lw_ref[...] + lb_ref[...]
    o_ref[0] = ((yn + bonus) * g).astype(BF16)


def _rwkv(pr, shift0, s0, prm, tr, t_valid):
    b, t_pad, _ = pr.shape
    n_state = s0.shape[0]
    assert shift0.shape[0] == n_state
    last = n_state - 1
    assert t_pad % tr == 0 and tr % RW_CHUNK == 0
    n_ch = tr // RW_CHUNK
    tri = np.zeros((tr, tr), np.float32)
    for c in range(n_ch):
        tri[c * RW_CHUNK:(c + 1) * RW_CHUNK, c * RW_CHUNK:(c + 1) * RW_CHUNK] = np.tril(
            np.ones((RW_CHUNK, RW_CHUNK), np.float32))
    tri = jnp.asarray(tri, BF16)
    vec = _const_spec((1, RW_WIDTH))
    mat = lambda dt: pltpu.VMEM((n_ch, N_PAIR, PAIR, PAIR), dt)
    tile = lambda: pltpu.VMEM((tr, RW_WIDTH), F32)
    return pl.pallas_call(
        functools.partial(_rwkv_kernel, tr=tr, t_valid=t_valid, n_state=n_state),
        grid=(b, t_pad // tr),
        in_specs=[pl.BlockSpec((1, tr, RW_PROJ), lambda bi, ti: (bi, ti, 0)),
                  pl.BlockSpec((1, 1, RW_PROJ), lambda bi, ti: (jnp.minimum(bi, last), 0, 0)),
                  pl.BlockSpec((1, N_PAIR, PAIR, HEAD_DIM),
                               lambda bi, ti: (jnp.minimum(bi, last), 0, 0, 0)),
                  _const_spec((1, RW_PROJ)), vec, vec, vec, vec, vec, vec, vec,
                  _const_spec((W_LORA + A_LORA, RW_WIDTH)), _const_spec((W_LORA + A_LORA, RW_WIDTH)),
                  _const_spec((G_LORA, RW_WIDTH)), _const_spec((LANES, LANES)),
                  _const_spec((tr, tr))],
        out_specs=[pl.BlockSpec((1, tr, RW_WIDTH), lambda bi, ti: (bi, ti, 0)),
                   pl.BlockSpec((1, N_PAIR, PAIR, HEAD_DIM), lambda bi, ti: (bi, 0, 0, 0))],
        out_shape=[jax.ShapeDtypeStruct((b, t_pad, RW_WIDTH), BF16),
                   jax.ShapeDtypeStruct((b, N_PAIR, PAIR, HEAD_DIM), F32)],
        scratch_shapes=[pltpu.VMEM((N_PAIR, PAIR, PAIR), F32),
                        pltpu.VMEM((8 + tr, RW_PROJ), F32), pltpu.VMEM((1, RW_PROJ), F32),
                        tile(), tile(), tile(), tile(), tile(), tile(), tile(),
                        mat(BF16), mat(F32), mat(BF16), mat(F32), mat(BF16), mat(F32), mat(BF16),
                        pltpu.VMEM((n_ch, N_PAIR, PAIR, 1), F32)],
        compiler_params=_cparams(("arbitrary", "arbitrary")),
        name="rwkv",
    )(pr, shift0, s0, prm["mu"], prm["w0"], prm["a0"], prm["k_k"], prm["k_a"], prm["r_k"],
      prm["lnx_w"], prm["lnx_b"], prm["w2p"], prm["a2p"], prm["g2"], prm["gsum"], tri)


FF_COLS = 256


def _ffn_kernel(x_ref, oa_ref, orw_ref, c0_ref, wo_ref, g_ref, wu_ref, cw_ref, cb_ref, wd_ref,
                y_ref, cn_ref, zbuf, cbuf, h_ref, x1_ref, *, tm, ts, d_ff):
    off = zbuf.shape[1] - tm
    da = oa_ref.shape[1]

    @pl.when(pl.program_id(0) == 0)
    def _():
        cbuf[...] = c0_ref[...]

    x1 = x_ref[...] + _dot(oa_ref[...], wo_ref[0:da, :]) + _dot(orw_ref[...], wo_ref[da:, :])
    x1_ref[...] = x1
    ms = jnp.mean(x1 * x1, axis=-1, keepdims=True)
    h_ref[...] = (x1 * lax.rsqrt(ms + NORM_EPS) * g_ref[...]).astype(BF16)

    def up_proj(c0, zb):
        cols = slice(c0, c0 + FF_COLS)
        zb[off - 2 * ts:off, :] = cbuf[:, cols]
        zb[off:off + tm, :] = _dot(h_ref[...], wu_ref[:, cols])
        cbuf[:, cols] = zb[off + tm - 2 * ts:off + tm, :]

    def conv_cols(c0, zb):
        cols = slice(c0, c0 + FF_COLS)
        z2 = zb[off - 2 * ts:off - 2 * ts + tm, :]
        z1 = zb[off - ts:off - ts + tm, :]
        z = zb[off:off + tm, :]
        return (cb_ref[:, cols] + z2 * cw_ref[0:1, cols] + z1 * cw_ref[1:2, cols]
                + z * cw_ref[2:3, cols])

    def stage(c):
        up_proj(c * FF_COLS, zbuf.at[2 * (c % 2)])
        up_proj(d_ff + c * FF_COLS, zbuf.at[2 * (c % 2) + 1])

    n_chunks = d_ff // FF_COLS
    acc = None
    stage(0)
    for c in range(n_chunks):
        if c + 1 < n_chunks:
            stage(c + 1)
        gate = conv_cols(c * FF_COLS, zbuf.at[2 * (c % 2)])
        up = conv_cols(d_ff + c * FF_COLS, zbuf.at[2 * (c % 2) + 1])
        act = (gate * _sigmoid(gate) * up).astype(BF16)
        part = _dot(act, wd_ref[c * FF_COLS:(c + 1) * FF_COLS, :])
        acc = part if acc is None else acc + part

    y_ref[...] = x1_ref[...] + acc
    cn_ref[...] = cbuf[...]


def _ffn(x, oa, orw, conv0, prm, tm, ts):
    rows, d = x.shape
    d_ff = prm["w_down"].shape[0]
    assert rows % tm == 0 and d_ff % FF_COLS == 0 and (ts == 1 or ts % 8 == 0)
    off = -(-2 * ts // 8) * 8
    row = lambda w: pl.BlockSpec((tm, w), lambda i: (i, 0))
    return pl.pallas_call(
        functools.partial(_ffn_kernel, tm=tm, ts=ts, d_ff=d_ff),
        grid=(rows // tm,),
        in_specs=[row(d), row(DA_WIDTH), row(RW_WIDTH), _const_spec((2 * ts, 2 * d_ff)),
                  _resident_spec((DA_WIDTH + RW_WIDTH, d)), _const_spec((1, d)),
                  _resident_spec((d, 2 * d_ff)), _const_spec((CONV_W, 2 * d_ff)),
                  _const_spec((1, 2 * d_ff)), _resident_spec((d_ff, d))],
        out_specs=[row(d), _const_spec((2 * ts, 2 * d_ff))],
        out_shape=[jax.ShapeDtypeStruct((rows, d), F32),
                   jax.ShapeDtypeStruct((2 * ts, 2 * d_ff), F32)],
        scratch_shapes=[pltpu.VMEM((4, off + tm, FF_COLS), F32), pltpu.VMEM((2 * ts, 2 * d_ff), F32),
                        pltpu.VMEM((tm, d), BF16), pltpu.VMEM((tm, d), F32)],
        compiler_params=_cparams(("arbitrary",)),
        name="ffn",
    )(x, oa, orw, conv0, prm["w_out"], prm["ln2_g"], prm["w_up"], prm["conv_w"], prm["conv_b"],
      prm["w_down"])


def _rel_bucket(rel):
    nb = N_BUCKETS // 2
    max_exact = nb // 2
    bucket = jnp.where(rel > 0, nb, 0)
    n = jnp.abs(rel)
    nf = jnp.maximum(n, 1).astype(F32)
    large = max_exact + (jnp.log(nf / max_exact) / math.log(MAX_DISTANCE / max_exact)
                         * (nb - max_exact)).astype(jnp.int32)
    large = jnp.minimum(large, nb - 1)
    return bucket + jnp.where(n < max_exact, n, large)


def _bias_table(rel_bias, q_pos, k_pos, mask):
    n_q, n_k = len(q_pos), len(k_pos)
    assert np.all(np.diff(q_pos) == 1) and np.all(np.diff(k_pos) == 1)
    span = n_q + n_k - 1
    rel = jnp.asarray(int(k_pos[0]) - int(q_pos[0]) - (n_q - 1) + np.arange(span), jnp.int32)
    w = jnp.transpose(rel_bias[_rel_bucket(rel)]).astype(F32) * LOG2E
    return jnp.where(jnp.asarray(mask)[None], _toeplitz(w, n_q, n_k), NEG_INF)


def _skew(w, n_q, n_k):
    span = n_q + n_k - 1
    x = jnp.pad(w[:, :span], ((0, 0), (0, 1)))
    rows = jnp.tile(x, (1, n_q))[:, :n_q * span].reshape(w.shape[0], n_q, span)
    return rows[:, :, n_q - 1:n_q - 1 + n_k]


def _toeplitz(w, n_q, n_k):
    blk = LANES
    if n_q % blk or n_k % blk or n_q * n_k <= blk * blk:
        return _skew(w, n_q, n_k)
    nbq, nbk = n_q // blk, n_k // blk
    n_d = nbq + nbk - 1
    n_h = w.shape[0]
    wp = jnp.pad(w[:, :n_q + n_k - 1], ((0, 0), (0, 1)))
    seg = jnp.concatenate([wp[:, :n_d * blk].reshape(n_h, n_d, blk),
                           wp[:, blk:].reshape(n_h, n_d, blk)[..., :blk - 1]], axis=-1)
    x = jnp.pad(seg, ((0, 0), (0, 0), (0, 1)))
    rows = jnp.tile(x, (1, 1, blk))[..., :blk * (2 * blk - 1)].reshape(n_h, n_d, blk, 2 * blk - 1)
    blocks = rows[..., blk - 1:]
    return jnp.concatenate([jnp.concatenate([blocks[:, c - r + nbq - 1] for c in range(nbk)], axis=2)
                            for r in range(nbq)], axis=1)


BOUND_MAX_SPREAD = 100.0


def _score_bound(q_g, k_g, rel_bias):
    s_max = (HEAD_DIM ** 0.5) * LOG2E * 1.01 * jnp.max(jnp.abs(q_g)) * jnp.max(jnp.abs(k_g))
    b_hi = jnp.max(rel_bias) * LOG2E
    b_lo = jnp.min(rel_bias) * LOG2E
    use = (2.0 * s_max + (b_hi - b_lo)) <= BOUND_MAX_SPREAD
    return jnp.stack([use.astype(F32), (s_max + b_hi).astype(F32)])


def _ext_chunk(pos):
    return np.where(pos < N_META, -1, (pos - N_META) // CHUNK)


def _prompt_bias(rel_bias):
    tq = ATT_T
    fr = np.arange(tq) + N_META
    meta = np.arange(N_META)
    causal = _ext_chunk(fr)[None, :] <= _ext_chunk(fr)[:, None]
    tr = lambda b: jnp.swapaxes(b, 1, 2)
    b0 = tr(_bias_table(rel_bias, fr, fr, causal))
    bm1 = tr(_bias_table(rel_bias, fr + tq, fr, np.ones((tq, tq), bool)))
    bq0 = tr(_bias_table(rel_bias, fr, meta, np.ones((tq, N_META), bool)))
    bmm = _bias_table(rel_bias, meta, meta, np.ones((N_META, N_META), bool))
    assert tq + 1 >= MAX_DISTANCE
    cfar = rel_bias[_rel_bucket(jnp.asarray(-(tq + 1), jnp.int32))].astype(F32) * LOG2E
    return cfar, b0, bm1, bq0, bmm


def _decode_bias(rel_bias, n_cache, nq):
    k_pos = np.arange(n_cache + nq)
    q_pos = k_pos[n_cache:]
    mask = _ext_chunk(k_pos)[None, :] <= _ext_chunk(q_pos)[:, None]
    return _bias_table(rel_bias, q_pos, k_pos, mask)


def _block_ones(n, blk, dtype):
    idx = np.arange(n) // blk
    return jnp.asarray((idx[:, None] == idx[None, :]).astype(np.float32), dtype)


def kernel(x_prompt, x_sample, cache_k, cache_v, state_rwkv, state_shift, state_conv, meta_tokens,
           rel_bias, ln1_g, w_in, q_norm_g, k_norm_g, lam_q1, lam_k1, lam_q2, lam_k2, subln_g,
           mu_shift, w0, w2, a0, a2, g2, k_k, k_a, r_k, lnx_w, lnx_b, w_out, ln2_g, w_up, conv_w,
           conv_b, w_down):
    bp, seq, d = x_prompt.shape
    db, dt, _ = x_sample.shape
    depth = w_in.shape[0]
    d_ff = w_down.shape[1]
    n_cache = cache_k.shape[2]
    assert bp == 1 and dt == N_META, "the meta stream rides with the decode streams"
    assert cache_k.shape[3] == DA_HEADS and cache_k.shape[4] == 2 * HEAD_DIM
    nb = db + 1
    nb_pad = -(-nb // 8) * 8

    cfar, b0, bm1, bq0, bmm = _prompt_bias(rel_bias)
    bias_dec = _decode_bias(rel_bias, n_cache, dt)
    gsum = _block_ones(LANES, HEAD_DIM, BF16)
    gmean = gsum * (1.0 / HEAD_DIM)
    zrow = lambda n: jnp.zeros((n, RW_WIDTH), BF16)

    x_f = x_prompt[0]
    x_s = jnp.concatenate([x_sample, meta_tokens.astype(x_sample.dtype)[None]], axis=0)
    outs = [[] for _ in range(10)]
    for l in range(depth):
        lam_init = 0.8 - 0.6 * math.exp(-0.3 * l)
        lam = (jnp.exp(jnp.sum(lam_q1[l].astype(F32) * lam_k1[l].astype(F32)))
               - jnp.exp(jnp.sum(lam_q2[l].astype(F32) * lam_k2[l].astype(F32))) + lam_init).reshape(1)
        out_scale = 1.0 - lam_init
        tile128 = lambda g_: jnp.tile(g_.reshape(1, -1), (1, DA_WIDTH // g_.shape[-1]))
        qg, kg = tile128(q_norm_g[l]), tile128(k_norm_g[l])
        sg = subln_g[l].reshape(1, LANES)
        w_in_bf = w_in[l].astype(BF16)
        rw = {
            "mu": mu_shift[l].reshape(1, -1), "w0": w0[l].reshape(1, -1), "a0": a0[l].reshape(1, -1),
            "k_k": k_k[l].reshape(1, -1), "k_a": k_a[l].reshape(1, -1), "r_k": r_k[l].reshape(1, -1),
            "lnx_w": lnx_w[l].reshape(1, -1), "lnx_b": lnx_b[l].reshape(1, -1),
            "w2p": jnp.concatenate([w2[l].astype(BF16), zrow(A_LORA)], axis=0),
            "a2p": jnp.concatenate([zrow(W_LORA), a2[l].astype(BF16)], axis=0),
            "g2": g2[l].astype(BF16), "gsum": gsum,
        }
        ff = {
            "w_out": w_out[l].astype(BF16), "ln2_g": ln2_g[l].reshape(1, -1),
            "w_up": w_up[l].astype(BF16), "conv_w": conv_w[l], "conv_b": conv_b[l].reshape(1, -1),
            "w_down": w_down[l].astype(BF16),
        }

        q_f, k_f, v_f, pr_f, kb_t, vt_t = _proj(x_f, ln1_g[l].reshape(1, -1), w_in_bf, qg, kg, gmean,
                                                 512, True)
        q_s, k_s, v_s, pr_s = _proj(x_s.reshape(nb * dt, d), ln1_g[l].reshape(1, -1), w_in_bf, qg, kg,
                                    gmean, nb * dt, False)
        m0 = db * dt

        kb_m = k_s[m0:].astype(BF16)
        vb_m = v_s[m0:].astype(BF16)
        o_f, o_m = _attn_prompt(cfar, lam, _score_bound(q_norm_g[l], k_norm_g[l], rel_bias), q_f, kb_t, vt_t, kb_m, vb_m, jnp.transpose(vb_m), q_s[m0:],
                                b0, bm1, bq0, bmm, sg, out_scale)
        o_d = _attn_decode(lam, q_s, cache_k, cache_v, l, k_s, v_s, bias_dec, sg, out_scale, dt)
        o_s = jnp.concatenate([o_d, o_m], axis=0)

        pr_s3 = pr_s.reshape(nb, dt, RW_PROJ)
        pr_pad = jnp.pad(pr_s3, ((0, 0), (0, RW_CHUNK - dt), (0, 0)))
        pair_view = (N_PAIR, PAIR, HEAD_DIM)
        orw_s, sn_s = _rwkv(pr_pad, state_shift[l][:, None, :], state_rwkv[l].reshape(db, *pair_view),
                            rw, RW_CHUNK, dt)
        orw_f, sn_f = _rwkv(pr_f[None], pr_s3[db:, dt - 1:dt, :], sn_s[db:], rw, 512, seq)

        def time_major(a):
            a = jnp.pad(a.reshape(nb, dt, -1), ((0, nb_pad - nb), (0, 0), (0, 0)))
            return jnp.swapaxes(a, 0, 1).reshape(dt * nb_pad, -1)

        conv_s = jnp.concatenate([state_conv[l], jnp.zeros((1, CONV_W - 1, 2 * d_ff), F32)], axis=0)
        conv_s = jnp.pad(conv_s, ((0, nb_pad - nb), (0, 0), (0, 0)))
        conv_s = jnp.swapaxes(conv_s, 0, 1).reshape(2 * nb_pad, 2 * d_ff)
        y_s, cn_s = _ffn(time_major(x_s), time_major(o_s), time_major(orw_s[:, :dt]), conv_s, ff,
                         dt * nb_pad, nb_pad)
        cn_s = jnp.swapaxes(cn_s.reshape(2, nb_pad, 2 * d_ff), 0, 1)
        y_f, cn_f = _ffn(x_f, o_f, orw_f[0], cn_s[db], ff, 512, 1)
        y_s = jnp.swapaxes(y_s.reshape(dt, nb_pad, d), 0, 1)[:nb]

        hw = (DA_HEADS, 2 * HEAD_DIM)
        outs[0].append(jnp.concatenate([k_s[m0:].reshape(N_META, *hw), k_f], axis=0)[None])
        outs[1].append(jnp.concatenate([v_s[m0:].reshape(N_META, *hw), v_f], axis=0)[None])
        outs[2].append(sn_f.reshape(bp, RW_HEADS, HEAD_DIM, HEAD_DIM))
        outs[3].append(pr_f[seq - 1:seq])
        outs[4].append(cn_f[None])
        outs[5].append(k_s[:m0].reshape(db, dt, *hw))
        outs[6].append(v_s[:m0].reshape(db, dt, *hw))
        outs[7].append(sn_s[:db].reshape(db, RW_HEADS, HEAD_DIM, HEAD_DIM))
        outs[8].append(pr_s3[:db, dt - 1])
        outs[9].append(cn_s[:db])
        x_f, x_s = y_f, y_s

    return (x_f[None], x_s[:db], *[jnp.stack(o) for o in outs])
```

```python
import functools
import math

import numpy as np
import jax
import jax.numpy as jnp
from jax import lax
from jax.experimental import pallas as pl
from jax.experimental.pallas import tpu as pltpu

F32 = jnp.float32
BF16 = jnp.bfloat16

CHUNK = 64
N_META = 16
HEAD_DIM = 64
DA_HEADS = 4
RW_HEADS = 8
W_LORA = 64
A_LORA = 64
G_LORA = 128
CONV_W = 3
N_BUCKETS = 32
MAX_DISTANCE = 128
NORM_EPS = 1e-6
GN_EPS = 64e-5
NEG_INF = -1e30
LOG2E = math.log2(math.e)
DECAY_SCALE = math.exp(-0.5)

DA_WIDTH = DA_HEADS * 2 * HEAD_DIM
RW_WIDTH = RW_HEADS * HEAD_DIM
RW_PROJ = 3 * RW_WIDTH + W_LORA + A_LORA + G_LORA
LANES = 128
PAIR = 2 * HEAD_DIM
N_PAIR = RW_WIDTH // PAIR
RW_CHUNK = 64
ATT_T = 512
VMEM_LIMIT = 56 * 1024 * 1024


def _dot(a, b):
    return jnp.dot(a, b, preferred_element_type=F32)


def _dot_nt(a, b):
    return lax.dot_general(a, b, (((1,), (1,)), ((), ())), preferred_element_type=F32)


def _group_sum(x, blk):
    xb = x.astype(BF16)
    return jnp.concatenate([_dot(xb[:, LANES * p:LANES * (p + 1)], blk)
                            for p in range(x.shape[1] // LANES)], axis=1)


def _cparams(sem):
    return pltpu.CompilerParams(dimension_semantics=sem, vmem_limit_bytes=VMEM_LIMIT)


def _const_spec(shape):
    nd = len(shape)
    return pl.BlockSpec(shape, lambda *_: (0,) * nd)


def _resident_spec(shape):
    nd = len(shape)
    return pl.BlockSpec(shape, lambda *_: (0,) * nd, pipeline_mode=pl.Buffered(1))


def _proj_kernel(x_ref, g_ref, w_ref, qg_ref, kg_ref, gm_ref,
                 q_ref, k_ref, v_ref, pr_ref, *tile_refs, tm):
    x = x_ref[...]
    ms = jnp.mean(x * x, axis=-1, keepdims=True)
    h = (x * lax.rsqrt(ms + NORM_EPS) * g_ref[...]).astype(BF16)
    gm = gm_ref[...]

    def group_norm(t, g):
        ms_g = _group_sum(t * t, gm)
        return t * lax.rsqrt(ms_g + NORM_EPS) * g

    q = _dot(h, w_ref[:, 0:DA_WIDTH])
    k = _dot(h, w_ref[:, DA_WIDTH:2 * DA_WIDTH])
    pr_ref[...] = _dot(h, w_ref[:, 3 * DA_WIDTH:])
    v = _dot(h, w_ref[:, 2 * DA_WIDTH:3 * DA_WIDTH])
    q_ref[...] = (group_norm(q, qg_ref[...]) * (HEAD_DIM ** -0.5 * LOG2E)).astype(BF16)
    k = group_norm(k, kg_ref[...])
    if len(k_ref.shape) == 3:
        for hd in range(DA_HEADS):
            k_ref[:, hd, :] = k[:, hd * LANES:(hd + 1) * LANES]
            v_ref[:, hd, :] = v[:, hd * LANES:(hd + 1) * LANES]
    else:
        k_ref[...] = k
        v_ref[...] = v
    if tile_refs:
        kb_ref, vt_ref = tile_refs
        kb = k.astype(BF16)
        vt = jnp.transpose(v).astype(BF16)
        for hd in range(DA_HEADS):
            for jj in range(tm // ATT_T):
                kb_ref[hd, jj] = kb[jj * ATT_T:(jj + 1) * ATT_T, hd * LANES:(hd + 1) * LANES]
                vt_ref[hd, jj] = vt[hd * LANES:(hd + 1) * LANES, jj * ATT_T:(jj + 1) * ATT_T]


def _proj(x, ln1_g, w_in_bf, qg, kg, gmean, tm, emit_tiles):
    rows, d = x.shape
    n_in = w_in_bf.shape[1]
    assert rows % tm == 0
    row = lambda w: pl.BlockSpec((tm, w), lambda i: (i, 0))
    out_specs = [row(DA_WIDTH), row(DA_WIDTH), row(DA_WIDTH), row(RW_PROJ)]
    out_shape = [jax.ShapeDtypeStruct((rows, DA_WIDTH), BF16),
                 jax.ShapeDtypeStruct((rows, DA_WIDTH), F32),
                 jax.ShapeDtypeStruct((rows, DA_WIDTH), F32),
                 jax.ShapeDtypeStruct((rows, RW_PROJ), F32)]
    if emit_tiles:
        assert tm % ATT_T == 0
        tpt = tm // ATT_T
        for o in (1, 2):
            out_specs[o] = pl.BlockSpec((tm, DA_HEADS, LANES), lambda i: (i, 0, 0))
            out_shape[o] = jax.ShapeDtypeStruct((rows, DA_HEADS, LANES), F32)
        out_specs += [pl.BlockSpec((DA_HEADS, tpt, ATT_T, LANES), lambda i: (0, i, 0, 0)),
                      pl.BlockSpec((DA_HEADS, tpt, LANES, ATT_T), lambda i: (0, i, 0, 0))]
        out_shape += [jax.ShapeDtypeStruct((DA_HEADS, rows // ATT_T, ATT_T, LANES), BF16),
                      jax.ShapeDtypeStruct((DA_HEADS, rows // ATT_T, LANES, ATT_T), BF16)]
    return pl.pallas_call(
        functools.partial(_proj_kernel, tm=tm),
        grid=(rows // tm,),
        in_specs=[row(d), _const_spec((1, d)), _resident_spec((d, n_in)),
                  _const_spec((1, DA_WIDTH)), _const_spec((1, DA_WIDTH)),
                  _const_spec((LANES, LANES))],
        out_specs=out_specs,
        out_shape=out_shape,
        compiler_params=_cparams(("arbitrary",)),
        name="proj",
    )(x, ln1_g, w_in_bf, qg, kg, gmean)


def _stack_components(q):
    lo = lax.broadcasted_iota(jnp.int32, q.shape, 1) < HEAD_DIM
    zero = jnp.zeros_like(q)
    return jnp.concatenate([jnp.where(lo, q, zero), jnp.where(lo, zero, q)], axis=0)


def _sub_layer_norm(o, g, out_scale):
    ms = jnp.mean(o * o, axis=-1, keepdims=True)
    return o * lax.rsqrt(ms + NORM_EPS) * g * out_scale


def _attn_prompt_kernel(cfar_ref, lam_ref, fix_ref, q_ref, k_ref, vt_ref, km_ref, vm_ref, vmt_ref, qm_ref,
                        b0_ref, bm1_ref, bq0_ref, bmm_ref, g_ref, o_ref, om_ref,
                        acc_ref, m_ref, l_ref, s_buf, p_buf, a_buf, *, out_scale):
    h = pl.program_id(0)
    i = pl.program_id(1)
    tq = ATT_T
    cf = cfar_ref[h]
    lam = lam_ref[0]
    qst = _stack_components(q_ref[...])

    def both(b):
        return jnp.concatenate([b, b], axis=1)

    s = _dot_nt(km_ref[...], qst)
    s = s + both(jnp.where(i == 0, bq0_ref[0], cf))
    use_bound = fix_ref[0] > 0.5
    bound = fix_ref[1]
    m0 = jnp.where(use_bound, bound, jnp.max(s, axis=0, keepdims=True))
    p = jnp.exp2(s - m0)
    m_ref[...] = m0
    l_ref[...] = jnp.sum(p, axis=0, keepdims=True)
    acc_ref[...] = _dot(vmt_ref[...], p.astype(BF16))

    def scores(idx, slot):
        s_buf[slot] = _dot_nt(k_ref[0, idx], qst)

    def softmax(slot, bias):
        s = s_buf[slot]
        m_prev = m_ref[...]
        if bias.ndim == 0:
            m_new = jnp.maximum(m_prev, jnp.max(s, axis=0, keepdims=True) + bias)
            p = jnp.exp2(s - (m_new - bias))
        else:
            s = s + both(bias)
            m_new = jnp.maximum(m_prev, jnp.max(s, axis=0, keepdims=True))
            p = jnp.exp2(s - m_new)
        alpha = jnp.exp2(m_prev - m_new)
        l_ref[...] = alpha * l_ref[...] + jnp.sum(p, axis=0, keepdims=True)
        m_ref[...] = m_new
        p_buf[slot] = p.astype(BF16)
        a_buf[slot] = alpha

    def accumulate(idx, slot):
        acc_ref[...] = a_buf[slot] * acc_ref[...] + _dot(vt_ref[0, idx], p_buf[slot])

    n_far = jnp.maximum(i - 1, 0)
    off = n_far % 2
    p_buf[1] = jnp.zeros(p_buf.shape[1:], BF16)
    a_buf[1] = jnp.ones(a_buf.shape[1:], F32)
    scores(0, 0)

    def far_body(t, carry):
        u = 2 * t - off
        scores(u + 1, 1)
        softmax(0, jnp.where(u >= 0, cf, NEG_INF))
        accumulate(jnp.maximum(u - 1, 0), 1)
        scores(u + 2, 0)
        softmax(1, cf)
        accumulate(jnp.maximum(u, 0), 0)
        return carry

    def bound_softmax(slot, bias):
        if bias.ndim == 0:
            p = jnp.exp2(s_buf[slot] - (bound - bias))
        else:
            p = jnp.exp2(s_buf[slot] + (both(bias) - bound))
        l_ref[...] += jnp.sum(p, axis=0, keepdims=True)
        p_buf[slot] = p.astype(BF16)

    def bound_accumulate(idx, slot):
        acc_ref[...] += _dot(vt_ref[0, idx], p_buf[slot])

    def far_body_bound(t, carry):
        u = 2 * t - off
        scores(u + 1, 1)
        bound_softmax(0, jnp.where(u >= 0, cf, NEG_INF))
        bound_accumulate(jnp.maximum(u - 1, 0), 1)
        scores(u + 2, 0)
        bound_softmax(1, cf)
        bound_accumulate(jnp.maximum(u, 0), 0)
        return carry

    def tail(softmax_fn, accumulate_fn):
        scores(i, 1)
        softmax_fn(0, jnp.where(i >= 1, bm1_ref[0], NEG_INF))
        accumulate_fn(jnp.maximum(n_far - 1, 0), 1)
        softmax_fn(1, b0_ref[0])
        accumulate_fn(n_far, 0)
        accumulate_fn(i, 1)

    @pl.when(use_bound)
    def _():
        lax.fori_loop(0, (n_far + 1) // 2, far_body_bound, 0)
        tail(bound_softmax, bound_accumulate)

    @pl.when(jnp.logical_not(use_bound))
    def _():
        lax.fori_loop(0, (n_far + 1) // 2, far_body, 0)
        tail(softmax, accumulate)

    accn = acc_ref[...] * (1.0 / l_ref[...])
    o_t = accn[:, 0:tq] - lam * accn[:, tq:]
    o_ref[...] = _sub_layer_norm(jnp.transpose(o_t), g_ref[...], out_scale).astype(BF16)

    @pl.when(i == 0)
    def _():
        qm = _stack_components(qm_ref[...])
        bmm = bmm_ref[0]
        sm = _dot_nt(qm, km_ref[...]) + jnp.concatenate([bmm, bmm], axis=0)
        mm = jnp.max(sm, axis=-1, keepdims=True)
        pm = jnp.exp2(sm - mm)
        accm = _dot(pm.astype(BF16), vm_ref[...]) / jnp.sum(pm, axis=-1, keepdims=True)
        om = accm[0:N_META] - lam * accm[N_META:]
        om_ref[...] = _sub_layer_norm(om, g_ref[...], out_scale).astype(BF16)


def _attn_prompt(cfar, lam, fix, q_f, kb_t, vt_t, kb_m, vb_m, vbt_m, q_m, b0, bm1, bq0, bmm, subln_g,
                 out_scale):
    tf = q_f.shape[0]
    tq = ATT_T
    assert tf % tq == 0
    n_t = tf // tq
    smem = pl.BlockSpec(memory_space=pltpu.SMEM)
    head_col = lambda rows: pl.BlockSpec((rows, LANES), lambda h, i: (0, h))
    head_tile = lambda a, b: pl.BlockSpec((1, a, b), lambda h, i: (h, 0, 0))
    return pl.pallas_call(
        functools.partial(_attn_prompt_kernel, out_scale=out_scale),
        grid=(DA_HEADS, n_t),
        in_specs=[smem, smem, smem,
                  pl.BlockSpec((tq, LANES), lambda h, i: (i, h)),
                  pl.BlockSpec((1, n_t, tq, LANES), lambda h, i: (h, 0, 0, 0)),
                  pl.BlockSpec((1, n_t, LANES, tq), lambda h, i: (h, 0, 0, 0)),
                  head_col(N_META), head_col(N_META),
                  pl.BlockSpec((LANES, N_META), lambda h, i: (h, 0)),
                  head_col(N_META),
                  head_tile(tq, tq), head_tile(tq, tq), head_tile(N_META, tq),
                  head_tile(N_META, N_META), _const_spec((1, LANES))],
        out_specs=[pl.BlockSpec((tq, LANES), lambda h, i: (i, h)), head_col(N_META)],
        out_shape=[jax.ShapeDtypeStruct((tf, DA_WIDTH), BF16),
                   jax.ShapeDtypeStruct((N_META, DA_WIDTH), BF16)],
        scratch_shapes=[pltpu.VMEM((LANES, 2 * tq), F32), pltpu.VMEM((1, 2 * tq), F32),
                        pltpu.VMEM((1, 2 * tq), F32), pltpu.VMEM((2, tq, 2 * tq), F32),
                        pltpu.VMEM((2, tq, 2 * tq), BF16), pltpu.VMEM((2, 1, 2 * tq), F32)],
        compiler_params=_cparams(("arbitrary", "arbitrary")),
        name="attn_prompt",
    )(cfar, lam, fix, q_f, kb_t, vt_t, kb_m, vb_m, vbt_m, q_m, b0, bm1, bq0, bmm, subln_g)


DEC_HEADS = 4


def _attn_decode_kernel(lam_ref, q_ref, k_hbm, v_hbm, kn_ref, vn_ref, b_ref, g_ref, o_ref,
                        kbuf, vbuf, sem, *, layer, n_main, out_scale):
    bi = pl.program_id(0)
    hg = pl.program_id(1)
    n_hg = pl.num_programs(1)
    step = bi * n_hg + hg
    slot = step % 2

    def cache_copies(b_, hg_, slot_):
        cps = []
        for hh in range(DEC_HEADS):
            h_ = hg_ * DEC_HEADS + hh
            cps.append(pltpu.make_async_copy(k_hbm.at[layer, b_, :, h_, :], kbuf.at[slot_, hh],
                                             sem.at[0, slot_, hh]))
            cps.append(pltpu.make_async_copy(v_hbm.at[layer, b_, :, h_, :], vbuf.at[slot_, hh],
                                             sem.at[1, slot_, hh]))
        return cps

    @pl.when(step == 0)
    def _():
        for cp in cache_copies(bi, hg, slot):
            cp.start()

    @pl.when(step + 1 < pl.num_programs(0) * n_hg)
    def _():
        nxt = step + 1
        for cp in cache_copies(nxt // n_hg, nxt % n_hg, 1 - slot):
            cp.start()

    for cp in cache_copies(bi, hg, slot):
        cp.wait()

    lam = lam_ref[0]
    nq = q_ref.shape[0]
    n_cache = kbuf.shape[2]
    for hh in range(DEC_HEADS):
        cols = slice(LANES * hh, LANES * (hh + 1))
        k_ref = kbuf.at[slot, hh]
        v_ref = vbuf.at[slot, hh]
        qst = _stack_components(q_ref[:, cols])
        k_main = k_ref[0:n_main, :].astype(BF16)
        v_main = v_ref[0:n_main, :].astype(BF16)
        k_tail = jnp.concatenate([k_ref[n_main:n_cache, :], kn_ref[:, cols]], axis=0).astype(BF16)
        v_tail = jnp.concatenate([v_ref[n_main:n_cache, :], vn_ref[:, cols]], axis=0).astype(BF16)
        bias = jnp.concatenate([b_ref[hh], b_ref[hh]], axis=0)
        s1 = _dot_nt(qst, k_main) + bias[:, 0:n_main]
        s2 = _dot_nt(qst, k_tail) + bias[:, n_main:]
        m = jnp.maximum(jnp.max(s1, axis=-1, keepdims=True), jnp.max(s2, axis=-1, keepdims=True))
        p1 = jnp.exp2(s1 - m)
        p2 = jnp.exp2(s2 - m)
        l = jnp.sum(p1, axis=-1, keepdims=True) + jnp.sum(p2, axis=-1, keepdims=True)
        res = (_dot(p1.astype(BF16), v_main) + _dot(p2.astype(BF16), v_tail)) / l
        o = res[0:nq] - lam * res[nq:]
        o_ref[:, cols] = _sub_layer_norm(o, g_ref[...], out_scale).astype(BF16)


def _attn_decode(lam, q_s, cache_k, cache_v, layer, k_s, v_s, bias, subln_g, out_scale, nq):
    _, b, n_cache, n_h, _ = cache_k.shape
    assert n_h % DEC_HEADS == 0
    n_main = (n_cache // LANES) * LANES
    n_keys = n_cache + nq
    smem = pl.BlockSpec(memory_space=pltpu.SMEM)
    hbm = pl.BlockSpec(memory_space=pl.ANY)
    new = pl.BlockSpec((nq, DEC_HEADS * LANES), lambda bi, hg: (bi, hg))
    buf = pltpu.VMEM((2, DEC_HEADS, n_cache, LANES), F32)
    return pl.pallas_call(
        functools.partial(_attn_decode_kernel, layer=layer, n_main=n_main, out_scale=out_scale),
        grid=(b, n_h // DEC_HEADS),
        in_specs=[smem, new, hbm, hbm, new, new,
                  pl.BlockSpec((DEC_HEADS, nq, n_keys), lambda bi, hg: (hg, 0, 0)),
                  _const_spec((1, LANES))],
        out_specs=new,
        out_shape=jax.ShapeDtypeStruct((b * nq, DA_WIDTH), BF16),
        scratch_shapes=[buf, buf, pltpu.SemaphoreType.DMA((2, 2, DEC_HEADS))],
        compiler_params=_cparams(("arbitrary", "arbitrary")),
        name="attn_decode",
    )(lam, q_s, cache_k, cache_v, k_s, v_s, bias, subln_g)


def _sigmoid(x):
    return 1.0 / (1.0 + jnp.exp(-x))


def _split2(x):
    x1 = x.astype(BF16)
    x2 = (x - x1.astype(F32)).astype(BF16)
    return x1, x2


def _stack_heads(x):
    lo = lax.broadcasted_iota(jnp.int32, x.shape, 1) < HEAD_DIM
    zero = jnp.zeros_like(x)
    return jnp.concatenate([jnp.where(lo, x, zero), jnp.where(lo, zero, x)], axis=0)


def _rwkv_kernel(pr_ref, sh0_ref, h0_ref, mu_ref, w0_ref, a0_ref, kk_ref, ka_ref, rk_ref,
                 lw_ref, lb_ref, w2_ref, a2_ref, g2_ref, gsum_ref, tri_ref,
                 o_ref, sn_ref,
                 h_st, xbuf, carry, at_s, rt_s, bt_s, kt_s, v_s, ep_s, y_s,
                 t_s, tav_s, lrb_s, lrkv_s, zbt_s, zkv_s, ar_s, pc_s,
                 *, tr, t_valid, n_state):
    c_len = RW_CHUNK
    n_ch = tr // c_len
    ti = pl.program_id(1)

    @pl.when(ti == 0)
    def _():
        has_state = pl.program_id(0) < n_state
        carry[...] = jnp.where(has_state, sh0_ref[0], 0.0)
        hd_r = lax.broadcasted_iota(jnp.int32, (PAIR, PAIR), 0) // HEAD_DIM
        hd_c = lax.broadcasted_iota(jnp.int32, (PAIR, PAIR), 1) // HEAD_DIM
        for p in range(N_PAIR):
            s_p = h0_ref[0, p]
            blk = jnp.where((hd_r == hd_c) & has_state, jnp.concatenate([s_p, s_p], axis=1), 0.0)
            h_st[p] = jnp.transpose(blk)

    x = pr_ref[0]
    xbuf[8:8 + tr, :] = x
    xbuf[7:8, :] = carry[...]
    prev = xbuf[7:7 + tr, :]
    carry[...] = pr_ref[0, tr - 1:tr, :]
    xm = x + (prev - x) * mu_ref[...]
    r = xm[:, 0:RW_WIDTH]
    kr = xm[:, RW_WIDTH:2 * RW_WIDTH]
    vr = xm[:, 2 * RW_WIDTH:3 * RW_WIDTH]
    wa = xm[:, 3 * RW_WIDTH:3 * RW_WIDTH + W_LORA + A_LORA]
    gd = xm[:, 3 * RW_WIDTH + W_LORA + A_LORA:]
    lane_wa = lax.broadcasted_iota(jnp.int32, wa.shape, 1)
    twa = jnp.where(lane_wa < W_LORA, jnp.tanh(wa), wa).astype(BF16)
    logw = -DECAY_SCALE * _sigmoid(w0_ref[...] + _dot(twa, w2_ref[...]))
    a_sig = _sigmoid(a0_ref[...] + _dot(twa, a2_ref[...]))
    g = _dot(_sigmoid(gd).astype(BF16), g2_ref[...])
    gsum = gsum_ref[...]
    kk = kr * kk_ref[...]
    kk = kk * lax.rsqrt(jnp.maximum(_group_sum(kk * kk, gsum), 1e-24))
    kr2 = kr * (1.0 + (a_sig - 1.0) * ka_ref[...])
    a_vec = -kk
    b_vec = kk * a_sig
    bonus = _group_sum(r * kr2 * rk_ref[...], gsum) * vr
    if t_valid % tr != 0:
        row = lax.broadcasted_iota(jnp.int32, (tr, 1), 0) + ti * tr
        valid = row < t_valid
        logw = jnp.where(valid, logw, 0.0)
        a_vec = jnp.where(valid, a_vec, 0.0)
        b_vec = jnp.where(valid, b_vec, 0.0)
        kr2 = jnp.where(valid, kr2, 0.0)
        vr = jnp.where(valid, vr, 0.0)
        bonus = jnp.where(valid, bonus, 0.0)
    l1, l2 = _split2(logw)
    tri = tri_ref[...]
    cs = _dot(tri, l1) + _dot(tri, l2)
    e_pos = jnp.exp(cs)
    e_neg = jnp.exp(-cs)
    at_s[...] = a_vec * jnp.exp(cs - logw)
    rt_s[...] = r * e_pos
    bt_s[...] = b_vec * e_neg
    kt_s[...] = kr2 * e_neg
    v_s[...] = vr
    ep_s[...] = e_pos

    idx_r = lax.broadcasted_iota(jnp.int32, (PAIR, PAIR), 0)
    idx_c = lax.broadcasted_iota(jnp.int32, (PAIR, PAIR), 1)
    same = (idx_r // c_len) == (idx_c // c_len)
    strict = same & ((idx_r % c_len) > (idx_c % c_len))
    incl = same & ((idx_r % c_len) >= (idx_c % c_len))
    eye = idx_r == idx_c
    eye_f = jnp.where(eye, 1.0, 0.0).astype(F32)

    zero = jnp.zeros((PAIR, PAIR), F32)
    n_par = next(n for n in (8, 4, 2, 1) if n_ch % n == 0)
    n_grp = n_ch // n_par
    items = [(dc, p) for dc in range(n_par) for p in range(N_PAIR)]
    pairs = range(N_PAIR)

    def s1_load(g):
        sel = [(pl.ds(pl.multiple_of((g * n_par + dc) * c_len, c_len), c_len),
                slice(PAIR * p, PAIR * (p + 1))) for dc, p in items]
        return dict(at=[at_s[rw, cl] for rw, cl in sel], rt=[rt_s[rw, cl] for rw, cl in sel],
                    bt=[bt_s[rw, cl] for rw, cl in sel], kt=[kt_s[rw, cl] for rw, cl in sel],
                    v=[v_s[rw, cl] for rw, cl in sel], ep=[ep_s[rw, cl] for rw, cl in sel])

    def s1_compute(ld):
        q_all = range(len(items))
        at, rt = ld["at"], ld["rt"]
        yb = [_stack_heads(x) for x in ld["bt"]]
        yk = [_stack_heads(x) for x in ld["kt"]]
        vst = [_stack_heads(x).astype(BF16) for x in ld["v"]]
        pc = [x[c_len - 1:c_len] for x in ld["ep"]]
        gmat = [_dot_nt(jnp.concatenate([_stack_heads(at[q]), _stack_heads(rt[q])], axis=0).astype(BF16),
                        jnp.concatenate([yb[q], yk[q]], axis=0).astype(BF16)) for q in q_all]
        aab = [jnp.where(strict, gmat[q][0:PAIR, 0:PAIR], zero) for q in q_all]
        aak = [jnp.where(strict, gmat[q][0:PAIR, PAIR:], zero).astype(BF16) for q in q_all]
        lrb = [jnp.where(incl, gmat[q][PAIR:, 0:PAIR], zero).astype(BF16) for q in q_all]
        lrk = [jnp.where(incl, gmat[q][PAIR:, PAIR:], zero).astype(BF16) for q in q_all]
        tinv = [eye_f + aab[q] for q in q_all]
        lp = aab
        n = 1
        while 2 * n < c_len:
            lpb = [x.astype(BF16) for x in lp]
            lp = [_dot(x, x) for x in lpb]
            tinv = [tinv[q] + _dot(tinv[q].astype(BF16), lp[q].astype(BF16)) for q in q_all]
            n *= 2
        tb = [x.astype(BF16) for x in tinv]
        av = [_dot(aak[q], vst[q]).astype(BF16) for q in q_all]
        return dict(
            t=tb, tav=[_dot(tb[q], av[q]) for q in q_all], lrb=lrb,
            lrkv=[_dot(lrk[q], vst[q]) for q in q_all],
            zbt=[jnp.transpose(yb[q] * pc[q]).astype(BF16) for q in q_all],
            zkv=[_dot(jnp.transpose(yk[q] * pc[q]).astype(BF16), vst[q]) for q in q_all],
            ar=[jnp.concatenate([at[q], rt[q]], axis=0).astype(BF16) for q in q_all],
            pc=[jnp.sum(jnp.where(eye, jnp.broadcast_to(pc[q], (PAIR, PAIR)), zero), axis=-1, keepdims=True)
                for q in q_all])

    stage_bufs = dict(t=t_s, tav=tav_s, lrb=lrb_s, lrkv=lrkv_s, zbt=zbt_s, zkv=zkv_s, ar=ar_s, pc=pc_s)

    def s1_store(g, res):
        for q, (dc, p) in enumerate(items):
            for name, buf in stage_bufs.items():
                buf[g * n_par + dc, p] = res[name][q]

    def s2_load(g):
        return [{name: [buf[g * n_par + dc, p] for p in pairs] for name, buf in stage_bufs.items()}
                for dc in range(n_par)]

    def s2_compute(ld_group, hbd):
        ys = []
        for ld in ld_group:
            arh = [_dot(ld["ar"][p], hbd[p].astype(BF16)) for p in pairs]
            ub = [(_dot(ld["t"][p], _stack_heads(arh[p][0:c_len]).astype(BF16)) + ld["tav"][p]).astype(BF16)
                  for p in pairs]
            yst = [_dot(ld["lrb"][p], ub[p]) + ld["lrkv"][p] for p in pairs]
            ys.append([arh[p][c_len:] + yst[p][0:c_len] + yst[p][c_len:] for p in pairs])
            hbd = [ld["pc"][p] * hbd[p] + _dot(ld["zbt"][p], ub[p]) + ld["zkv"][p] for p in pairs]
        return ys, hbd

    def s2_store(g, ys, hbd):
        for dc in range(n_par):
            rows = pl.ds(pl.multiple_of((g * n_par + dc) * c_len, c_len), c_len)
            for p in pairs:
                y_s[rows, PAIR * p:PAIR * (p + 1)] = ys[dc][p]
        for p in pairs:
            h_st[p] = hbd[p]

    s1_store(0, s1_compute(s1_load(0)))

    def body(g, carry_):
        ld2 = s2_load(g)
        ld1 = s1_load(g + 1)
        hbd = [h_st[p] for p in pairs]
        res1 = s1_compute(ld1)
        ys, hbd = s2_compute(ld2, hbd)
        s2_store(g, ys, hbd)
        s1_store(g + 1, res1)
        return carry_

    lax.fori_loop(0, n_grp - 1, body, 0)
    ys_last, h_last = s2_compute(s2_load(n_grp - 1), [h_st[p] for p in pairs])
    s2_store(n_grp - 1, ys_last, h_last)

    @pl.when(ti == pl.num_programs(1) - 1)
    def _():
        first = lax.broadcasted_iota(jnp.int32, (PAIR, HEAD_DIM), 0) < HEAD_DIM
        for p in range(N_PAIR):
            s_t = jnp.transpose(h_st[p])
            sn_ref[0, p] = jnp.where(first, s_t[:, 0:HEAD_DIM], s_t[:, HEAD_DIM:])

    y = y_s[...]
    inv_n = 1.0 / HEAD_DIM
    mean = _group_sum(y, gsum) * inv_n
    d = y - mean
    var = _group_sum(d * d, gsum) * inv_n
    yn = d * lax.rsqrt(var + GN_EPS) * lw_ref[...] + lb_ref[...]
    o_ref[0] = ((yn + bonus) * g).astype(BF16)


def _rwkv(pr, shift0, s0, prm, tr, t_valid):
    b, t_pad, _ = pr.shape
    n_state = s0.shape[0]
    assert shift0.shape[0] == n_state
    last = n_state - 1
    assert t_pad % tr == 0 and tr % RW_CHUNK == 0
    n_ch = tr // RW_CHUNK
    tri = np.zeros((tr, tr), np.float32)
    for c in range(n_ch):
        tri[c * RW_CHUNK:(c + 1) * RW_CHUNK, c * RW_CHUNK:(c + 1) * RW_CHUNK] = np.tril(
            np.ones((RW_CHUNK, RW_CHUNK), np.float32))
    tri = jnp.asarray(tri, BF16)
    vec = _const_spec((1, RW_WIDTH))
    mat = lambda dt: pltpu.VMEM((n_ch, N_PAIR, PAIR, PAIR), dt)
    tile = lambda: pltpu.VMEM((tr, RW_WIDTH), F32)
    return pl.pallas_call(
        functools.partial(_rwkv_kernel, tr=tr, t_valid=t_valid, n_state=n_state),
        grid=(b, t_pad // tr),
        in_specs=[pl.BlockSpec((1, tr, RW_PROJ), lambda bi, ti: (bi, ti, 0)),
                  pl.BlockSpec((1, 1, RW_PROJ), lambda bi, ti: (jnp.minimum(bi, last), 0, 0)),
                  pl.BlockSpec((1, N_PAIR, PAIR, HEAD_DIM),
                               lambda bi, ti: (jnp.minimum(bi, last), 0, 0, 0)),
                  _const_spec((1, RW_PROJ)), vec, vec, vec, vec, vec, vec, vec,
                  _const_spec((W_LORA + A_LORA, RW_WIDTH)), _const_spec((W_LORA + A_LORA, RW_WIDTH)),
                  _const_spec((G_LORA, RW_WIDTH)), _const_spec((LANES, LANES)),
                  _const_spec((tr, tr))],
        out_specs=[pl.BlockSpec((1, tr, RW_WIDTH), lambda bi, ti: (bi, ti, 0)),
                   pl.BlockSpec((1, N_PAIR, PAIR, HEAD_DIM), lambda bi, ti: (bi, 0, 0, 0))],
        out_shape=[jax.ShapeDtypeStruct((b, t_pad, RW_WIDTH), BF16),
                   jax.ShapeDtypeStruct((b, N_PAIR, PAIR, HEAD_DIM), F32)],
        scratch_shapes=[pltpu.VMEM((N_PAIR, PAIR, PAIR), F32),
                        pltpu.VMEM((8 + tr, RW_PROJ), F32), pltpu.VMEM((1, RW_PROJ), F32),
                        tile(), tile(), tile(), tile(), tile(), tile(), tile(),
                        mat(BF16), mat(F32), mat(BF16), mat(F32), mat(BF16), mat(F32), mat(BF16),
                        pltpu.VMEM((n_ch, N_PAIR, PAIR, 1), F32)],
        compiler_params=_cparams(("arbitrary", "arbitrary")),
        name="rwkv",
    )(pr, shift0, s0, prm["mu"], prm["w0"], prm["a0"], prm["k_k"], prm["k_a"], prm["r_k"],
      prm["lnx_w"], prm["lnx_b"], prm["w2p"], prm["a2p"], prm["g2"], prm["gsum"], tri)


FF_COLS = 256


def _ffn_kernel(x_ref, oa_ref, orw_ref, c0_ref, wo_ref, g_ref, wu_ref, cw_ref, cb_ref, wd_ref,
                y_ref, cn_ref, zbuf, cbuf, h_ref, x1_ref, act_ref, *, tm, ts, d_ff):
    off = zbuf.shape[1] - tm
    da = oa_ref.shape[1]

    @pl.when(pl.program_id(0) == 0)
    def _():
        cbuf[...] = c0_ref[...]

    x1 = x_ref[...] + _dot(oa_ref[...], wo_ref[0:da, :]) + _dot(orw_ref[...], wo_ref[da:, :])
    x1_ref[...] = x1
    ms = jnp.mean(x1 * x1, axis=-1, keepdims=True)
    h_ref[...] = (x1 * lax.rsqrt(ms + NORM_EPS) * g_ref[...]).astype(BF16)

    def up_proj(c0, zb):
        cols = slice(c0, c0 + FF_COLS)
        zb[off - 2 * ts:off, :] = cbuf[:, cols]
        zb[off:off + tm, :] = _dot(h_ref[...], wu_ref[:, cols])
        cbuf[:, cols] = zb[off + tm - 2 * ts:off + tm, :]

    def conv_cols(c0, zb):
        cols = slice(c0, c0 + FF_COLS)
        z2 = zb[off - 2 * ts:off - 2 * ts + tm, :]
        z1 = zb[off - ts:off - ts + tm, :]
        z = zb[off:off + tm, :]
        return (cb_ref[:, cols] + z2 * cw_ref[0:1, cols] + z1 * cw_ref[1:2, cols]
                + z * cw_ref[2:3, cols])

    def stage(c):
        up_proj(c * FF_COLS, zbuf.at[2 * (c % 2)])
        up_proj(d_ff + c * FF_COLS, zbuf.at[2 * (c % 2) + 1])

    n_chunks = d_ff // FF_COLS
    stage(0)
    for c in range(n_chunks):
        if c + 1 < n_chunks:
            stage(c + 1)
        gate = conv_cols(c * FF_COLS, zbuf.at[2 * (c % 2)])
        up = conv_cols(d_ff + c * FF_COLS, zbuf.at[2 * (c % 2) + 1])
        act_ref[:, c * FF_COLS:(c + 1) * FF_COLS] = (gate * _sigmoid(gate) * up).astype(BF16)

    y_ref[...] = x1_ref[...] + _dot(act_ref[...], wd_ref[...])
    cn_ref[...] = cbuf[...]


def _ffn(x, oa, orw, conv0, prm, tm, ts):
    rows, d = x.shape
    d_ff = prm["w_down"].shape[0]
    assert rows % tm == 0 and d_ff % FF_COLS == 0 and (ts == 1 or ts % 8 == 0)
    off = -(-2 * ts // 8) * 8
    row = lambda w: pl.BlockSpec((tm, w), lambda i: (i, 0))
    return pl.pallas_call(
        functools.partial(_ffn_kernel, tm=tm, ts=ts, d_ff=d_ff),
        grid=(rows // tm,),
        in_specs=[row(d), row(DA_WIDTH), row(RW_WIDTH), _const_spec((2 * ts, 2 * d_ff)),
                  _resident_spec((DA_WIDTH + RW_WIDTH, d)), _const_spec((1, d)),
                  _resident_spec((d, 2 * d_ff)), _const_spec((CONV_W, 2 * d_ff)),
                  _const_spec((1, 2 * d_ff)), _resident_spec((d_ff, d))],
        out_specs=[row(d), _const_spec((2 * ts, 2 * d_ff))],
        out_shape=[jax.ShapeDtypeStruct((rows, d), F32),
                   jax.ShapeDtypeStruct((2 * ts, 2 * d_ff), F32)],
        scratch_shapes=[pltpu.VMEM((4, off + tm, FF_COLS), F32), pltpu.VMEM((2 * ts, 2 * d_ff), F32),
                        pltpu.VMEM((tm, d), BF16), pltpu.VMEM((tm, d), F32),
                        pltpu.VMEM((tm, d_ff), BF16)],
        compiler_params=_cparams(("arbitrary",)),
        name="ffn",
    )(x, oa, orw, conv0, prm["w_out"], prm["ln2_g"], prm["w_up"], prm["conv_w"], prm["conv_b"],
      prm["w_down"])


def _rel_bucket(rel):
    nb = N_BUCKETS // 2
    max_exact = nb // 2
    bucket = jnp.where(rel > 0, nb, 0)
    n = jnp.abs(rel)
    nf = jnp.maximum(n, 1).astype(F32)
    large = max_exact + (jnp.log(nf / max_exact) / math.log(MAX_DISTANCE / max_exact)
                         * (nb - max_exact)).astype(jnp.int32)
    large = jnp.minimum(large, nb - 1)
    return bucket + jnp.where(n < max_exact, n, large)


def _bias_table(rel_bias, q_pos, k_pos, mask):
    n_q, n_k = len(q_pos), len(k_pos)
    assert np.all(np.diff(q_pos) == 1) and np.all(np.diff(k_pos) == 1)
    span = n_q + n_k - 1
    rel = jnp.asarray(int(k_pos[0]) - int(q_pos[0]) - (n_q - 1) + np.arange(span), jnp.int32)
    w = jnp.transpose(rel_bias[_rel_bucket(rel)]).astype(F32) * LOG2E
    return jnp.where(jnp.asarray(mask)[None], _toeplitz(w, n_q, n_k), NEG_INF)


def _skew(w, n_q, n_k):
    span = n_q + n_k - 1
    x = jnp.pad(w[:, :span], ((0, 0), (0, 1)))
    rows = jnp.tile(x, (1, n_q))[:, :n_q * span].reshape(w.shape[0], n_q, span)
    return rows[:, :, n_q - 1:n_q - 1 + n_k]


def _toeplitz(w, n_q, n_k):
    blk = LANES
    if n_q % blk or n_k % blk or n_q * n_k <= blk * blk:
        return _skew(w, n_q, n_k)
    nbq, nbk = n_q // blk, n_k // blk
    n_d = nbq + nbk - 1
    n_h = w.shape[0]
    wp = jnp.pad(w[:, :n_q + n_k - 1], ((0, 0), (0, 1)))
    seg = jnp.concatenate([wp[:, :n_d * blk].reshape(n_h, n_d, blk),
                           wp[:, blk:].reshape(n_h, n_d, blk)[..., :blk - 1]], axis=-1)
    x = jnp.pad(seg, ((0, 0), (0, 0), (0, 1)))
    rows = jnp.tile(x, (1, 1, blk))[..., :blk * (2 * blk - 1)].reshape(n_h, n_d, blk, 2 * blk - 1)
    blocks = rows[..., blk - 1:]
    return jnp.concatenate([jnp.concatenate([blocks[:, c - r + nbq - 1] for c in range(nbk)], axis=2)
                            for r in range(nbq)], axis=1)


BOUND_MAX_SPREAD = 100.0


def _score_bound(q_g, k_g, rel_bias):
    s_max = (HEAD_DIM ** 0.5) * LOG2E * 1.01 * jnp.max(jnp.abs(q_g)) * jnp.max(jnp.abs(k_g))
    b_hi = jnp.max(rel_bias) * LOG2E
    b_lo = jnp.min(rel_bias) * LOG2E
    use = (2.0 * s_max + (b_hi - b_lo)) <= BOUND_MAX_SPREAD
    return jnp.stack([use.astype(F32), (s_max + b_hi).astype(F32)])


def _ext_chunk(pos):
    return np.where(pos < N_META, -1, (pos - N_META) // CHUNK)


def _prompt_bias(rel_bias):
    tq = ATT_T
    fr = np.arange(tq) + N_META
    meta = np.arange(N_META)
    causal = _ext_chunk(fr)[None, :] <= _ext_chunk(fr)[:, None]
    tr = lambda b: jnp.swapaxes(b, 1, 2)
    b0 = tr(_bias_table(rel_bias, fr, fr, causal))
    bm1 = tr(_bias_table(rel_bias, fr + tq, fr, np.ones((tq, tq), bool)))
    bq0 = tr(_bias_table(rel_bias, fr, meta, np.ones((tq, N_META), bool)))
    bmm = _bias_table(rel_bias, meta, meta, np.ones((N_META, N_META), bool))
    assert tq + 1 >= MAX_DISTANCE
    cfar = rel_bias[_rel_bucket(jnp.asarray(-(tq + 1), jnp.int32))].astype(F32) * LOG2E
    return cfar, b0, bm1, bq0, bmm


def _decode_bias(rel_bias, n_cache, nq):
    k_pos = np.arange(n_cache + nq)
    q_pos = k_pos[n_cache:]
    mask = _ext_chunk(k_pos)[None, :] <= _ext_chunk(q_pos)[:, None]
    return _bias_table(rel_bias, q_pos, k_pos, mask)


def _block_ones(n, blk, dtype):
    idx = np.arange(n) // blk
    return jnp.asarray((idx[:, None] == idx[None, :]).astype(np.float32), dtype)


def kernel(x_prompt, x_sample, cache_k, cache_v, state_rwkv, state_shift, state_conv, meta_tokens,
           rel_bias, ln1_g, w_in, q_norm_g, k_norm_g, lam_q1, lam_k1, lam_q2, lam_k2, subln_g,
           mu_shift, w0, w2, a0, a2, g2, k_k, k_a, r_k, lnx_w, lnx_b, w_out, ln2_g, w_up, conv_w,
           conv_b, w_down):
    bp, seq, d = x_prompt.shape
    db, dt, _ = x_sample.shape
    depth = w_in.shape[0]
    d_ff = w_down.shape[1]
    n_cache = cache_k.shape[2]
    assert bp == 1 and dt == N_META, "the meta stream rides with the decode streams"
    assert cache_k.shape[3] == DA_HEADS and cache_k.shape[4] == 2 * HEAD_DIM
    nb = db + 1
    nb_pad = -(-nb // 8) * 8

    cfar, b0, bm1, bq0, bmm = _prompt_bias(rel_bias)
    bias_dec = _decode_bias(rel_bias, n_cache, dt)
    gsum = _block_ones(LANES, HEAD_DIM, BF16)
    gmean = gsum * (1.0 / HEAD_DIM)
    zrow = lambda n: jnp.zeros((n, RW_WIDTH), BF16)

    x_f = x_prompt[0]
    x_s = jnp.concatenate([x_sample, meta_tokens.astype(x_sample.dtype)[None]], axis=0)
    outs = [[] for _ in range(10)]
    for l in range(depth):
        lam_init = 0.8 - 0.6 * math.exp(-0.3 * l)
        lam = (jnp.exp(jnp.sum(lam_q1[l].astype(F32) * lam_k1[l].astype(F32)))
               - jnp.exp(jnp.sum(lam_q2[l].astype(F32) * lam_k2[l].astype(F32))) + lam_init).reshape(1)
        out_scale = 1.0 - lam_init
        tile128 = lambda g_: jnp.tile(g_.reshape(1, -1), (1, DA_WIDTH // g_.shape[-1]))
        qg, kg = tile128(q_norm_g[l]), tile128(k_norm_g[l])
        sg = subln_g[l].reshape(1, LANES)
        w_in_bf = w_in[l].astype(BF16)
        rw = {
            "mu": mu_shift[l].reshape(1, -1), "w0": w0[l].reshape(1, -1), "a0": a0[l].reshape(1, -1),
            "k_k": k_k[l].reshape(1, -1), "k_a": k_a[l].reshape(1, -1), "r_k": r_k[l].reshape(1, -1),
            "lnx_w": lnx_w[l].reshape(1, -1), "lnx_b": lnx_b[l].reshape(1, -1),
            "w2p": jnp.concatenate([w2[l].astype(BF16), zrow(A_LORA)], axis=0),
            "a2p": jnp.concatenate([zrow(W_LORA), a2[l].astype(BF16)], axis=0),
            "g2": g2[l].astype(BF16), "gsum": gsum,
        }
        ff = {
            "w_out": w_out[l].astype(BF16), "ln2_g": ln2_g[l].reshape(1, -1),
            "w_up": w_up[l].astype(BF16), "conv_w": conv_w[l], "conv_b": conv_b[l].reshape(1, -1),
            "w_down": w_down[l].astype(BF16),
        }

        q_f, k_f, v_f, pr_f, kb_t, vt_t = _proj(x_f, ln1_g[l].reshape(1, -1), w_in_bf, qg, kg, gmean,
                                                 512, True)
        q_s, k_s, v_s, pr_s = _proj(x_s.reshape(nb * dt, d), ln1_g[l].reshape(1, -1), w_in_bf, qg, kg,
                                    gmean, nb * dt, False)
        m0 = db * dt

        kb_m = k_s[m0:].astype(BF16)
        vb_m = v_s[m0:].astype(BF16)
        o_f, o_m = _attn_prompt(cfar, lam, _score_bound(q_norm_g[l], k_norm_g[l], rel_bias), q_f, kb_t, vt_t, kb_m, vb_m, jnp.transpose(vb_m), q_s[m0:],
                                b0, bm1, bq0, bmm, sg, out_scale)
        o_d = _attn_decode(lam, q_s, cache_k, cache_v, l, k_s, v_s, bias_dec, sg, out_scale, dt)
        o_s = jnp.concatenate([o_d, o_m], axis=0)

        pr_s3 = pr_s.reshape(nb, dt, RW_PROJ)
        pr_pad = jnp.pad(pr_s3, ((0, 0), (0, RW_CHUNK - dt), (0, 0)))
        pair_view = (N_PAIR, PAIR, HEAD_DIM)
        orw_s, sn_s = _rwkv(pr_pad, state_shift[l][:, None, :], state_rwkv[l].reshape(db, *pair_view),
                            rw, RW_CHUNK, dt)
        orw_f, sn_f = _rwkv(pr_f[None], pr_s3[db:, dt - 1:dt, :], sn_s[db:], rw, 512, seq)

        def time_major(a):
            a = jnp.pad(a.reshape(nb, dt, -1), ((0, nb_pad - nb), (0, 0), (0, 0)))
            return jnp.swapaxes(a, 0, 1).reshape(dt * nb_pad, -1)

        conv_s = jnp.concatenate([state_conv[l], jnp.zeros((1, CONV_W - 1, 2 * d_ff), F32)], axis=0)
        conv_s = jnp.pad(conv_s, ((0, nb_pad - nb), (0, 0), (0, 0)))
        conv_s = jnp.swapaxes(conv_s, 0, 1).reshape(2 * nb_pad, 2 * d_ff)
        y_s, cn_s = _ffn(time_major(x_s), time_major(o_s), time_major(orw_s[:, :dt]), conv_s, ff,
                         dt * nb_pad, nb_pad)
        cn_s = jnp.swapaxes(cn_s.reshape(2, nb_pad, 2 * d_ff), 0, 1)
        y_f, cn_f = _ffn(x_f, o_f, orw_f[0], cn_s[db], ff, 512, 1)
        y_s = jnp.swapaxes(y_s.reshape(dt, nb_pad, d), 0, 1)[:nb]

        hw = (DA_HEADS, 2 * HEAD_DIM)
        outs[0].append(jnp.concatenate([k_s[m0:].reshape(N_META, *hw), k_f], axis=0)[None])
        outs[1].append(jnp.concatenate([v_s[m0:].reshape(N_META, *hw), v_f], axis=0)[None])
        outs[2].append(sn_f.reshape(bp, RW_HEADS, HEAD_DIM, HEAD_DIM))
        outs[3].append(pr_f[seq - 1:seq])
        outs[4].append(cn_f[None])
        outs[5].append(k_s[:m0].reshape(db, dt, *hw))
        outs[6].append(v_s[:m0].reshape(db, dt, *hw))
        outs[7].append(sn_s[:db].reshape(db, RW_HEADS, HEAD_DIM, HEAD_DIM))
        outs[8].append(pr_s3[:db, dt - 1])
        outs[9].append(cn_s[:db])
        x_f, x_s = y_f, y_s

    return (x_f[None], x_s[:db], *[jnp.stack(o) for o in outs])
```

```python
import functools
import math

import numpy as np
import jax
import jax.numpy as jnp
from jax import lax
from jax.experimental import pallas as pl
from jax.experimental.pallas import tpu as pltpu

F32 = jnp.float32
BF16 = jnp.bfloat16

CHUNK = 64
N_META = 16
HEAD_DIM = 64
DA_HEADS = 4
RW_HEADS = 8
W_LORA = 64
A_LORA = 64
G_LORA = 128
CONV_W = 3
N_BUCKETS = 32
MAX_DISTANCE = 128
NORM_EPS = 1e-6
GN_EPS = 64e-5
NEG_INF = -1e30
LOG2E = math.log2(math.e)
DECAY_SCALE = math.exp(-0.5)

DA_WIDTH = DA_HEADS * 2 * HEAD_DIM
RW_WIDTH = RW_HEADS * HEAD_DIM
RW_PROJ = 3 * RW_WIDTH + W_LORA + A_LORA + G_LORA
LANES = 128
PAIR = 2 * HEAD_DIM
N_PAIR = RW_WIDTH // PAIR
RW_CHUNK = 64
ATT_T = 512
VMEM_LIMIT = 56 * 1024 * 1024


def _dot(a, b):
    return jnp.dot(a, b, preferred_element_type=F32)


def _dot_nt(a, b):
    return lax.dot_general(a, b, (((1,), (1,)), ((), ())), preferred_element_type=F32)


def _group_sum(x, blk):
    xb = x.astype(BF16)
    return jnp.concatenate([_dot(xb[:, LANES * p:LANES * (p + 1)], blk)
                            for p in range(x.shape[1] // LANES)], axis=1)


def _cparams(sem):
    return pltpu.CompilerParams(dimension_semantics=sem, vmem_limit_bytes=VMEM_LIMIT)


def _const_spec(shape):
    nd = len(shape)
    return pl.BlockSpec(shape, lambda *_: (0,) * nd)


def _resident_spec(shape):
    nd = len(shape)
    return pl.BlockSpec(shape, lambda *_: (0,) * nd, pipeline_mode=pl.Buffered(1))


def _proj_kernel(x_ref, g_ref, w_ref, qg_ref, kg_ref, gm_ref,
                 q_ref, k_ref, v_ref, pr_ref, *tile_refs, tm):
    x = x_ref[...]
    ms = jnp.mean(x * x, axis=-1, keepdims=True)
    h = (x * lax.rsqrt(ms + NORM_EPS) * g_ref[...]).astype(BF16)
    gm = gm_ref[...]

    def group_norm(t, g):
        ms_g = _group_sum(t * t, gm)
        return t * lax.rsqrt(ms_g + NORM_EPS) * g

    q = _dot(h, w_ref[:, 0:DA_WIDTH])
    k = _dot(h, w_ref[:, DA_WIDTH:2 * DA_WIDTH])
    pr_ref[...] = _dot(h, w_ref[:, 3 * DA_WIDTH:])
    v = _dot(h, w_ref[:, 2 * DA_WIDTH:3 * DA_WIDTH])
    q_ref[...] = (group_norm(q, qg_ref[...]) * (HEAD_DIM ** -0.5 * LOG2E)).astype(BF16)
    k = group_norm(k, kg_ref[...])
    if len(k_ref.shape) == 3:
        for hd in range(DA_HEADS):
            k_ref[:, hd, :] = k[:, hd * LANES:(hd + 1) * LANES]
            v_ref[:, hd, :] = v[:, hd * LANES:(hd + 1) * LANES]
    else:
        k_ref[...] = k
        v_ref[...] = v
    if tile_refs:
        kb_ref, vt_ref = tile_refs
        kb = k.astype(BF16)
        vt = jnp.transpose(v).astype(BF16)
        for hd in range(DA_HEADS):
            for jj in range(tm // ATT_T):
                kb_ref[hd, jj] = kb[jj * ATT_T:(jj + 1) * ATT_T, hd * LANES:(hd + 1) * LANES]
                vt_ref[hd, jj] = vt[hd * LANES:(hd + 1) * LANES, jj * ATT_T:(jj + 1) * ATT_T]


def _proj(x, ln1_g, w_in_bf, qg, kg, gmean, tm, emit_tiles):
    rows, d = x.shape
    n_in = w_in_bf.shape[1]
    assert rows % tm == 0
    row = lambda w: pl.BlockSpec((tm, w), lambda i: (i, 0))
    out_specs = [row(DA_WIDTH), row(DA_WIDTH), row(DA_WIDTH), row(RW_PROJ)]
    out_shape = [jax.ShapeDtypeStruct((rows, DA_WIDTH), BF16),
                 jax.ShapeDtypeStruct((rows, DA_WIDTH), F32),
                 jax.ShapeDtypeStruct((rows, DA_WIDTH), F32),
                 jax.ShapeDtypeStruct((rows, RW_PROJ), F32)]
    if emit_tiles:
        assert tm % ATT_T == 0
        tpt = tm // ATT_T
        for o in (1, 2):
            out_specs[o] = pl.BlockSpec((tm, DA_HEADS, LANES), lambda i: (i, 0, 0))
            out_shape[o] = jax.ShapeDtypeStruct((rows, DA_HEADS, LANES), F32)
        out_specs += [pl.BlockSpec((DA_HEADS, tpt, ATT_T, LANES), lambda i: (0, i, 0, 0)),
                      pl.BlockSpec((DA_HEADS, tpt, LANES, ATT_T), lambda i: (0, i, 0, 0))]
        out_shape += [jax.ShapeDtypeStruct((DA_HEADS, rows // ATT_T, ATT_T, LANES), BF16),
                      jax.ShapeDtypeStruct((DA_HEADS, rows // ATT_T, LANES, ATT_T), BF16)]
    return pl.pallas_call(
        functools.partial(_proj_kernel, tm=tm),
        grid=(rows // tm,),
        in_specs=[row(d), _const_spec((1, d)), _resident_spec((d, n_in)),
                  _const_spec((1, DA_WIDTH)), _const_spec((1, DA_WIDTH)),
                  _const_spec((LANES, LANES))],
        out_specs=out_specs,
        out_shape=out_shape,
        compiler_params=_cparams(("arbitrary",)),
        name="proj",
    )(x, ln1_g, w_in_bf, qg, kg, gmean)


def _stack_components(q):
    lo = lax.broadcasted_iota(jnp.int32, q.shape, 1) < HEAD_DIM
    zero = jnp.zeros_like(q)
    return jnp.concatenate([jnp.where(lo, q, zero), jnp.where(lo, zero, q)], axis=0)


def _sub_layer_norm(o, g, out_scale):
    ms = jnp.mean(o * o, axis=-1, keepdims=True)
    return o * lax.rsqrt(ms + NORM_EPS) * g * out_scale


def _attn_prompt_kernel(cfar_ref, lam_ref, fix_ref, q_ref, k_ref, vt_ref, km_ref, vm_ref, vmt_ref, qm_ref,
                        b0_ref, bm1_ref, bq0_ref, bmm_ref, g_ref, o_ref, om_ref,
                        acc_ref, m_ref, l_ref, s_buf, p_buf, a_buf, *, out_scale):
    h = pl.program_id(0)
    i = pl.program_id(1)
    tq = ATT_T
    cf = cfar_ref[h]
    lam = lam_ref[0]
    qst = _stack_components(q_ref[...])

    def both(b):
        return jnp.concatenate([b, b], axis=1)

    s = _dot_nt(km_ref[...], qst)
    s = s + both(jnp.where(i == 0, bq0_ref[0], cf))
    use_bound = fix_ref[0] > 0.5
    bound = fix_ref[1]
    m0 = jnp.where(use_bound, bound, jnp.max(s, axis=0, keepdims=True))
    p = jnp.exp2(s - m0)
    m_ref[...] = m0
    l_ref[...] = jnp.sum(p, axis=0, keepdims=True)
    acc_ref[...] = _dot(vmt_ref[...], p.astype(BF16))

    def scores(idx, slot):
        s_buf[slot] = _dot_nt(k_ref[0, idx], qst)

    def softmax(slot, bias):
        s = s_buf[slot]
        m_prev = m_ref[...]
        if bias.ndim == 0:
            m_new = jnp.maximum(m_prev, jnp.max(s, axis=0, keepdims=True) + bias)
            p = jnp.exp2(s - (m_new - bias))
        else:
            s = s + both(bias)
            m_new = jnp.maximum(m_prev, jnp.max(s, axis=0, keepdims=True))
            p = jnp.exp2(s - m_new)
        alpha = jnp.exp2(m_prev - m_new)
        l_ref[...] = alpha * l_ref[...] + jnp.sum(p, axis=0, keepdims=True)
        m_ref[...] = m_new
        p_buf[slot] = p.astype(BF16)
        a_buf[slot] = alpha

    def accumulate(idx, slot):
        acc_ref[...] = a_buf[slot] * acc_ref[...] + _dot(vt_ref[0, idx], p_buf[slot])

    n_far = jnp.maximum(i - 1, 0)
    off = n_far % 2
    p_buf[1] = jnp.zeros(p_buf.shape[1:], BF16)
    a_buf[1] = jnp.ones(a_buf.shape[1:], F32)
    scores(0, 0)

    def far_body(t, carry):
        u = 2 * t - off
        scores(u + 1, 1)
        softmax(0, jnp.where(u >= 0, cf, NEG_INF))
        accumulate(jnp.maximum(u - 1, 0), 1)
        scores(u + 2, 0)
        softmax(1, cf)
        accumulate(jnp.maximum(u, 0), 0)
        return carry

    def bound_softmax(slot, bias):
        if bias.ndim == 0:
            p = jnp.exp2(s_buf[slot] - (bound - bias))
        else:
            p = jnp.exp2(s_buf[slot] + (both(bias) - bound))
        l_ref[...] += jnp.sum(p, axis=0, keepdims=True)
        p_buf[slot] = p.astype(BF16)

    def bound_accumulate(idx, slot):
        acc_ref[...] += _dot(vt_ref[0, idx], p_buf[slot])

    def far_body_bound(t, carry):
        u = 2 * t - off
        scores(u + 1, 1)
        bound_softmax(0, jnp.where(u >= 0, cf, NEG_INF))
        bound_accumulate(jnp.maximum(u - 1, 0), 1)
        scores(u + 2, 0)
        bound_softmax(1, cf)
        bound_accumulate(jnp.maximum(u, 0), 0)
        return carry

    def tail(softmax_fn, accumulate_fn):
        scores(i, 1)
        softmax_fn(0, jnp.where(i >= 1, bm1_ref[0], NEG_INF))
        accumulate_fn(jnp.maximum(n_far - 1, 0), 1)
        softmax_fn(1, b0_ref[0])
        accumulate_fn(n_far, 0)
        accumulate_fn(i, 1)

    @pl.when(use_bound)
    def _():
        lax.fori_loop(0, (n_far + 1) // 2, far_body_bound, 0)
        tail(bound_softmax, bound_accumulate)

    @pl.when(jnp.logical_not(use_bound))
    def _():
        lax.fori_loop(0, (n_far + 1) // 2, far_body, 0)
        tail(softmax, accumulate)

    accn = acc_ref[...] * (1.0 / l_ref[...])
    o_t = accn[:, 0:tq] - lam * accn[:, tq:]
    o_ref[...] = _sub_layer_norm(jnp.transpose(o_t), g_ref[...], out_scale).astype(BF16)

    @pl.when(i == 0)
    def _():
        qm = _stack_components(qm_ref[...])
        bmm = bmm_ref[0]
        sm = _dot_nt(qm, km_ref[...]) + jnp.concatenate([bmm, bmm], axis=0)
        mm = jnp.max(sm, axis=-1, keepdims=True)
        pm = jnp.exp2(sm - mm)
        accm = _dot(pm.astype(BF16), vm_ref[...]) / jnp.sum(pm, axis=-1, keepdims=True)
        om = accm[0:N_META] - lam * accm[N_META:]
        om_ref[...] = _sub_layer_norm(om, g_ref[...], out_scale).astype(BF16)


def _attn_prompt(cfar, lam, fix, q_f, kb_t, vt_t, kb_m, vb_m, vbt_m, q_m, b0, bm1, bq0, bmm, subln_g,
                 out_scale):
    tf = q_f.shape[0]
    tq = ATT_T
    assert tf % tq == 0
    n_t = tf // tq
    smem = pl.BlockSpec(memory_space=pltpu.SMEM)
    head_col = lambda rows: pl.BlockSpec((rows, LANES), lambda h, i: (0, h))
    head_tile = lambda a, b: pl.BlockSpec((1, a, b), lambda h, i: (h, 0, 0))
    return pl.pallas_call(
        functools.partial(_attn_prompt_kernel, out_scale=out_scale),
        grid=(DA_HEADS, n_t),
        in_specs=[smem, smem, smem,
                  pl.BlockSpec((tq, LANES), lambda h, i: (i, h)),
                  pl.BlockSpec((1, n_t, tq, LANES), lambda h, i: (h, 0, 0, 0)),
                  pl.BlockSpec((1, n_t, LANES, tq), lambda h, i: (h, 0, 0, 0)),
                  head_col(N_META), head_col(N_META),
                  pl.BlockSpec((LANES, N_META), lambda h, i: (h, 0)),
                  head_col(N_META),
                  head_tile(tq, tq), head_tile(tq, tq), head_tile(N_META, tq),
                  head_tile(N_META, N_META), _const_spec((1, LANES))],
        out_specs=[pl.BlockSpec((tq, LANES), lambda h, i: (i, h)), head_col(N_META)],
        out_shape=[jax.ShapeDtypeStruct((tf, DA_WIDTH), BF16),
                   jax.ShapeDtypeStruct((N_META, DA_WIDTH), BF16)],
        scratch_shapes=[pltpu.VMEM((LANES, 2 * tq), F32), pltpu.VMEM((1, 2 * tq), F32),
                        pltpu.VMEM((1, 2 * tq), F32), pltpu.VMEM((2, tq, 2 * tq), F32),
                        pltpu.VMEM((2, tq, 2 * tq), BF16), pltpu.VMEM((2, 1, 2 * tq), F32)],
        compiler_params=_cparams(("arbitrary", "arbitrary")),
        name="attn_prompt",
    )(cfar, lam, fix, q_f, kb_t, vt_t, kb_m, vb_m, vbt_m, q_m, b0, bm1, bq0, bmm, subln_g)


DEC_HEADS = 4


def _attn_decode_kernel(lam_ref, q_ref, k_hbm, v_hbm, kn_ref, vn_ref, b_ref, g_ref, o_ref,
                        kbuf, vbuf, sem, *, layer, n_main, out_scale):
    bi = pl.program_id(0)
    hg = pl.program_id(1)
    n_hg = pl.num_programs(1)
    step = bi * n_hg + hg
    slot = step % 2

    def cache_copies(b_, hg_, slot_):
        cps = []
        for hh in range(DEC_HEADS):
            h_ = hg_ * DEC_HEADS + hh
            cps.append(pltpu.make_async_copy(k_hbm.at[layer, b_, :, h_, :], kbuf.at[slot_, hh],
                                             sem.at[0, slot_, hh]))
            cps.append(pltpu.make_async_copy(v_hbm.at[layer, b_, :, h_, :], vbuf.at[slot_, hh],
                                             sem.at[1, slot_, hh]))
        return cps

    @pl.when(step == 0)
    def _():
        for cp in cache_copies(bi, hg, slot):
            cp.start()

    @pl.when(step + 1 < pl.num_programs(0) * n_hg)
    def _():
        nxt = step + 1
        for cp in cache_copies(nxt // n_hg, nxt % n_hg, 1 - slot):
            cp.start()

    for cp in cache_copies(bi, hg, slot):
        cp.wait()

    lam = lam_ref[0]
    nq = q_ref.shape[0]
    n_cache = kbuf.shape[2]
    for hh in range(DEC_HEADS):
        cols = slice(LANES * hh, LANES * (hh + 1))
        k_ref = kbuf.at[slot, hh]
        v_ref = vbuf.at[slot, hh]
        qst = _stack_components(q_ref[:, cols])
        k_main = k_ref[0:n_main, :].astype(BF16)
        v_main = v_ref[0:n_main, :].astype(BF16)
        k_tail = jnp.concatenate([k_ref[n_main:n_cache, :], kn_ref[:, cols]], axis=0).astype(BF16)
        v_tail = jnp.concatenate([v_ref[n_main:n_cache, :], vn_ref[:, cols]], axis=0).astype(BF16)
        bias = jnp.concatenate([b_ref[hh], b_ref[hh]], axis=0)
        s1 = _dot_nt(qst, k_main) + bias[:, 0:n_main]
        s2 = _dot_nt(qst, k_tail) + bias[:, n_main:]
        m = jnp.maximum(jnp.max(s1, axis=-1, keepdims=True), jnp.max(s2, axis=-1, keepdims=True))
        p1 = jnp.exp2(s1 - m)
        p2 = jnp.exp2(s2 - m)
        l = jnp.sum(p1, axis=-1, keepdims=True) + jnp.sum(p2, axis=-1, keepdims=True)
        res = (_dot(p1.astype(BF16), v_main) + _dot(p2.astype(BF16), v_tail)) / l
        o = res[0:nq] - lam * res[nq:]
        o_ref[:, cols] = _sub_layer_norm(o, g_ref[...], out_scale).astype(BF16)


def _attn_decode(lam, q_s, cache_k, cache_v, layer, k_s, v_s, bias, subln_g, out_scale, nq):
    _, b, n_cache, n_h, _ = cache_k.shape
    assert n_h % DEC_HEADS == 0
    n_main = (n_cache // LANES) * LANES
    n_keys = n_cache + nq
    smem = pl.BlockSpec(memory_space=pltpu.SMEM)
    hbm = pl.BlockSpec(memory_space=pl.ANY)
    new = pl.BlockSpec((nq, DEC_HEADS * LANES), lambda bi, hg: (bi, hg))
    buf = pltpu.VMEM((2, DEC_HEADS, n_cache, LANES), F32)
    return pl.pallas_call(
        functools.partial(_attn_decode_kernel, layer=layer, n_main=n_main, out_scale=out_scale),
        grid=(b, n_h // DEC_HEADS),
        in_specs=[smem, new, hbm, hbm, new, new,
                  pl.BlockSpec((DEC_HEADS, nq, n_keys), lambda bi, hg: (hg, 0, 0)),
                  _const_spec((1, LANES))],
        out_specs=new,
        out_shape=jax.ShapeDtypeStruct((b * nq, DA_WIDTH), BF16),
        scratch_shapes=[buf, buf, pltpu.SemaphoreType.DMA((2, 2, DEC_HEADS))],
        compiler_params=_cparams(("arbitrary", "arbitrary")),
        name="attn_decode",
    )(lam, q_s, cache_k, cache_v, k_s, v_s, bias, subln_g)


def _sigmoid(x):
    return 1.0 / (1.0 + jnp.exp(-x))


def _split2(x):
    x1 = x.astype(BF16)
    x2 = (x - x1.astype(F32)).astype(BF16)
    return x1, x2


def _stack_heads(x):
    lo = lax.broadcasted_iota(jnp.int32, x.shape, 1) < HEAD_DIM
    zero = jnp.zeros_like(x)
    return jnp.concatenate([jnp.where(lo, x, zero), jnp.where(lo, zero, x)], axis=0)


def _rwkv_kernel(pr_ref, sh0_ref, h0_ref, mu_ref, w0_ref, a0_ref, kk_ref, ka_ref, rk_ref,
                 lw_ref, lb_ref, w2_ref, a2_ref, g2_ref, gsum_ref, tri_ref,
                 o_ref, sn_ref,
                 h_st, xbuf, carry, at_s, rt_s, bt_s, kt_s, v_s, ep_s, y_s,
                 t_s, tav_s, lrb_s, lrkv_s, zbt_s, zkv_s, ar_s, pc_s,
                 *, tr, t_valid, n_state):
    c_len = RW_CHUNK
    n_ch = tr // c_len
    ti = pl.program_id(1)

    @pl.when(ti == 0)
    def _():
        has_state = pl.program_id(0) < n_state
        carry[...] = jnp.where(has_state, sh0_ref[0], 0.0)
        hd_r = lax.broadcasted_iota(jnp.int32, (PAIR, PAIR), 0) // HEAD_DIM
        hd_c = lax.broadcasted_iota(jnp.int32, (PAIR, PAIR), 1) // HEAD_DIM
        for p in range(N_PAIR):
            s_p = h0_ref[0, p]
            blk = jnp.where((hd_r == hd_c) & has_state, jnp.concatenate([s_p, s_p], axis=1), 0.0)
            h_st[p] = jnp.transpose(blk)

    x = pr_ref[0]
    xbuf[8:8 + tr, :] = x
    xbuf[7:8, :] = carry[...]
    prev = xbuf[7:7 + tr, :]
    carry[...] = pr_ref[0, tr - 1:tr, :]
    xm = x + (prev - x) * mu_ref[...]
    r = xm[:, 0:RW_WIDTH]
    kr = xm[:, RW_WIDTH:2 * RW_WIDTH]
    vr = xm[:, 2 * RW_WIDTH:3 * RW_WIDTH]
    wa = xm[:, 3 * RW_WIDTH:3 * RW_WIDTH + W_LORA + A_LORA]
    gd = xm[:, 3 * RW_WIDTH + W_LORA + A_LORA:]
    lane_wa = lax.broadcasted_iota(jnp.int32, wa.shape, 1)
    twa = jnp.where(lane_wa < W_LORA, jnp.tanh(wa), wa).astype(BF16)
    logw = -DECAY_SCALE * _sigmoid(w0_ref[...] + _dot(twa, w2_ref[...]))
    a_sig = _sigmoid(a0_ref[...] + _dot(twa, a2_ref[...]))
    g = _dot(_sigmoid(gd).astype(BF16), g2_ref[...])
    gsum = gsum_ref[...]
    kk = kr * kk_ref[...]
    kk = kk * lax.rsqrt(jnp.maximum(_group_sum(kk * kk, gsum), 1e-24))
    kr2 = kr * (1.0 + (a_sig - 1.0) * ka_ref[...])
    a_vec = -kk
    b_vec = kk * a_sig
    bonus = _group_sum(r * kr2 * rk_ref[...], gsum) * vr
    if t_valid % tr != 0:
        row = lax.broadcasted_iota(jnp.int32, (tr, 1), 0) + ti * tr
        valid = row < t_valid
        logw = jnp.where(valid, logw, 0.0)
        a_vec = jnp.where(valid, a_vec, 0.0)
        b_vec = jnp.where(valid, b_vec, 0.0)
        kr2 = jnp.where(valid, kr2, 0.0)
        vr = jnp.where(valid, vr, 0.0)
        bonus = jnp.where(valid, bonus, 0.0)
    l1, l2 = _split2(logw)
    tri = tri_ref[...]
    cs = _dot(tri, l1) + _dot(tri, l2)
    e_pos = jnp.exp(cs)
    e_neg = jnp.exp(-cs)
    at_s[...] = a_vec * jnp.exp(cs - logw)
    rt_s[...] = r * e_pos
    bt_s[...] = b_vec * e_neg
    kt_s[...] = kr2 * e_neg
    v_s[...] = vr
    ep_s[...] = e_pos

    idx_r = lax.broadcasted_iota(jnp.int32, (PAIR, PAIR), 0)
    idx_c = lax.broadcasted_iota(jnp.int32, (PAIR, PAIR), 1)
    same = (idx_r // c_len) == (idx_c // c_len)
    strict = same & ((idx_r % c_len) > (idx_c % c_len))
    incl = same & ((idx_r % c_len) >= (idx_c % c_len))
    eye = idx_r == idx_c
    eye_f = jnp.where(eye, 1.0, 0.0).astype(F32)

    zero = jnp.zeros((PAIR, PAIR), F32)
    n_par = next(n for n in (8, 4, 2, 1) if n_ch % n == 0)
    n_grp = n_ch // n_par
    items = [(dc, p) for dc in range(n_par) for p in range(N_PAIR)]
    pairs = range(N_PAIR)

    def s1_load(g):
        sel = [(pl.ds(pl.multiple_of((g * n_par + dc) * c_len, c_len), c_len),
                slice(PAIR * p, PAIR * (p + 1))) for dc, p in items]
        return dict(at=[at_s[rw, cl] for rw, cl in sel], rt=[rt_s[rw, cl] for rw, cl in sel],
                    bt=[bt_s[rw, cl] for rw, cl in sel], kt=[kt_s[rw, cl] for rw, cl in sel],
                    v=[v_s[rw, cl] for rw, cl in sel], ep=[ep_s[rw, cl] for rw, cl in sel])

    def s1_compute(ld):
        q_all = range(len(items))
        at, rt = ld["at"], ld["rt"]
        yb = [_stack_heads(x) for x in ld["bt"]]
        yk = [_stack_heads(x) for x in ld["kt"]]
        vst = [_stack_heads(x).astype(BF16) for x in ld["v"]]
        pc = [x[c_len - 1:c_len] for x in ld["ep"]]
        gmat = [_dot_nt(jnp.concatenate([_stack_heads(at[q]), _stack_heads(rt[q])], axis=0).astype(BF16),
                        jnp.concatenate([yb[q], yk[q]], axis=0).astype(BF16)) for q in q_all]
        aab = [jnp.where(strict, gmat[q][0:PAIR, 0:PAIR], zero) for q in q_all]
        aak = [jnp.where(strict, gmat[q][0:PAIR, PAIR:], zero).astype(BF16) for q in q_all]
        lrb = [jnp.where(incl, gmat[q][PAIR:, 0:PAIR], zero).astype(BF16) for q in q_all]
        lrk = [jnp.where(incl, gmat[q][PAIR:, PAIR:], zero).astype(BF16) for q in q_all]
        tinv = [eye_f + aab[q] for q in q_all]
        lp = aab
        n = 1
        order = min(c_len, t_valid)
        while 2 * n < order:
            lpb = [x.astype(BF16) for x in lp]
            lp = [_dot(x, x) for x in lpb]
            tinv = [tinv[q] + _dot(tinv[q].astype(BF16), lp[q].astype(BF16)) for q in q_all]
            n *= 2
        tb = [x.astype(BF16) for x in tinv]
        av = [_dot(aak[q], vst[q]).astype(BF16) for q in q_all]
        return dict(
            t=tb, tav=[_dot(tb[q], av[q]) for q in q_all], lrb=lrb,
            lrkv=[_dot(lrk[q], vst[q]) for q in q_all],
            zbt=[jnp.transpose(yb[q] * pc[q]).astype(BF16) for q in q_all],
            zkv=[_dot(jnp.transpose(yk[q] * pc[q]).astype(BF16), vst[q]) for q in q_all],
            ar=[jnp.concatenate([at[q], rt[q]], axis=0).astype(BF16) for q in q_all],
            pc=[jnp.sum(jnp.where(eye, jnp.broadcast_to(pc[q], (PAIR, PAIR)), zero), axis=-1, keepdims=True)
                for q in q_all])

    stage_bufs = dict(t=t_s, tav=tav_s, lrb=lrb_s, lrkv=lrkv_s, zbt=zbt_s, zkv=zkv_s, ar=ar_s, pc=pc_s)

    def s1_store(g, res):
        for q, (dc, p) in enumerate(items):
            for name, buf in stage_bufs.items():
                buf[g * n_par + dc, p] = res[name][q]

    def s2_load(g):
        return [{name: [buf[g * n_par + dc, p] for p in pairs] for name, buf in stage_bufs.items()}
                for dc in range(n_par)]

    def s2_compute(ld_group, hbd):
        ys = []
        for ld in ld_group:
            arh = [_dot(ld["ar"][p], hbd[p].astype(BF16)) for p in pairs]
            ub = [(_dot(ld["t"][p], _stack_heads(arh[p][0:c_len]).astype(BF16)) + ld["tav"][p]).astype(BF16)
                  for p in pairs]
            yst = [_dot(ld["lrb"][p], ub[p]) + ld["lrkv"][p] for p in pairs]
            ys.append([arh[p][c_len:] + yst[p][0:c_len] + yst[p][c_len:] for p in pairs])
            hbd = [ld["pc"][p] * hbd[p] + _dot(ld["zbt"][p], ub[p]) + ld["zkv"][p] for p in pairs]
        return ys, hbd

    def s2_store(g, ys, hbd):
        for dc in range(n_par):
            rows = pl.ds(pl.multiple_of((g * n_par + dc) * c_len, c_len), c_len)
            for p in pairs:
                y_s[rows, PAIR * p:PAIR * (p + 1)] = ys[dc][p]
        for p in pairs:
            h_st[p] = hbd[p]

    s1_store(0, s1_compute(s1_load(0)))

    def body(g, carry_):
        ld2 = s2_load(g)
        ld1 = s1_load(g + 1)
        hbd = [h_st[p] for p in pairs]
        res1 = s1_compute(ld1)
        ys, hbd = s2_compute(ld2, hbd)
        s2_store(g, ys, hbd)
        s1_store(g + 1, res1)
        return carry_

    lax.fori_loop(0, n_grp - 1, body, 0)
    ys_last, h_last = s2_compute(s2_load(n_grp - 1), [h_st[p] for p in pairs])
    s2_store(n_grp - 1, ys_last, h_last)

    @pl.when(ti == pl.num_programs(1) - 1)
    def _():
        first = lax.broadcasted_iota(jnp.int32, (PAIR, HEAD_DIM), 0) < HEAD_DIM
        for p in range(N_PAIR):
            s_t = jnp.transpose(h_st[p])
            sn_ref[0, p] = jnp.where(first, s_t[:, 0:HEAD_DIM], s_t[:, HEAD_DIM:])

    y = y_s[...]
    inv_n = 1.0 / HEAD_DIM
    mean = _group_sum(y, gsum) * inv_n
    d = y - mean
    var = _group_sum(d * d, gsum) * inv_n
    yn = d * lax.rsqrt(var + GN_EPS) * lw_ref[...] + lb_ref[...]
    o_ref[0] = ((yn + bonus) * g).astype(BF16)


def _rwkv(pr, shift0, s0, prm, tr, t_valid):
    b, t_pad, _ = pr.shape
    n_state = s0.shape[0]
    assert shift0.shape[0] == n_state
    last = n_state - 1
    assert t_pad % tr == 0 and tr % RW_CHUNK == 0
    n_ch = tr // RW_CHUNK
    tri = np.zeros((tr, tr), np.float32)
    for c in range(n_ch):
        tri[c * RW_CHUNK:(c + 1) * RW_CHUNK, c * RW_CHUNK:(c + 1) * RW_CHUNK] = np.tril(
            np.ones((RW_CHUNK, RW_CHUNK), np.float32))
    tri = jnp.asarray(tri, BF16)
    vec = _const_spec((1, RW_WIDTH))
    mat = lambda dt: pltpu.VMEM((n_ch, N_PAIR, PAIR, PAIR), dt)
    tile = lambda: pltpu.VMEM((tr, RW_WIDTH), F32)
    return pl.pallas_call(
        functools.partial(_rwkv_kernel, tr=tr, t_valid=t_valid, n_state=n_state),
        grid=(b, t_pad // tr),
        in_specs=[pl.BlockSpec((1, tr, RW_PROJ), lambda bi, ti: (bi, ti, 0)),
                  pl.BlockSpec((1, 1, RW_PROJ), lambda bi, ti: (jnp.minimum(bi, last), 0, 0)),
                  pl.BlockSpec((1, N_PAIR, PAIR, HEAD_DIM),
                               lambda bi, ti: (jnp.minimum(bi, last), 0, 0, 0)),
                  _const_spec((1, RW_PROJ)), vec, vec, vec, vec, vec, vec, vec,
                  _const_spec((W_LORA + A_LORA, RW_WIDTH)), _const_spec((W_LORA + A_LORA, RW_WIDTH)),
                  _const_spec((G_LORA, RW_WIDTH)), _const_spec((LANES, LANES)),
                  _const_spec((tr, tr))],
        out_specs=[pl.BlockSpec((1, tr, RW_WIDTH), lambda bi, ti: (bi, ti, 0)),
                   pl.BlockSpec((1, N_PAIR, PAIR, HEAD_DIM), lambda bi, ti: (bi, 0, 0, 0))],
        out_shape=[jax.ShapeDtypeStruct((b, t_pad, RW_WIDTH), BF16),
                   jax.ShapeDtypeStruct((b, N_PAIR, PAIR, HEAD_DIM), F32)],
        scratch_shapes=[pltpu.VMEM((N_PAIR, PAIR, PAIR), F32),
                        pltpu.VMEM((8 + tr, RW_PROJ), F32), pltpu.VMEM((1, RW_PROJ), F32),
                        tile(), tile(), tile(), tile(), tile(), tile(), tile(),
                        mat(BF16), mat(F32), mat(BF16), mat(F32), mat(BF16), mat(F32), mat(BF16),
                        pltpu.VMEM((n_ch, N_PAIR, PAIR, 1), F32)],
        compiler_params=_cparams(("arbitrary", "arbitrary")),
        name="rwkv",
    )(pr, shift0, s0, prm["mu"], prm["w0"], prm["a0"], prm["k_k"], prm["k_a"], prm["r_k"],
      prm["lnx_w"], prm["lnx_b"], prm["w2p"], prm["a2p"], prm["g2"], prm["gsum"], tri)


FF_COLS = 256
FF_STAGES = 4


def _ffn_kernel(x_ref, oa_ref, orw_ref, c0_ref, wo_ref, g_ref, wu_ref, cw_ref, cb_ref, wd_ref,
                y_ref, cn_ref, zbuf, cbuf, h_ref, x1_ref, act_ref, *, tm, ts, d_ff):
    off = zbuf.shape[1] - tm
    da = oa_ref.shape[1]

    @pl.when(pl.program_id(0) == 0)
    def _():
        cbuf[...] = c0_ref[...]

    x1 = x_ref[...] + _dot(oa_ref[...], wo_ref[0:da, :]) + _dot(orw_ref[...], wo_ref[da:, :])
    x1_ref[...] = x1
    ms = jnp.mean(x1 * x1, axis=-1, keepdims=True)
    h_ref[...] = (x1 * lax.rsqrt(ms + NORM_EPS) * g_ref[...]).astype(BF16)

    def up_proj(c0, zb):
        cols = slice(c0, c0 + FF_COLS)
        zb[off - 2 * ts:off, :] = cbuf[:, cols]
        zb[off:off + tm, :] = _dot(h_ref[...], wu_ref[:, cols])
        cbuf[:, cols] = zb[off + tm - 2 * ts:off + tm, :]

    def conv_cols(c0, zb):
        cols = slice(c0, c0 + FF_COLS)
        z2 = zb[off - 2 * ts:off - 2 * ts + tm, :]
        z1 = zb[off - ts:off - ts + tm, :]
        z = zb[off:off + tm, :]
        return (cb_ref[:, cols] + z2 * cw_ref[0:1, cols] + z1 * cw_ref[1:2, cols]
                + z * cw_ref[2:3, cols])

    def stage(c):
        up_proj(c * FF_COLS, zbuf.at[2 * (c % FF_STAGES)])
        up_proj(d_ff + c * FF_COLS, zbuf.at[2 * (c % FF_STAGES) + 1])

    n_chunks = d_ff // FF_COLS
    stage(0)
    for c in range(n_chunks):
        if c + 1 < n_chunks:
            stage(c + 1)
        gate = conv_cols(c * FF_COLS, zbuf.at[2 * (c % FF_STAGES)])
        up = conv_cols(d_ff + c * FF_COLS, zbuf.at[2 * (c % FF_STAGES) + 1])
        act_ref[:, c * FF_COLS:(c + 1) * FF_COLS] = (gate * _sigmoid(gate) * up).astype(BF16)

    y_ref[...] = x1_ref[...] + _dot(act_ref[...], wd_ref[...])
    cn_ref[...] = cbuf[...]


def _ffn(x, oa, orw, conv0, prm, tm, ts):
    rows, d = x.shape
    d_ff = prm["w_down"].shape[0]
    assert rows % tm == 0 and d_ff % FF_COLS == 0 and (ts == 1 or ts % 8 == 0)
    off = -(-2 * ts // 8) * 8
    row = lambda w: pl.BlockSpec((tm, w), lambda i: (i, 0))
    return pl.pallas_call(
        functools.partial(_ffn_kernel, tm=tm, ts=ts, d_ff=d_ff),
        grid=(rows // tm,),
        in_specs=[row(d), row(DA_WIDTH), row(RW_WIDTH), _const_spec((2 * ts, 2 * d_ff)),
                  _resident_spec((DA_WIDTH + RW_WIDTH, d)), _const_spec((1, d)),
                  _resident_spec((d, 2 * d_ff)), _const_spec((CONV_W, 2 * d_ff)),
                  _const_spec((1, 2 * d_ff)), _resident_spec((d_ff, d))],
        out_specs=[row(d), _const_spec((2 * ts, 2 * d_ff))],
        out_shape=[jax.ShapeDtypeStruct((rows, d), F32),
                   jax.ShapeDtypeStruct((2 * ts, 2 * d_ff), F32)],
        scratch_shapes=[pltpu.VMEM((2 * FF_STAGES, off + tm, FF_COLS), F32),
                        pltpu.VMEM((2 * ts, 2 * d_ff), F32),
                        pltpu.VMEM((tm, d), BF16), pltpu.VMEM((tm, d), F32),
                        pltpu.VMEM((tm, d_ff), BF16)],
        compiler_params=_cparams(("arbitrary",)),
        name="ffn",
    )(x, oa, orw, conv0, prm["w_out"], prm["ln2_g"], prm["w_up"], prm["conv_w"], prm["conv_b"],
      prm["w_down"])


def _rel_bucket(rel):
    nb = N_BUCKETS // 2
    max_exact = nb // 2
    bucket = jnp.where(rel > 0, nb, 0)
    n = jnp.abs(rel)
    nf = jnp.maximum(n, 1).astype(F32)
    large = max_exact + (jnp.log(nf / max_exact) / math.log(MAX_DISTANCE / max_exact)
                         * (nb - max_exact)).astype(jnp.int32)
    large = jnp.minimum(large, nb - 1)
    return bucket + jnp.where(n < max_exact, n, large)


def _bias_table(rel_bias, q_pos, k_pos, mask):
    n_q, n_k = len(q_pos), len(k_pos)
    assert np.all(np.diff(q_pos) == 1) and np.all(np.diff(k_pos) == 1)
    span = n_q + n_k - 1
    rel = jnp.asarray(int(k_pos[0]) - int(q_pos[0]) - (n_q - 1) + np.arange(span), jnp.int32)
    w = jnp.transpose(rel_bias[_rel_bucket(rel)]).astype(F32) * LOG2E
    return jnp.where(jnp.asarray(mask)[None], _toeplitz(w, n_q, n_k), NEG_INF)


def _skew(w, n_q, n_k):
    span = n_q + n_k - 1
    x = jnp.pad(w[:, :span], ((0, 0), (0, 1)))
    rows = jnp.tile(x, (1, n_q))[:, :n_q * span].reshape(w.shape[0], n_q, span)
    return rows[:, :, n_q - 1:n_q - 1 + n_k]


def _toeplitz(w, n_q, n_k):
    blk = LANES
    if n_q % blk or n_k % blk or n_q * n_k <= blk * blk:
        return _skew(w, n_q, n_k)
    nbq, nbk = n_q // blk, n_k // blk
    n_d = nbq + nbk - 1
    n_h = w.shape[0]
    wp = jnp.pad(w[:, :n_q + n_k - 1], ((0, 0), (0, 1)))
    seg = jnp.concatenate([wp[:, :n_d * blk].reshape(n_h, n_d, blk),
                           wp[:, blk:].reshape(n_h, n_d, blk)[..., :blk - 1]], axis=-1)
    x = jnp.pad(seg, ((0, 0), (0, 0), (0, 1)))
    rows = jnp.tile(x, (1, 1, blk))[..., :blk * (2 * blk - 1)].reshape(n_h, n_d, blk, 2 * blk - 1)
    blocks = rows[..., blk - 1:]
    return jnp.concatenate([jnp.concatenate([blocks[:, c - r + nbq - 1] for c in range(nbk)], axis=2)
                            for r in range(nbq)], axis=1)


BOUND_MAX_SPREAD = 100.0


def _score_bound(q_g, k_g, rel_bias):
    s_max = (HEAD_DIM ** 0.5) * LOG2E * 1.01 * jnp.max(jnp.abs(q_g)) * jnp.max(jnp.abs(k_g))
    b_hi = jnp.max(rel_bias) * LOG2E
    b_lo = jnp.min(rel_bias) * LOG2E
    use = (2.0 * s_max + (b_hi - b_lo)) <= BOUND_MAX_SPREAD
    return jnp.stack([use.astype(F32), (s_max + b_hi).astype(F32)])


def _ext_chunk(pos):
    return np.where(pos < N_META, -1, (pos - N_META) // CHUNK)


def _prompt_bias(rel_bias):
    tq = ATT_T
    fr = np.arange(tq) + N_META
    meta = np.arange(N_META)
    causal = _ext_chunk(fr)[None, :] <= _ext_chunk(fr)[:, None]
    tr = lambda b: jnp.swapaxes(b, 1, 2)
    b0 = tr(_bias_table(rel_bias, fr, fr, causal))
    bm1 = tr(_bias_table(rel_bias, fr + tq, fr, np.ones((tq, tq), bool)))
    bq0 = tr(_bias_table(rel_bias, fr, meta, np.ones((tq, N_META), bool)))
    bmm = _bias_table(rel_bias, meta, meta, np.ones((N_META, N_META), bool))
    assert tq + 1 >= MAX_DISTANCE
    cfar = rel_bias[_rel_bucket(jnp.asarray(-(tq + 1), jnp.int32))].astype(F32) * LOG2E
    return cfar, b0, bm1, bq0, bmm


def _decode_bias(rel_bias, n_cache, nq):
    k_pos = np.arange(n_cache + nq)
    q_pos = k_pos[n_cache:]
    mask = _ext_chunk(k_pos)[None, :] <= _ext_chunk(q_pos)[:, None]
    return _bias_table(rel_bias, q_pos, k_pos, mask)


def _block_ones(n, blk, dtype):
    idx = np.arange(n) // blk
    return jnp.asarray((idx[:, None] == idx[None, :]).astype(np.float32), dtype)


def kernel(x_prompt, x_sample, cache_k, cache_v, state_rwkv, state_shift, state_conv, meta_tokens,
           rel_bias, ln1_g, w_in, q_norm_g, k_norm_g, lam_q1, lam_k1, lam_q2, lam_k2, subln_g,
           mu_shift, w0, w2, a0, a2, g2, k_k, k_a, r_k, lnx_w, lnx_b, w_out, ln2_g, w_up, conv_w,
           conv_b, w_down):
    bp, seq, d = x_prompt.shape
    db, dt, _ = x_sample.shape
    depth = w_in.shape[0]
    d_ff = w_down.shape[1]
    n_cache = cache_k.shape[2]
    assert bp == 1 and dt == N_META, "the meta stream rides with the decode streams"
    assert cache_k.shape[3] == DA_HEADS and cache_k.shape[4] == 2 * HEAD_DIM
    nb = db + 1
    nb_pad = -(-nb // 8) * 8

    cfar, b0, bm1, bq0, bmm = _prompt_bias(rel_bias)
    bias_dec = _decode_bias(rel_bias, n_cache, dt)
    gsum = _block_ones(LANES, HEAD_DIM, BF16)
    gmean = gsum * (1.0 / HEAD_DIM)
    zrow = lambda n: jnp.zeros((n, RW_WIDTH), BF16)

    x_f = x_prompt[0]
    x_s = jnp.concatenate([x_sample, meta_tokens.astype(x_sample.dtype)[None]], axis=0)
    outs = [[] for _ in range(10)]
    for l in range(depth):
        lam_init = 0.8 - 0.6 * math.exp(-0.3 * l)
        lam = (jnp.exp(jnp.sum(lam_q1[l].astype(F32) * lam_k1[l].astype(F32)))
               - jnp.exp(jnp.sum(lam_q2[l].astype(F32) * lam_k2[l].astype(F32))) + lam_init).reshape(1)
        out_scale = 1.0 - lam_init
        tile128 = lambda g_: jnp.tile(g_.reshape(1, -1), (1, DA_WIDTH // g_.shape[-1]))
        qg, kg = tile128(q_norm_g[l]), tile128(k_norm_g[l])
        sg = subln_g[l].reshape(1, LANES)
        w_in_bf = w_in[l].astype(BF16)
        rw = {
            "mu": mu_shift[l].reshape(1, -1), "w0": w0[l].reshape(1, -1), "a0": a0[l].reshape(1, -1),
            "k_k": k_k[l].reshape(1, -1), "k_a": k_a[l].reshape(1, -1), "r_k": r_k[l].reshape(1, -1),
            "lnx_w": lnx_w[l].reshape(1, -1), "lnx_b": lnx_b[l].reshape(1, -1),
            "w2p": jnp.concatenate([w2[l].astype(BF16), zrow(A_LORA)], axis=0),
            "a2p": jnp.concatenate([zrow(W_LORA), a2[l].astype(BF16)], axis=0),
            "g2": g2[l].astype(BF16), "gsum": gsum,
        }
        ff = {
            "w_out": w_out[l].astype(BF16), "ln2_g": ln2_g[l].reshape(1, -1),
            "w_up": w_up[l].astype(BF16), "conv_w": conv_w[l], "conv_b": conv_b[l].reshape(1, -1),
            "w_down": w_down[l].astype(BF16),
        }

        q_f, k_f, v_f, pr_f, kb_t, vt_t = _proj(x_f, ln1_g[l].reshape(1, -1), w_in_bf, qg, kg, gmean,
                                                 512, True)
        q_s, k_s, v_s, pr_s = _proj(x_s.reshape(nb * dt, d), ln1_g[l].reshape(1, -1), w_in_bf, qg, kg,
                                    gmean, nb * dt, False)
        m0 = db * dt

        kb_m = k_s[m0:].astype(BF16)
        vb_m = v_s[m0:].astype(BF16)
        o_f, o_m = _attn_prompt(cfar, lam, _score_bound(q_norm_g[l], k_norm_g[l], rel_bias), q_f, kb_t, vt_t, kb_m, vb_m, jnp.transpose(vb_m), q_s[m0:],
                                b0, bm1, bq0, bmm, sg, out_scale)
        o_d = _attn_decode(lam, q_s, cache_k, cache_v, l, k_s, v_s, bias_dec, sg, out_scale, dt)
        o_s = jnp.concatenate([o_d, o_m], axis=0)

        pr_s3 = pr_s.reshape(nb, dt, RW_PROJ)
        pr_pad = jnp.pad(pr_s3, ((0, 0), (0, RW_CHUNK - dt), (0, 0)))
        pair_view = (N_PAIR, PAIR, HEAD_DIM)
        orw_s, sn_s = _rwkv(pr_pad, state_shift[l][:, None, :], state_rwkv[l].reshape(db, *pair_view),
                            rw, RW_CHUNK, dt)
        orw_f, sn_f = _rwkv(pr_f[None], pr_s3[db:, dt - 1:dt, :], sn_s[db:], rw, 512, seq)

        def time_major(a):
            a = jnp.pad(a.reshape(nb, dt, -1), ((0, nb_pad - nb), (0, 0), (0, 0)))
            return jnp.swapaxes(a, 0, 1).reshape(dt * nb_pad, -1)

        conv_s = jnp.concatenate([state_conv[l], jnp.zeros((1, CONV_W - 1, 2 * d_ff), F32)], axis=0)
        conv_s = jnp.pad(conv_s, ((0, nb_pad - nb), (0, 0), (0, 0)))
        conv_s = jnp.swapaxes(conv_s, 0, 1).reshape(2 * nb_pad, 2 * d_ff)
        y_s, cn_s = _ffn(time_major(x_s), time_major(o_s), time_major(orw_s[:, :dt]), conv_s, ff,
                         dt * nb_pad, nb_pad)
        cn_s = jnp.swapaxes(cn_s.reshape(2, nb_pad, 2 * d_ff), 0, 1)
        y_f, cn_f = _ffn(x_f, o_f, orw_f[0], cn_s[db], ff, 512, 1)
        y_s = jnp.swapaxes(y_s.reshape(dt, nb_pad, d), 0, 1)[:nb]

        hw = (DA_HEADS, 2 * HEAD_DIM)
        outs[0].append(jnp.concatenate([k_s[m0:].reshape(N_META, *hw), k_f], axis=0)[None])
        outs[1].append(jnp.concatenate([v_s[m0:].reshape(N_META, *hw), v_f], axis=0)[None])
        outs[2].append(sn_f.reshape(bp, RW_HEADS, HEAD_DIM, HEAD_DIM))
        outs[3].append(pr_f[seq - 1:seq])
        outs[4].append(cn_f[None])
        outs[5].append(k_s[:m0].reshape(db, dt, *hw))
        outs[6].append(v_s[:m0].reshape(db, dt, *hw))
        outs[7].append(sn_s[:db].reshape(db, RW_HEADS, HEAD_DIM, HEAD_DIM))
        outs[8].append(pr_s3[:db, dt - 1])
        outs[9].append(cn_s[:db])
        x_f, x_s = y_f, y_s

    return (x_f[None], x_s[:db], *[jnp.stack(o) for o in outs])
```

```python
import functools
import math

import numpy as np
import jax
import jax.numpy as jnp
from jax import lax
from jax.experimental import pallas as pl
from jax.experimental.pallas import tpu as pltpu

F32 = jnp.float32
BF16 = jnp.bfloat16

CHUNK = 64
N_META = 16
HEAD_DIM = 64
DA_HEADS = 4
RW_HEADS = 8
W_LORA = 64
A_LORA = 64
G_LORA = 128
CONV_W = 3
N_BUCKETS = 32
MAX_DISTANCE = 128
NORM_EPS = 1e-6
GN_EPS = 64e-5
NEG_INF = -1e30
LOG2E = math.log2(math.e)
DECAY_SCALE = math.exp(-0.5)

DA_WIDTH = DA_HEADS * 2 * HEAD_DIM
RW_WIDTH = RW_HEADS * HEAD_DIM
RW_PROJ = 3 * RW_WIDTH + W_LORA + A_LORA + G_LORA
LANES = 128
PAIR = 2 * HEAD_DIM
N_PAIR = RW_WIDTH // PAIR
RW_CHUNK = 64
ATT_T = 512
VMEM_LIMIT = 56 * 1024 * 1024


def _dot(a, b):
    return jnp.dot(a, b, preferred_element_type=F32)


def _dot_nt(a, b):
    return lax.dot_general(a, b, (((1,), (1,)), ((), ())), preferred_element_type=F32)


def _group_sum(x, blk):
    xb = x.astype(BF16)
    return jnp.concatenate([_dot(xb[:, LANES * p:LANES * (p + 1)], blk)
                            for p in range(x.shape[1] // LANES)], axis=1)


def _cparams(sem):
    return pltpu.CompilerParams(dimension_semantics=sem, vmem_limit_bytes=VMEM_LIMIT)


def _const_spec(shape):
    nd = len(shape)
    return pl.BlockSpec(shape, lambda *_: (0,) * nd)


def _resident_spec(shape):
    nd = len(shape)
    return pl.BlockSpec(shape, lambda *_: (0,) * nd, pipeline_mode=pl.Buffered(1))


def _proj_kernel(x_ref, g_ref, w_ref, qg_ref, kg_ref, gm_ref,
                 q_ref, k_ref, v_ref, pr_ref, *tile_refs, tm):
    x = x_ref[...]
    ms = jnp.mean(x * x, axis=-1, keepdims=True)
    h = (x * lax.rsqrt(ms + NORM_EPS) * g_ref[...]).astype(BF16)
    gm = gm_ref[...]

    def group_norm(t, g):
        ms_g = _group_sum(t * t, gm)
        return t * lax.rsqrt(ms_g + NORM_EPS) * g

    q = _dot(h, w_ref[:, 0:DA_WIDTH])
    k = _dot(h, w_ref[:, DA_WIDTH:2 * DA_WIDTH])
    pr_ref[...] = _dot(h, w_ref[:, 3 * DA_WIDTH:])
    v = _dot(h, w_ref[:, 2 * DA_WIDTH:3 * DA_WIDTH])
    q_ref[...] = (group_norm(q, qg_ref[...]) * (HEAD_DIM ** -0.5 * LOG2E)).astype(BF16)
    k = group_norm(k, kg_ref[...])
    if len(k_ref.shape) == 3:
        for hd in range(DA_HEADS):
            k_ref[:, hd, :] = k[:, hd * LANES:(hd + 1) * LANES]
            v_ref[:, hd, :] = v[:, hd * LANES:(hd + 1) * LANES]
    else:
        k_ref[...] = k
        v_ref[...] = v
    if tile_refs:
        kb_ref, vt_ref = tile_refs
        kb = k.astype(BF16)
        vt = jnp.transpose(v).astype(BF16)
        for hd in range(DA_HEADS):
            for jj in range(tm // ATT_T):
                kb_ref[hd, jj] = kb[jj * ATT_T:(jj + 1) * ATT_T, hd * LANES:(hd + 1) * LANES]
                vt_ref[hd, jj] = vt[hd * LANES:(hd + 1) * LANES, jj * ATT_T:(jj + 1) * ATT_T]


def _proj(x, ln1_g, w_in_bf, qg, kg, gmean, tm, emit_tiles):
    rows, d = x.shape
    n_in = w_in_bf.shape[1]
    assert rows % tm == 0
    row = lambda w: pl.BlockSpec((tm, w), lambda i: (i, 0))
    out_specs = [row(DA_WIDTH), row(DA_WIDTH), row(DA_WIDTH), row(RW_PROJ)]
    out_shape = [jax.ShapeDtypeStruct((rows, DA_WIDTH), BF16),
                 jax.ShapeDtypeStruct((rows, DA_WIDTH), F32),
                 jax.ShapeDtypeStruct((rows, DA_WIDTH), F32),
                 jax.ShapeDtypeStruct((rows, RW_PROJ), F32)]
    if emit_tiles:
        assert tm % ATT_T == 0
        tpt = tm // ATT_T
        for o in (1, 2):
            out_specs[o] = pl.BlockSpec((tm, DA_HEADS, LANES), lambda i: (i, 0, 0))
            out_shape[o] = jax.ShapeDtypeStruct((rows, DA_HEADS, LANES), F32)
        out_specs += [pl.BlockSpec((DA_HEADS, tpt, ATT_T, LANES), lambda i: (0, i, 0, 0)),
                      pl.BlockSpec((DA_HEADS, tpt, LANES, ATT_T), lambda i: (0, i, 0, 0))]
        out_shape += [jax.ShapeDtypeStruct((DA_HEADS, rows // ATT_T, ATT_T, LANES), BF16),
                      jax.ShapeDtypeStruct((DA_HEADS, rows // ATT_T, LANES, ATT_T), BF16)]
    return pl.pallas_call(
        functools.partial(_proj_kernel, tm=tm),
        grid=(rows // tm,),
        in_specs=[row(d), _const_spec((1, d)), _resident_spec((d, n_in)),
                  _const_spec((1, DA_WIDTH)), _const_spec((1, DA_WIDTH)),
                  _const_spec((LANES, LANES))],
        out_specs=out_specs,
        out_shape=out_shape,
        compiler_params=_cparams(("arbitrary",)),
        name="proj",
    )(x, ln1_g, w_in_bf, qg, kg, gmean)


def _stack_components(q):
    lo = lax.broadcasted_iota(jnp.int32, q.shape, 1) < HEAD_DIM
    zero = jnp.zeros_like(q)
    return jnp.concatenate([jnp.where(lo, q, zero), jnp.where(lo, zero, q)], axis=0)


def _sub_layer_norm(o, g, out_scale):
    ms = jnp.mean(o * o, axis=-1, keepdims=True)
    return o * lax.rsqrt(ms + NORM_EPS) * g * out_scale


def _attn_prompt_kernel(cfar_ref, lam_ref, fix_ref, q_ref, k_ref, vt_ref, km_ref, vm_ref, vmt_ref, qm_ref,
                        b0_ref, bm1_ref, bq0_ref, bmm_ref, g_ref, o_ref, om_ref,
                        acc_ref, m_ref, l_ref, s_buf, p_buf, a_buf, *, out_scale):
    h = pl.program_id(0)
    i = pl.program_id(1)
    tq = ATT_T
    cf = cfar_ref[h]
    lam = lam_ref[0]
    qst = _stack_components(q_ref[...])

    def both(b):
        return jnp.concatenate([b, b], axis=1)

    s = _dot_nt(km_ref[...], qst)
    s = s + both(jnp.where(i == 0, bq0_ref[0], cf))
    use_bound = fix_ref[0] > 0.5
    bound = fix_ref[1]
    m0 = jnp.where(use_bound, bound, jnp.max(s, axis=0, keepdims=True))
    p = jnp.exp2(s - m0)
    m_ref[...] = m0
    l_ref[...] = jnp.sum(p, axis=0, keepdims=True)
    acc_ref[...] = _dot(vmt_ref[...], p.astype(BF16))

    def scores(idx, slot):
        s_buf[slot] = _dot_nt(k_ref[0, idx], qst)

    def softmax(slot, bias):
        s = s_buf[slot]
        m_prev = m_ref[...]
        if bias.ndim == 0:
            m_new = jnp.maximum(m_prev, jnp.max(s, axis=0, keepdims=True) + bias)
            p = jnp.exp2(s - (m_new - bias))
        else:
            s = s + both(bias)
            m_new = jnp.maximum(m_prev, jnp.max(s, axis=0, keepdims=True))
            p = jnp.exp2(s - m_new)
        alpha = jnp.exp2(m_prev - m_new)
        l_ref[...] = alpha * l_ref[...] + jnp.sum(p, axis=0, keepdims=True)
        m_ref[...] = m_new
        p_buf[slot] = p.astype(BF16)
        a_buf[slot] = alpha

    def accumulate(idx, slot):
        acc_ref[...] = a_buf[slot] * acc_ref[...] + _dot(vt_ref[0, idx], p_buf[slot])

    n_far = jnp.maximum(i - 1, 0)
    off = n_far % 2
    p_buf[1] = jnp.zeros(p_buf.shape[1:], BF16)
    a_buf[1] = jnp.ones(a_buf.shape[1:], F32)
    scores(0, 0)

    def far_body(t, carry):
        u = 2 * t - off
        scores(u + 1, 1)
        softmax(0, jnp.where(u >= 0, cf, NEG_INF))
        accumulate(jnp.maximum(u - 1, 0), 1)
        scores(u + 2, 0)
        softmax(1, cf)
        accumulate(jnp.maximum(u, 0), 0)
        return carry

    def bound_softmax(slot, bias):
        if bias.ndim == 0:
            p = jnp.exp2(s_buf[slot] - (bound - bias))
        else:
            p = jnp.exp2(s_buf[slot] + (both(bias) - bound))
        l_ref[...] += jnp.sum(p, axis=0, keepdims=True)
        p_buf[slot] = p.astype(BF16)

    def bound_accumulate(idx, slot):
        acc_ref[...] += _dot(vt_ref[0, idx], p_buf[slot])

    def far_body_bound(t, carry):
        u = 2 * t - off
        scores(u + 1, 1)
        bound_softmax(0, jnp.where(u >= 0, cf, NEG_INF))
        bound_accumulate(jnp.maximum(u - 1, 0), 1)
        scores(u + 2, 0)
        bound_softmax(1, cf)
        bound_accumulate(jnp.maximum(u, 0), 0)
        return carry

    def tail(softmax_fn, accumulate_fn):
        scores(i, 1)
        softmax_fn(0, jnp.where(i >= 1, bm1_ref[0], NEG_INF))
        accumulate_fn(jnp.maximum(n_far - 1, 0), 1)
        softmax_fn(1, b0_ref[0])
        accumulate_fn(n_far, 0)
        accumulate_fn(i, 1)

    @pl.when(use_bound)
    def _():
        lax.fori_loop(0, (n_far + 1) // 2, far_body_bound, 0)
        tail(bound_softmax, bound_accumulate)

    @pl.when(jnp.logical_not(use_bound))
    def _():
        lax.fori_loop(0, (n_far + 1) // 2, far_body, 0)
        tail(softmax, accumulate)

    accn = acc_ref[...] * (1.0 / l_ref[...])
    o_t = accn[:, 0:tq] - lam * accn[:, tq:]
    o_ref[...] = _sub_layer_norm(jnp.transpose(o_t), g_ref[...], out_scale).astype(BF16)

    @pl.when(i == 0)
    def _():
        qm = _stack_components(qm_ref[...])
        bmm = bmm_ref[0]
        sm = _dot_nt(qm, km_ref[...]) + jnp.concatenate([bmm, bmm], axis=0)
        mm = jnp.max(sm, axis=-1, keepdims=True)
        pm = jnp.exp2(sm - mm)
        accm = _dot(pm.astype(BF16), vm_ref[...]) / jnp.sum(pm, axis=-1, keepdims=True)
        om = accm[0:N_META] - lam * accm[N_META:]
        om_ref[...] = _sub_layer_norm(om, g_ref[...], out_scale).astype(BF16)


def _attn_prompt(cfar, lam, fix, q_f, kb_t, vt_t, kb_m, vb_m, vbt_m, q_m, b0, bm1, bq0, bmm, subln_g,
                 out_scale):
    tf = q_f.shape[0]
    tq = ATT_T
    assert tf % tq == 0
    n_t = tf // tq
    smem = pl.BlockSpec(memory_space=pltpu.SMEM)
    head_col = lambda rows: pl.BlockSpec((rows, LANES), lambda h, i: (0, h))
    head_tile = lambda a, b: pl.BlockSpec((1, a, b), lambda h, i: (h, 0, 0))
    return pl.pallas_call(
        functools.partial(_attn_prompt_kernel, out_scale=out_scale),
        grid=(DA_HEADS, n_t),
        in_specs=[smem, smem, smem,
                  pl.BlockSpec((tq, LANES), lambda h, i: (i, h)),
                  pl.BlockSpec((1, n_t, tq, LANES), lambda h, i: (h, 0, 0, 0)),
                  pl.BlockSpec((1, n_t, LANES, tq), lambda h, i: (h, 0, 0, 0)),
                  head_col(N_META), head_col(N_META),
                  pl.BlockSpec((LANES, N_META), lambda h, i: (h, 0)),
                  head_col(N_META),
                  head_tile(tq, tq), head_tile(tq, tq), head_tile(N_META, tq),
                  head_tile(N_META, N_META), _const_spec((1, LANES))],
        out_specs=[pl.BlockSpec((tq, LANES), lambda h, i: (i, h)), head_col(N_META)],
        out_shape=[jax.ShapeDtypeStruct((tf, DA_WIDTH), BF16),
                   jax.ShapeDtypeStruct((N_META, DA_WIDTH), BF16)],
        scratch_shapes=[pltpu.VMEM((LANES, 2 * tq), F32), pltpu.VMEM((1, 2 * tq), F32),
                        pltpu.VMEM((1, 2 * tq), F32), pltpu.VMEM((2, tq, 2 * tq), F32),
                        pltpu.VMEM((2, tq, 2 * tq), BF16), pltpu.VMEM((2, 1, 2 * tq), F32)],
        compiler_params=_cparams(("arbitrary", "arbitrary")),
        name="attn_prompt",
    )(cfar, lam, fix, q_f, kb_t, vt_t, kb_m, vb_m, vbt_m, q_m, b0, bm1, bq0, bmm, subln_g)


DEC_HEADS = 4


def _attn_decode_kernel(lam_ref, q_ref, k_hbm, v_hbm, kn_ref, vn_ref, b_ref, g_ref, o_ref,
                        kbuf, vbuf, sem, *, layer, n_main, out_scale):
    bi = pl.program_id(0)
    hg = pl.program_id(1)
    n_hg = pl.num_programs(1)
    step = bi * n_hg + hg
    slot = step % 2

    def cache_copies(b_, hg_, slot_):
        cps = []
        for hh in range(DEC_HEADS):
            h_ = hg_ * DEC_HEADS + hh
            cps.append(pltpu.make_async_copy(k_hbm.at[layer, b_, :, h_, :], kbuf.at[slot_, hh],
                                             sem.at[0, slot_, hh]))
            cps.append(pltpu.make_async_copy(v_hbm.at[layer, b_, :, h_, :], vbuf.at[slot_, hh],
                                             sem.at[1, slot_, hh]))
        return cps

    @pl.when(step == 0)
    def _():
        for cp in cache_copies(bi, hg, slot):
            cp.start()

    @pl.when(step + 1 < pl.num_programs(0) * n_hg)
    def _():
        nxt = step + 1
        for cp in cache_copies(nxt // n_hg, nxt % n_hg, 1 - slot):
            cp.start()

    for cp in cache_copies(bi, hg, slot):
        cp.wait()

    lam = lam_ref[0]
    nq = q_ref.shape[0]
    n_cache = kbuf.shape[2]
    for hh in range(DEC_HEADS):
        cols = slice(LANES * hh, LANES * (hh + 1))
        k_ref = kbuf.at[slot, hh]
        v_ref = vbuf.at[slot, hh]
        qst = _stack_components(q_ref[:, cols])
        k_main = k_ref[0:n_main, :].astype(BF16)
        v_main = v_ref[0:n_main, :].astype(BF16)
        k_tail = jnp.concatenate([k_ref[n_main:n_cache, :], kn_ref[:, cols]], axis=0).astype(BF16)
        v_tail = jnp.concatenate([v_ref[n_main:n_cache, :], vn_ref[:, cols]], axis=0).astype(BF16)
        bias = jnp.concatenate([b_ref[hh], b_ref[hh]], axis=0)
        s1 = _dot_nt(qst, k_main) + bias[:, 0:n_main]
        s2 = _dot_nt(qst, k_tail) + bias[:, n_main:]
        m = jnp.maximum(jnp.max(s1, axis=-1, keepdims=True), jnp.max(s2, axis=-1, keepdims=True))
        p1 = jnp.exp2(s1 - m)
        p2 = jnp.exp2(s2 - m)
        l = jnp.sum(p1, axis=-1, keepdims=True) + jnp.sum(p2, axis=-1, keepdims=True)
        res = (_dot(p1.astype(BF16), v_main) + _dot(p2.astype(BF16), v_tail)) / l
        o = res[0:nq] - lam * res[nq:]
        o_ref[:, cols] = _sub_layer_norm(o, g_ref[...], out_scale).astype(BF16)


def _attn_decode(lam, q_s, cache_k, cache_v, layer, k_s, v_s, bias, subln_g, out_scale, nq):
    _, b, n_cache, n_h, _ = cache_k.shape
    assert n_h % DEC_HEADS == 0
    n_main = (n_cache // LANES) * LANES
    n_keys = n_cache + nq
    smem = pl.BlockSpec(memory_space=pltpu.SMEM)
    hbm = pl.BlockSpec(memory_space=pl.ANY)
    new = pl.BlockSpec((nq, DEC_HEADS * LANES), lambda bi, hg: (bi, hg))
    buf = pltpu.VMEM((2, DEC_HEADS, n_cache, LANES), F32)
    return pl.pallas_call(
        functools.partial(_attn_decode_kernel, layer=layer, n_main=n_main, out_scale=out_scale),
        grid=(b, n_h // DEC_HEADS),
        in_specs=[smem, new, hbm, hbm, new, new,
                  pl.BlockSpec((DEC_HEADS, nq, n_keys), lambda bi, hg: (hg, 0, 0)),
                  _const_spec((1, LANES))],
        out_specs=new,
        out_shape=jax.ShapeDtypeStruct((b * nq, DA_WIDTH), BF16),
        scratch_shapes=[buf, buf, pltpu.SemaphoreType.DMA((2, 2, DEC_HEADS))],
        compiler_params=_cparams(("arbitrary", "arbitrary")),
        name="attn_decode",
    )(lam, q_s, cache_k, cache_v, k_s, v_s, bias, subln_g)


def _sigmoid(x):
    return 1.0 / (1.0 + jnp.exp(-x))


def _split2(x):
    x1 = x.astype(BF16)
    x2 = (x - x1.astype(F32)).astype(BF16)
    return x1, x2


def _stack_heads(x):
    lo = lax.broadcasted_iota(jnp.int32, x.shape, 1) < HEAD_DIM
    zero = jnp.zeros_like(x)
    return jnp.concatenate([jnp.where(lo, x, zero), jnp.where(lo, zero, x)], axis=0)


def _rwkv_kernel(pr_ref, sh0_ref, h0_ref, mu_ref, w0_ref, a0_ref, kk_ref, ka_ref, rk_ref,
                 lw_ref, lb_ref, w2_ref, a2_ref, g2_ref, gsum_ref, tri_ref,
                 o_ref, sn_ref,
                 h_st, xbuf, carry, at_s, rt_s, bt_s, kt_s, v_s, ep_s, y_s,
                 t_s, tav_s, lrb_s, lrkv_s, zbt_s, zkv_s, ar_s, pc_s,
                 *, tr, t_valid, n_state):
    c_len = RW_CHUNK
    n_ch = tr // c_len
    ti = pl.program_id(1)

    @pl.when(ti == 0)
    def _():
        has_state = pl.program_id(0) < n_state
        carry[...] = jnp.where(has_state, sh0_ref[0], 0.0)
        hd_r = lax.broadcasted_iota(jnp.int32, (PAIR, PAIR), 0) // HEAD_DIM
        hd_c = lax.broadcasted_iota(jnp.int32, (PAIR, PAIR), 1) // HEAD_DIM
        for p in range(N_PAIR):
            s_p = h0_ref[0, p]
            blk = jnp.where((hd_r == hd_c) & has_state, jnp.concatenate([s_p, s_p], axis=1), 0.0)
            h_st[p] = jnp.transpose(blk)

    x = pr_ref[0]
    xbuf[8:8 + tr, :] = x
    xbuf[7:8, :] = carry[...]
    prev = xbuf[7:7 + tr, :]
    carry[...] = pr_ref[0, tr - 1:tr, :]
    xm = x + (prev - x) * mu_ref[...]
    r = xm[:, 0:RW_WIDTH]
    kr = xm[:, RW_WIDTH:2 * RW_WIDTH]
    vr = xm[:, 2 * RW_WIDTH:3 * RW_WIDTH]
    wa = xm[:, 3 * RW_WIDTH:3 * RW_WIDTH + W_LORA + A_LORA]
    gd = xm[:, 3 * RW_WIDTH + W_LORA + A_LORA:]
    lane_wa = lax.broadcasted_iota(jnp.int32, wa.shape, 1)
    twa = jnp.where(lane_wa < W_LORA, jnp.tanh(wa), wa).astype(BF16)
    logw = -DECAY_SCALE * _sigmoid(w0_ref[...] + _dot(twa, w2_ref[...]))
    a_sig = _sigmoid(a0_ref[...] + _dot(twa, a2_ref[...]))
    g = _dot(_sigmoid(gd).astype(BF16), g2_ref[...])
    gsum = gsum_ref[...]
    kk = kr * kk_ref[...]
    kk = kk * lax.rsqrt(jnp.maximum(_group_sum(kk * kk, gsum), 1e-24))
    kr2 = kr * (1.0 + (a_sig - 1.0) * ka_ref[...])
    a_vec = -kk
    b_vec = kk * a_sig
    bonus = _group_sum(r * kr2 * rk_ref[...], gsum) * vr
    if t_valid % tr != 0:
        row = lax.broadcasted_iota(jnp.int32, (tr, 1), 0) + ti * tr
        valid = row < t_valid
        logw = jnp.where(valid, logw, 0.0)
        a_vec = jnp.where(valid, a_vec, 0.0)
        b_vec = jnp.where(valid, b_vec, 0.0)
        kr2 = jnp.where(valid, kr2, 0.0)
        vr = jnp.where(valid, vr, 0.0)
        bonus = jnp.where(valid, bonus, 0.0)
    l1, l2 = _split2(logw)
    tri = tri_ref[...]
    cs = _dot(tri, l1) + _dot(tri, l2)
    e_pos = jnp.exp(cs)
    e_neg = jnp.exp(-cs)
    at_s[...] = a_vec * jnp.exp(cs - logw)
    rt_s[...] = r * e_pos
    bt_s[...] = b_vec * e_neg
    kt_s[...] = kr2 * e_neg
    v_s[...] = vr
    ep_s[...] = e_pos

    idx_r = lax.broadcasted_iota(jnp.int32, (PAIR, PAIR), 0)
    idx_c = lax.broadcasted_iota(jnp.int32, (PAIR, PAIR), 1)
    same = (idx_r // c_len) == (idx_c // c_len)
    strict = same & ((idx_r % c_len) > (idx_c % c_len))
    incl = same & ((idx_r % c_len) >= (idx_c % c_len))
    eye = idx_r == idx_c
    eye_f = jnp.where(eye, 1.0, 0.0).astype(F32)

    zero = jnp.zeros((PAIR, PAIR), F32)
    n_par = next(n for n in (8, 4, 2, 1) if n_ch % n == 0)
    n_grp = n_ch // n_par
    items = [(dc, p) for dc in range(n_par) for p in range(N_PAIR)]
    pairs = range(N_PAIR)

    def s1_load(g):
        sel = [(pl.ds(pl.multiple_of((g * n_par + dc) * c_len, c_len), c_len),
                slice(PAIR * p, PAIR * (p + 1))) for dc, p in items]
        return dict(at=[at_s[rw, cl] for rw, cl in sel], rt=[rt_s[rw, cl] for rw, cl in sel],
                    bt=[bt_s[rw, cl] for rw, cl in sel], kt=[kt_s[rw, cl] for rw, cl in sel],
                    v=[v_s[rw, cl] for rw, cl in sel], ep=[ep_s[rw, cl] for rw, cl in sel])

    def s1_compute(ld):
        q_all = range(len(items))
        at, rt = ld["at"], ld["rt"]
        yb = [_stack_heads(x) for x in ld["bt"]]
        yk = [_stack_heads(x) for x in ld["kt"]]
        vst = [_stack_heads(x).astype(BF16) for x in ld["v"]]
        pc = [x[c_len - 1:c_len] for x in ld["ep"]]
        gmat = [_dot_nt(jnp.concatenate([_stack_heads(at[q]), _stack_heads(rt[q])], axis=0).astype(BF16),
                        jnp.concatenate([yb[q], yk[q]], axis=0).astype(BF16)) for q in q_all]
        aab = [jnp.where(strict, gmat[q][0:PAIR, 0:PAIR], zero) for q in q_all]
        aak = [jnp.where(strict, gmat[q][0:PAIR, PAIR:], zero).astype(BF16) for q in q_all]
        lrb = [jnp.where(incl, gmat[q][PAIR:, 0:PAIR], zero).astype(BF16) for q in q_all]
        lrk = [jnp.where(incl, gmat[q][PAIR:, PAIR:], zero).astype(BF16) for q in q_all]
        tinv = [eye_f + aab[q] for q in q_all]
        lp = aab
        n = 1
        order = min(c_len, t_valid)
        while 2 * n < order:
            lpb = [x.astype(BF16) for x in lp]
            lp = [_dot(x, x) for x in lpb]
            tinv = [tinv[q] + _dot(tinv[q].astype(BF16), lp[q].astype(BF16)) for q in q_all]
            n *= 2
        tb = [x.astype(BF16) for x in tinv]
        av = [_dot(aak[q], vst[q]).astype(BF16) for q in q_all]
        return dict(
            t=tb, tav=[_dot(tb[q], av[q]) for q in q_all], lrb=lrb,
            lrkv=[_dot(lrk[q], vst[q]) for q in q_all],
            zbt=[jnp.transpose(yb[q] * pc[q]).astype(BF16) for q in q_all],
            zkv=[_dot(jnp.transpose(yk[q] * pc[q]).astype(BF16), vst[q]) for q in q_all],
            ar=[jnp.concatenate([at[q], rt[q]], axis=0).astype(BF16) for q in q_all],
            pc=[jnp.sum(jnp.where(eye, jnp.broadcast_to(pc[q], (PAIR, PAIR)), zero), axis=-1, keepdims=True)
                for q in q_all])

    stage_bufs = dict(t=t_s, tav=tav_s, lrb=lrb_s, lrkv=lrkv_s, zbt=zbt_s, zkv=zkv_s, ar=ar_s, pc=pc_s)

    def s1_store(g, res):
        for q, (dc, p) in enumerate(items):
            for name, buf in stage_bufs.items():
                buf[g * n_par + dc, p] = res[name][q]

    def s2_load(g):
        return [{name: [buf[g * n_par + dc, p] for p in pairs] for name, buf in stage_bufs.items()}
                for dc in range(n_par)]

    def s2_compute(ld_group, hbd):
        ys = []
        for ld in ld_group:
            arh = [_dot(ld["ar"][p], hbd[p].astype(BF16)) for p in pairs]
            ub = [(_dot(ld["t"][p], _stack_heads(arh[p][0:c_len]).astype(BF16)) + ld["tav"][p]).astype(BF16)
                  for p in pairs]
            yst = [_dot(ld["lrb"][p], ub[p]) + ld["lrkv"][p] for p in pairs]
            ys.append([arh[p][c_len:] + yst[p][0:c_len] + yst[p][c_len:] for p in pairs])
            hbd = [ld["pc"][p] * hbd[p] + _dot(ld["zbt"][p], ub[p]) + ld["zkv"][p] for p in pairs]
        return ys, hbd

    def s2_store(g, ys, hbd):
        for dc in range(n_par):
            rows = pl.ds(pl.multiple_of((g * n_par + dc) * c_len, c_len), c_len)
            for p in pairs:
                y_s[rows, PAIR * p:PAIR * (p + 1)] = ys[dc][p]
        for p in pairs:
            h_st[p] = hbd[p]

    s1_store(0, s1_compute(s1_load(0)))

    def body(g, carry_):
        ld2 = s2_load(g)
        ld1 = s1_load(g + 1)
        hbd = [h_st[p] for p in pairs]
        res1 = s1_compute(ld1)
        ys, hbd = s2_compute(ld2, hbd)
        s2_store(g, ys, hbd)
        s1_store(g + 1, res1)
        return carry_

    lax.fori_loop(0, n_grp - 1, body, 0)
    ys_last, h_last = s2_compute(s2_load(n_grp - 1), [h_st[p] for p in pairs])
    s2_store(n_grp - 1, ys_last, h_last)

    @pl.when(ti == pl.num_programs(1) - 1)
    def _():
        first = lax.broadcasted_iota(jnp.int32, (PAIR, HEAD_DIM), 0) < HEAD_DIM
        for p in range(N_PAIR):
            s_t = jnp.transpose(h_st[p])
            sn_ref[0, p] = jnp.where(first, s_t[:, 0:HEAD_DIM], s_t[:, HEAD_DIM:])

    y = y_s[...]
    inv_n = 1.0 / HEAD_DIM
    mean = _group_sum(y, gsum) * inv_n
    d = y - mean
    var = _group_sum(d * d, gsum) * inv_n
    yn = d * lax.rsqrt(var + GN_EPS) * lw_ref[...] + lb_ref[...]
    o_ref[0] = ((yn + bonus) * g).astype(BF16)


def _rwkv(pr, shift0, s0, prm, tr, t_valid):
    b, t_pad, _ = pr.shape
    n_state = s0.shape[0]
    assert shift0.shape[0] == n_state
    last = n_state - 1
    assert t_pad % tr == 0 and tr % RW_CHUNK == 0
    n_ch = tr // RW_CHUNK
    tri = np.zeros((tr, tr), np.float32)
    for c in range(n_ch):
        tri[c * RW_CHUNK:(c + 1) * RW_CHUNK, c * RW_CHUNK:(c + 1) * RW_CHUNK] = np.tril(
            np.ones((RW_CHUNK, RW_CHUNK), np.float32))
    tri = jnp.asarray(tri, BF16)
    vec = _const_spec((1, RW_WIDTH))
    mat = lambda dt: pltpu.VMEM((n_ch, N_PAIR, PAIR, PAIR), dt)
    tile = lambda: pltpu.VMEM((tr, RW_WIDTH), F32)
    return pl.pallas_call(
        functools.partial(_rwkv_kernel, tr=tr, t_valid=t_valid, n_state=n_state),
        grid=(b, t_pad // tr),
        in_specs=[pl.BlockSpec((1, tr, RW_PROJ), lambda bi, ti: (bi, ti, 0)),
                  pl.BlockSpec((1, 1, RW_PROJ), lambda bi, ti: (jnp.minimum(bi, last), 0, 0)),
                  pl.BlockSpec((1, N_PAIR, PAIR, HEAD_DIM),
                               lambda bi, ti: (jnp.minimum(bi, last), 0, 0, 0)),
                  _const_spec((1, RW_PROJ)), vec, vec, vec, vec, vec, vec, vec,
                  _const_spec((W_LORA + A_LORA, RW_WIDTH)), _const_spec((W_LORA + A_LORA, RW_WIDTH)),
                  _const_spec((G_LORA, RW_WIDTH)), _const_spec((LANES, LANES)),
                  _const_spec((tr, tr))],
        out_specs=[pl.BlockSpec((1, tr, RW_WIDTH), lambda bi, ti: (bi, ti, 0)),
                   pl.BlockSpec((1, N_PAIR, PAIR, HEAD_DIM), lambda bi, ti: (bi, 0, 0, 0))],
        out_shape=[jax.ShapeDtypeStruct((b, t_pad, RW_WIDTH), BF16),
                   jax.ShapeDtypeStruct((b, N_PAIR, PAIR, HEAD_DIM), F32)],
        scratch_shapes=[pltpu.VMEM((N_PAIR, PAIR, PAIR), F32),
                        pltpu.VMEM((8 + tr, RW_PROJ), F32), pltpu.VMEM((1, RW_PROJ), F32),
                        tile(), tile(), tile(), tile(), tile(), tile(), tile(),
                        mat(BF16), mat(F32), mat(BF16), mat(F32), mat(BF16), mat(F32), mat(BF16),
                        pltpu.VMEM((n_ch, N_PAIR, PAIR, 1), F32)],
        compiler_params=_cparams(("arbitrary", "arbitrary")),
        name="rwkv",
    )(pr, shift0, s0, prm["mu"], prm["w0"], prm["a0"], prm["k_k"], prm["k_a"], prm["r_k"],
      prm["lnx_w"], prm["lnx_b"], prm["w2p"], prm["a2p"], prm["g2"], prm["gsum"], tri)


FF_COLS = 256
FF_STAGES = 11


def _ffn_kernel(x_ref, oa_ref, orw_ref, c0_ref, wo_ref, g_ref, wu_ref, cw_ref, cb_ref, wd_ref,
                y_ref, cn_ref, zbuf, cbuf, h_ref, x1_ref, act_ref, *, tm, ts, d_ff):
    off = zbuf.shape[1] - tm
    da = oa_ref.shape[1]

    @pl.when(pl.program_id(0) == 0)
    def _():
        cbuf[...] = c0_ref[...]

    x1 = x_ref[...] + _dot(oa_ref[...], wo_ref[0:da, :]) + _dot(orw_ref[...], wo_ref[da:, :])
    x1_ref[...] = x1
    ms = jnp.mean(x1 * x1, axis=-1, keepdims=True)
    h_ref[...] = (x1 * lax.rsqrt(ms + NORM_EPS) * g_ref[...]).astype(BF16)

    def up_proj(c0, zb):
        cols = slice(c0, c0 + FF_COLS)
        zb[off - 2 * ts:off, :] = cbuf[:, cols]
        zb[off:off + tm, :] = _dot(h_ref[...], wu_ref[:, cols])
        cbuf[:, cols] = zb[off + tm - 2 * ts:off + tm, :]

    def conv_cols(c0, zb):
        cols = slice(c0, c0 + FF_COLS)
        z2 = zb[off - 2 * ts:off - 2 * ts + tm, :]
        z1 = zb[off - ts:off - ts + tm, :]
        z = zb[off:off + tm, :]
        return (cb_ref[:, cols] + z2 * cw_ref[0:1, cols] + z1 * cw_ref[1:2, cols]
                + z * cw_ref[2:3, cols])

    def stage(c):
        up_proj(c * FF_COLS, zbuf.at[2 * (c % FF_STAGES)])
        up_proj(d_ff + c * FF_COLS, zbuf.at[2 * (c % FF_STAGES) + 1])

    n_chunks = d_ff // FF_COLS
    stage(0)
    for c in range(n_chunks):
        if c + 1 < n_chunks:
            stage(c + 1)
        gate = conv_cols(c * FF_COLS, zbuf.at[2 * (c % FF_STAGES)])
        up = conv_cols(d_ff + c * FF_COLS, zbuf.at[2 * (c % FF_STAGES) + 1])
        act_ref[:, c * FF_COLS:(c + 1) * FF_COLS] = (gate * _sigmoid(gate) * up).astype(BF16)

    y_ref[...] = x1_ref[...] + _dot(act_ref[...], wd_ref[...])
    cn_ref[...] = cbuf[...]


def _ffn(x, oa, orw, conv0, prm, tm, ts):
    rows, d = x.shape
    d_ff = prm["w_down"].shape[0]
    assert rows % tm == 0 and d_ff % FF_COLS == 0 and (ts == 1 or ts % 8 == 0)
    off = -(-2 * ts // 8) * 8
    row = lambda w: pl.BlockSpec((tm, w), lambda i: (i, 0))
    return pl.pallas_call(
        functools.partial(_ffn_kernel, tm=tm, ts=ts, d_ff=d_ff),
        grid=(rows // tm,),
        in_specs=[row(d), row(DA_WIDTH), row(RW_WIDTH), _const_spec((2 * ts, 2 * d_ff)),
                  _resident_spec((DA_WIDTH + RW_WIDTH, d)), _const_spec((1, d)),
                  _resident_spec((d, 2 * d_ff)), _const_spec((CONV_W, 2 * d_ff)),
                  _const_spec((1, 2 * d_ff)), _resident_spec((d_ff, d))],
        out_specs=[row(d), _const_spec((2 * ts, 2 * d_ff))],
        out_shape=[jax.ShapeDtypeStruct((rows, d), F32),
                   jax.ShapeDtypeStruct((2 * ts, 2 * d_ff), F32)],
        scratch_shapes=[pltpu.VMEM((2 * FF_STAGES, off + tm, FF_COLS), F32),
                        pltpu.VMEM((2 * ts, 2 * d_ff), F32),
                        pltpu.VMEM((tm, d), BF16), pltpu.VMEM((tm, d), F32),
                        pltpu.VMEM((tm, d_ff), BF16)],
        compiler_params=_cparams(("arbitrary",)),
        name="ffn",
    )(x, oa, orw, conv0, prm["w_out"], prm["ln2_g"], prm["w_up"], prm["conv_w"], prm["conv_b"],
      prm["w_down"])


def _rel_bucket(rel):
    nb = N_BUCKETS // 2
    max_exact = nb // 2
    bucket = jnp.where(rel > 0, nb, 0)
    n = jnp.abs(rel)
    nf = jnp.maximum(n, 1).astype(F32)
    large = max_exact + (jnp.log(nf / max_exact) / math.log(MAX_DISTANCE / max_exact)
                         * (nb - max_exact)).astype(jnp.int32)
    large = jnp.minimum(large, nb - 1)
    return bucket + jnp.where(n < max_exact, n, large)


def _bias_table(rel_bias, q_pos, k_pos, mask):
    n_q, n_k = len(q_pos), len(k_pos)
    assert np.all(np.diff(q_pos) == 1) and np.all(np.diff(k_pos) == 1)
    span = n_q + n_k - 1
    rel = jnp.asarray(int(k_pos[0]) - int(q_pos[0]) - (n_q - 1) + np.arange(span), jnp.int32)
    w = jnp.transpose(rel_bias[_rel_bucket(rel)]).astype(F32) * LOG2E
    return jnp.where(jnp.asarray(mask)[None], _toeplitz(w, n_q, n_k), NEG_INF)


def _skew(w, n_q, n_k):
    span = n_q + n_k - 1
    x = jnp.pad(w[:, :span], ((0, 0), (0, 1)))
    rows = jnp.tile(x, (1, n_q))[:, :n_q * span].reshape(w.shape[0], n_q, span)
    return rows[:, :, n_q - 1:n_q - 1 + n_k]


def _toeplitz(w, n_q, n_k):
    blk = LANES
    if n_q % blk or n_k % blk or n_q * n_k <= blk * blk:
        return _skew(w, n_q, n_k)
    nbq, nbk = n_q // blk, n_k // blk
    n_d = nbq + nbk - 1
    n_h = w.shape[0]
    wp = jnp.pad(w[:, :n_q + n_k - 1], ((0, 0), (0, 1)))
    seg = jnp.concatenate([wp[:, :n_d * blk].reshape(n_h, n_d, blk),
                           wp[:, blk:].reshape(n_h, n_d, blk)[..., :blk - 1]], axis=-1)
    x = jnp.pad(seg, ((0, 0), (0, 0), (0, 1)))
    rows = jnp.tile(x, (1, 1, blk))[..., :blk * (2 * blk - 1)].reshape(n_h, n_d, blk, 2 * blk - 1)
    blocks = rows[..., blk - 1:]
    return jnp.concatenate([jnp.concatenate([blocks[:, c - r + nbq - 1] for c in range(nbk)], axis=2)
                            for r in range(nbq)], axis=1)


BOUND_MAX_SPREAD = 100.0


def _score_bound(q_g, k_g, rel_bias):
    s_max = (HEAD_DIM ** 0.5) * LOG2E * 1.01 * jnp.max(jnp.abs(q_g)) * jnp.max(jnp.abs(k_g))
    b_hi = jnp.max(rel_bias) * LOG2E
    b_lo = jnp.min(rel_bias) * LOG2E
    use = (2.0 * s_max + (b_hi - b_lo)) <= BOUND_MAX_SPREAD
    return jnp.stack([use.astype(F32), (s_max + b_hi).astype(F32)])


def _ext_chunk(pos):
    return np.where(pos < N_META, -1, (pos - N_META) // CHUNK)


def _prompt_bias(rel_bias):
    tq = ATT_T
    fr = np.arange(tq) + N_META
    meta = np.arange(N_META)
    causal = _ext_chunk(fr)[None, :] <= _ext_chunk(fr)[:, None]
    tr = lambda b: jnp.swapaxes(b, 1, 2)
    b0 = tr(_bias_table(rel_bias, fr, fr, causal))
    bm1 = tr(_bias_table(rel_bias, fr + tq, fr, np.ones((tq, tq), bool)))
    bq0 = tr(_bias_table(rel_bias, fr, meta, np.ones((tq, N_META), bool)))
    bmm = _bias_table(rel_bias, meta, meta, np.ones((N_META, N_META), bool))
    assert tq + 1 >= MAX_DISTANCE
    cfar = rel_bias[_rel_bucket(jnp.asarray(-(tq + 1), jnp.int32))].astype(F32) * LOG2E
    return cfar, b0, bm1, bq0, bmm


def _decode_bias(rel_bias, n_cache, nq):
    k_pos = np.arange(n_cache + nq)
    q_pos = k_pos[n_cache:]
    mask = _ext_chunk(k_pos)[None, :] <= _ext_chunk(q_pos)[:, None]
    return _bias_table(rel_bias, q_pos, k_pos, mask)


def _block_ones(n, blk, dtype):
    idx = np.arange(n) // blk
    return jnp.asarray((idx[:, None] == idx[None, :]).astype(np.float32), dtype)


def kernel(x_prompt, x_sample, cache_k, cache_v, state_rwkv, state_shift, state_conv, meta_tokens,
           rel_bias, ln1_g, w_in, q_norm_g, k_norm_g, lam_q1, lam_k1, lam_q2, lam_k2, subln_g,
           mu_shift, w0, w2, a0, a2, g2, k_k, k_a, r_k, lnx_w, lnx_b, w_out, ln2_g, w_up, conv_w,
           conv_b, w_down):
    bp, seq, d = x_prompt.shape
    db, dt, _ = x_sample.shape
    depth = w_in.shape[0]
    d_ff = w_down.shape[1]
    n_cache = cache_k.shape[2]
    assert bp == 1 and dt == N_META, "the meta stream rides with the decode streams"
    assert cache_k.shape[3] == DA_HEADS and cache_k.shape[4] == 2 * HEAD_DIM
    nb = db + 1
    nb_pad = -(-nb // 8) * 8

    cfar, b0, bm1, bq0, bmm = _prompt_bias(rel_bias)
    bias_dec = _decode_bias(rel_bias, n_cache, dt)
    gsum = _block_ones(LANES, HEAD_DIM, BF16)
    gmean = gsum * (1.0 / HEAD_DIM)
    zrow = lambda n: jnp.zeros((n, RW_WIDTH), BF16)

    x_f = x_prompt[0]
    x_s = jnp.concatenate([x_sample, meta_tokens.astype(x_sample.dtype)[None]], axis=0)
    outs = [[] for _ in range(10)]
    for l in range(depth):
        lam_init = 0.8 - 0.6 * math.exp(-0.3 * l)
        lam = (jnp.exp(jnp.sum(lam_q1[l].astype(F32) * lam_k1[l].astype(F32)))
               - jnp.exp(jnp.sum(lam_q2[l].astype(F32) * lam_k2[l].astype(F32))) + lam_init).reshape(1)
        out_scale = 1.0 - lam_init
        tile128 = lambda g_: jnp.tile(g_.reshape(1, -1), (1, DA_WIDTH // g_.shape[-1]))
        qg, kg = tile128(q_norm_g[l]), tile128(k_norm_g[l])
        sg = subln_g[l].reshape(1, LANES)
        w_in_bf = w_in[l].astype(BF16)
        rw = {
            "mu": mu_shift[l].reshape(1, -1), "w0": w0[l].reshape(1, -1), "a0": a0[l].reshape(1, -1),
            "k_k": k_k[l].reshape(1, -1), "k_a": k_a[l].reshape(1, -1), "r_k": r_k[l].reshape(1, -1),
            "lnx_w": lnx_w[l].reshape(1, -1), "lnx_b": lnx_b[l].reshape(1, -1),
            "w2p": jnp.concatenate([w2[l].astype(BF16), zrow(A_LORA)], axis=0),
            "a2p": jnp.concatenate([zrow(W_LORA), a2[l].astype(BF16)], axis=0),
            "g2": g2[l].astype(BF16), "gsum": gsum,
        }
        ff = {
            "w_out": w_out[l].astype(BF16), "ln2_g": ln2_g[l].reshape(1, -1),
            "w_up": w_up[l].astype(BF16), "conv_w": conv_w[l], "conv_b": conv_b[l].reshape(1, -1),
            "w_down": w_down[l].astype(BF16),
        }

        q_f, k_f, v_f, pr_f, kb_t, vt_t = _proj(x_f, ln1_g[l].reshape(1, -1), w_in_bf, qg, kg, gmean,
                                                 512, True)
        q_s, k_s, v_s, pr_s = _proj(x_s.reshape(nb * dt, d), ln1_g[l].reshape(1, -1), w_in_bf, qg, kg,
                                    gmean, nb * dt, False)
        m0 = db * dt

        kb_m = k_s[m0:].astype(BF16)
        vb_m = v_s[m0:].astype(BF16)
        o_f, o_m = _attn_prompt(cfar, lam, _score_bound(q_norm_g[l], k_norm_g[l], rel_bias), q_f, kb_t, vt_t, kb_m, vb_m, jnp.transpose(vb_m), q_s[m0:],
                                b0, bm1, bq0, bmm, sg, out_scale)
        o_d = _attn_decode(lam, q_s, cache_k, cache_v, l, k_s, v_s, bias_dec, sg, out_scale, dt)
        o_s = jnp.concatenate([o_d, o_m], axis=0)

        pr_s3 = pr_s.reshape(nb, dt, RW_PROJ)
        pr_pad = jnp.pad(pr_s3, ((0, 0), (0, RW_CHUNK - dt), (0, 0)))
        pair_view = (N_PAIR, PAIR, HEAD_DIM)
        orw_s, sn_s = _rwkv(pr_pad, state_shift[l][:, None, :], state_rwkv[l].reshape(db, *pair_view),
                            rw, RW_CHUNK, dt)
        orw_f, sn_f = _rwkv(pr_f[None], pr_s3[db:, dt - 1:dt, :], sn_s[db:], rw, 512, seq)

        def time_major(a):
            a = jnp.pad(a.reshape(nb, dt, -1), ((0, nb_pad - nb), (0, 0), (0, 0)))
            return jnp.swapaxes(a, 0, 1).reshape(dt * nb_pad, -1)

        conv_s = jnp.concatenate([state_conv[l], jnp.zeros((1, CONV_W - 1, 2 * d_ff), F32)], axis=0)
        conv_s = jnp.pad(conv_s, ((0, nb_pad - nb), (0, 0), (0, 0)))
        conv_s = jnp.swapaxes(conv_s, 0, 1).reshape(2 * nb_pad, 2 * d_ff)
        y_s, cn_s = _ffn(time_major(x_s), time_major(o_s), time_major(orw_s[:, :dt]), conv_s, ff,
                         dt * nb_pad, nb_pad)
        cn_s = jnp.swapaxes(cn_s.reshape(2, nb_pad, 2 * d_ff), 0, 1)
        y_f, cn_f = _ffn(x_f, o_f, orw_f[0], cn_s[db], ff, 512, 1)
        y_s = jnp.swapaxes(y_s.reshape(dt, nb_pad, d), 0, 1)[:nb]

        hw = (DA_HEADS, 2 * HEAD_DIM)
        outs[0].append(jnp.concatenate([k_s[m0:].reshape(N_META, *hw), k_f], axis=0)[None])
        outs[1].append(jnp.concatenate([v_s[m0:].reshape(N_META, *hw), v_f], axis=0)[None])
        outs[2].append(sn_f.reshape(bp, RW_HEADS, HEAD_DIM, HEAD_DIM))
        outs[3].append(pr_f[seq - 1:seq])
        outs[4].append(cn_f[None])
        outs[5].append(k_s[:m0].reshape(db, dt, *hw))
        outs[6].append(v_s[:m0].reshape(db, dt, *hw))
        outs[7].append(sn_s[:db].reshape(db, RW_HEADS, HEAD_DIM, HEAD_DIM))
        outs[8].append(pr_s3[:db, dt - 1])
        outs[9].append(cn_s[:db])
        x_f, x_s = y_f, y_s

    return (x_f[None], x_s[:db], *[jnp.stack(o) for o in outs])
```

```python
import functools
import math

import numpy as np
import jax
import jax.numpy as jnp
from jax import lax
from jax.experimental import pallas as pl
from jax.experimental.pallas import tpu as pltpu

F32 = jnp.float32
BF16 = jnp.bfloat16

CHUNK = 64
N_META = 16
HEAD_DIM = 64
DA_HEADS = 4
RW_HEADS = 8
W_LORA = 64
A_LORA = 64
G_LORA = 128
CONV_W = 3
N_BUCKETS = 32
MAX_DISTANCE = 128
NORM_EPS = 1e-6
GN_EPS = 64e-5
NEG_INF = -1e30
LOG2E = math.log2(math.e)
DECAY_SCALE = math.exp(-0.5)

DA_WIDTH = DA_HEADS * 2 * HEAD_DIM
RW_WIDTH = RW_HEADS * HEAD_DIM
RW_PROJ = 3 * RW_WIDTH + W_LORA + A_LORA + G_LORA
LANES = 128
PAIR = 2 * HEAD_DIM
N_PAIR = RW_WIDTH // PAIR
RW_CHUNK = 64
ATT_T = 512
VMEM_LIMIT = 56 * 1024 * 1024


def _dot(a, b):
    return jnp.dot(a, b, preferred_element_type=F32)


def _dot_nt(a, b):
    return lax.dot_general(a, b, (((1,), (1,)), ((), ())), preferred_element_type=F32)


def _group_sum(x, blk):
    xb = x.astype(BF16)
    return jnp.concatenate([_dot(xb[:, LANES * p:LANES * (p + 1)], blk)
                            for p in range(x.shape[1] // LANES)], axis=1)


def _cparams(sem):
    return pltpu.CompilerParams(dimension_semantics=sem, vmem_limit_bytes=VMEM_LIMIT)


def _const_spec(shape):
    nd = len(shape)
    return pl.BlockSpec(shape, lambda *_: (0,) * nd)


def _resident_spec(shape):
    nd = len(shape)
    return pl.BlockSpec(shape, lambda *_: (0,) * nd, pipeline_mode=pl.Buffered(1))


def _proj_kernel(x_ref, g_ref, wf_ref, qg_ref, kg_ref, gm_ref,
                 q_ref, k_ref, v_ref, pr_ref, *rest, tm):
    *tile_refs, w_ref = rest

    @pl.when(pl.program_id(0) == 0)
    def _():
        w_ref[...] = wf_ref[...].astype(BF16)

    x = x_ref[...]
    ms = jnp.mean(x * x, axis=-1, keepdims=True)
    h = (x * lax.rsqrt(ms + NORM_EPS) * g_ref[...]).astype(BF16)
    gm = gm_ref[...]

    def group_norm(t, g):
        ms_g = _group_sum(t * t, gm)
        return t * lax.rsqrt(ms_g + NORM_EPS) * g

    q = _dot(h, w_ref[:, 0:DA_WIDTH])
    k = _dot(h, w_ref[:, DA_WIDTH:2 * DA_WIDTH])
    pr_ref[...] = _dot(h, w_ref[:, 3 * DA_WIDTH:])
    v = _dot(h, w_ref[:, 2 * DA_WIDTH:3 * DA_WIDTH])
    q_ref[...] = (group_norm(q, qg_ref[...]) * (HEAD_DIM ** -0.5 * LOG2E)).astype(BF16)
    k = group_norm(k, kg_ref[...])
    if len(k_ref.shape) == 3:
        for hd in range(DA_HEADS):
            k_ref[:, hd, :] = k[:, hd * LANES:(hd + 1) * LANES]
            v_ref[:, hd, :] = v[:, hd * LANES:(hd + 1) * LANES]
    else:
        k_ref[...] = k
        v_ref[...] = v
    if tile_refs:
        kb_ref, vt_ref = tile_refs
        kb = k.astype(BF16)
        vt = jnp.transpose(v).astype(BF16)
        for hd in range(DA_HEADS):
            for jj in range(tm // ATT_T):
                kb_ref[hd, jj] = kb[jj * ATT_T:(jj + 1) * ATT_T, hd * LANES:(hd + 1) * LANES]
                vt_ref[hd, jj] = vt[hd * LANES:(hd + 1) * LANES, jj * ATT_T:(jj + 1) * ATT_T]


def _proj(x, ln1_g, w_in_bf, qg, kg, gmean, tm, emit_tiles):
    rows, d = x.shape
    n_in = w_in_bf.shape[1]
    assert rows % tm == 0
    row = lambda w: pl.BlockSpec((tm, w), lambda i: (i, 0))
    out_specs = [row(DA_WIDTH), row(DA_WIDTH), row(DA_WIDTH), row(RW_PROJ)]
    out_shape = [jax.ShapeDtypeStruct((rows, DA_WIDTH), BF16),
                 jax.ShapeDtypeStruct((rows, DA_WIDTH), F32),
                 jax.ShapeDtypeStruct((rows, DA_WIDTH), F32),
                 jax.ShapeDtypeStruct((rows, RW_PROJ), F32)]
    if emit_tiles:
        assert tm % ATT_T == 0
        tpt = tm // ATT_T
        for o in (1, 2):
            out_specs[o] = pl.BlockSpec((tm, DA_HEADS, LANES), lambda i: (i, 0, 0))
            out_shape[o] = jax.ShapeDtypeStruct((rows, DA_HEADS, LANES), F32)
        out_specs += [pl.BlockSpec((DA_HEADS, tpt, ATT_T, LANES), lambda i: (0, i, 0, 0)),
                      pl.BlockSpec((DA_HEADS, tpt, LANES, ATT_T), lambda i: (0, i, 0, 0))]
        out_shape += [jax.ShapeDtypeStruct((DA_HEADS, rows // ATT_T, ATT_T, LANES), BF16),
                      jax.ShapeDtypeStruct((DA_HEADS, rows // ATT_T, LANES, ATT_T), BF16)]
    return pl.pallas_call(
        functools.partial(_proj_kernel, tm=tm),
        grid=(rows // tm,),
        in_specs=[row(d), _const_spec((1, d)), _resident_spec((d, n_in)),
                  _const_spec((1, DA_WIDTH)), _const_spec((1, DA_WIDTH)),
                  _const_spec((LANES, LANES))],
        out_specs=out_specs,
        out_shape=out_shape,
        scratch_shapes=[pltpu.VMEM((d, n_in), BF16)],
        compiler_params=_cparams(("arbitrary",)),
        name="proj",
    )(x, ln1_g, w_in_bf, qg, kg, gmean)


def _stack_components(q):
    lo = lax.broadcasted_iota(jnp.int32, q.shape, 1) < HEAD_DIM
    zero = jnp.zeros_like(q)
    return jnp.concatenate([jnp.where(lo, q, zero), jnp.where(lo, zero, q)], axis=0)


def _sub_layer_norm(o, g, out_scale):
    ms = jnp.mean(o * o, axis=-1, keepdims=True)
    return o * lax.rsqrt(ms + NORM_EPS) * g * out_scale


def _attn_prompt_kernel(cfar_ref, lam_ref, fix_ref, q_ref, k_ref, vt_ref, km_ref, vm_ref, vmt_ref, qm_ref,
                        b0_ref, bm1_ref, bq0_ref, bmm_ref, g_ref, o_ref, om_ref,
                        acc_ref, m_ref, l_ref, s_buf, p_buf, a_buf, *, out_scale):
    h = pl.program_id(0)
    i = pl.program_id(1)
    tq = ATT_T
    cf = cfar_ref[h]
    lam = lam_ref[0]
    qst = _stack_components(q_ref[...])

    def both(b):
        return jnp.concatenate([b, b], axis=1)

    s = _dot_nt(km_ref[...], qst)
    s = s + both(jnp.where(i == 0, bq0_ref[0], cf))
    use_bound = fix_ref[0] > 0.5
    bound = fix_ref[1]
    m0 = jnp.where(use_bound, bound, jnp.max(s, axis=0, keepdims=True))
    p = jnp.exp2(s - m0)
    m_ref[...] = m0
    l_ref[...] = jnp.sum(p, axis=0, keepdims=True)
    acc_ref[...] = _dot(vmt_ref[...], p.astype(BF16))

    def scores(idx, slot):
        s_buf[slot] = _dot_nt(k_ref[0, idx], qst)

    def softmax(slot, bias):
        s = s_buf[slot]
        m_prev = m_ref[...]
        if bias.ndim == 0:
            m_new = jnp.maximum(m_prev, jnp.max(s, axis=0, keepdims=True) + bias)
            p = jnp.exp2(s - (m_new - bias))
        else:
            s = s + both(bias)
            m_new = jnp.maximum(m_prev, jnp.max(s, axis=0, keepdims=True))
            p = jnp.exp2(s - m_new)
        alpha = jnp.exp2(m_prev - m_new)
        l_ref[...] = alpha * l_ref[...] + jnp.sum(p, axis=0, keepdims=True)
        m_ref[...] = m_new
        p_buf[slot] = p.astype(BF16)
        a_buf[slot] = alpha

    def accumulate(idx, slot):
        acc_ref[...] = a_buf[slot] * acc_ref[...] + _dot(vt_ref[0, idx], p_buf[slot])

    n_far = jnp.maximum(i - 1, 0)
    off = n_far % 2
    p_buf[1] = jnp.zeros(p_buf.shape[1:], BF16)
    a_buf[1] = jnp.ones(a_buf.shape[1:], F32)
    scores(0, 0)

    def far_body(t, carry):
        u = 2 * t - off
        scores(u + 1, 1)
        softmax(0, jnp.where(u >= 0, cf, NEG_INF))
        accumulate(jnp.maximum(u - 1, 0), 1)
        scores(u + 2, 0)
        softmax(1, cf)
        accumulate(jnp.maximum(u, 0), 0)
        return carry

    def bound_softmax(slot, bias):
        if bias.ndim == 0:
            p = jnp.exp2(s_buf[slot] - (bound - bias))
        else:
            p = jnp.exp2(s_buf[slot] + (both(bias) - bound))
        l_ref[...] += jnp.sum(p, axis=0, keepdims=True)
        p_buf[slot] = p.astype(BF16)

    def bound_accumulate(idx, slot):
        acc_ref[...] += _dot(vt_ref[0, idx], p_buf[slot])

    def far_body_bound(t, carry):
        u = 2 * t - off
        scores(u + 1, 1)
        bound_softmax(0, jnp.where(u >= 0, cf, NEG_INF))
        bound_accumulate(jnp.maximum(u - 1, 0), 1)
        scores(u + 2, 0)
        bound_softmax(1, cf)
        bound_accumulate(jnp.maximum(u, 0), 0)
        return carry

    def tail(softmax_fn, accumulate_fn):
        scores(i, 1)
        softmax_fn(0, jnp.where(i >= 1, bm1_ref[0], NEG_INF))
        accumulate_fn(jnp.maximum(n_far - 1, 0), 1)
        softmax_fn(1, b0_ref[0])
        accumulate_fn(n_far, 0)
        accumulate_fn(i, 1)

    @pl.when(use_bound)
    def _():
        lax.fori_loop(0, (n_far + 1) // 2, far_body_bound, 0)
        tail(bound_softmax, bound_accumulate)

    @pl.when(jnp.logical_not(use_bound))
    def _():
        lax.fori_loop(0, (n_far + 1) // 2, far_body, 0)
        tail(softmax, accumulate)

    accn = acc_ref[...] * (1.0 / l_ref[...])
    o_t = accn[:, 0:tq] - lam * accn[:, tq:]
    o_ref[...] = _sub_layer_norm(jnp.transpose(o_t), g_ref[...], out_scale).astype(BF16)

    @pl.when(i == 0)
    def _():
        qm = _stack_components(qm_ref[...])
        bmm = bmm_ref[0]
        sm = _dot_nt(qm, km_ref[...]) + jnp.concatenate([bmm, bmm], axis=0)
        mm = jnp.max(sm, axis=-1, keepdims=True)
        pm = jnp.exp2(sm - mm)
        accm = _dot(pm.astype(BF16), vm_ref[...]) / jnp.sum(pm, axis=-1, keepdims=True)
        om = accm[0:N_META] - lam * accm[N_META:]
        om_ref[...] = _sub_layer_norm(om, g_ref[...], out_scale).astype(BF16)


def _attn_prompt(cfar, lam, fix, q_f, kb_t, vt_t, kb_m, vb_m, vbt_m, q_m, b0, bm1, bq0, bmm, subln_g,
                 out_scale):
    tf = q_f.shape[0]
    tq = ATT_T
    assert tf % tq == 0
    n_t = tf // tq
    smem = pl.BlockSpec(memory_space=pltpu.SMEM)
    head_col = lambda rows: pl.BlockSpec((rows, LANES), lambda h, i: (0, h))
    head_tile = lambda a, b: pl.BlockSpec((1, a, b), lambda h, i: (h, 0, 0))
    return pl.pallas_call(
        functools.partial(_attn_prompt_kernel, out_scale=out_scale),
        grid=(DA_HEADS, n_t),
        in_specs=[smem, smem, smem,
                  pl.BlockSpec((tq, LANES), lambda h, i: (i, h)),
                  pl.BlockSpec((1, n_t, tq, LANES), lambda h, i: (h, 0, 0, 0)),
                  pl.BlockSpec((1, n_t, LANES, tq), lambda h, i: (h, 0, 0, 0)),
                  head_col(N_META), head_col(N_META),
                  pl.BlockSpec((LANES, N_META), lambda h, i: (h, 0)),
                  head_col(N_META),
                  head_tile(tq, tq), head_tile(tq, tq), head_tile(N_META, tq),
                  head_tile(N_META, N_META), _const_spec((1, LANES))],
        out_specs=[pl.BlockSpec((tq, LANES), lambda h, i: (i, h)), head_col(N_META)],
        out_shape=[jax.ShapeDtypeStruct((tf, DA_WIDTH), BF16),
                   jax.ShapeDtypeStruct((N_META, DA_WIDTH), BF16)],
        scratch_shapes=[pltpu.VMEM((LANES, 2 * tq), F32), pltpu.VMEM((1, 2 * tq), F32),
                        pltpu.VMEM((1, 2 * tq), F32), pltpu.VMEM((2, tq, 2 * tq), F32),
                        pltpu.VMEM((2, tq, 2 * tq), BF16), pltpu.VMEM((2, 1, 2 * tq), F32)],
        compiler_params=_cparams(("arbitrary", "arbitrary")),
        name="attn_prompt",
    )(cfar, lam, fix, q_f, kb_t, vt_t, kb_m, vb_m, vbt_m, q_m, b0, bm1, bq0, bmm, subln_g)


DEC_HEADS = 4


def _attn_decode_kernel(lam_ref, q_ref, k_hbm, v_hbm, kn_ref, vn_ref, b_ref, g_ref, o_ref,
                        kbuf, vbuf, sem, *, layer, n_main, out_scale):
    bi = pl.program_id(0)
    hg = pl.program_id(1)
    n_hg = pl.num_programs(1)
    step = bi * n_hg + hg
    slot = step % 2

    def cache_copies(b_, hg_, slot_):
        cps = []
        for hh in range(DEC_HEADS):
            h_ = hg_ * DEC_HEADS + hh
            cps.append(pltpu.make_async_copy(k_hbm.at[layer, b_, :, h_, :], kbuf.at[slot_, hh],
                                             sem.at[0, slot_, hh]))
            cps.append(pltpu.make_async_copy(v_hbm.at[layer, b_, :, h_, :], vbuf.at[slot_, hh],
                                             sem.at[1, slot_, hh]))
        return cps

    @pl.when(step == 0)
    def _():
        for cp in cache_copies(bi, hg, slot):
            cp.start()

    @pl.when(step + 1 < pl.num_programs(0) * n_hg)
    def _():
        nxt = step + 1
        for cp in cache_copies(nxt // n_hg, nxt % n_hg, 1 - slot):
            cp.start()

    for cp in cache_copies(bi, hg, slot):
        cp.wait()

    lam = lam_ref[0]
    nq = q_ref.shape[0]
    n_cache = kbuf.shape[2]
    for hh in range(DEC_HEADS):
        cols = slice(LANES * hh, LANES * (hh + 1))
        k_ref = kbuf.at[slot, hh]
        v_ref = vbuf.at[slot, hh]
        qst = _stack_components(q_ref[:, cols])
        k_main = k_ref[0:n_main, :].astype(BF16)
        v_main = v_ref[0:n_main, :].astype(BF16)
        k_tail = jnp.concatenate([k_ref[n_main:n_cache, :], kn_ref[:, cols]], axis=0).astype(BF16)
        v_tail = jnp.concatenate([v_ref[n_main:n_cache, :], vn_ref[:, cols]], axis=0).astype(BF16)
        bias = jnp.concatenate([b_ref[hh], b_ref[hh]], axis=0)
        s1 = _dot_nt(qst, k_main) + bias[:, 0:n_main]
        s2 = _dot_nt(qst, k_tail) + bias[:, n_main:]
        m = jnp.maximum(jnp.max(s1, axis=-1, keepdims=True), jnp.max(s2, axis=-1, keepdims=True))
        p1 = jnp.exp2(s1 - m)
        p2 = jnp.exp2(s2 - m)
        l = jnp.sum(p1, axis=-1, keepdims=True) + jnp.sum(p2, axis=-1, keepdims=True)
        res = (_dot(p1.astype(BF16), v_main) + _dot(p2.astype(BF16), v_tail)) / l
        o = res[0:nq] - lam * res[nq:]
        o_ref[:, cols] = _sub_layer_norm(o, g_ref[...], out_scale).astype(BF16)


def _attn_decode(lam, q_s, cache_k, cache_v, layer, k_s, v_s, bias, subln_g, out_scale, nq):
    _, b, n_cache, n_h, _ = cache_k.shape
    assert n_h % DEC_HEADS == 0
    n_main = (n_cache // LANES) * LANES
    n_keys = n_cache + nq
    smem = pl.BlockSpec(memory_space=pltpu.SMEM)
    hbm = pl.BlockSpec(memory_space=pl.ANY)
    new = pl.BlockSpec((nq, DEC_HEADS * LANES), lambda bi, hg: (bi, hg))
    buf = pltpu.VMEM((2, DEC_HEADS, n_cache, LANES), F32)
    return pl.pallas_call(
        functools.partial(_attn_decode_kernel, layer=layer, n_main=n_main, out_scale=out_scale),
        grid=(b, n_h // DEC_HEADS),
        in_specs=[smem, new, hbm, hbm, new, new,
                  pl.BlockSpec((DEC_HEADS, nq, n_keys), lambda bi, hg: (hg, 0, 0)),
                  _const_spec((1, LANES))],
        out_specs=new,
        out_shape=jax.ShapeDtypeStruct((b * nq, DA_WIDTH), BF16),
        scratch_shapes=[buf, buf, pltpu.SemaphoreType.DMA((2, 2, DEC_HEADS))],
        compiler_params=_cparams(("arbitrary", "arbitrary")),
        name="attn_decode",
    )(lam, q_s, cache_k, cache_v, k_s, v_s, bias, subln_g)


def _sigmoid(x):
    return 1.0 / (1.0 + jnp.exp(-x))


def _split2(x):
    x1 = x.astype(BF16)
    x2 = (x - x1.astype(F32)).astype(BF16)
    return x1, x2


def _stack_heads(x):
    lo = lax.broadcasted_iota(jnp.int32, x.shape, 1) < HEAD_DIM
    zero = jnp.zeros_like(x)
    return jnp.concatenate([jnp.where(lo, x, zero), jnp.where(lo, zero, x)], axis=0)


def _rwkv_kernel(pr_ref, sh0_ref, h0_ref, mu_ref, w0_ref, a0_ref, kk_ref, ka_ref, rk_ref,
                 lw_ref, lb_ref, w2_ref, a2_ref, g2_ref, gsum_ref, tri_ref,
                 o_ref, sn_ref,
                 h_st, xbuf, carry, at_s, rt_s, bt_s, kt_s, v_s, ep_s, y_s,
                 t_s, tav_s, lrb_s, lrkv_s, zbt_s, zkv_s, ar_s, pc_s,
                 *, tr, t_valid, n_state):
    c_len = RW_CHUNK
    n_ch = tr // c_len
    ti = pl.program_id(1)

    @pl.when(ti == 0)
    def _():
        has_state = pl.program_id(0) < n_state
        carry[...] = jnp.where(has_state, sh0_ref[0], 0.0)
        hd_r = lax.broadcasted_iota(jnp.int32, (PAIR, PAIR), 0) // HEAD_DIM
        hd_c = lax.broadcasted_iota(jnp.int32, (PAIR, PAIR), 1) // HEAD_DIM
        for p in range(N_PAIR):
            s_p = h0_ref[0, p]
            blk = jnp.where((hd_r == hd_c) & has_state, jnp.concatenate([s_p, s_p], axis=1), 0.0)
            h_st[p] = jnp.transpose(blk)

    x = pr_ref[0]
    xbuf[8:8 + tr, :] = x
    xbuf[7:8, :] = carry[...]
    prev = xbuf[7:7 + tr, :]
    carry[...] = pr_ref[0, tr - 1:tr, :]
    xm = x + (prev - x) * mu_ref[...]
    r = xm[:, 0:RW_WIDTH]
    kr = xm[:, RW_WIDTH:2 * RW_WIDTH]
    vr = xm[:, 2 * RW_WIDTH:3 * RW_WIDTH]
    wa = xm[:, 3 * RW_WIDTH:3 * RW_WIDTH + W_LORA + A_LORA]
    gd = xm[:, 3 * RW_WIDTH + W_LORA + A_LORA:]
    lane_wa = lax.broadcasted_iota(jnp.int32, wa.shape, 1)
    twa = jnp.where(lane_wa < W_LORA, jnp.tanh(wa), wa).astype(BF16)
    logw = -DECAY_SCALE * _sigmoid(w0_ref[...] + _dot(twa, w2_ref[...]))
    a_sig = _sigmoid(a0_ref[...] + _dot(twa, a2_ref[...]))
    g = _dot(_sigmoid(gd).astype(BF16), g2_ref[...])
    gsum = gsum_ref[...]
    kk = kr * kk_ref[...]
    kk = kk * lax.rsqrt(jnp.maximum(_group_sum(kk * kk, gsum), 1e-24))
    kr2 = kr * (1.0 + (a_sig - 1.0) * ka_ref[...])
    a_vec = -kk
    b_vec = kk * a_sig
    bonus = _group_sum(r * kr2 * rk_ref[...], gsum) * vr
    if t_valid % tr != 0:
        row = lax.broadcasted_iota(jnp.int32, (tr, 1), 0) + ti * tr
        valid = row < t_valid
        logw = jnp.where(valid, logw, 0.0)
        a_vec = jnp.where(valid, a_vec, 0.0)
        b_vec = jnp.where(valid, b_vec, 0.0)
        kr2 = jnp.where(valid, kr2, 0.0)
        vr = jnp.where(valid, vr, 0.0)
        bonus = jnp.where(valid, bonus, 0.0)
    l1, l2 = _split2(logw)
    tri = tri_ref[...]
    cs = _dot(tri, l1) + _dot(tri, l2)
    e_pos = jnp.exp(cs)
    e_neg = jnp.exp(-cs)
    at_s[...] = a_vec * jnp.exp(cs - logw)
    rt_s[...] = r * e_pos
    bt_s[...] = b_vec * e_neg
    kt_s[...] = kr2 * e_neg
    v_s[...] = vr
    ep_s[...] = e_pos

    idx_r = lax.broadcasted_iota(jnp.int32, (PAIR, PAIR), 0)
    idx_c = lax.broadcasted_iota(jnp.int32, (PAIR, PAIR), 1)
    same = (idx_r // c_len) == (idx_c // c_len)
    strict = same & ((idx_r % c_len) > (idx_c % c_len))
    incl = same & ((idx_r % c_len) >= (idx_c % c_len))
    eye = idx_r == idx_c
    eye_f = jnp.where(eye, 1.0, 0.0).astype(F32)

    zero = jnp.zeros((PAIR, PAIR), F32)
    n_par = next(n for n in (8, 4, 2, 1) if n_ch % n == 0)
    n_grp = n_ch // n_par
    items = [(dc, p) for dc in range(n_par) for p in range(N_PAIR)]
    pairs = range(N_PAIR)

    def s1_load(g):
        sel = [(pl.ds(pl.multiple_of((g * n_par + dc) * c_len, c_len), c_len),
                slice(PAIR * p, PAIR * (p + 1))) for dc, p in items]
        return dict(at=[at_s[rw, cl] for rw, cl in sel], rt=[rt_s[rw, cl] for rw, cl in sel],
                    bt=[bt_s[rw, cl] for rw, cl in sel], kt=[kt_s[rw, cl] for rw, cl in sel],
                    v=[v_s[rw, cl] for rw, cl in sel], ep=[ep_s[rw, cl] for rw, cl in sel])

    def s1_compute(ld):
        q_all = range(len(items))
        at, rt = ld["at"], ld["rt"]
        yb = [_stack_heads(x) for x in ld["bt"]]
        yk = [_stack_heads(x) for x in ld["kt"]]
        vst = [_stack_heads(x).astype(BF16) for x in ld["v"]]
        pc = [x[c_len - 1:c_len] for x in ld["ep"]]
        gmat = [_dot_nt(jnp.concatenate([_stack_heads(at[q]), _stack_heads(rt[q])], axis=0).astype(BF16),
                        jnp.concatenate([yb[q], yk[q]], axis=0).astype(BF16)) for q in q_all]
        aab = [jnp.where(strict, gmat[q][0:PAIR, 0:PAIR], zero) for q in q_all]
        aak = [jnp.where(strict, gmat[q][0:PAIR, PAIR:], zero).astype(BF16) for q in q_all]
        lrb = [jnp.where(incl, gmat[q][PAIR:, 0:PAIR], zero).astype(BF16) for q in q_all]
        lrk = [jnp.where(incl, gmat[q][PAIR:, PAIR:], zero).astype(BF16) for q in q_all]
        tinv = [eye_f + aab[q] for q in q_all]
        lp = aab
        n = 1
        order = min(c_len, t_valid)
        while 2 * n < order:
            lpb = [x.astype(BF16) for x in lp]
            lp = [_dot(x, x) for x in lpb]
            tinv = [tinv[q] + _dot(tinv[q].astype(BF16), lp[q].astype(BF16)) for q in q_all]
            n *= 2
        tb = [x.astype(BF16) for x in tinv]
        av = [_dot(aak[q], vst[q]).astype(BF16) for q in q_all]
        return dict(
            t=tb, tav=[_dot(tb[q], av[q]) for q in q_all], lrb=lrb,
            lrkv=[_dot(lrk[q], vst[q]) for q in q_all],
            zbt=[jnp.transpose(yb[q] * pc[q]).astype(BF16) for q in q_all],
            zkv=[_dot(jnp.transpose(yk[q] * pc[q]).astype(BF16), vst[q]) for q in q_all],
            ar=[jnp.concatenate([at[q], rt[q]], axis=0).astype(BF16) for q in q_all],
            pc=[jnp.sum(jnp.where(eye, jnp.broadcast_to(pc[q], (PAIR, PAIR)), zero), axis=-1, keepdims=True)
                for q in q_all])

    stage_bufs = dict(t=t_s, tav=tav_s, lrb=lrb_s, lrkv=lrkv_s, zbt=zbt_s, zkv=zkv_s, ar=ar_s, pc=pc_s)

    def s1_store(g, res):
        for q, (dc, p) in enumerate(items):
            for name, buf in stage_bufs.items():
                buf[g * n_par + dc, p] = res[name][q]

    def s2_load(g):
        return [{name: [buf[g * n_par + dc, p] for p in pairs] for name, buf in stage_bufs.items()}
                for dc in range(n_par)]

    def s2_compute(ld_group, hbd):
        ys = []
        for ld in ld_group:
            arh = [_dot(ld["ar"][p], hbd[p].astype(BF16)) for p in pairs]
            ub = [(_dot(ld["t"][p], _stack_heads(arh[p][0:c_len]).astype(BF16)) + ld["tav"][p]).astype(BF16)
                  for p in pairs]
            yst = [_dot(ld["lrb"][p], ub[p]) + ld["lrkv"][p] for p in pairs]
            ys.append([arh[p][c_len:] + yst[p][0:c_len] + yst[p][c_len:] for p in pairs])
            hbd = [ld["pc"][p] * hbd[p] + _dot(ld["zbt"][p], ub[p]) + ld["zkv"][p] for p in pairs]
        return ys, hbd

    def s2_store(g, ys, hbd):
        for dc in range(n_par):
            rows = pl.ds(pl.multiple_of((g * n_par + dc) * c_len, c_len), c_len)
            for p in pairs:
                y_s[rows, PAIR * p:PAIR * (p + 1)] = ys[dc][p]
        for p in pairs:
            h_st[p] = hbd[p]

    s1_store(0, s1_compute(s1_load(0)))

    def body(g, carry_):
        ld2 = s2_load(g)
        ld1 = s1_load(g + 1)
        hbd = [h_st[p] for p in pairs]
        res1 = s1_compute(ld1)
        ys, hbd = s2_compute(ld2, hbd)
        s2_store(g, ys, hbd)
        s1_store(g + 1, res1)
        return carry_

    lax.fori_loop(0, n_grp - 1, body, 0)
    ys_last, h_last = s2_compute(s2_load(n_grp - 1), [h_st[p] for p in pairs])
    s2_store(n_grp - 1, ys_last, h_last)

    @pl.when(ti == pl.num_programs(1) - 1)
    def _():
        first = lax.broadcasted_iota(jnp.int32, (PAIR, HEAD_DIM), 0) < HEAD_DIM
        for p in range(N_PAIR):
            s_t = jnp.transpose(h_st[p])
            sn_ref[0, p] = jnp.where(first, s_t[:, 0:HEAD_DIM], s_t[:, HEAD_DIM:])

    y = y_s[...]
    inv_n = 1.0 / HEAD_DIM
    mean = _group_sum(y, gsum) * inv_n
    d = y - mean
    var = _group_sum(d * d, gsum) * inv_n
    yn = d * lax.rsqrt(var + GN_EPS) * lw_ref[...] + lb_ref[...]
    o_ref[0] = ((yn + bonus) * g).astype(BF16)


def _rwkv(pr, shift0, s0, prm, tr, t_valid):
    b, t_pad, _ = pr.shape
    n_state = s0.shape[0]
    assert shift0.shape[0] == n_state
    last = n_state - 1
    assert t_pad % tr == 0 and tr % RW_CHUNK == 0
    n_ch = tr // RW_CHUNK
    tri = np.zeros((tr, tr), np.float32)
    for c in range(n_ch):
        tri[c * RW_CHUNK:(c + 1) * RW_CHUNK, c * RW_CHUNK:(c + 1) * RW_CHUNK] = np.tril(
            np.ones((RW_CHUNK, RW_CHUNK), np.float32))
    tri = jnp.asarray(tri, BF16)
    vec = _const_spec((1, RW_WIDTH))
    mat = lambda dt: pltpu.VMEM((n_ch, N_PAIR, PAIR, PAIR), dt)
    tile = lambda: pltpu.VMEM((tr, RW_WIDTH), F32)
    return pl.pallas_call(
        functools.partial(_rwkv_kernel, tr=tr, t_valid=t_valid, n_state=n_state),
        grid=(b, t_pad // tr),
        in_specs=[pl.BlockSpec((1, tr, RW_PROJ), lambda bi, ti: (bi, ti, 0)),
                  pl.BlockSpec((1, 1, RW_PROJ), lambda bi, ti: (jnp.minimum(bi, last), 0, 0)),
                  pl.BlockSpec((1, N_PAIR, PAIR, HEAD_DIM),
                               lambda bi, ti: (jnp.minimum(bi, last), 0, 0, 0)),
                  _const_spec((1, RW_PROJ)), vec, vec, vec, vec, vec, vec, vec,
                  _const_spec((W_LORA + A_LORA, RW_WIDTH)), _const_spec((W_LORA + A_LORA, RW_WIDTH)),
                  _const_spec((G_LORA, RW_WIDTH)), _const_spec((LANES, LANES)),
                  _const_spec((tr, tr))],
        out_specs=[pl.BlockSpec((1, tr, RW_WIDTH), lambda bi, ti: (bi, ti, 0)),
                   pl.BlockSpec((1, N_PAIR, PAIR, HEAD_DIM), lambda bi, ti: (bi, 0, 0, 0))],
        out_shape=[jax.ShapeDtypeStruct((b, t_pad, RW_WIDTH), BF16),
                   jax.ShapeDtypeStruct((b, N_PAIR, PAIR, HEAD_DIM), F32)],
        scratch_shapes=[pltpu.VMEM((N_PAIR, PAIR, PAIR), F32),
                        pltpu.VMEM((8 + tr, RW_PROJ), F32), pltpu.VMEM((1, RW_PROJ), F32),
                        tile(), tile(), tile(), tile(), tile(), tile(), tile(),
                        mat(BF16), mat(F32), mat(BF16), mat(F32), mat(BF16), mat(F32), mat(BF16),
                        pltpu.VMEM((n_ch, N_PAIR, PAIR, 1), F32)],
        compiler_params=_cparams(("arbitrary", "arbitrary")),
        name="rwkv",
    )(pr, shift0, s0, prm["mu"], prm["w0"], prm["a0"], prm["k_k"], prm["k_a"], prm["r_k"],
      prm["lnx_w"], prm["lnx_b"], prm["w2p"], prm["a2p"], prm["g2"], prm["gsum"], tri)


FF_COLS = 256
FF_STAGES = 11


def _ffn_kernel(x_ref, oa_ref, orw_ref, c0_ref, wo_ref, g_ref, wu_ref, cw_ref, cb_ref, wd_ref,
                y_ref, cn_ref, zbuf, cbuf, h_ref, x1_ref, act_ref, *, tm, ts, d_ff):
    off = zbuf.shape[1] - tm
    da = oa_ref.shape[1]

    @pl.when(pl.program_id(0) == 0)
    def _():
        cbuf[...] = c0_ref[...]

    x1 = x_ref[...] + _dot(oa_ref[...], wo_ref[0:da, :]) + _dot(orw_ref[...], wo_ref[da:, :])
    x1_ref[...] = x1
    ms = jnp.mean(x1 * x1, axis=-1, keepdims=True)
    h_ref[...] = (x1 * lax.rsqrt(ms + NORM_EPS) * g_ref[...]).astype(BF16)

    def up_proj(c0, zb):
        cols = slice(c0, c0 + FF_COLS)
        zb[off - 2 * ts:off, :] = cbuf[:, cols]
        zb[off:off + tm, :] = _dot(h_ref[...], wu_ref[:, cols])
        cbuf[:, cols] = zb[off + tm - 2 * ts:off + tm, :]

    def conv_cols(c0, zb):
        cols = slice(c0, c0 + FF_COLS)
        z2 = zb[off - 2 * ts:off - 2 * ts + tm, :]
        z1 = zb[off - ts:off - ts + tm, :]
        z = zb[off:off + tm, :]
        return (cb_ref[:, cols] + z2 * cw_ref[0:1, cols] + z1 * cw_ref[1:2, cols]
                + z * cw_ref[2:3, cols])

    def stage(c):
        up_proj(c * FF_COLS, zbuf.at[2 * (c % FF_STAGES)])
        up_proj(d_ff + c * FF_COLS, zbuf.at[2 * (c % FF_STAGES) + 1])

    n_chunks = d_ff // FF_COLS
    stage(0)
    for c in range(n_chunks):
        if c + 1 < n_chunks:
            stage(c + 1)
        gate = conv_cols(c * FF_COLS, zbuf.at[2 * (c % FF_STAGES)])
        up = conv_cols(d_ff + c * FF_COLS, zbuf.at[2 * (c % FF_STAGES) + 1])
        act_ref[:, c * FF_COLS:(c + 1) * FF_COLS] = (gate * _sigmoid(gate) * up).astype(BF16)

    y_ref[...] = x1_ref[...] + _dot(act_ref[...], wd_ref[...])
    cn_ref[...] = cbuf[...]


def _ffn(x, oa, orw, conv0, prm, tm, ts):
    rows, d = x.shape
    d_ff = prm["w_down"].shape[0]
    assert rows % tm == 0 and d_ff % FF_COLS == 0 and (ts == 1 or ts % 8 == 0)
    off = -(-2 * ts // 8) * 8
    row = lambda w: pl.BlockSpec((tm, w), lambda i: (i, 0))
    return pl.pallas_call(
        functools.partial(_ffn_kernel, tm=tm, ts=ts, d_ff=d_ff),
        grid=(rows // tm,),
        in_specs=[row(d), row(DA_WIDTH), row(RW_WIDTH), _const_spec((2 * ts, 2 * d_ff)),
                  _resident_spec((DA_WIDTH + RW_WIDTH, d)), _const_spec((1, d)),
                  _resident_spec((d, 2 * d_ff)), _const_spec((CONV_W, 2 * d_ff)),
                  _const_spec((1, 2 * d_ff)), _resident_spec((d_ff, d))],
        out_specs=[row(d), _const_spec((2 * ts, 2 * d_ff))],
        out_shape=[jax.ShapeDtypeStruct((rows, d), F32),
                   jax.ShapeDtypeStruct((2 * ts, 2 * d_ff), F32)],
        scratch_shapes=[pltpu.VMEM((2 * FF_STAGES, off + tm, FF_COLS), F32),
                        pltpu.VMEM((2 * ts, 2 * d_ff), F32),
                        pltpu.VMEM((tm, d), BF16), pltpu.VMEM((tm, d), F32),
                        pltpu.VMEM((tm, d_ff), BF16)],
        compiler_params=_cparams(("arbitrary",)),
        name="ffn",
    )(x, oa, orw, conv0, prm["w_out"], prm["ln2_g"], prm["w_up"], prm["conv_w"], prm["conv_b"],
      prm["w_down"])


def _rel_bucket(rel):
    nb = N_BUCKETS // 2
    max_exact = nb // 2
    bucket = jnp.where(rel > 0, nb, 0)
    n = jnp.abs(rel)
    nf = jnp.maximum(n, 1).astype(F32)
    large = max_exact + (jnp.log(nf / max_exact) / math.log(MAX_DISTANCE / max_exact)
                         * (nb - max_exact)).astype(jnp.int32)
    large = jnp.minimum(large, nb - 1)
    return bucket + jnp.where(n < max_exact, n, large)


def _bias_table(rel_bias, q_pos, k_pos, mask):
    n_q, n_k = len(q_pos), len(k_pos)
    assert np.all(np.diff(q_pos) == 1) and np.all(np.diff(k_pos) == 1)
    span = n_q + n_k - 1
    rel = jnp.asarray(int(k_pos[0]) - int(q_pos[0]) - (n_q - 1) + np.arange(span), jnp.int32)
    w = jnp.transpose(rel_bias[_rel_bucket(rel)]).astype(F32) * LOG2E
    return jnp.where(jnp.asarray(mask)[None], _toeplitz(w, n_q, n_k), NEG_INF)


def _skew(w, n_q, n_k):
    span = n_q + n_k - 1
    x = jnp.pad(w[:, :span], ((0, 0), (0, 1)))
    rows = jnp.tile(x, (1, n_q))[:, :n_q * span].reshape(w.shape[0], n_q, span)
    return rows[:, :, n_q - 1:n_q - 1 + n_k]


def _toeplitz(w, n_q, n_k):
    blk = LANES
    if n_q % blk or n_k % blk or n_q * n_k <= blk * blk:
        return _skew(w, n_q, n_k)
    nbq, nbk = n_q // blk, n_k // blk
    n_d = nbq + nbk - 1
    n_h = w.shape[0]
    wp = jnp.pad(w[:, :n_q + n_k - 1], ((0, 0), (0, 1)))
    seg = jnp.concatenate([wp[:, :n_d * blk].reshape(n_h, n_d, blk),
                           wp[:, blk:].reshape(n_h, n_d, blk)[..., :blk - 1]], axis=-1)
    x = jnp.pad(seg, ((0, 0), (0, 0), (0, 1)))
    rows = jnp.tile(x, (1, 1, blk))[..., :blk * (2 * blk - 1)].reshape(n_h, n_d, blk, 2 * blk - 1)
    blocks = rows[..., blk - 1:]
    return jnp.concatenate([jnp.concatenate([blocks[:, c - r + nbq - 1] for c in range(nbk)], axis=2)
                            for r in range(nbq)], axis=1)


BOUND_MAX_SPREAD = 100.0


def _score_bound(q_g, k_g, rel_bias):
    s_max = (HEAD_DIM ** 0.5) * LOG2E * 1.01 * jnp.max(jnp.abs(q_g)) * jnp.max(jnp.abs(k_g))
    b_hi = jnp.max(rel_bias) * LOG2E
    b_lo = jnp.min(rel_bias) * LOG2E
    use = (2.0 * s_max + (b_hi - b_lo)) <= BOUND_MAX_SPREAD
    return jnp.stack([use.astype(F32), (s_max + b_hi).astype(F32)])


def _ext_chunk(pos):
    return np.where(pos < N_META, -1, (pos - N_META) // CHUNK)


def _prompt_bias(rel_bias):
    tq = ATT_T
    fr = np.arange(tq) + N_META
    meta = np.arange(N_META)
    causal = _ext_chunk(fr)[None, :] <= _ext_chunk(fr)[:, None]
    tr = lambda b: jnp.swapaxes(b, 1, 2)
    b0 = tr(_bias_table(rel_bias, fr, fr, causal))
    bm1 = tr(_bias_table(rel_bias, fr + tq, fr, np.ones((tq, tq), bool)))
    bq0 = tr(_bias_table(rel_bias, fr, meta, np.ones((tq, N_META), bool)))
    bmm = _bias_table(rel_bias, meta, meta, np.ones((N_META, N_META), bool))
    assert tq + 1 >= MAX_DISTANCE
    cfar = rel_bias[_rel_bucket(jnp.asarray(-(tq + 1), jnp.int32))].astype(F32) * LOG2E
    return cfar, b0, bm1, bq0, bmm


def _decode_bias(rel_bias, n_cache, nq):
    k_pos = np.arange(n_cache + nq)
    q_pos = k_pos[n_cache:]
    mask = _ext_chunk(k_pos)[None, :] <= _ext_chunk(q_pos)[:, None]
    return _bias_table(rel_bias, q_pos, k_pos, mask)


def _block_ones(n, blk, dtype):
    idx = np.arange(n) // blk
    return jnp.asarray((idx[:, None] == idx[None, :]).astype(np.float32), dtype)


def kernel(x_prompt, x_sample, cache_k, cache_v, state_rwkv, state_shift, state_conv, meta_tokens,
           rel_bias, ln1_g, w_in, q_norm_g, k_norm_g, lam_q1, lam_k1, lam_q2, lam_k2, subln_g,
           mu_shift, w0, w2, a0, a2, g2, k_k, k_a, r_k, lnx_w, lnx_b, w_out, ln2_g, w_up, conv_w,
           conv_b, w_down):
    bp, seq, d = x_prompt.shape
    db, dt, _ = x_sample.shape
    depth = w_in.shape[0]
    d_ff = w_down.shape[1]
    n_cache = cache_k.shape[2]
    assert bp == 1 and dt == N_META, "the meta stream rides with the decode streams"
    assert cache_k.shape[3] == DA_HEADS and cache_k.shape[4] == 2 * HEAD_DIM
    nb = db + 1
    nb_pad = -(-nb // 8) * 8

    cfar, b0, bm1, bq0, bmm = _prompt_bias(rel_bias)
    bias_dec = _decode_bias(rel_bias, n_cache, dt)
    gsum = _block_ones(LANES, HEAD_DIM, BF16)
    gmean = gsum * (1.0 / HEAD_DIM)
    zrow = lambda n: jnp.zeros((n, RW_WIDTH), BF16)

    x_f = x_prompt[0]
    x_s = jnp.concatenate([x_sample, meta_tokens.astype(x_sample.dtype)[None]], axis=0)
    outs = [[] for _ in range(10)]
    for l in range(depth):
        lam_init = 0.8 - 0.6 * math.exp(-0.3 * l)
        lam = (jnp.exp(jnp.sum(lam_q1[l].astype(F32) * lam_k1[l].astype(F32)))
               - jnp.exp(jnp.sum(lam_q2[l].astype(F32) * lam_k2[l].astype(F32))) + lam_init).reshape(1)
        out_scale = 1.0 - lam_init
        tile128 = lambda g_: jnp.tile(g_.reshape(1, -1), (1, DA_WIDTH // g_.shape[-1]))
        qg, kg = tile128(q_norm_g[l]), tile128(k_norm_g[l])
        sg = subln_g[l].reshape(1, LANES)
        w_in_bf = w_in[l]
        rw = {
            "mu": mu_shift[l].reshape(1, -1), "w0": w0[l].reshape(1, -1), "a0": a0[l].reshape(1, -1),
            "k_k": k_k[l].reshape(1, -1), "k_a": k_a[l].reshape(1, -1), "r_k": r_k[l].reshape(1, -1),
            "lnx_w": lnx_w[l].reshape(1, -1), "lnx_b": lnx_b[l].reshape(1, -1),
            "w2p": jnp.concatenate([w2[l].astype(BF16), zrow(A_LORA)], axis=0),
            "a2p": jnp.concatenate([zrow(W_LORA), a2[l].astype(BF16)], axis=0),
            "g2": g2[l].astype(BF16), "gsum": gsum,
        }
        ff = {
            "w_out": w_out[l].astype(BF16), "ln2_g": ln2_g[l].reshape(1, -1),
            "w_up": w_up[l].astype(BF16), "conv_w": conv_w[l], "conv_b": conv_b[l].reshape(1, -1),
            "w_down": w_down[l].astype(BF16),
        }

        q_f, k_f, v_f, pr_f, kb_t, vt_t = _proj(x_f, ln1_g[l].reshape(1, -1), w_in_bf, qg, kg, gmean,
                                                 512, True)
        q_s, k_s, v_s, pr_s = _proj(x_s.reshape(nb * dt, d), ln1_g[l].reshape(1, -1), w_in_bf, qg, kg,
                                    gmean, nb * dt, False)
        m0 = db * dt

        kb_m = k_s[m0:].astype(BF16)
        vb_m = v_s[m0:].astype(BF16)
        o_f, o_m = _attn_prompt(cfar, lam, _score_bound(q_norm_g[l], k_norm_g[l], rel_bias), q_f, kb_t, vt_t, kb_m, vb_m, jnp.transpose(vb_m), q_s[m0:],
                                b0, bm1, bq0, bmm, sg, out_scale)
        o_d = _attn_decode(lam, q_s, cache_k, cache_v, l, k_s, v_s, bias_dec, sg, out_scale, dt)
        o_s = jnp.concatenate([o_d, o_m], axis=0)

        pr_s3 = pr_s.reshape(nb, dt, RW_PROJ)
        pr_pad = jnp.pad(pr_s3, ((0, 0), (0, RW_CHUNK - dt), (0, 0)))
        pair_view = (N_PAIR, PAIR, HEAD_DIM)
        orw_s, sn_s = _rwkv(pr_pad, state_shift[l][:, None, :], state_rwkv[l].reshape(db, *pair_view),
                            rw, RW_CHUNK, dt)
        orw_f, sn_f = _rwkv(pr_f[None], pr_s3[db:, dt - 1:dt, :], sn_s[db:], rw, 512, seq)

        def time_major(a):
            a = jnp.pad(a.reshape(nb, dt, -1), ((0, nb_pad - nb), (0, 0), (0, 0)))
            return jnp.swapaxes(a, 0, 1).reshape(dt * nb_pad, -1)

        conv_s = jnp.concatenate([state_conv[l], jnp.zeros((1, CONV_W - 1, 2 * d_ff), F32)], axis=0)
        conv_s = jnp.pad(conv_s, ((0, nb_pad - nb), (0, 0), (0, 0)))
        conv_s = jnp.swapaxes(conv_s, 0, 1).reshape(2 * nb_pad, 2 * d_ff)
        y_s, cn_s = _ffn(time_major(x_s), time_major(o_s), time_major(orw_s[:, :dt]), conv_s, ff,
                         dt * nb_pad, nb_pad)
        cn_s = jnp.swapaxes(cn_s.reshape(2, nb_pad, 2 * d_ff), 0, 1)
        y_f, cn_f = _ffn(x_f, o_f, orw_f[0], cn_s[db], ff, 512, 1)
        y_s = jnp.swapaxes(y_s.reshape(dt, nb_pad, d), 0, 1)[:nb]

        hw = (DA_HEADS, 2 * HEAD_DIM)
        outs[0].append(jnp.concatenate([k_s[m0:].reshape(N_META, *hw), k_f], axis=0)[None])
        outs[1].append(jnp.concatenate([v_s[m0:].reshape(N_META, *hw), v_f], axis=0)[None])
        outs[2].append(sn_f.reshape(bp, RW_HEADS, HEAD_DIM, HEAD_DIM))
        outs[3].append(pr_f[seq - 1:seq])
        outs[4].append(cn_f[None])
        outs[5].append(k_s[:m0].reshape(db, dt, *hw))
        outs[6].append(v_s[:m0].reshape(db, dt, *hw))
        outs[7].append(sn_s[:db].reshape(db, RW_HEADS, HEAD_DIM, HEAD_DIM))
        outs[8].append(pr_s3[:db, dt - 1])
        outs[9].append(cn_s[:db])
        x_f, x_s = y_f, y_s

    return (x_f[None], x_s[:db], *[jnp.stack(o) for o in outs])
```
